```python
import math, functools
import jax, jax.numpy as jnp
from jax import lax
import numpy as np

D_MODEL = 1024
BATCH = 4
SEQ = 4096
DEPTH = 2
DEC_BATCH = 128
DEC_SEQ = 8
PAST_LEN = 16384
PAGE_SIZE = 128

N_META = 16
BLOCK = 128
META_PAD = BLOCK - N_META
A_HEADS = 4
A_DK = 128
A_DV = 128
A_KW = A_HEADS * A_DK
A_VW = A_HEADS * A_DV
A_CONV = 4
A_CONV_CH = 2 * A_KW + A_VW
A_CHUNK = 64
B_HEADS = 8
B_KV_HEADS = 2
B_GROUP = B_HEADS // B_KV_HEADS
B_HD = 64
WINDOW = 128
C_CH = 512
C_CONV = 31
N_BRANCH = 3
BRANCH_W = 512
N_GROUPS = 4
EXP_PER_GROUP = 4
N_EXPERTS = N_GROUPS * EXP_PER_GROUP
TOP_K_IN_GROUP = 2
D_EXPERT = 256
EPS = 1e-6
SPLIT_SIZES = (A_KW, A_KW, A_VW, A_VW, A_HEADS, A_HEADS,
               B_HEADS * B_HD, B_KV_HEADS * B_HD, B_KV_HEADS * B_HD,
               2 * C_CH, N_BRANCH * D_MODEL)
IN_W = sum(SPLIT_SIZES)

kernel_name = 'hybrid_gdn_swa_conformer_hmoe_step'


def rms_norm(x, w):
    xf = x.astype(jnp.float32)
    y = xf * lax.rsqrt(jnp.mean(xf * xf, axis=-1, keepdims=True) + EPS)
    return (y * w.astype(jnp.float32)).astype(x.dtype)


def layer_norm(x, w, b):
    xf = x.astype(jnp.float32)
    mu = jnp.mean(xf, axis=-1, keepdims=True)
    var = jnp.mean(jnp.square(xf - mu), axis=-1, keepdims=True)
    y = (xf - mu) * lax.rsqrt(var + EPS)
    return (y * w.astype(jnp.float32) + b.astype(jnp.float32)).astype(x.dtype)


def l2norm(x):
    return x * lax.rsqrt(jnp.sum(x * x, axis=-1, keepdims=True) + EPS)


def split_cols(p):
    out, off = [], 0
    for s in SPLIT_SIZES:
        out.append(p[..., off:off + s])
        off += s
    return out


def causal_dwconv(u, hist, w):
    ext = jnp.concatenate([hist.astype(u.dtype), u], axis=1)
    y = lax.conv_general_dilated(ext, w.astype(u.dtype)[:, None, :], window_strides=(1,), padding='VALID',
                                 dimension_numbers=('NWC', 'WIO', 'NWC'), feature_group_count=u.shape[-1])
    return y, ext[:, ext.shape[1] - (w.shape[0] - 1):]


def gated_delta_chunked(q, k, v, g, beta, S0, chunk):
    N, L, H, DK = q.shape
    n = L // chunk

    def blocks(t):
        return jnp.moveaxis(t.reshape(N, n, chunk, H, *t.shape[3:]), 3, 2)

    q, k, v, g, beta = blocks(q), blocks(k), blocks(v), blocks(g), blocks(beta)
    G = jnp.cumsum(g, axis=-1)
    idx = jnp.arange(chunk)
    incl = idx[:, None] >= idx[None, :]
    strict = idx[:, None] > idx[None, :]
    decay = jnp.exp(jnp.where(incl, G[..., :, None] - G[..., None, :], -jnp.inf))
    kk = jnp.einsum('bnhid,bnhjd->bnhij', k, k)
    tri = jnp.where(strict, kk * decay * beta[..., :, None], 0.0) + jnp.eye(chunk, dtype=kk.dtype)
    u = lax.linalg.triangular_solve(tri, v * beta[..., None], left_side=True, lower=True, unit_diagonal=True)
    w = lax.linalg.triangular_solve(tri, k * (beta * jnp.exp(G))[..., None], left_side=True, lower=True,
                                    unit_diagonal=True)
    qk = jnp.einsum('bnhid,bnhjd->bnhij', q, k) * decay
    q_dec = q * jnp.exp(G)[..., None]
    k_dec = k * jnp.exp(G[..., -1:] - G)[..., None]
    g_last = jnp.exp(G[..., -1])
    xs = tuple(jnp.moveaxis(t, 1, 0) for t in (qk, q_dec, k_dec, u, w, g_last))

    def step(S, blk):
        qk_c, qd_c, kd_c, u_c, w_c, gl_c = blk
        v_new = u_c - jnp.einsum('bhcd,bhde->bhce', w_c, S)
        o = jnp.einsum('bhcd,bhde->bhce', qd_c, S) + jnp.einsum('bhij,bhje->bhie', qk_c, v_new)
        S = S * gl_c[..., None, None] + jnp.einsum('bhcd,bhce->bhde', kd_c, v_new)
        return S, o

    S, o = lax.scan(step, S0, xs)
    o = jnp.moveaxis(jnp.moveaxis(o, 0, 1), 2, 3).reshape(N, L, H, v.shape[-1])
    return o, S


def gdn_branch(q, k, v, z, a, b, valid, state, hist, conv_w, a_log, dt_bias, norm_w, chunk):
    N, L, _ = q.shape
    f32 = jnp.float32
    vm = valid[None, :, None]
    qkv = jnp.where(vm, jnp.concatenate([q, k, v], axis=-1), 0)
    qkv, hist = causal_dwconv(qkv, hist, conv_w)
    qkv = jax.nn.silu(qkv.astype(f32))
    qc = l2norm(qkv[..., :A_KW].reshape(N, L, A_HEADS, A_DK)) * (A_DK ** -0.5)
    kc = l2norm(qkv[..., A_KW:2 * A_KW].reshape(N, L, A_HEADS, A_DK))
    vc = qkv[..., 2 * A_KW:].reshape(N, L, A_HEADS, A_DV)
    beta = jnp.where(vm, jax.nn.sigmoid(b.astype(f32)), 0.0)
    g = jnp.where(vm, -jnp.exp(a_log.astype(f32)) * jax.nn.softplus(a.astype(f32) + dt_bias.astype(f32)), 0.0)
    o, S = gated_delta_chunked(qc, kc, vc, g, beta, state.astype(f32), chunk)
    o = rms_norm(o, norm_w) * jax.nn.silu(z.astype(f32).reshape(N, L, A_HEADS, A_DV))
    return o.reshape(N, L, A_VW).astype(q.dtype), S.astype(state.dtype), hist


def sink_attention(q, k, v, mask, sinks):
    f32 = jnp.float32
    s = jnp.einsum('bnqhgd,bnkhd->bnhgqk', q.astype(f32), k.astype(f32)) * (B_HD ** -0.5)
    s = jnp.where(mask[None, :, None, None], s, -jnp.inf)
    sink = jnp.broadcast_to(sinks.astype(f32).reshape(B_KV_HEADS, B_GROUP)[None, None, :, :, None, None],
                            s.shape[:-1] + (1,))
    p = jax.nn.softmax(jnp.concatenate([s, sink], axis=-1), axis=-1)[..., :-1]
    return jnp.einsum('bnhgqk,bnkhd->bnqhgd', p, v.astype(f32))


def swa_prompt(q, k, v, q_norm_w, k_norm_w, sinks):
    n, lp, _ = q.shape
    nb = lp // BLOCK
    q = rms_norm(q.reshape(n, lp, B_KV_HEADS, B_GROUP, B_HD), q_norm_w)
    k = rms_norm(k.reshape(n, lp, B_KV_HEADS, B_HD), k_norm_w)
    v = v.reshape(n, lp, B_KV_HEADS, B_HD)

    def band(t):
        tb = t.reshape(n, nb, BLOCK, B_KV_HEADS, B_HD)
        prev = jnp.concatenate([jnp.zeros_like(tb[:, :1]), tb[:, :-1]], axis=1)
        meta = jnp.broadcast_to(t[:, None, META_PAD:BLOCK], (n, nb, N_META, B_KV_HEADS, B_HD))
        return jnp.concatenate([meta, prev, tb], axis=2)

    qi = jnp.arange(nb)[:, None] * BLOCK + jnp.arange(BLOCK)[None, :]
    kj = (jnp.arange(nb)[:, None] - 1) * BLOCK + jnp.arange(2 * BLOCK)[None, :]
    mj = META_PAD + jnp.arange(N_META)
    meta_ok = mj[None, None, :] <= qi[:, :, None]
    band_ok = ((kj[:, None, :] <= qi[:, :, None]) & (kj[:, None, :] > qi[:, :, None] - WINDOW)
               & (kj[:, None, :] >= BLOCK))
    mask = jnp.concatenate([meta_ok, band_ok], axis=-1)
    o = sink_attention(q.reshape(n, nb, BLOCK, B_KV_HEADS, B_GROUP, B_HD), band(k), band(v), mask, sinks)
    meta_kv = jnp.stack([k[:, META_PAD:BLOCK], v[:, META_PAD:BLOCK]], axis=2)
    win_kv = jnp.stack([k[:, lp - WINDOW:], v[:, lp - WINDOW:]], axis=2)
    return o.reshape(n, lp, B_HEADS * B_HD).astype(q.dtype), (meta_kv, win_kv)


def swa_sample(q, k, v, q_norm_w, k_norm_w, sinks, win_kv, meta_kv):
    n, t, _ = q.shape
    w = win_kv.shape[1]
    q = rms_norm(q.reshape(n, t, B_KV_HEADS, B_GROUP, B_HD), q_norm_w)
    k = rms_norm(k.reshape(n, t, B_KV_HEADS, B_HD), k_norm_w)
    v = v.reshape(n, t, B_KV_HEADS, B_HD)
    new_kv = jnp.stack([k, v], axis=2).astype(win_kv.dtype)
    kv = jnp.concatenate([meta_kv.astype(win_kv.dtype), win_kv, new_kv], axis=1)
    qpos = PAST_LEN + jnp.arange(t)
    wpos = PAST_LEN - w + jnp.arange(w)
    meta_ok = jnp.ones((t, N_META), dtype=bool)
    win_ok = (wpos[None, :] > qpos[:, None] - WINDOW) & (wpos[None, :] >= N_META)
    new_ok = (qpos[None, :] <= qpos[:, None]) & (qpos[None, :] > qpos[:, None] - WINDOW)
    mask = jnp.concatenate([meta_ok, win_ok, new_ok], axis=-1)[None]
    o = sink_attention(q[:, None], kv[:, None, :, 0], kv[:, None, :, 1], mask, sinks)
    new_win = jnp.concatenate([win_kv, new_kv], axis=1)[:, t:]
    return o.reshape(n, t, B_HEADS * B_HD).astype(q.dtype), new_win


def conformer_branch(u2, valid, hist, dw_w, dw_b, ln_w, ln_b):
    u = jnp.where(valid[None, :, None], u2[..., :C_CH] * jax.nn.sigmoid(u2[..., C_CH:]), 0)
    y, hist = causal_dwconv(u, hist, dw_w)
    y = jax.nn.silu(layer_norm(y + dw_b.astype(y.dtype), ln_w, ln_b))
    return y, hist


def token_mix(h, valid, chunk, swa_fn, gdn_state, gdn_hist, conf_hist, lw):
    n, L, _ = h.shape
    aq, ak, av, az, aa, ab, bq, bk, bv, cu, gl = split_cols(jnp.einsum('nld,de->nle', h, lw['w_in']))
    o_a, gdn_state, gdn_hist = gdn_branch(aq, ak, av, az, aa, ab, valid, gdn_state, gdn_hist, lw['gdn_conv_w'],
                                          lw['gdn_a_log'], lw['gdn_dt_bias'], lw['gdn_norm_w'], chunk)
    o_b, swa_new = swa_fn(bq, bk, bv, lw['swa_q_norm_w'], lw['swa_k_norm_w'], lw['swa_sinks'])
    o_c, conf_hist = conformer_branch(cu, valid, conf_hist, lw['conv_dw_w'], lw['conv_dw_b'],
                                      lw['conv_ln_w'], lw['conv_ln_b'])
    branches = jnp.stack([o_a.astype(h.dtype), o_b.astype(h.dtype), o_c.astype(h.dtype)], axis=0)
    per_branch = jnp.einsum('rnlc,rcd->nlrd', branches, lw['w_branch'])
    gates = jax.nn.sigmoid(gl.reshape(n, L, N_BRANCH, D_MODEL).astype(jnp.float32))
    merged = jnp.sum(gates * per_branch, axis=2).astype(h.dtype)
    return merged @ lw['w_out'], (gdn_state, gdn_hist, swa_new, conf_hist)


def hier_moe(h, wg, bg, we, be, w_gate, w_up, w_down):
    N, L, D = h.shape
    x = h.reshape(N * L, D)
    T = x.shape[0]
    f32 = jnp.float32
    rows = jnp.arange(T)
    g_logits = jnp.dot(x, wg).astype(f32) + bg.astype(f32)
    g_prob = jax.nn.softmax(g_logits, axis=-1)
    g_idx = jnp.argmax(g_logits, axis=-1)
    g_w = g_prob[rows, g_idx]
    e_logits = (jnp.dot(x, we).astype(f32) + be.astype(f32)).reshape(T, N_GROUPS, EXP_PER_GROUP)[rows, g_idx]
    top_v, top_i = lax.top_k(e_logits, TOP_K_IN_GROUP)
    wts = jax.nn.softmax(top_v, axis=-1) * g_w[:, None]
    eid = g_idx[:, None] * EXP_PER_GROUP + top_i
    combine = jnp.einsum('tk,tke->te', wts, jax.nn.one_hot(eid, N_EXPERTS, dtype=f32))
    hidden = jax.nn.silu(jnp.einsum('td,edf->tef', x, w_gate)) * jnp.einsum('td,edf->tef', x, w_up)
    hidden = hidden * combine[:, :, None].astype(hidden.dtype)
    return jnp.einsum('tef,efd->td', hidden, w_down).reshape(N, L, D).astype(h.dtype)


def setup_inputs(seed: int = 0) -> dict:
    key = jax.random.key(seed)
    ks = jax.random.split(key, 40)
    f32 = jnp.float32

    def nrm(k, shape, scale):
        return jax.random.normal(k, shape, f32) * scale

    win_buf = min(WINDOW, PAST_LEN)
    dt = jnp.exp(jax.random.uniform(ks[12], (DEPTH, A_HEADS), f32, math.log(1e-3), math.log(1e-1)))
    return {
        'x_prompt': nrm(ks[0], (BATCH, SEQ, D_MODEL), 1.0),
        'x_sample': nrm(ks[1], (DEC_BATCH, DEC_SEQ, D_MODEL), 1.0),
        'state_gdn': nrm(ks[2], (DEPTH, DEC_BATCH, A_HEADS, A_DK, A_DV), 0.1),
        'cache_gdn_conv': nrm(ks[3], (DEPTH, DEC_BATCH, A_CONV - 1, A_CONV_CH), 1.0),
        'cache_swa_kv': nrm(ks[4], (DEPTH, DEC_BATCH, win_buf, 2, B_KV_HEADS, B_HD), 1.0),
        'cache_meta_kv': nrm(ks[5], (DEPTH, DEC_BATCH, N_META, 2, B_KV_HEADS, B_HD), 1.0),
        'cache_conv': nrm(ks[6], (DEPTH, DEC_BATCH, C_CONV - 1, C_CH), 0.5),
        'meta_tokens': nrm(ks[7], (N_META, D_MODEL), 1.0),
        'norm1_w': 1.0 + nrm(ks[8], (DEPTH, D_MODEL), 0.02),
        'norm2_w': 1.0 + nrm(ks[9], (DEPTH, D_MODEL), 0.02),
        'w_in': nrm(ks[10], (DEPTH, D_MODEL, IN_W), D_MODEL ** -0.5),
        'gdn_conv_w': nrm(ks[11], (DEPTH, A_CONV, A_CONV_CH), A_CONV ** -0.5),
        'gdn_a_log': jnp.log(jax.random.uniform(ks[13], (DEPTH, A_HEADS), f32, 1.0, 16.0)),
        'gdn_dt_bias': dt + jnp.log(-jnp.expm1(-dt)),
        'gdn_norm_w': 1.0 + nrm(ks[14], (DEPTH, A_DV), 0.02),
        'swa_q_norm_w': 1.0 + nrm(ks[15], (DEPTH, B_HD), 0.02),
        'swa_k_norm_w': 1.0 + nrm(ks[16], (DEPTH, B_HD), 0.02),
        'swa_sinks': nrm(ks[17], (DEPTH, B_HEADS), 0.5),
        'conv_dw_w': nrm(ks[18], (DEPTH, C_CONV, C_CH), C_CONV ** -0.5),
        'conv_dw_b': nrm(ks[19], (DEPTH, C_CH), 0.02),
        'conv_ln_w': 1.0 + nrm(ks[20], (DEPTH, C_CH), 0.02),
        'conv_ln_b': nrm(ks[21], (DEPTH, C_CH), 0.02),
        'w_branch': nrm(ks[22], (DEPTH, N_BRANCH, BRANCH_W, D_MODEL), BRANCH_W ** -0.5),
        'w_out': nrm(ks[23], (DEPTH, D_MODEL, D_MODEL), D_MODEL ** -0.5),
        'router_group_w': nrm(ks[24], (DEPTH, D_MODEL, N_GROUPS), D_MODEL ** -0.5),
        'router_group_b': nrm(ks[25], (DEPTH, N_GROUPS), 0.01),
        'router_expert_w': nrm(ks[26], (DEPTH, D_MODEL, N_EXPERTS), D_MODEL ** -0.5),
        'router_expert_b': nrm(ks[27], (DEPTH, N_EXPERTS), 0.01),
        'moe_w_gate': nrm(ks[28], (DEPTH, N_EXPERTS, D_MODEL, D_EXPERT), D_MODEL ** -0.5),
        'moe_w_up': nrm(ks[29], (DEPTH, N_EXPERTS, D_MODEL, D_EXPERT), D_MODEL ** -0.5),
        'moe_w_down': nrm(ks[30], (DEPTH, N_EXPERTS, D_EXPERT, D_MODEL), D_EXPERT ** -0.5),
    }


def reference(x_prompt, x_sample, state_gdn, cache_gdn_conv, cache_swa_kv, cache_meta_kv, cache_conv,
              meta_tokens, norm1_w, norm2_w, w_in, gdn_conv_w, gdn_a_log, gdn_dt_bias, gdn_norm_w,
              swa_q_norm_w, swa_k_norm_w, swa_sinks, conv_dw_w, conv_dw_b, conv_ln_w, conv_ln_b,
              w_branch, w_out, router_group_w, router_group_b, router_expert_w, router_expert_b,
              moe_w_gate, moe_w_up, moe_w_down):
    dtp = x_prompt.dtype
    n_p = x_prompt.shape[0]
    t_s = x_sample.shape[1]
    xp = jnp.concatenate([jnp.zeros((n_p, META_PAD, D_MODEL), dtp),
                          jnp.broadcast_to(meta_tokens.astype(dtp)[None], (n_p, N_META, D_MODEL)),
                          x_prompt], axis=1)
    valid_p = jnp.arange(xp.shape[1]) >= META_PAD
    xs = x_sample
    valid_s = jnp.ones((t_s,), dtype=bool)
    gdn_p, gconv_p, swa_p, meta_p, conv_p = [], [], [], [], []
    gdn_s, gconv_s, swa_s, conv_s = [], [], [], []
    for l in range(DEPTH):
        lw = {'w_in': w_in[l], 'gdn_conv_w': gdn_conv_w[l], 'gdn_a_log': gdn_a_log[l],
              'gdn_dt_bias': gdn_dt_bias[l], 'gdn_norm_w': gdn_norm_w[l], 'swa_q_norm_w': swa_q_norm_w[l],
              'swa_k_norm_w': swa_k_norm_w[l], 'swa_sinks': swa_sinks[l], 'conv_dw_w': conv_dw_w[l],
              'conv_dw_b': conv_dw_b[l], 'conv_ln_w': conv_ln_w[l], 'conv_ln_b': conv_ln_b[l],
              'w_branch': w_branch[l], 'w_out': w_out[l]}
        mix, (s_new, gh_new, (mkv_new, wkv_new), ch_new) = token_mix(
            rms_norm(xp, norm1_w[l]), valid_p, A_CHUNK, swa_prompt,
            jnp.zeros((n_p, A_HEADS, A_DK, A_DV), dtp), jnp.zeros((n_p, A_CONV - 1, A_CONV_CH), dtp),
            jnp.zeros((n_p, C_CONV - 1, C_CH), dtp), lw)
        xp = xp + mix
        xp = xp + hier_moe(rms_norm(xp, norm2_w[l]), router_group_w[l], router_group_b[l], router_expert_w[l],
                           router_expert_b[l], moe_w_gate[l], moe_w_up[l], moe_w_down[l])
        gdn_p.append(s_new); gconv_p.append(gh_new); swa_p.append(wkv_new); meta_p.append(mkv_new)
        conv_p.append(ch_new)
        swa_fn = functools.partial(swa_sample, win_kv=cache_swa_kv[l], meta_kv=cache_meta_kv[l])
        mix, (s_new, gh_new, wkv_new, ch_new) = token_mix(
            rms_norm(xs, norm1_w[l]), valid_s, t_s, swa_fn, state_gdn[l], cache_gdn_conv[l], cache_conv[l], lw)
        xs = xs + mix
        xs = xs + hier_moe(rms_norm(xs, norm2_w[l]), router_group_w[l], router_group_b[l], router_expert_w[l],
                           router_expert_b[l], moe_w_gate[l], moe_w_up[l], moe_w_down[l])
        gdn_s.append(s_new); gconv_s.append(gh_new); swa_s.append(wkv_new); conv_s.append(ch_new)
    y_prompt = xp[:, BLOCK:]
    y_sample = xs
    return (y_prompt, y_sample,
            jnp.stack(gdn_p), jnp.stack(gconv_p), jnp.stack(swa_p), jnp.stack(meta_p), jnp.stack(conv_p),
            jnp.stack(gdn_s), jnp.stack(gconv_s), jnp.stack(swa_s), jnp.stack(conv_s))
```

```python
import functools

import jax
import jax.numpy as jnp
from jax import lax
from jax.experimental import pallas as pl
from jax.experimental.pallas import tpu as pltpu

F32 = jnp.float32
BF16 = jnp.bfloat16
HIGHEST = lax.Precision.HIGHEST
EPS = 1e-6
NEG = -1e30

VMEM_LIMIT_BYTES = 56 * 1024 * 1024
LANES = 128
SUBLANES = 8

PAST_LEN = 16384
N_META = 16
BLOCK = 128
WINDOW = 128
META_PAD = BLOCK - N_META
A_HEADS = 4
A_DK = 128
A_CONV = 4
B_HEADS = 8
B_KV_HEADS = 2
B_GROUP = B_HEADS // B_KV_HEADS
B_HD = 64
C_CH = 512
C_CONV = 31
BRANCH_W = 512
N_GROUPS = 4
EXP_PER_GROUP = 4
N_EXPERTS = N_GROUPS * EXP_PER_GROUP

A_KW = A_HEADS * A_DK
QKV_W = 3 * A_KW
Z_BLK = QKV_W // BRANCH_W
CU_BLK = 2
GL_BLK = 1
BQ_BLK = 12
BKV_W = 2 * B_KV_HEADS * B_HD
BKV_BLK = 26
AB_BLK = 54
PROJ_W = 7040
CONF_HALO = 32
GDN_HALO = SUBLANES


def _rms(x, w):
    return x * lax.rsqrt(jnp.mean(x * x, axis=-1, keepdims=True) + EPS) * w


def _silu(x):
    return x * jax.nn.sigmoid(x)


def _cparams(sem):
    return pltpu.CompilerParams(dimension_semantics=sem, vmem_limit_bytes=VMEM_LIMIT_BYTES)


def _inproj_kernel(x_ref, nw_ref, w_ref, o_ref, h_ref):
    @pl.when(pl.program_id(1) == 0)
    def _():
        h_ref[...] = _rms(x_ref[...], nw_ref[...]).astype(BF16)

    o_ref[...] = jnp.dot(h_ref[...], w_ref[...], preferred_element_type=F32)


def _inproj(x2, nw, w, tm, tn):
    t, d = x2.shape
    wd = w.shape[1]
    return pl.pallas_call(
        _inproj_kernel,
        grid=(t // tm, wd // tn),
        in_specs=[pl.BlockSpec((tm, d), lambda i, j: (i, 0)),
                  pl.BlockSpec((1, d), lambda i, j: (0, 0)),
                  pl.BlockSpec((d, tn), lambda i, j: (0, j))],
        out_specs=pl.BlockSpec((tm, tn), lambda i, j: (i, j)),
        out_shape=jax.ShapeDtypeStruct((t, wd), F32),
        scratch_shapes=[pltpu.VMEM((tm, d), BF16)],
        compiler_params=_cparams(("parallel", "arbitrary")),
        name="inproj",
    )(x2, nw, w)


def _conf_kernel(cu_ref, hist_ref, dww_ref, dwb_ref, lnw_ref, lnb_ref, y_ref, nh_ref, e_ref,
                 *, tm, rc, n_invalid):
    t = pl.program_id(1)
    nb = cu_ref.shape[0]
    halo = CONF_HALO

    @pl.when(t == 0)
    def _():
        e_ref[:, 0:halo, :] = hist_ref[...]

    @pl.when(t > 0)
    def _():
        e_ref[:, 0:halo, :] = e_ref[:, tm:tm + halo, :]

    cu = cu_ref[...]
    u = cu[:, :, :C_CH] * jax.nn.sigmoid(cu[:, :, C_CH:])
    if n_invalid:
        row = lax.broadcasted_iota(jnp.int32, (1, tm, 1), 1) + t * tm
        u = jnp.where(row >= n_invalid, u, 0.0)
    e_ref[:, halo:halo + tm, :] = u
    nh_ref[...] = e_ref[:, tm:tm + halo, :]

    off = halo - (C_CONV - 1)
    for c in range(C_CH // LANES):
        cs = slice(c * LANES, (c + 1) * LANES)

        def rbody(r, carry, cs=cs):
            r0 = pl.multiple_of(r * rc, rc)
            blk = e_ref[:, pl.ds(r0, rc + halo), cs]
            acc = jnp.zeros((nb, rc, LANES), F32)
            for k in range(C_CONV):
                acc = acc + dww_ref[k:k + 1, cs] * blk[:, off + k:off + k + rc, :]
            y_ref[:, pl.ds(r0, rc), cs] = acc + dwb_ref[:, cs]
            return carry

        lax.fori_loop(0, tm // rc, rbody, 0)

    y = y_ref[...]
    mu = jnp.mean(y, axis=-1, keepdims=True)
    var = jnp.mean(jnp.square(y - mu), axis=-1, keepdims=True)
    yn = (y - mu) * lax.rsqrt(var + EPS) * lnw_ref[...] + lnb_ref[...]
    y_ref[...] = _silu(yn)


def _conformer(p3, hist, dww, dwb, lnw, lnb, nb, tm, rc, n_invalid):
    n, l, _ = p3.shape
    kern = functools.partial(_conf_kernel, tm=tm, rc=rc, n_invalid=n_invalid)
    full = lambda shape: pl.BlockSpec(shape, lambda i, t: (0,) * len(shape))
    return pl.pallas_call(
        kern,
        grid=(n // nb, l // tm),
        in_specs=[pl.BlockSpec((nb, tm, 2 * C_CH), lambda i, t: (i, t, CU_BLK)),
                  pl.BlockSpec((nb, CONF_HALO, C_CH), lambda i, t: (i, 0, 0)),
                  full(dww.shape), full(dwb.shape), full(lnw.shape), full(lnb.shape)],
        out_specs=[pl.BlockSpec((nb, tm, C_CH), lambda i, t: (i, t, 0)),
                   pl.BlockSpec((nb, CONF_HALO, C_CH), lambda i, t: (i, 0, 0))],
        out_shape=[jax.ShapeDtypeStruct((n, l, C_CH), F32),
                   jax.ShapeDtypeStruct((n, CONF_HALO, C_CH), F32)],
        scratch_shapes=[pltpu.VMEM((nb, CONF_HALO + tm, C_CH), F32)],
        compiler_params=_cparams(("parallel", "arbitrary")),
        name="conformer",
    )(p3, hist, dww, dwb, lnw, lnb)


def _softmax_sink_pv(s, mask, sink, v, dims):
    s = jnp.where(mask, s, NEG)
    m = jnp.maximum(jnp.max(s, axis=-1, keepdims=True), sink)
    p = jnp.exp(s - m)
    den = jnp.sum(p, axis=-1, keepdims=True) + jnp.exp(sink - m)
    return lax.dot_general(p, v, dims, preferred_element_type=F32) / den


def _swa_prompt_kernel(q_ref, kv_ref, kvp_ref, kvm_ref, qnw_ref, knw_ref, sink_ref, o_ref, kn_ref):
    j = pl.program_id(1)
    q = q_ref[...]
    kv = kv_ref[...]
    kvp = kvp_ref[...]
    kvm = kvm_ref[META_PAD:BLOCK, :]
    qnw = qnw_ref[...]
    knw = knw_ref[...]
    nk = N_META + 2 * BLOCK
    rows = B_GROUP * BLOCK
    r = lax.broadcasted_iota(jnp.int32, (rows, nk), 0) % BLOCK
    c = lax.broadcasted_iota(jnp.int32, (rows, nk), 1)
    qi = j * BLOCK + r
    meta_ok = (c < N_META) & (META_PAD + c <= qi)
    cp = c - N_META
    prev_ok = (cp >= 0) & (cp < BLOCK) & (cp > r) & (j >= 2)
    co = cp - BLOCK
    own_ok = (co >= 0) & (co <= r) & (j >= 1)
    mask = meta_ok | prev_ok | own_ok
    hrow = lax.broadcasted_iota(jnp.int32, (rows, 1), 0) // BLOCK
    kn_parts = []
    for g in range(B_KV_HEADS):
        ks = slice(g * B_HD, (g + 1) * B_HD)
        vs = slice((B_KV_HEADS + g) * B_HD, (B_KV_HEADS + g + 1) * B_HD)
        k_own = _rms(kv[:, ks], knw)
        kn_parts.append(k_own)
        kk = jnp.concatenate([_rms(kvm[:, ks], knw), _rms(kvp[:, ks], knw), k_own], axis=0)
        vv = jnp.concatenate([kvm[:, vs], kvp[:, vs], kv[:, vs]], axis=0)
        qq = jnp.concatenate(
            [_rms(q[:, (g * B_GROUP + i) * B_HD:(g * B_GROUP + i + 1) * B_HD], qnw) for i in range(B_GROUP)],
            axis=0)
        sink = jnp.zeros((rows, 1), F32)
        for i in range(B_GROUP):
            h = g * B_GROUP + i
            sink = jnp.where(hrow == i, sink_ref[0:1, h:h + 1], sink)
        s = lax.dot_general(qq, kk, (((1,), (1,)), ((), ())), preferred_element_type=F32) * (B_HD ** -0.5)
        o = _softmax_sink_pv(s, mask, sink, vv, (((1,), (0,)), ((), ())))
        for i in range(B_GROUP):
            h = g * B_GROUP + i
            o_ref[:, h * B_HD:(h + 1) * B_HD] = o[i * BLOCK:(i + 1) * BLOCK, :]
    kn_ref[...] = jnp.concatenate(kn_parts, axis=-1)


def _swa_prompt(p3, qnw, knw, sinks):
    n, l, _ = p3.shape
    full = lambda shape: pl.BlockSpec(shape, lambda i, j: (0,) * len(shape))
    return pl.pallas_call(
        _swa_prompt_kernel,
        grid=(n, l // BLOCK),
        in_specs=[pl.BlockSpec((None, BLOCK, B_HEADS * B_HD), lambda i, j: (i, j, BQ_BLK)),
                  pl.BlockSpec((None, BLOCK, BKV_W), lambda i, j: (i, j, BKV_BLK)),
                  pl.BlockSpec((None, BLOCK, BKV_W), lambda i, j: (i, jnp.maximum(j - 1, 0), BKV_BLK)),
                  pl.BlockSpec((None, BLOCK, BKV_W), lambda i, j: (i, 0, BKV_BLK)),
                  full(qnw.shape), full(knw.shape), full(sinks.shape)],
        out_specs=[pl.BlockSpec((None, BLOCK, B_HEADS * B_HD), lambda i, j: (i, j, 0)),
                   pl.BlockSpec((None, BLOCK, B_KV_HEADS * B_HD), lambda i, j: (i, j, 0))],
        out_shape=[jax.ShapeDtypeStruct((n, l, B_HEADS * B_HD), F32),
                   jax.ShapeDtypeStruct((n, l, B_KV_HEADS * B_HD), F32)],
        compiler_params=_cparams(("parallel", "arbitrary")),
        name="swa_prompt",
    )(p3, p3, p3, p3, qnw, knw, sinks)


def _swa_sample_kernel(q_ref, kvn_ref, win_ref, meta_ref, qnw_ref, knw_ref, sink_ref, o_ref, nwin_ref):
    q = q_ref[...]
    kvn = kvn_ref[...]
    win = win_ref[...]
    meta = meta_ref[...]
    qnw = qnw_ref[...]
    knw = knw_ref[...]
    t_new = q.shape[1]
    w = win.shape[1]
    nk = N_META + w + t_new
    rows = B_GROUP * t_new
    tq = lax.broadcasted_iota(jnp.int32, (1, rows, nk), 1) % t_new
    c = lax.broadcasted_iota(jnp.int32, (1, rows, nk), 2)
    cw = c - N_META
    win_ok = (cw >= 0) & (cw < w) & (cw - w > tq - WINDOW) & (cw + (PAST_LEN - w) >= N_META)
    cn = cw - w
    new_ok = (cn >= 0) & (cn <= tq) & (cn > tq - WINDOW)
    mask = (c < N_META) | win_ok | new_ok
    hrow = lax.broadcasted_iota(jnp.int32, (1, rows, 1), 1) // t_new
    kn_parts = []
    for g in range(B_KV_HEADS):
        ks = slice(g * B_HD, (g + 1) * B_HD)
        vs = slice((B_KV_HEADS + g) * B_HD, (B_KV_HEADS + g + 1) * B_HD)
        k_new = _rms(kvn[:, :, ks], knw)
        kn_parts.append(k_new)
        kk = jnp.concatenate([meta[:, :, ks], win[:, :, ks], k_new], axis=1)
        vv = jnp.concatenate([meta[:, :, vs], win[:, :, vs], kvn[:, :, vs]], axis=1)
        qq = jnp.concatenate(
            [_rms(q[:, :, (g * B_GROUP + i) * B_HD:(g * B_GROUP + i + 1) * B_HD], qnw) for i in range(B_GROUP)],
            axis=1)
        sink = jnp.zeros((1, rows, 1), F32)
        for i in range(B_GROUP):
            h = g * B_GROUP + i
            sink = jnp.where(hrow == i, sink_ref[0:1, h:h + 1], sink)
        s = jnp.einsum('bqd,bkd->bqk', qq, kk, preferred_element_type=F32) * (B_HD ** -0.5)
        o = _softmax_sink_pv(s, mask, sink, vv, (((2,), (1,)), ((0,), (0,))))
        for i in range(B_GROUP):
            h = g * B_GROUP + i
            o_ref[:, :, h * B_HD:(h + 1) * B_HD] = o[:, i * t_new:(i + 1) * t_new, :]
    nwin_ref[:, 0:w - t_new, :] = win[:, t_new:w, :]
    nwin_ref[:, w - t_new:w, :] = jnp.concatenate(kn_parts + [kvn[:, :, B_KV_HEADS * B_HD:]], axis=-1)


def _swa_sample(p3, win, meta, qnw, knw, sinks, nb):
    n, t_new, _ = p3.shape
    w = win.shape[1]
    full = lambda shape: pl.BlockSpec(shape, lambda i: (0,) * len(shape))
    return pl.pallas_call(
        _swa_sample_kernel,
        grid=(n // nb,),
        in_specs=[pl.BlockSpec((nb, t_new, B_HEADS * B_HD), lambda i: (i, 0, BQ_BLK)),
                  pl.BlockSpec((nb, t_new, BKV_W), lambda i: (i, 0, BKV_BLK)),
                  pl.BlockSpec((nb, w, BKV_W), lambda i: (i, 0, 0)),
                  pl.BlockSpec((nb, N_META, BKV_W), lambda i: (i, 0, 0)),
                  full(qnw.shape), full(knw.shape), full(sinks.shape)],
        out_specs=[pl.BlockSpec((nb, t_new, B_HEADS * B_HD), lambda i: (i, 0, 0)),
                   pl.BlockSpec((nb, w, BKV_W), lambda i: (i, 0, 0))],
        out_shape=[jax.ShapeDtypeStruct((n, t_new, B_HEADS * B_HD), F32),
                   jax.ShapeDtypeStruct((n, w, BKV_W), F32)],
        compiler_params=_cparams(("parallel",)),
        name="swa_sample",
    )(p3, p3, win, meta, qnw, knw, sinks)


def _transpose_rows(x):
    c = x.shape[0]
    if c < LANES:
        x = jnp.concatenate([x, jnp.zeros((LANES - c, LANES), F32)], axis=0)
    return x.T[:, :c]


def _unit_lower_inverse(a, eye, c):
    p = eye - a
    ak = a
    k = 2
    while k < c:
        ak = jnp.dot(ak, ak, precision=HIGHEST, preferred_element_type=F32)
        p = p + jnp.dot(p, ak, precision=HIGHEST, preferred_element_type=F32)
        k *= 2
    return p


def _gdn_kernel(qkv_ref, z_ref, ab_ref, s0_ref, hist_ref, cw_ref, alog_ref, dtb_ref, nw_ref,
                o_ref, s_ref, nh_ref, e_ref, y_ref, *, chunk, n_invalid):
    c = chunk
    t = pl.program_id(1)
    nb = qkv_ref.shape[0]
    halo = GDN_HALO

    @pl.when(t == 0)
    def _():
        e_ref[:, 0:halo, :] = hist_ref[...]
        s_ref[...] = s0_ref[...]

    @pl.when(t > 0)
    def _():
        e_ref[:, 0:halo, :] = e_ref[:, c:c + halo, :]

    x = qkv_ref[...]
    if n_invalid:
        row3 = lax.broadcasted_iota(jnp.int32, (1, c, 1), 1) + t * c
        x = jnp.where(row3 >= n_invalid, x, 0.0)
    e_ref[:, halo:halo + c, :] = x
    nh_ref[...] = e_ref[:, c:c + halo, :]

    ext = e_ref[...]
    off = halo - (A_CONV - 1)
    acc = cw_ref[0:1, :] * ext[:, off:off + c, :]
    for k in range(1, A_CONV):
        acc = acc + cw_ref[k:k + 1, :] * ext[:, off + k:off + k + c, :]
    y_ref[...] = _silu(acc)

    ri = lax.broadcasted_iota(jnp.int32, (c, c), 0)
    ci = lax.broadcasted_iota(jnp.int32, (c, c), 1)
    incl = ri >= ci
    strict = ri > ci
    ltri = incl.astype(F32)
    eye = (ri == ci).astype(F32)
    alog = alog_ref[...]
    dtb = dtb_ref[...]
    nw = nw_ref[...]
    if n_invalid:
        valid = (lax.broadcasted_iota(jnp.int32, (c, 1), 0) + t * c) >= n_invalid

    def seq_body(n, carry):
        ab = ab_ref[n]
        sp = jnp.maximum(ab + dtb, 0.0) + jnp.log1p(jnp.exp(-jnp.abs(ab + dtb)))
        gfull = -jnp.exp(alog) * sp
        bfull = jax.nn.sigmoid(ab)
        if n_invalid:
            gfull = jnp.where(valid, gfull, 0.0)
            bfull = jnp.where(valid, bfull, 0.0)
        gcum = jnp.dot(ltri, gfull, precision=HIGHEST, preferred_element_type=F32)
        gcum_t = _transpose_rows(gcum)
        yn = y_ref[n]
        zn = z_ref[n]
        for h in range(A_HEADS):
            hs = slice(h * A_DK, (h + 1) * A_DK)
            q = yn[:, hs]
            k = yn[:, A_KW + h * A_DK:A_KW + (h + 1) * A_DK]
            v = yn[:, 2 * A_KW + h * A_DK:2 * A_KW + (h + 1) * A_DK]
            q = q * lax.rsqrt(jnp.sum(q * q, axis=-1, keepdims=True) + EPS) * (A_DK ** -0.5)
            k = k * lax.rsqrt(jnp.sum(k * k, axis=-1, keepdims=True) + EPS)
            beta = bfull[:, A_HEADS + h:A_HEADS + h + 1]
            gc = gcum[:, h:h + 1]
            gr = gcum_t[h:h + 1, :]
            glast = gcum[c - 1:c, h:h + 1]
            decay = jnp.where(incl, jnp.exp(jnp.where(incl, gc - gr, 0.0)), 0.0)
            kt = _transpose_rows(k)
            kk = jnp.dot(k, kt, preferred_element_type=F32)
            a = jnp.where(strict, kk * decay * beta, 0.0)
            tinv = _unit_lower_inverse(a, eye, c)
            eg = jnp.exp(gc)
            u = jnp.dot(tinv, v * beta, precision=HIGHEST, preferred_element_type=F32)
            w = jnp.dot(tinv, k * (beta * eg), precision=HIGHEST, preferred_element_type=F32)
            qk = jnp.dot(q, kt, preferred_element_type=F32) * decay
            qd = q * eg
            kdt = kt * jnp.exp(glast - gr)
            s = s_ref[n, h]
            vnew = u - jnp.dot(w, s, preferred_element_type=F32)
            o = jnp.dot(qd, s, preferred_element_type=F32) + jnp.dot(qk, vnew, preferred_element_type=F32)
            s_ref[n, h] = s * jnp.exp(glast) + jnp.dot(kdt, vnew, preferred_element_type=F32)
            o_ref[n, :, hs] = _rms(o, nw) * _silu(zn[:, hs])
        return carry

    lax.fori_loop(0, nb, seq_body, 0)


def _gdn(p3, s0, hist, cw, alog, dtb, nw, nb, chunk, n_invalid):
    n, l, _ = p3.shape
    kern = functools.partial(_gdn_kernel, chunk=chunk, n_invalid=n_invalid)
    full = lambda shape: pl.BlockSpec(shape, lambda i, t: (0,) * len(shape))
    vw = A_HEADS * A_DK
    return pl.pallas_call(
        kern,
        grid=(n // nb, l // chunk),
        in_specs=[pl.BlockSpec((nb, chunk, QKV_W), lambda i, t: (i, t, 0)),
                  pl.BlockSpec((nb, chunk, vw), lambda i, t: (i, t, Z_BLK)),
                  pl.BlockSpec((nb, chunk, LANES), lambda i, t: (i, t, AB_BLK)),
                  pl.BlockSpec((nb, A_HEADS, A_DK, A_DK), lambda i, t: (i, 0, 0, 0)),
                  pl.BlockSpec((nb, GDN_HALO, QKV_W), lambda i, t: (i, 0, 0)),
                  full(cw.shape), full(alog.shape), full(dtb.shape), full(nw.shape)],
        out_specs=[pl.BlockSpec((nb, chunk, vw), lambda i, t: (i, t, 0)),
                   pl.BlockSpec((nb, A_HEADS, A_DK, A_DK), lambda i, t: (i, 0, 0, 0)),
                   pl.BlockSpec((nb, GDN_HALO, QKV_W), lambda i, t: (i, 0, 0))],
        out_shape=[jax.ShapeDtypeStruct((n, l, vw), F32),
                   jax.ShapeDtypeStruct((n, A_HEADS, A_DK, A_DK), F32),
                   jax.ShapeDtypeStruct((n, GDN_HALO, QKV_W), F32)],
        scratch_shapes=[pltpu.VMEM((nb, GDN_HALO + chunk, QKV_W), F32),
                        pltpu.VMEM((nb, chunk, QKV_W), F32)],
        compiler_params=_cparams(("parallel", "arbitrary")),
        name="gdn",
    )(p3, p3, p3, s0, hist, cw, alog, dtb, nw)


def _merge_kernel(oa_ref, ob_ref, oc_ref, gl_ref, x_ref, wb_ref, wo_ref, out_ref):
    d = x_ref.shape[1]
    merged = None
    for r, o_ref in enumerate((oa_ref, ob_ref, oc_ref)):
        pb = jnp.dot(o_ref[...].astype(BF16), wb_ref[r], preferred_element_type=F32)
        term = jax.nn.sigmoid(gl_ref[:, r * d:(r + 1) * d]) * pb
        merged = term if merged is None else merged + term
    out_ref[...] = x_ref[...] + jnp.dot(merged.astype(BF16), wo_ref[...], preferred_element_type=F32)


def _merge(oa, ob, oc, p2, x2, wb, wo, tm):
    t, d = x2.shape
    bw = oa.shape[1]
    row = lambda w, blk: pl.BlockSpec((tm, w), lambda i: (i, blk))
    return pl.pallas_call(
        _merge_kernel,
        grid=(t // tm,),
        in_specs=[row(bw, 0), row(bw, 0), row(bw, 0), row(3 * d, GL_BLK), row(d, 0),
                  pl.BlockSpec(wb.shape, lambda i: (0, 0, 0)),
                  pl.BlockSpec(wo.shape, lambda i: (0, 0))],
        out_specs=row(d, 0),
        out_shape=jax.ShapeDtypeStruct((t, d), F32),
        compiler_params=_cparams(("parallel",)),
        name="merge",
    )(oa, ob, oc, p2, x2, wb, wo)


def _route(logits):
    lane = lax.broadcasted_iota(jnp.int32, logits.shape, 1)
    is_g = (lane >= N_EXPERTS) & (lane < N_EXPERTS + N_GROUPS)
    gl = jnp.where(is_g, logits, NEG)
    gmax = jnp.max(gl, axis=-1, keepdims=True)
    gidx = jnp.min(jnp.where(gl == gmax, lane - N_EXPERTS, LANES), axis=-1, keepdims=True)
    gw = 1.0 / jnp.sum(jnp.where(is_g, jnp.exp(gl - gmax), 0.0), axis=-1, keepdims=True)
    in_grp = (lane < N_EXPERTS) & ((lane // EXP_PER_GROUP) == gidx)
    el = jnp.where(in_grp, logits, NEG)
    v1 = jnp.max(el, axis=-1, keepdims=True)
    i1 = jnp.min(jnp.where(el == v1, lane, LANES), axis=-1, keepdims=True)
    el2 = jnp.where(lane == i1, NEG, el)
    v2 = jnp.max(el2, axis=-1, keepdims=True)
    i2 = jnp.min(jnp.where(el2 == v2, lane, LANES), axis=-1, keepdims=True)
    e21 = jnp.exp(v2 - v1)
    w1 = gw / (1.0 + e21)
    w2 = gw * e21 / (1.0 + e21)
    return jnp.where(lane == i1, w1, 0.0) + jnp.where(lane == i2, w2, 0.0)


def _moe_kernel(x_ref, nw_ref, wr_ref, br_ref, wg_ref, wu_ref, wd_ref, out_ref, h_ref, cmb_ref):
    e = pl.program_id(1)

    @pl.when(e == 0)
    def _():
        x = x_ref[...]
        h = _rms(x, nw_ref[...])
        h_ref[...] = h.astype(BF16)
        logits = jnp.dot(h, wr_ref[...], precision=HIGHEST, preferred_element_type=F32) + br_ref[...]
        cmb_ref[...] = _route(logits)
        out_ref[...] = x

    h = h_ref[...]
    lane = lax.broadcasted_iota(jnp.int32, cmb_ref.shape, 1)
    ce = jnp.sum(jnp.where(lane == e, cmb_ref[...], 0.0), axis=-1, keepdims=True)
    hid = _silu(jnp.dot(h, wg_ref[0], preferred_element_type=F32)) * jnp.dot(h, wu_ref[0], preferred_element_type=F32)
    hid = (hid * ce).astype(BF16)
    out_ref[...] += jnp.dot(hid, wd_ref[0], preferred_element_type=F32)


def _moe(x2, nw, wr, br, wg, wu, wd, tm):
    t, d = x2.shape
    ne, _, de = wg.shape
    return pl.pallas_call(
        _moe_kernel,
        grid=(t // tm, ne),
        in_specs=[pl.BlockSpec((tm, d), lambda i, e: (i, 0)),
                  pl.BlockSpec((1, d), lambda i, e: (0, 0)),
                  pl.BlockSpec(wr.shape, lambda i, e: (0, 0)),
                  pl.BlockSpec(br.shape, lambda i, e: (0, 0)),
                  pl.BlockSpec((1, d, de), lambda i, e: (e, 0, 0)),
                  pl.BlockSpec((1, d, de), lambda i, e: (e, 0, 0)),
                  pl.BlockSpec((1, de, d), lambda i, e: (e, 0, 0))],
        out_specs=pl.BlockSpec((tm, d), lambda i, e: (i, 0)),
        out_shape=jax.ShapeDtypeStruct((t, d), F32),
        scratch_shapes=[pltpu.VMEM((tm, d), BF16), pltpu.VMEM((tm, LANES), F32)],
        compiler_params=_cparams(("parallel", "arbitrary")),
        name="moe",
    )(x2, nw, wr, br, wg, wu, wd)


def _pad_lanes(v, width):
    v = v.reshape(1, -1)
    return jnp.pad(v, ((0, 0), (0, width - v.shape[1])))


def _layer_weights(l, norm1_w, norm2_w, w_in, gdn_conv_w, gdn_a_log, gdn_dt_bias, gdn_norm_w,
                   swa_q_norm_w, swa_k_norm_w, swa_sinks, conv_dw_w, conv_dw_b, conv_ln_w, conv_ln_b,
                   w_branch, w_out, router_group_w, router_group_b, router_expert_w, router_expert_b,
                   moe_w_gate, moe_w_up, moe_w_down):
    d = w_in.shape[1]
    wi = w_in[l]
    o_ab = 4 * A_KW
    o_bq = o_ab + 2 * A_HEADS
    o_cu = o_bq + B_HEADS * B_HD + BKV_W
    o_gl = o_cu + 2 * C_CH
    w_perm = jnp.concatenate(
        [wi[:, :o_ab], wi[:, o_cu:o_gl], wi[:, o_gl:], wi[:, o_bq:o_cu], wi[:, o_ab:o_bq],
         jnp.zeros((d, LANES - 2 * A_HEADS), wi.dtype)], axis=1).astype(BF16)
    assert w_perm.shape[1] == PROJ_W
    wr = jnp.concatenate([router_expert_w[l], router_group_w[l],
                          jnp.zeros((d, LANES - N_EXPERTS - N_GROUPS), F32)], axis=1)
    br = _pad_lanes(jnp.concatenate([router_expert_b[l], router_group_b[l]]), LANES)
    return dict(
        n1=norm1_w[l].reshape(1, d), n2=norm2_w[l].reshape(1, d), w_in=w_perm,
        gdn_cw=gdn_conv_w[l], alog=_pad_lanes(gdn_a_log[l], LANES), dtb=_pad_lanes(gdn_dt_bias[l], LANES),
        gdn_nw=gdn_norm_w[l].reshape(1, -1),
        qnw=swa_q_norm_w[l].reshape(1, -1), knw=swa_k_norm_w[l].reshape(1, -1), sinks=swa_sinks[l].reshape(1, -1),
        dww=conv_dw_w[l], dwb=conv_dw_b[l].reshape(1, -1), lnw=conv_ln_w[l].reshape(1, -1),
        lnb=conv_ln_b[l].reshape(1, -1),
        wb=w_branch[l].astype(BF16), wo=w_out[l].astype(BF16), wr=wr, br=br,
        wg=moe_w_gate[l].astype(BF16), wu=moe_w_up[l].astype(BF16), wd=moe_w_down[l].astype(BF16))


def _tiles(n, l):
    prompt = l > BLOCK
    t = n * l
    if prompt:
        return dict(proj_tm=1536, tok_tm=512, moe_tm=768, conf_nb=1, conf_tm=384, conf_rc=64,
                    gdn_nb=n, gdn_chunk=64, n_invalid=META_PAD)
    return dict(proj_tm=t, tok_tm=512, moe_tm=512, conf_nb=16, conf_tm=l, conf_rc=l,
                gdn_nb=8, gdn_chunk=l, n_invalid=0)


def _layer(x3, lw, gdn_state, gdn_hist, conf_hist, swa_cache):
    n, l, d = x3.shape
    tl = _tiles(n, l)
    x2 = x3.reshape(n * l, d)
    p2 = _inproj(x2, lw['n1'], lw['w_in'], tl['proj_tm'], PROJ_W // 5)
    p3 = p2.reshape(n, l, PROJ_W)
    o_a, s_new, gh_new = _gdn(p3, gdn_state, gdn_hist, lw['gdn_cw'], lw['alog'], lw['dtb'], lw['gdn_nw'],
                              tl['gdn_nb'], tl['gdn_chunk'], tl['n_invalid'])
    v_col = BKV_BLK * BKV_W + B_KV_HEADS * B_HD
    if swa_cache is None:
        o_b, kn = _swa_prompt(p3, lw['qnw'], lw['knw'], lw['sinks'])
        v_raw = p3[:, :, v_col:v_col + B_KV_HEADS * B_HD]
        kv = lambda sl: jnp.concatenate([kn[:, sl], v_raw[:, sl]], axis=-1).reshape(
            n, -1, 2, B_KV_HEADS, B_HD)
        swa_new = (kv(slice(META_PAD, BLOCK)), kv(slice(l - WINDOW, l)))
    else:
        win, meta = swa_cache
        o_b, nwin = _swa_sample(p3, win.reshape(n, win.shape[1], BKV_W), meta.reshape(n, N_META, BKV_W),
                                lw['qnw'], lw['knw'], lw['sinks'], 16)
        swa_new = nwin.reshape(win.shape)
    o_c, ch_new = _conformer(p3, conf_hist, lw['dww'], lw['dwb'], lw['lnw'], lw['lnb'],
                             tl['conf_nb'], tl['conf_tm'], tl['conf_rc'], tl['n_invalid'])
    t = n * l
    x2 = _merge(o_a.reshape(t, -1), o_b.reshape(t, -1), o_c.reshape(t, -1), p2, x2, lw['wb'], lw['wo'],
                tl['tok_tm'])
    x2 = _moe(x2, lw['n2'], lw['wr'], lw['br'], lw['wg'], lw['wu'], lw['wd'], tl['moe_tm'])
    gh_new = gh_new[:, GDN_HALO - (A_CONV - 1):]
    ch_new = ch_new[:, CONF_HALO - (C_CONV - 1):]
    return x2.reshape(n, l, d), s_new, gh_new, swa_new, ch_new


def kernel(x_prompt, x_sample, state_gdn, cache_gdn_conv, cache_swa_kv, cache_meta_kv, cache_conv, meta_tokens, norm1_w, norm2_w, w_in, gdn_conv_w, gdn_a_log, gdn_dt_bias, gdn_norm_w, swa_q_norm_w, swa_k_norm_w, swa_sinks, conv_dw_w, conv_dw_b, conv_ln_w, conv_ln_b, w_branch, w_out, router_group_w, router_group_b, router_expert_w, router_expert_b, moe_w_gate, moe_w_up, moe_w_down):
    dtp = x_prompt.dtype
    n_p, _, d = x_prompt.shape
    n_s = x_sample.shape[0]
    depth = w_in.shape[0]
    xp = jnp.concatenate([jnp.zeros((n_p, META_PAD, d), dtp),
                          jnp.broadcast_to(meta_tokens.astype(dtp)[None], (n_p, N_META, d)),
                          x_prompt], axis=1)
    xs = x_sample
    outs_p, outs_s = [], []
    for l in range(depth):
        lw = _layer_weights(l, norm1_w, norm2_w, w_in, gdn_conv_w, gdn_a_log, gdn_dt_bias, gdn_norm_w,
                            swa_q_norm_w, swa_k_norm_w, swa_sinks, conv_dw_w, conv_dw_b, conv_ln_w, conv_ln_b,
                            w_branch, w_out, router_group_w, router_group_b, router_expert_w, router_expert_b,
                            moe_w_gate, moe_w_up, moe_w_down)
        xp, s_p, gh_p, (mkv_p, wkv_p), ch_p = _layer(
            xp, lw, jnp.zeros((n_p, A_HEADS, A_DK, A_DK), dtp), jnp.zeros((n_p, GDN_HALO, QKV_W), dtp),
            jnp.zeros((n_p, CONF_HALO, C_CH), dtp), None)
        outs_p.append((s_p, gh_p, wkv_p, mkv_p, ch_p))
        gh0 = jnp.pad(cache_gdn_conv[l], ((0, 0), (GDN_HALO - (A_CONV - 1), 0), (0, 0)))
        ch0 = jnp.pad(cache_conv[l], ((0, 0), (CONF_HALO - (C_CONV - 1), 0), (0, 0)))
        xs, s_s, gh_s, wkv_s, ch_s = _layer(xs, lw, state_gdn[l], gh0, ch0, (cache_swa_kv[l], cache_meta_kv[l]))
        outs_s.append((s_s, gh_s, wkv_s, ch_s))
    stack = lambda outs, i: jnp.stack([o[i] for o in outs])
    return (xp[:, BLOCK:], xs,
            stack(outs_p, 0), stack(outs_p, 1), stack(outs_p, 2), stack(outs_p, 3), stack(outs_p, 4),
            stack(outs_s, 0), stack(outs_s, 1), stack(outs_s, 2), stack(outs_s, 3))
```

```python
import functools

import jax
import jax.numpy as jnp
from jax import lax
from jax.experimental import pallas as pl
from jax.experimental.pallas import tpu as pltpu

F32 = jnp.float32
BF16 = jnp.bfloat16
HIGHEST = lax.Precision.HIGHEST
EPS = 1e-6
NEG = -1e30

VMEM_LIMIT_BYTES = 56 * 1024 * 1024
LANES = 128
SUBLANES = 8

PAST_LEN = 16384
N_META = 16
BLOCK = 128
WINDOW = 128
META_PAD = BLOCK - N_META
A_HEADS = 4
A_DK = 128
A_CONV = 4
B_HEADS = 8
B_KV_HEADS = 2
B_GROUP = B_HEADS // B_KV_HEADS
B_HD = 64
C_CH = 512
C_CONV = 31
BRANCH_W = 512
N_GROUPS = 4
EXP_PER_GROUP = 4
N_EXPERTS = N_GROUPS * EXP_PER_GROUP

A_KW = A_HEADS * A_DK
QKV_W = 3 * A_KW
Z_BLK = QKV_W // BRANCH_W
CU_BLK = 2
GL_BLK = 1
BQ_BLK = 12
BKV_W = 2 * B_KV_HEADS * B_HD
BKV_BLK = 26
AB_BLK = 54
PROJ_W = 7040
CONF_HALO = 32
GDN_HALO = SUBLANES


def _rms(x, w):
    return x * lax.rsqrt(jnp.mean(x * x, axis=-1, keepdims=True) + EPS) * w


def _silu(x):
    return x * jax.nn.sigmoid(x)


def _cparams(sem):
    return pltpu.CompilerParams(dimension_semantics=sem, vmem_limit_bytes=VMEM_LIMIT_BYTES)


def _inproj_kernel(x_ref, nw_ref, w_ref, o_ref, h_ref):
    @pl.when(pl.program_id(1) == 0)
    def _():
        h_ref[...] = _rms(x_ref[...], nw_ref[...]).astype(BF16)

    o_ref[...] = jnp.dot(h_ref[...], w_ref[...], preferred_element_type=F32)


def _inproj(x2, nw, w, tm, tn):
    t, d = x2.shape
    wd = w.shape[1]
    return pl.pallas_call(
        _inproj_kernel,
        grid=(t // tm, wd // tn),
        in_specs=[pl.BlockSpec((tm, d), lambda i, j: (i, 0)),
                  pl.BlockSpec((1, d), lambda i, j: (0, 0)),
                  pl.BlockSpec((d, tn), lambda i, j: (0, j))],
        out_specs=pl.BlockSpec((tm, tn), lambda i, j: (i, j)),
        out_shape=jax.ShapeDtypeStruct((t, wd), F32),
        scratch_shapes=[pltpu.VMEM((tm, d), BF16)],
        compiler_params=_cparams(("parallel", "arbitrary")),
        name="inproj",
    )(x2, nw, w)


def _conf_kernel(cu_ref, hist_ref, dww_ref, dwb_ref, lnw_ref, lnb_ref, y_ref, nh_ref, e_ref,
                 *, tm, rc, n_invalid):
    t = pl.program_id(1)
    nb = cu_ref.shape[0]
    halo = CONF_HALO

    @pl.when(t == 0)
    def _():
        e_ref[:, 0:halo, :] = hist_ref[...]

    @pl.when(t > 0)
    def _():
        e_ref[:, 0:halo, :] = e_ref[:, tm:tm + halo, :]

    cu = cu_ref[...]
    u = cu[:, :, :C_CH] * jax.nn.sigmoid(cu[:, :, C_CH:])
    if n_invalid:
        row = lax.broadcasted_iota(jnp.int32, (1, tm, 1), 1) + t * tm
        u = jnp.where(row >= n_invalid, u, 0.0)
    e_ref[:, halo:halo + tm, :] = u
    nh_ref[...] = e_ref[:, tm:tm + halo, :]

    off = halo - (C_CONV - 1)
    for c in range(C_CH // LANES):
        cs = slice(c * LANES, (c + 1) * LANES)

        def rbody(r, carry, cs=cs):
            r0 = pl.multiple_of(r * rc, rc)
            blk = e_ref[:, pl.ds(r0, rc + halo), cs]
            acc = jnp.zeros((nb, rc, LANES), F32)
            for k in range(C_CONV):
                acc = acc + dww_ref[k:k + 1, cs] * blk[:, off + k:off + k + rc, :]
            y_ref[:, pl.ds(r0, rc), cs] = acc + dwb_ref[:, cs]
            return carry

        lax.fori_loop(0, tm // rc, rbody, 0)

    y = y_ref[...]
    mu = jnp.mean(y, axis=-1, keepdims=True)
    var = jnp.mean(jnp.square(y - mu), axis=-1, keepdims=True)
    yn = (y - mu) * lax.rsqrt(var + EPS) * lnw_ref[...] + lnb_ref[...]
    y_ref[...] = _silu(yn)


def _conformer(p3, hist, dww, dwb, lnw, lnb, nb, tm, rc, n_invalid):
    n, l, _ = p3.shape
    kern = functools.partial(_conf_kernel, tm=tm, rc=rc, n_invalid=n_invalid)
    full = lambda shape: pl.BlockSpec(shape, lambda i, t: (0,) * len(shape))
    return pl.pallas_call(
        kern,
        grid=(n // nb, l // tm),
        in_specs=[pl.BlockSpec((nb, tm, 2 * C_CH), lambda i, t: (i, t, CU_BLK)),
                  pl.BlockSpec((nb, CONF_HALO, C_CH), lambda i, t: (i, 0, 0)),
                  full(dww.shape), full(dwb.shape), full(lnw.shape), full(lnb.shape)],
        out_specs=[pl.BlockSpec((nb, tm, C_CH), lambda i, t: (i, t, 0)),
                   pl.BlockSpec((nb, CONF_HALO, C_CH), lambda i, t: (i, 0, 0))],
        out_shape=[jax.ShapeDtypeStruct((n, l, C_CH), F32),
                   jax.ShapeDtypeStruct((n, CONF_HALO, C_CH), F32)],
        scratch_shapes=[pltpu.VMEM((nb, CONF_HALO + tm, C_CH), F32)],
        compiler_params=_cparams(("parallel", "arbitrary")),
        name="conformer",
    )(p3, hist, dww, dwb, lnw, lnb)


def _softmax_sink_pv(s, mask, sink, v, dims):
    s = jnp.where(mask, s, NEG)
    m = jnp.maximum(jnp.max(s, axis=-1, keepdims=True), sink)
    p = jnp.exp(s - m)
    den = jnp.sum(p, axis=-1, keepdims=True) + jnp.exp(sink - m)
    return lax.dot_general(p, v, dims, preferred_element_type=F32) / den


def _swa_prompt_kernel(q_ref, kv_ref, kvp_ref, kvm_ref, qnw_ref, knw_ref, sink_ref, o_ref, kn_ref):
    j = pl.program_id(1)
    q = q_ref[...]
    kv = kv_ref[...]
    kvp = kvp_ref[...]
    kvm = kvm_ref[META_PAD:BLOCK, :]
    qnw = qnw_ref[...]
    knw = knw_ref[...]
    nk = N_META + 2 * BLOCK
    rows = B_GROUP * BLOCK
    r = lax.broadcasted_iota(jnp.int32, (rows, nk), 0) % BLOCK
    c = lax.broadcasted_iota(jnp.int32, (rows, nk), 1)
    qi = j * BLOCK + r
    meta_ok = (c < N_META) & (META_PAD + c <= qi)
    cp = c - N_META
    prev_ok = (cp >= 0) & (cp < BLOCK) & (cp > r) & (j >= 2)
    co = cp - BLOCK
    own_ok = (co >= 0) & (co <= r) & (j >= 1)
    mask = meta_ok | prev_ok | own_ok
    hrow = lax.broadcasted_iota(jnp.int32, (rows, 1), 0) // BLOCK
    kn_parts = []
    for g in range(B_KV_HEADS):
        ks = slice(g * B_HD, (g + 1) * B_HD)
        vs = slice((B_KV_HEADS + g) * B_HD, (B_KV_HEADS + g + 1) * B_HD)
        k_own = _rms(kv[:, ks], knw)
        kn_parts.append(k_own)
        kk = jnp.concatenate([_rms(kvm[:, ks], knw), _rms(kvp[:, ks], knw), k_own], axis=0)
        vv = jnp.concatenate([kvm[:, vs], kvp[:, vs], kv[:, vs]], axis=0)
        qq = jnp.concatenate(
            [_rms(q[:, (g * B_GROUP + i) * B_HD:(g * B_GROUP + i + 1) * B_HD], qnw) for i in range(B_GROUP)],
            axis=0)
        sink = jnp.zeros((rows, 1), F32)
        for i in range(B_GROUP):
            h = g * B_GROUP + i
            sink = jnp.where(hrow == i, sink_ref[0:1, h:h + 1], sink)
        s = lax.dot_general(qq, kk, (((1,), (1,)), ((), ())), preferred_element_type=F32) * (B_HD ** -0.5)
        o = _softmax_sink_pv(s, mask, sink, vv, (((1,), (0,)), ((), ())))
        for i in range(B_GROUP):
            h = g * B_GROUP + i
            o_ref[:, h * B_HD:(h + 1) * B_HD] = o[i * BLOCK:(i + 1) * BLOCK, :]
    kn_ref[...] = jnp.concatenate(kn_parts, axis=-1)


def _swa_prompt(p3, qnw, knw, sinks):
    n, l, _ = p3.shape
    full = lambda shape: pl.BlockSpec(shape, lambda i, j: (0,) * len(shape))
    return pl.pallas_call(
        _swa_prompt_kernel,
        grid=(n, l // BLOCK),
        in_specs=[pl.BlockSpec((None, BLOCK, B_HEADS * B_HD), lambda i, j: (i, j, BQ_BLK)),
                  pl.BlockSpec((None, BLOCK, BKV_W), lambda i, j: (i, j, BKV_BLK)),
                  pl.BlockSpec((None, BLOCK, BKV_W), lambda i, j: (i, jnp.maximum(j - 1, 0), BKV_BLK)),
                  pl.BlockSpec((None, BLOCK, BKV_W), lambda i, j: (i, 0, BKV_BLK)),
                  full(qnw.shape), full(knw.shape), full(sinks.shape)],
        out_specs=[pl.BlockSpec((None, BLOCK, B_HEADS * B_HD), lambda i, j: (i, j, 0)),
                   pl.BlockSpec((None, BLOCK, B_KV_HEADS * B_HD), lambda i, j: (i, j, 0))],
        out_shape=[jax.ShapeDtypeStruct((n, l, B_HEADS * B_HD), F32),
                   jax.ShapeDtypeStruct((n, l, B_KV_HEADS * B_HD), F32)],
        compiler_params=_cparams(("parallel", "arbitrary")),
        name="swa_prompt",
    )(p3, p3, p3, p3, qnw, knw, sinks)


def _swa_sample_kernel(q_ref, kvn_ref, win_ref, meta_ref, qnw_ref, knw_ref, sink_ref, o_ref, nwin_ref):
    q = q_ref[...]
    kvn = kvn_ref[...]
    win = win_ref[...]
    meta = meta_ref[...]
    qnw = qnw_ref[...]
    knw = knw_ref[...]
    t_new = q.shape[1]
    w = win.shape[1]
    nk = N_META + w + t_new
    rows = B_GROUP * t_new
    tq = lax.broadcasted_iota(jnp.int32, (1, rows, nk), 1) % t_new
    c = lax.broadcasted_iota(jnp.int32, (1, rows, nk), 2)
    cw = c - N_META
    win_ok = (cw >= 0) & (cw < w) & (cw - w > tq - WINDOW) & (cw + (PAST_LEN - w) >= N_META)
    cn = cw - w
    new_ok = (cn >= 0) & (cn <= tq) & (cn > tq - WINDOW)
    mask = (c < N_META) | win_ok | new_ok
    hrow = lax.broadcasted_iota(jnp.int32, (1, rows, 1), 1) // t_new
    kn_parts = []
    for g in range(B_KV_HEADS):
        ks = slice(g * B_HD, (g + 1) * B_HD)
        vs = slice((B_KV_HEADS + g) * B_HD, (B_KV_HEADS + g + 1) * B_HD)
        k_new = _rms(kvn[:, :, ks], knw)
        kn_parts.append(k_new)
        kk = jnp.concatenate([meta[:, :, ks], win[:, :, ks], k_new], axis=1)
        vv = jnp.concatenate([meta[:, :, vs], win[:, :, vs], kvn[:, :, vs]], axis=1)
        qq = jnp.concatenate(
            [_rms(q[:, :, (g * B_GROUP + i) * B_HD:(g * B_GROUP + i + 1) * B_HD], qnw) for i in range(B_GROUP)],
            axis=1)
        sink = jnp.zeros((1, rows, 1), F32)
        for i in range(B_GROUP):
            h = g * B_GROUP + i
            sink = jnp.where(hrow == i, sink_ref[0:1, h:h + 1], sink)
        s = jnp.einsum('bqd,bkd->bqk', qq, kk, preferred_element_type=F32) * (B_HD ** -0.5)
        o = _softmax_sink_pv(s, mask, sink, vv, (((2,), (1,)), ((0,), (0,))))
        for i in range(B_GROUP):
            h = g * B_GROUP + i
            o_ref[:, :, h * B_HD:(h + 1) * B_HD] = o[:, i * t_new:(i + 1) * t_new, :]
    nwin_ref[:, 0:w - t_new, :] = win[:, t_new:w, :]
    nwin_ref[:, w - t_new:w, :] = jnp.concatenate(kn_parts + [kvn[:, :, B_KV_HEADS * B_HD:]], axis=-1)


def _swa_sample(p3, win, meta, qnw, knw, sinks, nb):
    n, t_new, _ = p3.shape
    w = win.shape[1]
    full = lambda shape: pl.BlockSpec(shape, lambda i: (0,) * len(shape))
    return pl.pallas_call(
        _swa_sample_kernel,
        grid=(n // nb,),
        in_specs=[pl.BlockSpec((nb, t_new, B_HEADS * B_HD), lambda i: (i, 0, BQ_BLK)),
                  pl.BlockSpec((nb, t_new, BKV_W), lambda i: (i, 0, BKV_BLK)),
                  pl.BlockSpec((nb, w, BKV_W), lambda i: (i, 0, 0)),
                  pl.BlockSpec((nb, N_META, BKV_W), lambda i: (i, 0, 0)),
                  full(qnw.shape), full(knw.shape), full(sinks.shape)],
        out_specs=[pl.BlockSpec((nb, t_new, B_HEADS * B_HD), lambda i: (i, 0, 0)),
                   pl.BlockSpec((nb, w, BKV_W), lambda i: (i, 0, 0))],
        out_shape=[jax.ShapeDtypeStruct((n, t_new, B_HEADS * B_HD), F32),
                   jax.ShapeDtypeStruct((n, w, BKV_W), F32)],
        compiler_params=_cparams(("parallel",)),
        name="swa_sample",
    )(p3, p3, win, meta, qnw, knw, sinks)


GDN_ROWS = 64
GDN_R = A_HEADS * GDN_ROWS
GDN_PAIRS = A_HEADS // 2
PAIR_W = 2 * A_DK


def _mm(a, b):
    return jnp.dot(a.astype(BF16), b.astype(BF16), preferred_element_type=F32)


def _block_diag2(x):
    z = jnp.zeros((x.shape[0], A_DK), x.dtype)
    return jnp.concatenate([jnp.concatenate([x[:, :A_DK], z], axis=1),
                            jnp.concatenate([z, x[:, A_DK:]], axis=1)], axis=0)


def _gdn_prep_kernel(x_ref, halo_ref, ab_ref, cw_ref, alog_ref, dtb_ref, u_ref, wq_ref, qkk_ref, gl_ref,
                     *, sb, cpb, n_invalid, seq_halo):
    t = pl.program_id(1)
    ns = x_ref.shape[0]
    rows = x_ref.shape[1] // cpb
    c = GDN_ROWS
    r = GDN_R
    halo = halo_ref[...]
    if seq_halo:
        hrow = lax.broadcasted_iota(jnp.int32, (1, GDN_HALO, 1), 1) + (t * cpb * rows - GDN_HALO)
        halo = jnp.where(hrow >= n_invalid, halo, 0.0)
    cw = cw_ref[...]
    alog = alog_ref[...]
    dtb = dtb_ref[...]
    ri = lax.broadcasted_iota(jnp.int32, (r, r), 0)
    ci = lax.broadcasted_iota(jnp.int32, (r, r), 1)
    same = (ri // sb) == (ci // sb)
    incl = same & (ri >= ci)
    strict = same & (ri > ci)
    eye = (ri == ci).astype(F32)
    lmat = jnp.concatenate([incl.astype(F32), same.astype(F32)], axis=0)
    sel = (lax.broadcasted_iota(jnp.int32, (SUBLANES, c), 1)
           == (lax.broadcasted_iota(jnp.int32, (SUBLANES, c), 0) * SUBLANES // sb) * sb).astype(F32)
    off = GDN_HALO - (A_CONV - 1)
    for cc in range(cpb):
        x = x_ref[:, cc * rows:(cc + 1) * rows, :]
        ab = ab_ref[:, cc * rows:(cc + 1) * rows, :].reshape(c, LANES)
        if n_invalid:
            row0 = (t * cpb + cc) * rows
            xrow = lax.broadcasted_iota(jnp.int32, (1, rows, 1), 1) + row0
            x = jnp.where(xrow >= n_invalid, x, 0.0)
        ext = jnp.concatenate([halo, x], axis=1)
        acc = cw[0:1, :] * ext[:, off:off + rows, :]
        for k in range(1, A_CONV):
            acc = acc + cw[k:k + 1, :] * ext[:, off + k:off + k + rows, :]
        y = _silu(acc).reshape(c, QKV_W)
        halo = x[:, rows - GDN_HALO:rows, :]

        sp = jnp.maximum(ab + dtb, 0.0) + jnp.log1p(jnp.exp(-jnp.abs(ab + dtb)))
        gfull = -jnp.exp(alog) * sp
        bfull = jax.nn.sigmoid(ab)
        if n_invalid:
            valid = (lax.broadcasted_iota(jnp.int32, (c, 1), 0) + row0) >= n_invalid
            gfull = jnp.where(valid, gfull, 0.0)
            bfull = jnp.where(valid, bfull, 0.0)
        gcol = jnp.concatenate([gfull[:, h:h + 1] for h in range(A_HEADS)], axis=0)
        bcol = jnp.concatenate([bfull[:, A_HEADS + h:A_HEADS + h + 1] for h in range(A_HEADS)], axis=0)
        gg = jnp.dot(lmat, jnp.broadcast_to(gcol, (r, LANES)), precision=HIGHEST, preferred_element_type=F32)
        gcum = gg[:r]
        gtot = gg[r:]
        gc = gcum[:, 0:1]
        grow = gcum.T[0:1, :]
        drow = (gtot - gcum).T[0:1, :]
        decay = jnp.where(incl, jnp.exp(jnp.where(incl, gc - grow, 0.0)), 0.0)

        qs, ks, vs = [], [], []
        for h in range(A_HEADS):
            q = y[:, h * A_DK:(h + 1) * A_DK]
            k = y[:, A_KW + h * A_DK:A_KW + (h + 1) * A_DK]
            qs.append(q * lax.rsqrt(jnp.sum(q * q, axis=-1, keepdims=True) + EPS) * (A_DK ** -0.5))
            ks.append(k * lax.rsqrt(jnp.sum(k * k, axis=-1, keepdims=True) + EPS))
            vs.append(y[:, 2 * A_KW + h * A_DK:2 * A_KW + (h + 1) * A_DK])
        qr = jnp.concatenate(qs, axis=0)
        kr = jnp.concatenate(ks, axis=0)
        vr = jnp.concatenate(vs, axis=0)
        krt = kr.T
        a = jnp.where(strict, _mm(kr, krt) * decay * bcol, 0.0)
        tinv = eye - a
        ak = a
        kpow = 2
        while kpow < sb:
            ak = _mm(ak, ak)
            tinv = tinv + _mm(tinv, ak)
            kpow *= 2
        eg = jnp.exp(gc)
        ur = _mm(tinv, vr * bcol)
        wr = _mm(tinv, kr * (bcol * eg))
        qk = _mm(qr, krt) * decay
        qd = qr * eg
        kdt = krt * jnp.exp(drow)
        for p in range(GDN_PAIRS):
            r0 = slice(2 * p * c, (2 * p + 1) * c)
            r1 = slice((2 * p + 1) * c, (2 * p + 2) * c)
            pl_ = slice(2 * p * c, (2 * p + 2) * c)
            u_ref[cc, p] = jnp.concatenate([ur[r0], ur[r1]], axis=1)
            wq = jnp.concatenate([jnp.concatenate([wr[r0], wr[r1]], axis=1),
                                  jnp.concatenate([qd[r0], qd[r1]], axis=1)], axis=0)
            wq_ref[cc, p] = wq.astype(wq_ref.dtype)
            qkk = jnp.concatenate([qk[r0, pl_] + qk[r1, pl_], kdt[:, pl_]], axis=0)
            qkk_ref[cc, p] = qkk.astype(qkk_ref.dtype)
            gl = jnp.concatenate(
                [jnp.dot(sel, gtot[r0], precision=HIGHEST, preferred_element_type=F32),
                 jnp.dot(sel, gtot[r1], precision=HIGHEST, preferred_element_type=F32)], axis=1)
            gl_ref[cc, p] = jnp.exp(gl)


def _gdn_prep(x_arr, halo_arr, halo_spec, cw, alog, dtb, ns, rows, cpb, sb, n_invalid, op_dtype):
    n, l, _ = x_arr.shape
    grid = (n // ns, l // (cpb * rows))
    units = grid[0] * grid[1] * cpb
    kern = functools.partial(_gdn_prep_kernel, sb=sb, cpb=cpb, n_invalid=n_invalid,
                             seq_halo=halo_arr is x_arr)
    full = lambda shape: pl.BlockSpec(shape, lambda i, t: (0,) * len(shape))
    steps = grid[1]
    out_map = lambda i, t: (i * steps + t, 0, 0, 0)
    c = GDN_ROWS
    return pl.pallas_call(
        kern,
        grid=grid,
        in_specs=[pl.BlockSpec((ns, cpb * rows, QKV_W), lambda i, t: (i, t, 0)),
                  halo_spec,
                  pl.BlockSpec((ns, cpb * rows, LANES), lambda i, t: (i, t, AB_BLK)),
                  full(cw.shape), full(alog.shape), full(dtb.shape)],
        out_specs=[pl.BlockSpec((cpb, GDN_PAIRS, c, PAIR_W), out_map),
                   pl.BlockSpec((cpb, GDN_PAIRS, 2 * c, PAIR_W), out_map),
                   pl.BlockSpec((cpb, GDN_PAIRS, c + A_DK, 2 * c), out_map),
                   pl.BlockSpec((cpb, GDN_PAIRS, SUBLANES, PAIR_W), out_map)],
        out_shape=[jax.ShapeDtypeStruct((units, GDN_PAIRS, c, PAIR_W), F32),
                   jax.ShapeDtypeStruct((units, GDN_PAIRS, 2 * c, PAIR_W), op_dtype),
                   jax.ShapeDtypeStruct((units, GDN_PAIRS, c + A_DK, 2 * c), op_dtype),
                   jax.ShapeDtypeStruct((units, GDN_PAIRS, SUBLANES, PAIR_W), F32)],
        compiler_params=_cparams(("parallel", "arbitrary")),
        name="gdn_prep",
    )(x_arr, halo_arr, x_arr, cw, alog, dtb)


def _gdn_out(o_pair, z_ref, o_ref, nw, idx, p):
    for i in range(2):
        h = 2 * p + i
        hs = slice(h * A_DK, (h + 1) * A_DK)
        o_ref[idx + (hs,)] = _rms(o_pair[..., i * A_DK:(i + 1) * A_DK], nw) * _silu(z_ref[idx + (hs,)])


def _gdn_scan_kernel(u_ref, wq_ref, qkk_ref, gl_ref, z_ref, s0_ref, nw_ref, o_ref, sout_ref, s_ref):
    t = pl.program_id(0)
    n_seq = u_ref.shape[0]
    c = GDN_ROWS
    nw = nw_ref[...]

    @pl.when(t == 0)
    def _():
        for n in range(n_seq):
            for p in range(GDN_PAIRS):
                s_ref[n, p] = jnp.concatenate([s0_ref[n, 2 * p], s0_ref[n, 2 * p + 1]], axis=1)

    for n in range(n_seq):
        for p in range(GDN_PAIRS):
            s = s_ref[n, p]
            r1 = jnp.dot(wq_ref[n, 0, p], _block_diag2(s.astype(BF16)), preferred_element_type=F32)
            vnew = u_ref[n, 0, p] - r1[:c]
            r2 = jnp.dot(qkk_ref[n, 0, p], _block_diag2(vnew.astype(BF16)), preferred_element_type=F32)
            s_ref[n, p] = s * gl_ref[n, 0, p][0:1, :] + r2[c:]
            _gdn_out(r1[c:] + r2[:c], z_ref, o_ref, nw, (n, slice(None)), p)

    @pl.when(t == pl.num_programs(0) - 1)
    def _():
        for n in range(n_seq):
            for p in range(GDN_PAIRS):
                s = s_ref[n, p]
                sout_ref[n, 2 * p] = s[:, :A_DK]
                sout_ref[n, 2 * p + 1] = s[:, A_DK:]


def _gdn_scan(u, wq, qkk, gl, p3, s0, nw):
    n, l, _ = p3.shape
    c = GDN_ROWS
    steps = l // c
    v5 = lambda a: a.reshape((n, steps) + a.shape[1:])
    u, wq, qkk, gl = v5(u), v5(wq), v5(qkk), v5(gl)
    unit = lambda a: pl.BlockSpec((n, 1) + a.shape[2:], lambda t: (0, t, 0, 0, 0))
    vw = A_HEADS * A_DK
    return pl.pallas_call(
        _gdn_scan_kernel,
        grid=(steps,),
        in_specs=[unit(u), unit(wq), unit(qkk), unit(gl),
                  pl.BlockSpec((n, c, vw), lambda t: (0, t, Z_BLK)),
                  pl.BlockSpec(s0.shape, lambda t: (0, 0, 0, 0)),
                  pl.BlockSpec(nw.shape, lambda t: (0, 0))],
        out_specs=[pl.BlockSpec((n, c, vw), lambda t: (0, t, 0)),
                   pl.BlockSpec(s0.shape, lambda t: (0, 0, 0, 0))],
        out_shape=[jax.ShapeDtypeStruct((n, l, vw), F32),
                   jax.ShapeDtypeStruct(s0.shape, F32)],
        scratch_shapes=[pltpu.VMEM((n, GDN_PAIRS, A_DK, PAIR_W), F32)],
        compiler_params=_cparams(("arbitrary",)),
        name="gdn_scan",
    )(u, wq, qkk, gl, p3, s0, nw)


def _gdn_step_kernel(u_ref, wq_ref, qkk_ref, gl_ref, z_ref, s0_ref, nw_ref, o_ref, sout_ref):
    ns, rows, _ = z_ref.shape
    c = GDN_ROWS
    nw = nw_ref[...]
    lane_seq = (lax.broadcasted_iota(jnp.int32, (A_DK, 2 * c), 1) % c) // rows
    for p in range(GDN_PAIRS):
        wq = wq_ref[0, p]
        qkk = qkk_ref[0, p]
        gl = gl_ref[0, p]
        ws, qs, states = [], [], []
        for s_i in range(ns):
            sl = slice(s_i * rows, (s_i + 1) * rows)
            s = jnp.concatenate([s0_ref[s_i, 2 * p], s0_ref[s_i, 2 * p + 1]], axis=1)
            states.append(s)
            lhs = jnp.concatenate([wq[sl], wq[c + s_i * rows:c + (s_i + 1) * rows]], axis=0)
            r1 = _mm(lhs, _block_diag2(s.astype(BF16)))
            ws.append(r1[:rows])
            qs.append(r1[rows:])
        vnew = u_ref[0, p] - jnp.concatenate(ws, axis=0)
        bdv = _block_diag2(vnew.astype(BF16))
        o = jnp.concatenate(qs, axis=0) + _mm(qkk[:c], bdv)
        _gdn_out(o.reshape(ns, rows, PAIR_W), z_ref, o_ref, nw, (slice(None), slice(None)), p)
        kdt = qkk[c:]
        lhs = jnp.concatenate([jnp.where(lane_seq == s_i, kdt, 0.0) for s_i in range(ns)], axis=0)
        upd = _mm(lhs, bdv)
        for s_i in range(ns):
            s_new = states[s_i] * gl[s_i:s_i + 1, :] + upd[s_i * A_DK:(s_i + 1) * A_DK]
            sout_ref[s_i, 2 * p] = s_new[:, :A_DK]
            sout_ref[s_i, 2 * p + 1] = s_new[:, A_DK:]


def _gdn_step(u, wq, qkk, gl, p3, s0, nw, ns):
    n, rows, _ = p3.shape
    unit = lambda a: pl.BlockSpec((1,) + a.shape[1:], lambda i: (i, 0, 0, 0))
    vw = A_HEADS * A_DK
    sblk = (ns,) + s0.shape[1:]
    return pl.pallas_call(
        _gdn_step_kernel,
        grid=(n // ns,),
        in_specs=[unit(u), unit(wq), unit(qkk), unit(gl),
                  pl.BlockSpec((ns, rows, vw), lambda i: (i, 0, Z_BLK)),
                  pl.BlockSpec(sblk, lambda i: (i, 0, 0, 0)),
                  pl.BlockSpec(nw.shape, lambda i: (0, 0))],
        out_specs=[pl.BlockSpec((ns, rows, vw), lambda i: (i, 0, 0)),
                   pl.BlockSpec(sblk, lambda i: (i, 0, 0, 0))],
        out_shape=[jax.ShapeDtypeStruct((n, rows, vw), F32),
                   jax.ShapeDtypeStruct(s0.shape, F32)],
        compiler_params=_cparams(("parallel",)),
        name="gdn_step",
    )(u, wq, qkk, gl, p3, s0, nw)


def _gdn(p3, s0, hist, cw, alog, dtb, nw, cpb, n_invalid):
    n, l, _ = p3.shape
    c = GDN_ROWS
    if hist is None:
        steps8 = cpb * c // GDN_HALO
        halo_spec = pl.BlockSpec((1, GDN_HALO, QKV_W), lambda i, t: (i, jnp.maximum(t * steps8 - 1, 0), 0))
        u, wq, qkk, gl = _gdn_prep(p3, p3, halo_spec, cw, alog, dtb, 1, c, cpb, c, n_invalid, BF16)
        return _gdn_scan(u, wq, qkk, gl, p3, s0, nw)
    ns = c // l
    halo_spec = pl.BlockSpec((ns, GDN_HALO, QKV_W), lambda i, t: (i, 0, 0))
    u, wq, qkk, gl = _gdn_prep(p3, hist, halo_spec, cw, alog, dtb, ns, l, 1, l, n_invalid, F32)
    return _gdn_step(u, wq, qkk, gl, p3, s0, nw, ns)


def _merge_kernel(oa_ref, ob_ref, oc_ref, gl_ref, x_ref, wb_ref, wo_ref, out_ref):
    d = x_ref.shape[1]
    merged = None
    for r, o_ref in enumerate((oa_ref, ob_ref, oc_ref)):
        pb = jnp.dot(o_ref[...].astype(BF16), wb_ref[r], preferred_element_type=F32)
        term = jax.nn.sigmoid(gl_ref[:, r * d:(r + 1) * d]) * pb
        merged = term if merged is None else merged + term
    out_ref[...] = x_ref[...] + jnp.dot(merged.astype(BF16), wo_ref[...], preferred_element_type=F32)


def _merge(oa, ob, oc, p2, x2, wb, wo, tm):
    t, d = x2.shape
    bw = oa.shape[1]
    row = lambda w, blk: pl.BlockSpec((tm, w), lambda i: (i, blk))
    return pl.pallas_call(
        _merge_kernel,
        grid=(t // tm,),
        in_specs=[row(bw, 0), row(bw, 0), row(bw, 0), row(3 * d, GL_BLK), row(d, 0),
                  pl.BlockSpec(wb.shape, lambda i: (0, 0, 0)),
                  pl.BlockSpec(wo.shape, lambda i: (0, 0))],
        out_specs=row(d, 0),
        out_shape=jax.ShapeDtypeStruct((t, d), F32),
        compiler_params=_cparams(("parallel",)),
        name="merge",
    )(oa, ob, oc, p2, x2, wb, wo)


def _route(logits):
    lane = lax.broadcasted_iota(jnp.int32, logits.shape, 1)
    is_g = (lane >= N_EXPERTS) & (lane < N_EXPERTS + N_GROUPS)
    gl = jnp.where(is_g, logits, NEG)
    gmax = jnp.max(gl, axis=-1, keepdims=True)
    gidx = jnp.min(jnp.where(gl == gmax, lane - N_EXPERTS, LANES), axis=-1, keepdims=True)
    gw = 1.0 / jnp.sum(jnp.where(is_g, jnp.exp(gl - gmax), 0.0), axis=-1, keepdims=True)
    in_grp = (lane < N_EXPERTS) & ((lane // EXP_PER_GROUP) == gidx)
    el = jnp.where(in_grp, logits, NEG)
    v1 = jnp.max(el, axis=-1, keepdims=True)
    i1 = jnp.min(jnp.where(el == v1, lane, LANES), axis=-1, keepdims=True)
    el2 = jnp.where(lane == i1, NEG, el)
    v2 = jnp.max(el2, axis=-1, keepdims=True)
    i2 = jnp.min(jnp.where(el2 == v2, lane, LANES), axis=-1, keepdims=True)
    e21 = jnp.exp(v2 - v1)
    w1 = gw / (1.0 + e21)
    w2 = gw * e21 / (1.0 + e21)
    return jnp.where(lane == i1, w1, 0.0) + jnp.where(lane == i2, w2, 0.0)


def _moe_kernel(x_ref, nw_ref, wr_ref, br_ref, wg_ref, wu_ref, wd_ref, out_ref, h_ref, cmb_ref):
    e = pl.program_id(1)

    @pl.when(e == 0)
    def _():
        x = x_ref[...]
        h = _rms(x, nw_ref[...])
        h_ref[...] = h.astype(BF16)
        logits = jnp.dot(h, wr_ref[...], precision=HIGHEST, preferred_element_type=F32) + br_ref[...]
        cmb_ref[...] = _route(logits)
        out_ref[...] = x

    h = h_ref[...]
    lane = lax.broadcasted_iota(jnp.int32, cmb_ref.shape, 1)
    ce = jnp.sum(jnp.where(lane == e, cmb_ref[...], 0.0), axis=-1, keepdims=True)
    hid = _silu(jnp.dot(h, wg_ref[0], preferred_element_type=F32)) * jnp.dot(h, wu_ref[0], preferred_element_type=F32)
    hid = (hid * ce).astype(BF16)
    out_ref[...] += jnp.dot(hid, wd_ref[0], preferred_element_type=F32)


def _moe(x2, nw, wr, br, wg, wu, wd, tm):
    t, d = x2.shape
    ne, _, de = wg.shape
    return pl.pallas_call(
        _moe_kernel,
        grid=(t // tm, ne),
        in_specs=[pl.BlockSpec((tm, d), lambda i, e: (i, 0)),
                  pl.BlockSpec((1, d), lambda i, e: (0, 0)),
                  pl.BlockSpec(wr.shape, lambda i, e: (0, 0)),
                  pl.BlockSpec(br.shape, lambda i, e: (0, 0)),
                  pl.BlockSpec((1, d, de), lambda i, e: (e, 0, 0)),
                  pl.BlockSpec((1, d, de), lambda i, e: (e, 0, 0)),
                  pl.BlockSpec((1, de, d), lambda i, e: (e, 0, 0))],
        out_specs=pl.BlockSpec((tm, d), lambda i, e: (i, 0)),
        out_shape=jax.ShapeDtypeStruct((t, d), F32),
        scratch_shapes=[pltpu.VMEM((tm, d), BF16), pltpu.VMEM((tm, LANES), F32)],
        compiler_params=_cparams(("parallel", "arbitrary")),
        name="moe",
    )(x2, nw, wr, br, wg, wu, wd)


def _pad_lanes(v, width):
    v = v.reshape(1, -1)
    return jnp.pad(v, ((0, 0), (0, width - v.shape[1])))


def _layer_weights(l, norm1_w, norm2_w, w_in, gdn_conv_w, gdn_a_log, gdn_dt_bias, gdn_norm_w,
                   swa_q_norm_w, swa_k_norm_w, swa_sinks, conv_dw_w, conv_dw_b, conv_ln_w, conv_ln_b,
                   w_branch, w_out, router_group_w, router_group_b, router_expert_w, router_expert_b,
                   moe_w_gate, moe_w_up, moe_w_down):
    d = w_in.shape[1]
    wi = w_in[l]
    o_ab = 4 * A_KW
    o_bq = o_ab + 2 * A_HEADS
    o_cu = o_bq + B_HEADS * B_HD + BKV_W
    o_gl = o_cu + 2 * C_CH
    w_perm = jnp.concatenate(
        [wi[:, :o_ab], wi[:, o_cu:o_gl], wi[:, o_gl:], wi[:, o_bq:o_cu], wi[:, o_ab:o_bq],
         jnp.zeros((d, LANES - 2 * A_HEADS), wi.dtype)], axis=1).astype(BF16)
    assert w_perm.shape[1] == PROJ_W
    wr = jnp.concatenate([router_expert_w[l], router_group_w[l],
                          jnp.zeros((d, LANES - N_EXPERTS - N_GROUPS), F32)], axis=1)
    br = _pad_lanes(jnp.concatenate([router_expert_b[l], router_group_b[l]]), LANES)
    return dict(
        n1=norm1_w[l].reshape(1, d), n2=norm2_w[l].reshape(1, d), w_in=w_perm,
        gdn_cw=gdn_conv_w[l], alog=_pad_lanes(gdn_a_log[l], LANES), dtb=_pad_lanes(gdn_dt_bias[l], LANES),
        gdn_nw=gdn_norm_w[l].reshape(1, -1),
        qnw=swa_q_norm_w[l].reshape(1, -1), knw=swa_k_norm_w[l].reshape(1, -1), sinks=swa_sinks[l].reshape(1, -1),
        dww=conv_dw_w[l], dwb=conv_dw_b[l].reshape(1, -1), lnw=conv_ln_w[l].reshape(1, -1),
        lnb=conv_ln_b[l].reshape(1, -1),
        wb=w_branch[l].astype(BF16), wo=w_out[l].astype(BF16), wr=wr, br=br,
        wg=moe_w_gate[l].astype(BF16), wu=moe_w_up[l].astype(BF16), wd=moe_w_down[l].astype(BF16))


def _tiles(n, l):
    prompt = l > BLOCK
    t = n * l
    if prompt:
        return dict(proj_tm=1536, tok_tm=512, moe_tm=768, conf_nb=1, conf_tm=384, conf_rc=64,
                    gdn_cpb=2, n_invalid=META_PAD)
    return dict(proj_tm=t, tok_tm=512, moe_tm=512, conf_nb=16, conf_tm=l, conf_rc=l,
                gdn_cpb=1, n_invalid=0)


def _layer(x3, lw, gdn_state, gdn_hist, conf_hist, swa_cache):
    n, l, d = x3.shape
    tl = _tiles(n, l)
    x2 = x3.reshape(n * l, d)
    p2 = _inproj(x2, lw['n1'], lw['w_in'], tl['proj_tm'], PROJ_W // 5)
    p3 = p2.reshape(n, l, PROJ_W)
    o_a, s_new = _gdn(p3, gdn_state, gdn_hist, lw['gdn_cw'], lw['alog'], lw['dtb'], lw['gdn_nw'],
                      tl['gdn_cpb'], tl['n_invalid'])
    gh_new = p3[:, l - (A_CONV - 1):, :QKV_W]
    v_col = BKV_BLK * BKV_W + B_KV_HEADS * B_HD
    if swa_cache is None:
        o_b, kn = _swa_prompt(p3, lw['qnw'], lw['knw'], lw['sinks'])
        v_raw = p3[:, :, v_col:v_col + B_KV_HEADS * B_HD]
        kv = lambda sl: jnp.concatenate([kn[:, sl], v_raw[:, sl]], axis=-1).reshape(
            n, -1, 2, B_KV_HEADS, B_HD)
        swa_new = (kv(slice(META_PAD, BLOCK)), kv(slice(l - WINDOW, l)))
    else:
        win, meta = swa_cache
        o_b, nwin = _swa_sample(p3, win.reshape(n, win.shape[1], BKV_W), meta.reshape(n, N_META, BKV_W),
                                lw['qnw'], lw['knw'], lw['sinks'], 16)
        swa_new = nwin.reshape(win.shape)
    o_c, ch_new = _conformer(p3, conf_hist, lw['dww'], lw['dwb'], lw['lnw'], lw['lnb'],
                             tl['conf_nb'], tl['conf_tm'], tl['conf_rc'], tl['n_invalid'])
    t = n * l
    x2 = _merge(o_a.reshape(t, -1), o_b.reshape(t, -1), o_c.reshape(t, -1), p2, x2, lw['wb'], lw['wo'],
                tl['tok_tm'])
    x2 = _moe(x2, lw['n2'], lw['wr'], lw['br'], lw['wg'], lw['wu'], lw['wd'], tl['moe_tm'])
    ch_new = ch_new[:, CONF_HALO - (C_CONV - 1):]
    return x2.reshape(n, l, d), s_new, gh_new, swa_new, ch_new


def kernel(x_prompt, x_sample, state_gdn, cache_gdn_conv, cache_swa_kv, cache_meta_kv, cache_conv, meta_tokens, norm1_w, norm2_w, w_in, gdn_conv_w, gdn_a_log, gdn_dt_bias, gdn_norm_w, swa_q_norm_w, swa_k_norm_w, swa_sinks, conv_dw_w, conv_dw_b, conv_ln_w, conv_ln_b, w_branch, w_out, router_group_w, router_group_b, router_expert_w, router_expert_b, moe_w_gate, moe_w_up, moe_w_down):
    dtp = x_prompt.dtype
    n_p, _, d = x_prompt.shape
    n_s = x_sample.shape[0]
    depth = w_in.shape[0]
    xp = jnp.concatenate([jnp.zeros((n_p, META_PAD, d), dtp),
                          jnp.broadcast_to(meta_tokens.astype(dtp)[None], (n_p, N_META, d)),
                          x_prompt], axis=1)
    xs = x_sample
    outs_p, outs_s = [], []
    for l in range(depth):
        lw = _layer_weights(l, norm1_w, norm2_w, w_in, gdn_conv_w, gdn_a_log, gdn_dt_bias, gdn_norm_w,
                            swa_q_norm_w, swa_k_norm_w, swa_sinks, conv_dw_w, conv_dw_b, conv_ln_w, conv_ln_b,
                            w_branch, w_out, router_group_w, router_group_b, router_expert_w, router_expert_b,
                            moe_w_gate, moe_w_up, moe_w_down)
        xp, s_p, gh_p, (mkv_p, wkv_p), ch_p = _layer(
            xp, lw, jnp.zeros((n_p, A_HEADS, A_DK, A_DK), dtp), None,
            jnp.zeros((n_p, CONF_HALO, C_CH), dtp), None)
        outs_p.append((s_p, gh_p, wkv_p, mkv_p, ch_p))
        gh0 = jnp.pad(cache_gdn_conv[l], ((0, 0), (GDN_HALO - (A_CONV - 1), 0), (0, 0)))
        ch0 = jnp.pad(cache_conv[l], ((0, 0), (CONF_HALO - (C_CONV - 1), 0), (0, 0)))
        xs, s_s, gh_s, wkv_s, ch_s = _layer(xs, lw, state_gdn[l], gh0, ch0, (cache_swa_kv[l], cache_meta_kv[l]))
        outs_s.append((s_s, gh_s, wkv_s, ch_s))
    stack = lambda outs, i: jnp.stack([o[i] for o in outs])
    return (xp[:, BLOCK:], xs,
            stack(outs_p, 0), stack(outs_p, 1), stack(outs_p, 2), stack(outs_p, 3), stack(outs_p, 4),
            stack(outs_s, 0), stack(outs_s, 1), stack(outs_s, 2), stack(outs_s, 3))
```

```python
import functools

import jax
import jax.numpy as jnp
from jax import lax
from jax.experimental import pallas as pl
from jax.experimental.pallas import tpu as pltpu

F32 = jnp.float32
BF16 = jnp.bfloat16
HIGHEST = lax.Precision.HIGHEST
EPS = 1e-6
NEG = -1e30

VMEM_LIMIT_BYTES = 56 * 1024 * 1024
LANES = 128
SUBLANES = 8

PAST_LEN = 16384
N_META = 16
BLOCK = 128
WINDOW = 128
META_PAD = BLOCK - N_META
A_HEADS = 4
A_DK = 128
A_CONV = 4
B_HEADS = 8
B_KV_HEADS = 2
B_GROUP = B_HEADS // B_KV_HEADS
B_HD = 64
C_CH = 512
C_CONV = 31
BRANCH_W = 512
N_GROUPS = 4
EXP_PER_GROUP = 4
N_EXPERTS = N_GROUPS * EXP_PER_GROUP

A_KW = A_HEADS * A_DK
QKV_W = 3 * A_KW
Z_BLK = QKV_W // BRANCH_W
CU_BLK = 2
GL_BLK = 1
BQ_BLK = 12
BKV_W = 2 * B_KV_HEADS * B_HD
BKV_BLK = 26
AB_BLK = 54
PROJ_W = 7040
CONF_HALO = 32
GDN_HALO = SUBLANES


def _rms(x, w):
    return x * lax.rsqrt(jnp.mean(x * x, axis=-1, keepdims=True) + EPS) * w


def _silu(x):
    return x * jax.nn.sigmoid(x)


def _cparams(sem):
    return pltpu.CompilerParams(dimension_semantics=sem, vmem_limit_bytes=VMEM_LIMIT_BYTES)


def _inproj_kernel(x_ref, nw_ref, w_ref, o_ref, h_ref):
    @pl.when(pl.program_id(1) == 0)
    def _():
        h_ref[...] = _rms(x_ref[...], nw_ref[...]).astype(BF16)

    o_ref[...] = jnp.dot(h_ref[...], w_ref[...], preferred_element_type=F32)


def _inproj(x2, nw, w, tm, tn):
    t, d = x2.shape
    wd = w.shape[1]
    return pl.pallas_call(
        _inproj_kernel,
        grid=(t // tm, wd // tn),
        in_specs=[pl.BlockSpec((tm, d), lambda i, j: (i, 0)),
                  pl.BlockSpec((1, d), lambda i, j: (0, 0)),
                  pl.BlockSpec((d, tn), lambda i, j: (0, j))],
        out_specs=pl.BlockSpec((tm, tn), lambda i, j: (i, j)),
        out_shape=jax.ShapeDtypeStruct((t, wd), F32),
        scratch_shapes=[pltpu.VMEM((tm, d), BF16)],
        compiler_params=_cparams(("parallel", "arbitrary")),
        name="inproj",
    )(x2, nw, w)


def _conf_kernel(cu_ref, hist_ref, dww_ref, dwb_ref, lnw_ref, lnb_ref, y_ref, nh_ref, e_ref,
                 *, tm, rc, n_invalid):
    t = pl.program_id(1)
    nb = cu_ref.shape[0]
    halo = CONF_HALO

    @pl.when(t == 0)
    def _():
        e_ref[:, 0:halo, :] = hist_ref[...]
        e_ref[:, halo + tm:halo + tm + SUBLANES, :] = jnp.zeros((nb, SUBLANES, C_CH), F32)

    @pl.when(t > 0)
    def _():
        e_ref[:, 0:halo, :] = e_ref[:, tm:tm + halo, :]

    cu = cu_ref[...]
    u = cu[:, :, :C_CH] * jax.nn.sigmoid(cu[:, :, C_CH:])
    if n_invalid:
        row = lax.broadcasted_iota(jnp.int32, (1, tm, 1), 1) + t * tm
        u = jnp.where(row >= n_invalid, u, 0.0)
    e_ref[:, halo:halo + tm, :] = u
    nh_ref[...] = e_ref[:, tm:tm + halo, :]

    off = halo - (C_CONV - 1)
    for c in range(C_CH // LANES):
        cs = slice(c * LANES, (c + 1) * LANES)

        def rbody(r, carry, cs=cs):
            r0 = pl.multiple_of(r * rc, rc)
            blk = e_ref[:, pl.ds(r0, rc + halo + SUBLANES), cs]
            acc = jnp.zeros((nb, rc, LANES), F32)
            for s in range(SUBLANES):
                z = None
                for a in range((halo + SUBLANES) // SUBLANES):
                    k = SUBLANES * a + s - off
                    if 0 <= k < C_CONV:
                        term = dww_ref[k:k + 1, cs] * blk[:, SUBLANES * a:SUBLANES * a + rc + SUBLANES, :]
                        z = term if z is None else z + term
                acc = acc + z[:, s:s + rc, :]
            y_ref[:, pl.ds(r0, rc), cs] = acc + dwb_ref[:, cs]
            return carry

        lax.fori_loop(0, tm // rc, rbody, 0)

    y = y_ref[...]
    mu = jnp.mean(y, axis=-1, keepdims=True)
    var = jnp.mean(jnp.square(y - mu), axis=-1, keepdims=True)
    yn = (y - mu) * lax.rsqrt(var + EPS) * lnw_ref[...] + lnb_ref[...]
    y_ref[...] = _silu(yn)


def _conformer(p3, hist, dww, dwb, lnw, lnb, nb, tm, rc, n_invalid):
    n, l, _ = p3.shape
    kern = functools.partial(_conf_kernel, tm=tm, rc=rc, n_invalid=n_invalid)
    full = lambda shape: pl.BlockSpec(shape, lambda i, t: (0,) * len(shape))
    return pl.pallas_call(
        kern,
        grid=(n // nb, l // tm),
        in_specs=[pl.BlockSpec((nb, tm, 2 * C_CH), lambda i, t: (i, t, CU_BLK)),
                  pl.BlockSpec((nb, CONF_HALO, C_CH), lambda i, t: (i, 0, 0)),
                  full(dww.shape), full(dwb.shape), full(lnw.shape), full(lnb.shape)],
        out_specs=[pl.BlockSpec((nb, tm, C_CH), lambda i, t: (i, t, 0)),
                   pl.BlockSpec((nb, CONF_HALO, C_CH), lambda i, t: (i, 0, 0))],
        out_shape=[jax.ShapeDtypeStruct((n, l, C_CH), F32),
                   jax.ShapeDtypeStruct((n, CONF_HALO, C_CH), F32)],
        scratch_shapes=[pltpu.VMEM((nb, CONF_HALO + tm + SUBLANES, C_CH), F32)],
        compiler_params=_cparams(("parallel", "arbitrary")),
        name="conformer",
    )(p3, hist, dww, dwb, lnw, lnb)


def _softmax_sink_pv(s, mask, sink, v, dims):
    s = jnp.where(mask, s, NEG)
    m = jnp.maximum(jnp.max(s, axis=-1, keepdims=True), sink)
    p = jnp.exp(s - m)
    den = jnp.sum(p, axis=-1, keepdims=True) + jnp.exp(sink - m)
    return lax.dot_general(p, v, dims, preferred_element_type=F32) / den


def _swa_prompt_kernel(q_ref, kv_ref, kvp_ref, kvm_ref, qnw_ref, knw_ref, sink_ref, o_ref, kvn_ref, *, bps):
    step = pl.program_id(1)
    qnw = qnw_ref[...]
    knw = knw_ref[...]
    qw = B_HEADS * B_HD
    kw = B_KV_HEADS * B_HD
    nk = N_META + 2 * BLOCK
    rows = B_GROUP * BLOCK
    r = lax.broadcasted_iota(jnp.int32, (rows, nk), 0) % BLOCK
    c = lax.broadcasted_iota(jnp.int32, (rows, nk), 1)
    cp = c - N_META
    co = cp - BLOCK
    hrow = lax.broadcasted_iota(jnp.int32, (rows, 1), 0) // BLOCK
    m_meta = c < N_META
    m_own = (co >= 0) & (co <= r)
    m_prev = (cp >= 0) & (cp < BLOCK) & (cp > r)
    bias0 = jnp.where(m_meta & (META_PAD + c <= r), 0.0, NEG)
    bias1 = jnp.where(m_meta | m_own, 0.0, NEG)
    bias2 = jnp.where(m_meta | m_own | m_prev, 0.0, NEG)
    sinks = []
    for g in range(B_KV_HEADS):
        sink = jnp.zeros((rows, 1), F32)
        for i in range(B_GROUP):
            h = g * B_GROUP + i
            sink = jnp.where(hrow == i, sink_ref[0:1, h:h + 1], sink)
        sinks.append(sink)
    hb_q = (lax.broadcasted_iota(jnp.int32, (qw, qw), 0) // B_HD
            == lax.broadcasted_iota(jnp.int32, (qw, qw), 1) // B_HD).astype(BF16)
    hb_k = hb_q[:kw, :kw]
    half = lax.broadcasted_iota(jnp.int32, (1, LANES), 1) // B_HD

    def head_rms(x, hb, w):
        sq = x * x
        hi = sq.astype(BF16)
        lo = (sq - hi.astype(F32)).astype(BF16)
        ss = jnp.dot(hi, hb, preferred_element_type=F32) + jnp.dot(lo, hb, preferred_element_type=F32)
        return x * lax.rsqrt(ss * (1.0 / B_HD) + EPS) * w

    kvm = kvm_ref[META_PAD:BLOCK, :]
    km = head_rms(kvm[:, :kw], hb_k, knw)
    kv_prev = kvp_ref[...]
    k_prev = head_rms(kv_prev[:, :kw], hb_k, knw)
    for b in range(bps):
        j = step * bps + b
        rs = slice(b * BLOCK, (b + 1) * BLOCK)
        q = q_ref[rs, :]
        kv = kv_ref[rs, :]
        k_own = head_rms(kv[:, :kw], hb_k, knw)
        kvn_ref[rs, :] = jnp.concatenate([k_own, kv[:, kw:]], axis=-1)
        qn = head_rms(q, hb_q, qnw)
        bias = jnp.where(j >= 2, bias2, jnp.where(j >= 1, bias1, bias0))
        kcat = jnp.concatenate([km, k_prev, k_own], axis=0)
        vcat = jnp.concatenate([kvm[:, kw:], kv_prev[:, kw:], kv[:, kw:]], axis=0)
        kcat_b = kcat.astype(BF16)
        vcat_b = vcat.astype(BF16)
        tiles = [qn[:, m * LANES:(m + 1) * LANES] for m in range(qw // LANES)]
        swapped = [pltpu.roll(tl_, B_HD, 1) for tl_ in tiles]
        outs = [None] * B_HEADS
        for g in range(B_KV_HEADS):
            in_g = half == g
            parts = []
            for i in range(B_GROUP):
                h = g * B_GROUP + i
                src = tiles[h // 2] if h % 2 == g else swapped[h // 2]
                parts.append(jnp.where(in_g, src, 0.0))
            lhs = jnp.concatenate(parts, axis=0)
            s = lax.dot_general(lhs.astype(BF16), kcat_b, (((1,), (1,)), ((), ())), preferred_element_type=F32)
            s = s * (B_HD ** -0.5) + bias
            m = jnp.maximum(jnp.max(s, axis=-1, keepdims=True), sinks[g])
            p = jnp.exp(s - m)
            den = jnp.sum(p, axis=-1, keepdims=True) + jnp.exp(sinks[g] - m)
            res = jnp.dot(p.astype(BF16), vcat_b, preferred_element_type=F32) / den
            for i in range(B_GROUP):
                h = g * B_GROUP + i
                part = res[i * BLOCK:(i + 1) * BLOCK, :]
                outs[h] = part if h % 2 == g else pltpu.roll(part, B_HD, 1)
        for m in range(qw // LANES):
            o_ref[rs, m * LANES:(m + 1) * LANES] = jnp.where(half == 0, outs[2 * m], outs[2 * m + 1])
        kv_prev, k_prev = kv, k_own


def _swa_prompt(p3, qnw, knw, sinks, bps):
    n, l, _ = p3.shape
    full = lambda shape: pl.BlockSpec(shape, lambda i, j: (0,) * len(shape))
    tq = bps * BLOCK
    return pl.pallas_call(
        functools.partial(_swa_prompt_kernel, bps=bps),
        grid=(n, l // tq),
        in_specs=[pl.BlockSpec((None, tq, B_HEADS * B_HD), lambda i, j: (i, j, BQ_BLK)),
                  pl.BlockSpec((None, tq, BKV_W), lambda i, j: (i, j, BKV_BLK)),
                  pl.BlockSpec((None, BLOCK, BKV_W), lambda i, j: (i, jnp.maximum(j * bps - 1, 0), BKV_BLK)),
                  pl.BlockSpec((None, BLOCK, BKV_W), lambda i, j: (i, 0, BKV_BLK)),
                  full(qnw.shape), full(knw.shape), full(sinks.shape)],
        out_specs=[pl.BlockSpec((None, tq, B_HEADS * B_HD), lambda i, j: (i, j, 0)),
                   pl.BlockSpec((None, tq, BKV_W), lambda i, j: (i, j, 0))],
        out_shape=[jax.ShapeDtypeStruct((n, l, B_HEADS * B_HD), F32),
                   jax.ShapeDtypeStruct((n, l, BKV_W), F32)],
        compiler_params=_cparams(("parallel", "arbitrary")),
        name="swa_prompt",
    )(p3, p3, p3, p3, qnw, knw, sinks)


def _swa_sample_kernel(q_ref, kvn_ref, win_ref, meta_ref, qnw_ref, knw_ref, sink_ref, o_ref, nwin_ref):
    q = q_ref[...]
    kvn = kvn_ref[...]
    win = win_ref[...]
    meta = meta_ref[...]
    qnw = qnw_ref[...]
    knw = knw_ref[...]
    t_new = q.shape[1]
    w = win.shape[1]
    nk = N_META + w + t_new
    rows = B_GROUP * t_new
    tq = lax.broadcasted_iota(jnp.int32, (1, rows, nk), 1) % t_new
    c = lax.broadcasted_iota(jnp.int32, (1, rows, nk), 2)
    cw = c - N_META
    win_ok = (cw >= 0) & (cw < w) & (cw - w > tq - WINDOW) & (cw + (PAST_LEN - w) >= N_META)
    cn = cw - w
    new_ok = (cn >= 0) & (cn <= tq) & (cn > tq - WINDOW)
    mask = (c < N_META) | win_ok | new_ok
    hrow = lax.broadcasted_iota(jnp.int32, (1, rows, 1), 1) // t_new
    kn_parts = []
    for g in range(B_KV_HEADS):
        ks = slice(g * B_HD, (g + 1) * B_HD)
        vs = slice((B_KV_HEADS + g) * B_HD, (B_KV_HEADS + g + 1) * B_HD)
        k_new = _rms(kvn[:, :, ks], knw)
        kn_parts.append(k_new)
        kk = jnp.concatenate([meta[:, :, ks], win[:, :, ks], k_new], axis=1)
        vv = jnp.concatenate([meta[:, :, vs], win[:, :, vs], kvn[:, :, vs]], axis=1)
        qq = jnp.concatenate(
            [_rms(q[:, :, (g * B_GROUP + i) * B_HD:(g * B_GROUP + i + 1) * B_HD], qnw) for i in range(B_GROUP)],
            axis=1)
        sink = jnp.zeros((1, rows, 1), F32)
        for i in range(B_GROUP):
            h = g * B_GROUP + i
            sink = jnp.where(hrow == i, sink_ref[0:1, h:h + 1], sink)
        s = jnp.einsum('bqd,bkd->bqk', qq, kk, preferred_element_type=F32) * (B_HD ** -0.5)
        o = _softmax_sink_pv(s, mask, sink, vv, (((2,), (1,)), ((0,), (0,))))
        for i in range(B_GROUP):
            h = g * B_GROUP + i
            o_ref[:, :, h * B_HD:(h + 1) * B_HD] = o[:, i * t_new:(i + 1) * t_new, :]
    nwin_ref[:, 0:w - t_new, :] = win[:, t_new:w, :]
    nwin_ref[:, w - t_new:w, :] = jnp.concatenate(kn_parts + [kvn[:, :, B_KV_HEADS * B_HD:]], axis=-1)


def _swa_sample(p3, win, meta, qnw, knw, sinks, nb):
    n, t_new, _ = p3.shape
    w = win.shape[1]
    full = lambda shape: pl.BlockSpec(shape, lambda i: (0,) * len(shape))
    return pl.pallas_call(
        _swa_sample_kernel,
        grid=(n // nb,),
        in_specs=[pl.BlockSpec((nb, t_new, B_HEADS * B_HD), lambda i: (i, 0, BQ_BLK)),
                  pl.BlockSpec((nb, t_new, BKV_W), lambda i: (i, 0, BKV_BLK)),
                  pl.BlockSpec((nb, w, BKV_W), lambda i: (i, 0, 0)),
                  pl.BlockSpec((nb, N_META, BKV_W), lambda i: (i, 0, 0)),
                  full(qnw.shape), full(knw.shape), full(sinks.shape)],
        out_specs=[pl.BlockSpec((nb, t_new, B_HEADS * B_HD), lambda i: (i, 0, 0)),
                   pl.BlockSpec((nb, w, BKV_W), lambda i: (i, 0, 0))],
        out_shape=[jax.ShapeDtypeStruct((n, t_new, B_HEADS * B_HD), F32),
                   jax.ShapeDtypeStruct((n, w, BKV_W), F32)],
        compiler_params=_cparams(("parallel",)),
        name="swa_sample",
    )(p3, p3, win, meta, qnw, knw, sinks)


GDN_ROWS = 64
GDN_R = A_HEADS * GDN_ROWS
GDN_PAIRS = A_HEADS // 2
PAIR_W = 2 * A_DK


def _mm(a, b):
    return jnp.dot(a.astype(BF16), b.astype(BF16), preferred_element_type=F32)


def _block_diag2(x):
    z = jnp.zeros((x.shape[0], A_DK), x.dtype)
    return jnp.concatenate([jnp.concatenate([x[:, :A_DK], z], axis=1),
                            jnp.concatenate([z, x[:, A_DK:]], axis=1)], axis=0)


def _gdn_prep_kernel(x_ref, halo_ref, ab_ref, cw_ref, alog_ref, dtb_ref, u_ref, wq_ref, qkk_ref, gl_ref,
                     *, sb, cpb, n_invalid, seq_halo):
    t = pl.program_id(1)
    ns = x_ref.shape[0]
    rows = x_ref.shape[1] // cpb
    c = GDN_ROWS
    r = GDN_R
    halo = halo_ref[...]
    if seq_halo:
        hrow = lax.broadcasted_iota(jnp.int32, (1, GDN_HALO, 1), 1) + (t * cpb * rows - GDN_HALO)
        halo = jnp.where(hrow >= n_invalid, halo, 0.0)
    cw = cw_ref[...]
    alog = alog_ref[...]
    dtb = dtb_ref[...]
    ri = lax.broadcasted_iota(jnp.int32, (r, r), 0)
    ci = lax.broadcasted_iota(jnp.int32, (r, r), 1)
    same = (ri // sb) == (ci // sb)
    incl = same & (ri >= ci)
    strict = same & (ri > ci)
    eye = (ri == ci).astype(F32)
    rc_ = lax.broadcasted_iota(jnp.int32, (c, c), 0)
    cc_ = lax.broadcasted_iota(jnp.int32, (c, c), 1)
    same_c = (rc_ // sb) == (cc_ // sb)
    lmat = jnp.concatenate([(same_c & (rc_ >= cc_)).astype(F32), same_c.astype(F32)], axis=0)
    sel = (lax.broadcasted_iota(jnp.int32, (SUBLANES, 2 * c), 1)
           == c + (lax.broadcasted_iota(jnp.int32, (SUBLANES, 2 * c), 0) * SUBLANES // sb) * sb).astype(F32)
    off = GDN_HALO - (A_CONV - 1)
    for cc in range(cpb):
        x = x_ref[:, cc * rows:(cc + 1) * rows, :]
        ab = ab_ref[:, cc * rows:(cc + 1) * rows, :].reshape(c, LANES)
        if n_invalid:
            row0 = (t * cpb + cc) * rows
            xrow = lax.broadcasted_iota(jnp.int32, (1, rows, 1), 1) + row0
            x = jnp.where(xrow >= n_invalid, x, 0.0)
        ext = jnp.concatenate([halo, x], axis=1)
        acc = cw[0:1, :] * ext[:, off:off + rows, :]
        for k in range(1, A_CONV):
            acc = acc + cw[k:k + 1, :] * ext[:, off + k:off + k + rows, :]
        y = _silu(acc).reshape(c, QKV_W)
        halo = x[:, rows - GDN_HALO:rows, :]

        sp = jnp.maximum(ab + dtb, 0.0) + jnp.log1p(jnp.exp(-jnp.abs(ab + dtb)))
        gfull = -jnp.exp(alog) * sp
        bfull = jax.nn.sigmoid(ab)
        if n_invalid:
            valid = (lax.broadcasted_iota(jnp.int32, (c, 1), 0) + row0) >= n_invalid
            gfull = jnp.where(valid, gfull, 0.0)
            bfull = jnp.where(valid, bfull, 0.0)
        bcol = jnp.concatenate([bfull[:, A_HEADS + h:A_HEADS + h + 1] for h in range(A_HEADS)], axis=0)
        gg = jnp.dot(lmat, gfull, precision=HIGHEST, preferred_element_type=F32)
        ggt = gg.T
        gc = jnp.concatenate([gg[:c, h:h + 1] for h in range(A_HEADS)], axis=0)
        grow = jnp.concatenate([ggt[h:h + 1, :c] for h in range(A_HEADS)], axis=1)
        drow = jnp.concatenate([ggt[h:h + 1, c:] for h in range(A_HEADS)], axis=1) - grow
        g8 = jnp.exp(jnp.dot(sel, gg, precision=HIGHEST, preferred_element_type=F32))
        decay = jnp.where(incl, jnp.exp(jnp.where(incl, gc - grow, 0.0)), 0.0)

        qs, ks, vs = [], [], []
        for h in range(A_HEADS):
            q = y[:, h * A_DK:(h + 1) * A_DK]
            k = y[:, A_KW + h * A_DK:A_KW + (h + 1) * A_DK]
            qs.append(q * lax.rsqrt(jnp.sum(q * q, axis=-1, keepdims=True) + EPS) * (A_DK ** -0.5))
            ks.append(k * lax.rsqrt(jnp.sum(k * k, axis=-1, keepdims=True) + EPS))
            vs.append(y[:, 2 * A_KW + h * A_DK:2 * A_KW + (h + 1) * A_DK])
        qr = jnp.concatenate(qs, axis=0)
        kr = jnp.concatenate(ks, axis=0)
        vr = jnp.concatenate(vs, axis=0)
        krt = kr.T
        a = jnp.where(strict, _mm(kr, krt) * decay * bcol, 0.0)
        tinv = eye - a
        ak = a
        kpow = 2
        while kpow < sb:
            ak = _mm(ak, ak)
            tinv = tinv + _mm(tinv, ak)
            kpow *= 2
        eg = jnp.exp(gc)
        ur = _mm(tinv, vr * bcol)
        wr = _mm(tinv, kr * (bcol * eg))
        qk = _mm(qr, krt) * decay
        qd = qr * eg
        kdt = krt * jnp.exp(drow)
        for p in range(GDN_PAIRS):
            r0 = slice(2 * p * c, (2 * p + 1) * c)
            r1 = slice((2 * p + 1) * c, (2 * p + 2) * c)
            pl_ = slice(2 * p * c, (2 * p + 2) * c)
            u_ref[cc, p] = jnp.concatenate([ur[r0], ur[r1]], axis=1)
            wq = jnp.concatenate([jnp.concatenate([wr[r0], wr[r1]], axis=1),
                                  jnp.concatenate([qd[r0], qd[r1]], axis=1)], axis=0)
            wq_ref[cc, p] = wq.astype(wq_ref.dtype)
            qkk = jnp.concatenate([qk[r0, pl_] + qk[r1, pl_], kdt[:, pl_]], axis=0)
            qkk_ref[cc, p] = qkk.astype(qkk_ref.dtype)
            gl_ref[cc, p] = jnp.concatenate(
                [jnp.broadcast_to(g8[:, 2 * p:2 * p + 1], (SUBLANES, A_DK)),
                 jnp.broadcast_to(g8[:, 2 * p + 1:2 * p + 2], (SUBLANES, A_DK))], axis=1)


def _gdn_prep(x_arr, halo_arr, halo_spec, cw, alog, dtb, ns, rows, cpb, sb, n_invalid, op_dtype):
    n, l, _ = x_arr.shape
    grid = (n // ns, l // (cpb * rows))
    units = grid[0] * grid[1] * cpb
    kern = functools.partial(_gdn_prep_kernel, sb=sb, cpb=cpb, n_invalid=n_invalid,
                             seq_halo=halo_arr is x_arr)
    full = lambda shape: pl.BlockSpec(shape, lambda i, t: (0,) * len(shape))
    steps = grid[1]
    out_map = lambda i, t: (i * steps + t, 0, 0, 0)
    c = GDN_ROWS
    return pl.pallas_call(
        kern,
        grid=grid,
        in_specs=[pl.BlockSpec((ns, cpb * rows, QKV_W), lambda i, t: (i, t, 0)),
                  halo_spec,
                  pl.BlockSpec((ns, cpb * rows, LANES), lambda i, t: (i, t, AB_BLK)),
                  full(cw.shape), full(alog.shape), full(dtb.shape)],
        out_specs=[pl.BlockSpec((cpb, GDN_PAIRS, c, PAIR_W), out_map),
                   pl.BlockSpec((cpb, GDN_PAIRS, 2 * c, PAIR_W), out_map),
                   pl.BlockSpec((cpb, GDN_PAIRS, c + A_DK, 2 * c), out_map),
                   pl.BlockSpec((cpb, GDN_PAIRS, SUBLANES, PAIR_W), out_map)],
        out_shape=[jax.ShapeDtypeStruct((units, GDN_PAIRS, c, PAIR_W), F32),
                   jax.ShapeDtypeStruct((units, GDN_PAIRS, 2 * c, PAIR_W), op_dtype),
                   jax.ShapeDtypeStruct((units, GDN_PAIRS, c + A_DK, 2 * c), op_dtype),
                   jax.ShapeDtypeStruct((units, GDN_PAIRS, SUBLANES, PAIR_W), F32)],
        compiler_params=_cparams(("parallel", "arbitrary")),
        name="gdn_prep",
    )(x_arr, halo_arr, x_arr, cw, alog, dtb)


def _gdn_out(o_pair, z_ref, o_ref, nw, idx, p):
    for i in range(2):
        h = 2 * p + i
        hs = slice(h * A_DK, (h + 1) * A_DK)
        o_ref[idx + (hs,)] = _rms(o_pair[..., i * A_DK:(i + 1) * A_DK], nw) * _silu(z_ref[idx + (hs,)])


def _gdn_scan_kernel(u_ref, wq_ref, qkk_ref, gl_ref, z_ref, s0_ref, nw_ref, o_ref, sout_ref, s_ref):
    t = pl.program_id(0)
    n_seq = u_ref.shape[0]
    c = GDN_ROWS
    nw = nw_ref[...]

    @pl.when(t == 0)
    def _():
        for n in range(n_seq):
            for p in range(GDN_PAIRS):
                s_ref[n, p] = jnp.concatenate([s0_ref[n, 2 * p], s0_ref[n, 2 * p + 1]], axis=1)

    for n in range(n_seq):
        for p in range(GDN_PAIRS):
            s = s_ref[n, p]
            r1 = jnp.dot(wq_ref[n, 0, p], _block_diag2(s.astype(BF16)), preferred_element_type=F32)
            vnew = u_ref[n, 0, p] - r1[:c]
            r2 = jnp.dot(qkk_ref[n, 0, p], _block_diag2(vnew.astype(BF16)), preferred_element_type=F32)
            s_ref[n, p] = s * gl_ref[n, 0, p][0:1, :] + r2[c:]
            _gdn_out(r1[c:] + r2[:c], z_ref, o_ref, nw, (n, slice(None)), p)

    @pl.when(t == pl.num_programs(0) - 1)
    def _():
        for n in range(n_seq):
            for p in range(GDN_PAIRS):
                s = s_ref[n, p]
                sout_ref[n, 2 * p] = s[:, :A_DK]
                sout_ref[n, 2 * p + 1] = s[:, A_DK:]


def _gdn_scan(u, wq, qkk, gl, p3, s0, nw):
    n, l, _ = p3.shape
    c = GDN_ROWS
    steps = l // c
    v5 = lambda a: a.reshape((n, steps) + a.shape[1:])
    u, wq, qkk, gl = v5(u), v5(wq), v5(qkk), v5(gl)
    unit = lambda a: pl.BlockSpec((n, 1) + a.shape[2:], lambda t: (0, t, 0, 0, 0))
    vw = A_HEADS * A_DK
    return pl.pallas_call(
        _gdn_scan_kernel,
        grid=(steps,),
        in_specs=[unit(u), unit(wq), unit(qkk), unit(gl),
                  pl.BlockSpec((n, c, vw), lambda t: (0, t, Z_BLK)),
                  pl.BlockSpec(s0.shape, lambda t: (0, 0, 0, 0)),
                  pl.BlockSpec(nw.shape, lambda t: (0, 0))],
        out_specs=[pl.BlockSpec((n, c, vw), lambda t: (0, t, 0)),
                   pl.BlockSpec(s0.shape, lambda t: (0, 0, 0, 0))],
        out_shape=[jax.ShapeDtypeStruct((n, l, vw), F32),
                   jax.ShapeDtypeStruct(s0.shape, F32)],
        scratch_shapes=[pltpu.VMEM((n, GDN_PAIRS, A_DK, PAIR_W), F32)],
        compiler_params=_cparams(("arbitrary",)),
        name="gdn_scan",
    )(u, wq, qkk, gl, p3, s0, nw)


def _gdn_step_kernel(u_ref, wq_ref, qkk_ref, gl_ref, z_ref, s0_ref, nw_ref, o_ref, sout_ref):
    ns, rows, _ = z_ref.shape
    c = GDN_ROWS
    nw = nw_ref[...]
    lane_seq = (lax.broadcasted_iota(jnp.int32, (A_DK, 2 * c), 1) % c) // rows
    for p in range(GDN_PAIRS):
        wq = wq_ref[0, p]
        qkk = qkk_ref[0, p]
        gl = gl_ref[0, p]
        ws, qs, states = [], [], []
        for s_i in range(ns):
            sl = slice(s_i * rows, (s_i + 1) * rows)
            s = jnp.concatenate([s0_ref[s_i, 2 * p], s0_ref[s_i, 2 * p + 1]], axis=1)
            states.append(s)
            lhs = jnp.concatenate([wq[sl], wq[c + s_i * rows:c + (s_i + 1) * rows]], axis=0)
            r1 = _mm(lhs, _block_diag2(s.astype(BF16)))
            ws.append(r1[:rows])
            qs.append(r1[rows:])
        vnew = u_ref[0, p] - jnp.concatenate(ws, axis=0)
        bdv = _block_diag2(vnew.astype(BF16))
        o = jnp.concatenate(qs, axis=0) + _mm(qkk[:c], bdv)
        _gdn_out(o.reshape(ns, rows, PAIR_W), z_ref, o_ref, nw, (slice(None), slice(None)), p)
        kdt = qkk[c:]
        lhs = jnp.concatenate([jnp.where(lane_seq == s_i, kdt, 0.0) for s_i in range(ns)], axis=0)
        upd = _mm(lhs, bdv)
        for s_i in range(ns):
            s_new = states[s_i] * gl[s_i:s_i + 1, :] + upd[s_i * A_DK:(s_i + 1) * A_DK]
            sout_ref[s_i, 2 * p] = s_new[:, :A_DK]
            sout_ref[s_i, 2 * p + 1] = s_new[:, A_DK:]


def _gdn_step(u, wq, qkk, gl, p3, s0, nw, ns):
    n, rows, _ = p3.shape
    unit = lambda a: pl.BlockSpec((1,) + a.shape[1:], lambda i: (i, 0, 0, 0))
    vw = A_HEADS * A_DK
    sblk = (ns,) + s0.shape[1:]
    return pl.pallas_call(
        _gdn_step_kernel,
        grid=(n // ns,),
        in_specs=[unit(u), unit(wq), unit(qkk), unit(gl),
                  pl.BlockSpec((ns, rows, vw), lambda i: (i, 0, Z_BLK)),
                  pl.BlockSpec(sblk, lambda i: (i, 0, 0, 0)),
                  pl.BlockSpec(nw.shape, lambda i: (0, 0))],
        out_specs=[pl.BlockSpec((ns, rows, vw), lambda i: (i, 0, 0)),
                   pl.BlockSpec(sblk, lambda i: (i, 0, 0, 0))],
        out_shape=[jax.ShapeDtypeStruct((n, rows, vw), F32),
                   jax.ShapeDtypeStruct(s0.shape, F32)],
        compiler_params=_cparams(("parallel",)),
        name="gdn_step",
    )(u, wq, qkk, gl, p3, s0, nw)


def _gdn(p3, s0, hist, cw, alog, dtb, nw, cpb, n_invalid):
    n, l, _ = p3.shape
    c = GDN_ROWS
    if hist is None:
        steps8 = cpb * c // GDN_HALO
        halo_spec = pl.BlockSpec((1, GDN_HALO, QKV_W), lambda i, t: (i, jnp.maximum(t * steps8 - 1, 0), 0))
        u, wq, qkk, gl = _gdn_prep(p3, p3, halo_spec, cw, alog, dtb, 1, c, cpb, c, n_invalid, BF16)
        return _gdn_scan(u, wq, qkk, gl, p3, s0, nw)
    ns = c // l
    halo_spec = pl.BlockSpec((ns, GDN_HALO, QKV_W), lambda i, t: (i, 0, 0))
    u, wq, qkk, gl = _gdn_prep(p3, hist, halo_spec, cw, alog, dtb, ns, l, 1, l, n_invalid, F32)
    return _gdn_step(u, wq, qkk, gl, p3, s0, nw, ns)


def _merge_kernel(oa_ref, ob_ref, oc_ref, gl_ref, x_ref, wb_ref, wo_ref, out_ref):
    d = x_ref.shape[1]
    merged = None
    for r, o_ref in enumerate((oa_ref, ob_ref, oc_ref)):
        pb = jnp.dot(o_ref[...].astype(BF16), wb_ref[r], preferred_element_type=F32)
        term = jax.nn.sigmoid(gl_ref[:, r * d:(r + 1) * d]) * pb
        merged = term if merged is None else merged + term
    out_ref[...] = x_ref[...] + jnp.dot(merged.astype(BF16), wo_ref[...], preferred_element_type=F32)


def _merge(oa, ob, oc, p2, x2, wb, wo, tm):
    t, d = x2.shape
    bw = oa.shape[1]
    row = lambda w, blk: pl.BlockSpec((tm, w), lambda i: (i, blk))
    return pl.pallas_call(
        _merge_kernel,
        grid=(t // tm,),
        in_specs=[row(bw, 0), row(bw, 0), row(bw, 0), row(3 * d, GL_BLK), row(d, 0),
                  pl.BlockSpec(wb.shape, lambda i: (0, 0, 0)),
                  pl.BlockSpec(wo.shape, lambda i: (0, 0))],
        out_specs=row(d, 0),
        out_shape=jax.ShapeDtypeStruct((t, d), F32),
        compiler_params=_cparams(("parallel",)),
        name="merge",
    )(oa, ob, oc, p2, x2, wb, wo)


def _route(logits):
    lane = lax.broadcasted_iota(jnp.int32, logits.shape, 1)
    is_g = (lane >= N_EXPERTS) & (lane < N_EXPERTS + N_GROUPS)
    gl = jnp.where(is_g, logits, NEG)
    gmax = jnp.max(gl, axis=-1, keepdims=True)
    gidx = jnp.min(jnp.where(gl == gmax, lane - N_EXPERTS, LANES), axis=-1, keepdims=True)
    gw = 1.0 / jnp.sum(jnp.where(is_g, jnp.exp(gl - gmax), 0.0), axis=-1, keepdims=True)
    in_grp = (lane < N_EXPERTS) & ((lane // EXP_PER_GROUP) == gidx)
    el = jnp.where(in_grp, logits, NEG)
    v1 = jnp.max(el, axis=-1, keepdims=True)
    i1 = jnp.min(jnp.where(el == v1, lane, LANES), axis=-1, keepdims=True)
    el2 = jnp.where(lane == i1, NEG, el)
    v2 = jnp.max(el2, axis=-1, keepdims=True)
    i2 = jnp.min(jnp.where(el2 == v2, lane, LANES), axis=-1, keepdims=True)
    e21 = jnp.exp(v2 - v1)
    w1 = gw / (1.0 + e21)
    w2 = gw * e21 / (1.0 + e21)
    return jnp.where(lane == i1, w1, 0.0) + jnp.where(lane == i2, w2, 0.0)


def _moe_kernel(x_ref, nw_ref, wr_ref, br_ref, wg_ref, wu_ref, wd_ref, out_ref, h_ref, cmb_ref):
    e = pl.program_id(1)

    @pl.when(e == 0)
    def _():
        x = x_ref[...]
        h = _rms(x, nw_ref[...])
        h_ref[...] = h.astype(BF16)
        logits = jnp.dot(h, wr_ref[...], precision=HIGHEST, preferred_element_type=F32) + br_ref[...]
        cmb_ref[...] = _route(logits)
        out_ref[...] = x

    h = h_ref[...]
    lane = lax.broadcasted_iota(jnp.int32, cmb_ref.shape, 1)
    ce = jnp.sum(jnp.where(lane == e, cmb_ref[...], 0.0), axis=-1, keepdims=True)
    hid = _silu(jnp.dot(h, wg_ref[0], preferred_element_type=F32)) * jnp.dot(h, wu_ref[0], preferred_element_type=F32)
    hid = (hid * ce).astype(BF16)
    out_ref[...] += jnp.dot(hid, wd_ref[0], preferred_element_type=F32)


def _moe(x2, nw, wr, br, wg, wu, wd, tm):
    t, d = x2.shape
    ne, _, de = wg.shape
    return pl.pallas_call(
        _moe_kernel,
        grid=(t // tm, ne),
        in_specs=[pl.BlockSpec((tm, d), lambda i, e: (i, 0)),
                  pl.BlockSpec((1, d), lambda i, e: (0, 0)),
                  pl.BlockSpec(wr.shape, lambda i, e: (0, 0)),
                  pl.BlockSpec(br.shape, lambda i, e: (0, 0)),
                  pl.BlockSpec((1, d, de), lambda i, e: (e, 0, 0)),
                  pl.BlockSpec((1, d, de), lambda i, e: (e, 0, 0)),
                  pl.BlockSpec((1, de, d), lambda i, e: (e, 0, 0))],
        out_specs=pl.BlockSpec((tm, d), lambda i, e: (i, 0)),
        out_shape=jax.ShapeDtypeStruct((t, d), F32),
        scratch_shapes=[pltpu.VMEM((tm, d), BF16), pltpu.VMEM((tm, LANES), F32)],
        compiler_params=_cparams(("parallel", "arbitrary")),
        name="moe",
    )(x2, nw, wr, br, wg, wu, wd)


def _pad_lanes(v, width):
    v = v.reshape(1, -1)
    return jnp.pad(v, ((0, 0), (0, width - v.shape[1])))


def _layer_weights(l, norm1_w, norm2_w, w_in, gdn_conv_w, gdn_a_log, gdn_dt_bias, gdn_norm_w,
                   swa_q_norm_w, swa_k_norm_w, swa_sinks, conv_dw_w, conv_dw_b, conv_ln_w, conv_ln_b,
                   w_branch, w_out, router_group_w, router_group_b, router_expert_w, router_expert_b,
                   moe_w_gate, moe_w_up, moe_w_down):
    d = w_in.shape[1]
    wi = w_in[l]
    o_ab = 4 * A_KW
    o_bq = o_ab + 2 * A_HEADS
    o_cu = o_bq + B_HEADS * B_HD + BKV_W
    o_gl = o_cu + 2 * C_CH
    w_perm = jnp.concatenate(
        [wi[:, :o_ab], wi[:, o_cu:o_gl], wi[:, o_gl:], wi[:, o_bq:o_cu], wi[:, o_ab:o_bq],
         jnp.zeros((d, LANES - 2 * A_HEADS), wi.dtype)], axis=1).astype(BF16)
    assert w_perm.shape[1] == PROJ_W
    wr = jnp.concatenate([router_expert_w[l], router_group_w[l],
                          jnp.zeros((d, LANES - N_EXPERTS - N_GROUPS), F32)], axis=1)
    br = _pad_lanes(jnp.concatenate([router_expert_b[l], router_group_b[l]]), LANES)
    return dict(
        n1=norm1_w[l].reshape(1, d), n2=norm2_w[l].reshape(1, d), w_in=w_perm,
        gdn_cw=gdn_conv_w[l], alog=_pad_lanes(gdn_a_log[l], LANES), dtb=_pad_lanes(gdn_dt_bias[l], LANES),
        gdn_nw=gdn_norm_w[l].reshape(1, -1),
        qnw=swa_q_norm_w[l].reshape(1, -1), knw=swa_k_norm_w[l].reshape(1, -1), sinks=swa_sinks[l].reshape(1, -1),
        qnw_heads=jnp.tile(swa_q_norm_w[l].reshape(1, -1), (1, B_HEADS)),
        knw_heads=jnp.tile(swa_k_norm_w[l].reshape(1, -1), (1, B_KV_HEADS)),
        dww=conv_dw_w[l], dwb=conv_dw_b[l].reshape(1, -1), lnw=conv_ln_w[l].reshape(1, -1),
        lnb=conv_ln_b[l].reshape(1, -1),
        wb=w_branch[l].astype(BF16), wo=w_out[l].astype(BF16), wr=wr, br=br,
        wg=moe_w_gate[l].astype(BF16), wu=moe_w_up[l].astype(BF16), wd=moe_w_down[l].astype(BF16))


def _tiles(n, l):
    prompt = l > BLOCK
    t = n * l
    if prompt:
        return dict(proj_tm=1536, tok_tm=512, moe_tm=768, conf_nb=1, conf_tm=384, conf_rc=64,
                    gdn_cpb=3, swa_bps=3, n_invalid=META_PAD)
    return dict(proj_tm=t, tok_tm=512, moe_tm=512, conf_nb=16, conf_tm=l, conf_rc=l,
                gdn_cpb=1, n_invalid=0)


def _layer(x3, lw, gdn_state, gdn_hist, conf_hist, swa_cache):
    n, l, d = x3.shape
    tl = _tiles(n, l)
    x2 = x3.reshape(n * l, d)
    p2 = _inproj(x2, lw['n1'], lw['w_in'], tl['proj_tm'], PROJ_W // 5)
    p3 = p2.reshape(n, l, PROJ_W)
    o_a, s_new = _gdn(p3, gdn_state, gdn_hist, lw['gdn_cw'], lw['alog'], lw['dtb'], lw['gdn_nw'],
                      tl['gdn_cpb'], tl['n_invalid'])
    gh_new = p3[:, l - (A_CONV - 1):, :QKV_W]
    if swa_cache is None:
        o_b, kvn = _swa_prompt(p3, lw['qnw_heads'], lw['knw_heads'], lw['sinks'], tl['swa_bps'])
        kv = lambda sl: kvn[:, sl].reshape(n, -1, 2, B_KV_HEADS, B_HD)
        swa_new = (kv(slice(META_PAD, BLOCK)), kv(slice(l - WINDOW, l)))
    else:
        win, meta = swa_cache
        o_b, nwin = _swa_sample(p3, win.reshape(n, win.shape[1], BKV_W), meta.reshape(n, N_META, BKV_W),
                                lw['qnw'], lw['knw'], lw['sinks'], 16)
        swa_new = nwin.reshape(win.shape)
    o_c, ch_new = _conformer(p3, conf_hist, lw['dww'], lw['dwb'], lw['lnw'], lw['lnb'],
                             tl['conf_nb'], tl['conf_tm'], tl['conf_rc'], tl['n_invalid'])
    t = n * l
    x2 = _merge(o_a.reshape(t, -1), o_b.reshape(t, -1), o_c.reshape(t, -1), p2, x2, lw['wb'], lw['wo'],
                tl['tok_tm'])
    x2 = _moe(x2, lw['n2'], lw['wr'], lw['br'], lw['wg'], lw['wu'], lw['wd'], tl['moe_tm'])
    ch_new = ch_new[:, CONF_HALO - (C_CONV - 1):]
    return x2.reshape(n, l, d), s_new, gh_new, swa_new, ch_new


def kernel(x_prompt, x_sample, state_gdn, cache_gdn_conv, cache_swa_kv, cache_meta_kv, cache_conv, meta_tokens, norm1_w, norm2_w, w_in, gdn_conv_w, gdn_a_log, gdn_dt_bias, gdn_norm_w, swa_q_norm_w, swa_k_norm_w, swa_sinks, conv_dw_w, conv_dw_b, conv_ln_w, conv_ln_b, w_branch, w_out, router_group_w, router_group_b, router_expert_w, router_expert_b, moe_w_gate, moe_w_up, moe_w_down):
    dtp = x_prompt.dtype
    n_p, _, d = x_prompt.shape
    n_s = x_sample.shape[0]
    depth = w_in.shape[0]
    xp = jnp.concatenate([jnp.zeros((n_p, META_PAD, d), dtp),
                          jnp.broadcast_to(meta_tokens.astype(dtp)[None], (n_p, N_META, d)),
                          x_prompt], axis=1)
    xs = x_sample
    outs_p, outs_s = [], []
    for l in range(depth):
        lw = _layer_weights(l, norm1_w, norm2_w, w_in, gdn_conv_w, gdn_a_log, gdn_dt_bias, gdn_norm_w,
                            swa_q_norm_w, swa_k_norm_w, swa_sinks, conv_dw_w, conv_dw_b, conv_ln_w, conv_ln_b,
                            w_branch, w_out, router_group_w, router_group_b, router_expert_w, router_expert_b,
                            moe_w_gate, moe_w_up, moe_w_down)
        xp, s_p, gh_p, (mkv_p, wkv_p), ch_p = _layer(
            xp, lw, jnp.zeros((n_p, A_HEADS, A_DK, A_DK), dtp), None,
            jnp.zeros((n_p, CONF_HALO, C_CH), dtp), None)
        outs_p.append((s_p, gh_p, wkv_p, mkv_p, ch_p))
        gh0 = jnp.pad(cache_gdn_conv[l], ((0, 0), (GDN_HALO - (A_CONV - 1), 0), (0, 0)))
        ch0 = jnp.pad(cache_conv[l], ((0, 0), (CONF_HALO - (C_CONV - 1), 0), (0, 0)))
        xs, s_s, gh_s, wkv_s, ch_s = _layer(xs, lw, state_gdn[l], gh0, ch0, (cache_swa_kv[l], cache_meta_kv[l]))
        outs_s.append((s_s, gh_s, wkv_s, ch_s))
    stack = lambda outs, i: jnp.stack([o[i] for o in outs])
    return (xp[:, BLOCK:], xs,
            stack(outs_p, 0), stack(outs_p, 1), stack(outs_p, 2), stack(outs_p, 3), stack(outs_p, 4),
            stack(outs_s, 0), stack(outs_s, 1), stack(outs_s, 2), stack(outs_s, 3))
```

```python
import functools

import jax
import jax.numpy as jnp
from jax import lax
from jax.experimental import pallas as pl
from jax.experimental.pallas import tpu as pltpu

F32 = jnp.float32
BF16 = jnp.bfloat16
HIGHEST = lax.Precision.HIGHEST
EPS = 1e-6
NEG = -1e30

VMEM_LIMIT_BYTES = 56 * 1024 * 1024
LANES = 128
SUBLANES = 8

PAST_LEN = 16384
N_META = 16
BLOCK = 128
WINDOW = 128
META_PAD = BLOCK - N_META
A_HEADS = 4
A_DK = 128
A_CONV = 4
B_HEADS = 8
B_KV_HEADS = 2
B_GROUP = B_HEADS // B_KV_HEADS
B_HD = 64
C_CH = 512
C_CONV = 31
BRANCH_W = 512
N_GROUPS = 4
EXP_PER_GROUP = 4
N_EXPERTS = N_GROUPS * EXP_PER_GROUP

A_KW = A_HEADS * A_DK
QKV_W = 3 * A_KW
Z_BLK = QKV_W // BRANCH_W
CU_BLK = 2
GL_BLK = 1
BQ_BLK = 12
BKV_W = 2 * B_KV_HEADS * B_HD
BKV_BLK = 26
AB_BLK = 54
PROJ_W = 7040
CONF_HALO = 32
GDN_HALO = SUBLANES


def _rms(x, w):
    return x * lax.rsqrt(jnp.mean(x * x, axis=-1, keepdims=True) + EPS) * w


def _silu(x):
    return x * jax.nn.sigmoid(x)


def _cparams(sem):
    return pltpu.CompilerParams(dimension_semantics=sem, vmem_limit_bytes=VMEM_LIMIT_BYTES)


def _inproj_kernel(x_ref, nw_ref, w_ref, o_ref, h_ref):
    @pl.when(pl.program_id(1) == 0)
    def _():
        h_ref[...] = _rms(x_ref[...], nw_ref[...]).astype(BF16)

    o_ref[...] = jnp.dot(h_ref[...], w_ref[...], preferred_element_type=F32)


def _inproj(x2, nw, w, tm, tn):
    t, d = x2.shape
    wd = w.shape[1]
    return pl.pallas_call(
        _inproj_kernel,
        grid=(t // tm, wd // tn),
        in_specs=[pl.BlockSpec((tm, d), lambda i, j: (i, 0)),
                  pl.BlockSpec((1, d), lambda i, j: (0, 0)),
                  pl.BlockSpec((d, tn), lambda i, j: (0, j))],
        out_specs=pl.BlockSpec((tm, tn), lambda i, j: (i, j)),
        out_shape=jax.ShapeDtypeStruct((t, wd), F32),
        scratch_shapes=[pltpu.VMEM((tm, d), BF16)],
        compiler_params=_cparams(("parallel", "arbitrary")),
        name="inproj",
    )(x2, nw, w)


def _conf_kernel(cu_ref, hist_ref, dww_ref, dwb_ref, lnw_ref, lnb_ref, y_ref, nh_ref, e_ref,
                 *, tm, rc, n_invalid):
    t = pl.program_id(1)
    nb = cu_ref.shape[0]
    halo = CONF_HALO

    @pl.when(t == 0)
    def _():
        e_ref[:, 0:halo, :] = hist_ref[...]
        e_ref[:, halo + tm:halo + tm + SUBLANES, :] = jnp.zeros((nb, SUBLANES, C_CH), F32)

    @pl.when(t > 0)
    def _():
        e_ref[:, 0:halo, :] = e_ref[:, tm:tm + halo, :]

    cu = cu_ref[...]
    u = cu[:, :, :C_CH] * jax.nn.sigmoid(cu[:, :, C_CH:])
    if n_invalid:
        row = lax.broadcasted_iota(jnp.int32, (1, tm, 1), 1) + t * tm
        u = jnp.where(row >= n_invalid, u, 0.0)
    e_ref[:, halo:halo + tm, :] = u
    nh_ref[...] = e_ref[:, tm:tm + halo, :]

    off = halo - (C_CONV - 1)
    for c in range(C_CH // LANES):
        cs = slice(c * LANES, (c + 1) * LANES)

        def rbody(r, carry, cs=cs):
            r0 = pl.multiple_of(r * rc, rc)
            blk = e_ref[:, pl.ds(r0, rc + halo + SUBLANES), cs]
            acc = jnp.zeros((nb, rc, LANES), F32)
            for s in range(SUBLANES):
                z = None
                for a in range((halo + SUBLANES) // SUBLANES):
                    k = SUBLANES * a + s - off
                    if 0 <= k < C_CONV:
                        term = dww_ref[k:k + 1, cs] * blk[:, SUBLANES * a:SUBLANES * a + rc + SUBLANES, :]
                        z = term if z is None else z + term
                acc = acc + z[:, s:s + rc, :]
            y_ref[:, pl.ds(r0, rc), cs] = acc + dwb_ref[:, cs]
            return carry

        lax.fori_loop(0, tm // rc, rbody, 0)

    y = y_ref[...]
    mu = jnp.mean(y, axis=-1, keepdims=True)
    var = jnp.mean(jnp.square(y - mu), axis=-1, keepdims=True)
    yn = (y - mu) * lax.rsqrt(var + EPS) * lnw_ref[...] + lnb_ref[...]
    y_ref[...] = _silu(yn)


def _conformer(p3, hist, dww, dwb, lnw, lnb, nb, tm, rc, n_invalid):
    n, l, _ = p3.shape
    kern = functools.partial(_conf_kernel, tm=tm, rc=rc, n_invalid=n_invalid)
    full = lambda shape: pl.BlockSpec(shape, lambda i, t: (0,) * len(shape))
    return pl.pallas_call(
        kern,
        grid=(n // nb, l // tm),
        in_specs=[pl.BlockSpec((nb, tm, 2 * C_CH), lambda i, t: (i, t, CU_BLK)),
                  pl.BlockSpec((nb, CONF_HALO, C_CH), lambda i, t: (i, 0, 0)),
                  full(dww.shape), full(dwb.shape), full(lnw.shape), full(lnb.shape)],
        out_specs=[pl.BlockSpec((nb, tm, C_CH), lambda i, t: (i, t, 0)),
                   pl.BlockSpec((nb, CONF_HALO, C_CH), lambda i, t: (i, 0, 0))],
        out_shape=[jax.ShapeDtypeStruct((n, l, C_CH), F32),
                   jax.ShapeDtypeStruct((n, CONF_HALO, C_CH), F32)],
        scratch_shapes=[pltpu.VMEM((nb, CONF_HALO + tm + SUBLANES, C_CH), F32)],
        compiler_params=_cparams(("parallel", "arbitrary")),
        name="conformer",
    )(p3, hist, dww, dwb, lnw, lnb)


def _softmax_sink_pv(s, mask, sink, v, dims):
    s = jnp.where(mask, s, NEG)
    m = jnp.maximum(jnp.max(s, axis=-1, keepdims=True), sink)
    p = jnp.exp(s - m)
    den = jnp.sum(p, axis=-1, keepdims=True) + jnp.exp(sink - m)
    return lax.dot_general(p, v, dims, preferred_element_type=F32) / den


def _swa_prompt_kernel(q_ref, kv_ref, kvp_ref, kvm_ref, qnw_ref, knw_ref, sink_ref, o_ref, kvn_ref, *, bps):
    step = pl.program_id(1)
    qnw = qnw_ref[...]
    knw = knw_ref[...]
    qw = B_HEADS * B_HD
    kw = B_KV_HEADS * B_HD
    nk = N_META + 2 * BLOCK
    rows = B_GROUP * BLOCK
    r = lax.broadcasted_iota(jnp.int32, (rows, nk), 0) % BLOCK
    c = lax.broadcasted_iota(jnp.int32, (rows, nk), 1)
    cp = c - N_META
    co = cp - BLOCK
    hrow = lax.broadcasted_iota(jnp.int32, (rows, 1), 0) // BLOCK
    m_meta = c < N_META
    m_own = (co >= 0) & (co <= r)
    m_prev = (cp >= 0) & (cp < BLOCK) & (cp > r)
    bias0 = jnp.where(m_meta & (META_PAD + c <= r), 0.0, NEG)
    bias1 = jnp.where(m_meta | m_own, 0.0, NEG)
    bias2 = jnp.where(m_meta | m_own | m_prev, 0.0, NEG)
    sinks = []
    for g in range(B_KV_HEADS):
        sink = jnp.zeros((rows, 1), F32)
        for i in range(B_GROUP):
            h = g * B_GROUP + i
            sink = jnp.where(hrow == i, sink_ref[0:1, h:h + 1], sink)
        sinks.append(sink)
    hb_q = (lax.broadcasted_iota(jnp.int32, (qw, qw), 0) // B_HD
            == lax.broadcasted_iota(jnp.int32, (qw, qw), 1) // B_HD).astype(BF16)
    hb_k = hb_q[:kw, :kw]
    half = lax.broadcasted_iota(jnp.int32, (1, LANES), 1) // B_HD

    def head_rms(x, hb, w):
        sq = x * x
        hi = sq.astype(BF16)
        lo = (sq - hi.astype(F32)).astype(BF16)
        ss = jnp.dot(hi, hb, preferred_element_type=F32) + jnp.dot(lo, hb, preferred_element_type=F32)
        return x * lax.rsqrt(ss * (1.0 / B_HD) + EPS) * w

    kvm = kvm_ref[META_PAD:BLOCK, :]
    km = head_rms(kvm[:, :kw], hb_k, knw)
    kv_prev = kvp_ref[...]
    k_prev = head_rms(kv_prev[:, :kw], hb_k, knw)
    for b in range(bps):
        j = step * bps + b
        rs = slice(b * BLOCK, (b + 1) * BLOCK)
        q = q_ref[rs, :]
        kv = kv_ref[rs, :]
        k_own = head_rms(kv[:, :kw], hb_k, knw)
        kvn_ref[rs, :] = jnp.concatenate([k_own, kv[:, kw:]], axis=-1)
        qn = head_rms(q, hb_q, qnw)
        bias = jnp.where(j >= 2, bias2, jnp.where(j >= 1, bias1, bias0))
        kcat = jnp.concatenate([km, k_prev, k_own], axis=0)
        vcat = jnp.concatenate([kvm[:, kw:], kv_prev[:, kw:], kv[:, kw:]], axis=0)
        kcat_b = kcat.astype(BF16)
        vcat_b = vcat.astype(BF16)
        tiles = [qn[:, m * LANES:(m + 1) * LANES] for m in range(qw // LANES)]
        swapped = [pltpu.roll(tl_, B_HD, 1) for tl_ in tiles]
        outs = [None] * B_HEADS
        for g in range(B_KV_HEADS):
            in_g = half == g
            parts = []
            for i in range(B_GROUP):
                h = g * B_GROUP + i
                src = tiles[h // 2] if h % 2 == g else swapped[h // 2]
                parts.append(jnp.where(in_g, src, 0.0))
            lhs = jnp.concatenate(parts, axis=0)
            s = lax.dot_general(lhs.astype(BF16), kcat_b, (((1,), (1,)), ((), ())), preferred_element_type=F32)
            s = s * (B_HD ** -0.5) + bias
            m = jnp.maximum(jnp.max(s, axis=-1, keepdims=True), sinks[g])
            p = jnp.exp(s - m)
            den = jnp.sum(p, axis=-1, keepdims=True) + jnp.exp(sinks[g] - m)
            res = jnp.dot(p.astype(BF16), vcat_b, preferred_element_type=F32) / den
            for i in range(B_GROUP):
                h = g * B_GROUP + i
                part = res[i * BLOCK:(i + 1) * BLOCK, :]
                outs[h] = part if h % 2 == g else pltpu.roll(part, B_HD, 1)
        for m in range(qw // LANES):
            o_ref[rs, m * LANES:(m + 1) * LANES] = jnp.where(half == 0, outs[2 * m], outs[2 * m + 1])
        kv_prev, k_prev = kv, k_own


def _swa_prompt(p3, qnw, knw, sinks, bps):
    n, l, _ = p3.shape
    full = lambda shape: pl.BlockSpec(shape, lambda i, j: (0,) * len(shape))
    tq = bps * BLOCK
    return pl.pallas_call(
        functools.partial(_swa_prompt_kernel, bps=bps),
        grid=(n, l // tq),
        in_specs=[pl.BlockSpec((None, tq, B_HEADS * B_HD), lambda i, j: (i, j, BQ_BLK)),
                  pl.BlockSpec((None, tq, BKV_W), lambda i, j: (i, j, BKV_BLK)),
                  pl.BlockSpec((None, BLOCK, BKV_W), lambda i, j: (i, jnp.maximum(j * bps - 1, 0), BKV_BLK)),
                  pl.BlockSpec((None, BLOCK, BKV_W), lambda i, j: (i, 0, BKV_BLK)),
                  full(qnw.shape), full(knw.shape), full(sinks.shape)],
        out_specs=[pl.BlockSpec((None, tq, B_HEADS * B_HD), lambda i, j: (i, j, 0)),
                   pl.BlockSpec((None, tq, BKV_W), lambda i, j: (i, j, 0))],
        out_shape=[jax.ShapeDtypeStruct((n, l, B_HEADS * B_HD), F32),
                   jax.ShapeDtypeStruct((n, l, BKV_W), F32)],
        compiler_params=_cparams(("parallel", "arbitrary")),
        name="swa_prompt",
    )(p3, p3, p3, p3, qnw, knw, sinks)


def _swa_sample_kernel(q_ref, kvn_ref, win_ref, meta_ref, qnw_ref, knw_ref, sink_ref, o_ref, nwin_ref):
    q = q_ref[...]
    kvn = kvn_ref[...]
    win = win_ref[...]
    meta = meta_ref[...]
    qnw = qnw_ref[...]
    knw = knw_ref[...]
    t_new = q.shape[1]
    w = win.shape[1]
    nk = N_META + w + t_new
    rows = B_GROUP * t_new
    tq = lax.broadcasted_iota(jnp.int32, (1, rows, nk), 1) % t_new
    c = lax.broadcasted_iota(jnp.int32, (1, rows, nk), 2)
    cw = c - N_META
    win_ok = (cw >= 0) & (cw < w) & (cw - w > tq - WINDOW) & (cw + (PAST_LEN - w) >= N_META)
    cn = cw - w
    new_ok = (cn >= 0) & (cn <= tq) & (cn > tq - WINDOW)
    mask = (c < N_META) | win_ok | new_ok
    hrow = lax.broadcasted_iota(jnp.int32, (1, rows, 1), 1) // t_new
    kn_parts = []
    for g in range(B_KV_HEADS):
        ks = slice(g * B_HD, (g + 1) * B_HD)
        vs = slice((B_KV_HEADS + g) * B_HD, (B_KV_HEADS + g + 1) * B_HD)
        k_new = _rms(kvn[:, :, ks], knw)
        kn_parts.append(k_new)
        kk = jnp.concatenate([meta[:, :, ks], win[:, :, ks], k_new], axis=1)
        vv = jnp.concatenate([meta[:, :, vs], win[:, :, vs], kvn[:, :, vs]], axis=1)
        qq = jnp.concatenate(
            [_rms(q[:, :, (g * B_GROUP + i) * B_HD:(g * B_GROUP + i + 1) * B_HD], qnw) for i in range(B_GROUP)],
            axis=1)
        sink = jnp.zeros((1, rows, 1), F32)
        for i in range(B_GROUP):
            h = g * B_GROUP + i
            sink = jnp.where(hrow == i, sink_ref[0:1, h:h + 1], sink)
        s = jnp.einsum('bqd,bkd->bqk', qq, kk, preferred_element_type=F32) * (B_HD ** -0.5)
        o = _softmax_sink_pv(s, mask, sink, vv, (((2,), (1,)), ((0,), (0,))))
        for i in range(B_GROUP):
            h = g * B_GROUP + i
            o_ref[:, :, h * B_HD:(h + 1) * B_HD] = o[:, i * t_new:(i + 1) * t_new, :]
    nwin_ref[:, 0:w - t_new, :] = win[:, t_new:w, :]
    nwin_ref[:, w - t_new:w, :] = jnp.concatenate(kn_parts + [kvn[:, :, B_KV_HEADS * B_HD:]], axis=-1)


def _swa_sample(p3, win, meta, qnw, knw, sinks, nb):
    n, t_new, _ = p3.shape
    w = win.shape[1]
    full = lambda shape: pl.BlockSpec(shape, lambda i: (0,) * len(shape))
    return pl.pallas_call(
        _swa_sample_kernel,
        grid=(n // nb,),
        in_specs=[pl.BlockSpec((nb, t_new, B_HEADS * B_HD), lambda i: (i, 0, BQ_BLK)),
                  pl.BlockSpec((nb, t_new, BKV_W), lambda i: (i, 0, BKV_BLK)),
                  pl.BlockSpec((nb, w, BKV_W), lambda i: (i, 0, 0)),
                  pl.BlockSpec((nb, N_META, BKV_W), lambda i: (i, 0, 0)),
                  full(qnw.shape), full(knw.shape), full(sinks.shape)],
        out_specs=[pl.BlockSpec((nb, t_new, B_HEADS * B_HD), lambda i: (i, 0, 0)),
                   pl.BlockSpec((nb, w, BKV_W), lambda i: (i, 0, 0))],
        out_shape=[jax.ShapeDtypeStruct((n, t_new, B_HEADS * B_HD), F32),
                   jax.ShapeDtypeStruct((n, w, BKV_W), F32)],
        compiler_params=_cparams(("parallel",)),
        name="swa_sample",
    )(p3, p3, win, meta, qnw, knw, sinks)


GDN_ROWS = 64
GDN_R = A_HEADS * GDN_ROWS
GDN_PAIRS = A_HEADS // 2
PAIR_W = 2 * A_DK


def _mm(a, b):
    return jnp.dot(a.astype(BF16), b.astype(BF16), preferred_element_type=F32)


def _block_diag2(x):
    z = jnp.zeros((x.shape[0], A_DK), x.dtype)
    return jnp.concatenate([jnp.concatenate([x[:, :A_DK], z], axis=1),
                            jnp.concatenate([z, x[:, A_DK:]], axis=1)], axis=0)


def _gdn_prep_kernel(x_ref, halo_ref, ab_ref, cw_ref, alog_ref, dtb_ref, u_ref, wq_ref, qkk_ref, gl_ref,
                     *, sb, cpb, n_invalid, seq_halo):
    t = pl.program_id(1)
    ns = x_ref.shape[0]
    rows = x_ref.shape[1] // cpb
    c = GDN_ROWS
    r = GDN_R
    halo = halo_ref[...]
    if seq_halo:
        hrow = lax.broadcasted_iota(jnp.int32, (1, GDN_HALO, 1), 1) + (t * cpb * rows - GDN_HALO)
        halo = jnp.where(hrow >= n_invalid, halo, 0.0)
    cw = cw_ref[...]
    alog = alog_ref[...]
    dtb = dtb_ref[...]
    ri = lax.broadcasted_iota(jnp.int32, (r, r), 0)
    ci = lax.broadcasted_iota(jnp.int32, (r, r), 1)
    same = (ri // sb) == (ci // sb)
    incl = same & (ri >= ci)
    strict = same & (ri > ci)
    eye = (ri == ci).astype(F32)
    rc_ = lax.broadcasted_iota(jnp.int32, (c, c), 0)
    cc_ = lax.broadcasted_iota(jnp.int32, (c, c), 1)
    same_c = (rc_ // sb) == (cc_ // sb)
    lmat = jnp.concatenate([(same_c & (rc_ >= cc_)).astype(F32), same_c.astype(F32)], axis=0)
    sel = (lax.broadcasted_iota(jnp.int32, (SUBLANES, 2 * c), 1)
           == c + (lax.broadcasted_iota(jnp.int32, (SUBLANES, 2 * c), 0) * SUBLANES // sb) * sb).astype(F32)
    off = GDN_HALO - (A_CONV - 1)
    for cc in range(cpb):
        x = x_ref[:, cc * rows:(cc + 1) * rows, :]
        ab = ab_ref[:, cc * rows:(cc + 1) * rows, :].reshape(c, LANES)
        if n_invalid:
            row0 = (t * cpb + cc) * rows
            xrow = lax.broadcasted_iota(jnp.int32, (1, rows, 1), 1) + row0
            x = jnp.where(xrow >= n_invalid, x, 0.0)
        ext = jnp.concatenate([halo, x], axis=1)
        acc = cw[0:1, :] * ext[:, off:off + rows, :]
        for k in range(1, A_CONV):
            acc = acc + cw[k:k + 1, :] * ext[:, off + k:off + k + rows, :]
        y = _silu(acc).reshape(c, QKV_W)
        halo = x[:, rows - GDN_HALO:rows, :]

        sp = jnp.maximum(ab + dtb, 0.0) + jnp.log1p(jnp.exp(-jnp.abs(ab + dtb)))
        gfull = -jnp.exp(alog) * sp
        bfull = jax.nn.sigmoid(ab)
        if n_invalid:
            valid = (lax.broadcasted_iota(jnp.int32, (c, 1), 0) + row0) >= n_invalid
            gfull = jnp.where(valid, gfull, 0.0)
            bfull = jnp.where(valid, bfull, 0.0)
        bcol = jnp.concatenate([bfull[:, A_HEADS + h:A_HEADS + h + 1] for h in range(A_HEADS)], axis=0)
        gg = jnp.dot(lmat, gfull, precision=HIGHEST, preferred_element_type=F32)
        ggt = gg.T
        gc = jnp.concatenate([gg[:c, h:h + 1] for h in range(A_HEADS)], axis=0)
        grow = jnp.concatenate([ggt[h:h + 1, :c] for h in range(A_HEADS)], axis=1)
        drow = jnp.concatenate([ggt[h:h + 1, c:] for h in range(A_HEADS)], axis=1) - grow
        g8 = jnp.exp(jnp.dot(sel, gg, precision=HIGHEST, preferred_element_type=F32))
        decay = jnp.where(incl, jnp.exp(jnp.where(incl, gc - grow, 0.0)), 0.0)

        qs, ks, vs = [], [], []
        for h in range(A_HEADS):
            q = y[:, h * A_DK:(h + 1) * A_DK]
            k = y[:, A_KW + h * A_DK:A_KW + (h + 1) * A_DK]
            qs.append(q * lax.rsqrt(jnp.sum(q * q, axis=-1, keepdims=True) + EPS) * (A_DK ** -0.5))
            ks.append(k * lax.rsqrt(jnp.sum(k * k, axis=-1, keepdims=True) + EPS))
            vs.append(y[:, 2 * A_KW + h * A_DK:2 * A_KW + (h + 1) * A_DK])
        qr = jnp.concatenate(qs, axis=0)
        kr = jnp.concatenate(ks, axis=0)
        vr = jnp.concatenate(vs, axis=0)
        krt = kr.T
        a = jnp.where(strict, _mm(kr, krt) * decay * bcol, 0.0)
        tinv = eye - a
        ak = a
        kpow = 2
        while kpow < sb:
            ak = _mm(ak, ak)
            tinv = tinv + _mm(tinv, ak)
            kpow *= 2
        eg = jnp.exp(gc)
        ur = _mm(tinv, vr * bcol)
        wr = _mm(tinv, kr * (bcol * eg))
        qk = _mm(qr, krt) * decay
        qd = qr * eg
        kdt = krt * jnp.exp(drow)
        for p in range(GDN_PAIRS):
            r0 = slice(2 * p * c, (2 * p + 1) * c)
            r1 = slice((2 * p + 1) * c, (2 * p + 2) * c)
            pl_ = slice(2 * p * c, (2 * p + 2) * c)
            u_ref[cc, p] = jnp.concatenate([ur[r0], ur[r1]], axis=1)
            wq = jnp.concatenate([jnp.concatenate([wr[r0], wr[r1]], axis=1),
                                  jnp.concatenate([qd[r0], qd[r1]], axis=1)], axis=0)
            wq_ref[cc, p] = wq.astype(wq_ref.dtype)
            qkk = jnp.concatenate([qk[r0, pl_] + qk[r1, pl_], kdt[:, pl_]], axis=0)
            qkk_ref[cc, p] = qkk.astype(qkk_ref.dtype)
            gl_ref[cc, p] = jnp.concatenate(
                [jnp.broadcast_to(g8[:, 2 * p:2 * p + 1], (SUBLANES, A_DK)),
                 jnp.broadcast_to(g8[:, 2 * p + 1:2 * p + 2], (SUBLANES, A_DK))], axis=1)


def _gdn_prep(x_arr, halo_arr, halo_spec, cw, alog, dtb, ns, rows, cpb, sb, n_invalid, op_dtype):
    n, l, _ = x_arr.shape
    grid = (n // ns, l // (cpb * rows))
    units = grid[0] * grid[1] * cpb
    kern = functools.partial(_gdn_prep_kernel, sb=sb, cpb=cpb, n_invalid=n_invalid,
                             seq_halo=halo_arr is x_arr)
    full = lambda shape: pl.BlockSpec(shape, lambda i, t: (0,) * len(shape))
    steps = grid[1]
    out_map = lambda i, t: (i * steps + t, 0, 0, 0)
    c = GDN_ROWS
    return pl.pallas_call(
        kern,
        grid=grid,
        in_specs=[pl.BlockSpec((ns, cpb * rows, QKV_W), lambda i, t: (i, t, 0)),
                  halo_spec,
                  pl.BlockSpec((ns, cpb * rows, LANES), lambda i, t: (i, t, AB_BLK)),
                  full(cw.shape), full(alog.shape), full(dtb.shape)],
        out_specs=[pl.BlockSpec((cpb, GDN_PAIRS, c, PAIR_W), out_map),
                   pl.BlockSpec((cpb, GDN_PAIRS, 2 * c, PAIR_W), out_map),
                   pl.BlockSpec((cpb, GDN_PAIRS, c + A_DK, 2 * c), out_map),
                   pl.BlockSpec((cpb, GDN_PAIRS, SUBLANES, PAIR_W), out_map)],
        out_shape=[jax.ShapeDtypeStruct((units, GDN_PAIRS, c, PAIR_W), F32),
                   jax.ShapeDtypeStruct((units, GDN_PAIRS, 2 * c, PAIR_W), op_dtype),
                   jax.ShapeDtypeStruct((units, GDN_PAIRS, c + A_DK, 2 * c), op_dtype),
                   jax.ShapeDtypeStruct((units, GDN_PAIRS, SUBLANES, PAIR_W), F32)],
        compiler_params=_cparams(("parallel", "arbitrary")),
        name="gdn_prep",
    )(x_arr, halo_arr, x_arr, cw, alog, dtb)


def _gdn_out(o_pair, z_ref, o_ref, nw, idx, p):
    for i in range(2):
        h = 2 * p + i
        hs = slice(h * A_DK, (h + 1) * A_DK)
        o_ref[idx + (hs,)] = _rms(o_pair[..., i * A_DK:(i + 1) * A_DK], nw) * _silu(z_ref[idx + (hs,)])


def _gdn_scan_kernel(u_ref, wq_ref, qkk_ref, gl_ref, z_ref, s0_ref, nw_ref, o_ref, sout_ref, s_ref):
    t = pl.program_id(0)
    n_seq = u_ref.shape[0]
    c = GDN_ROWS
    nw = nw_ref[...]

    @pl.when(t == 0)
    def _():
        for n in range(n_seq):
            for p in range(GDN_PAIRS):
                s_ref[n, p] = jnp.concatenate([s0_ref[n, 2 * p], s0_ref[n, 2 * p + 1]], axis=1)

    for n in range(n_seq):
        for p in range(GDN_PAIRS):
            s = s_ref[n, p]
            r1 = jnp.dot(wq_ref[n, 0, p], _block_diag2(s.astype(BF16)), preferred_element_type=F32)
            vnew = u_ref[n, 0, p] - r1[:c]
            r2 = jnp.dot(qkk_ref[n, 0, p], _block_diag2(vnew.astype(BF16)), preferred_element_type=F32)
            s_ref[n, p] = s * gl_ref[n, 0, p][0:1, :] + r2[c:]
            _gdn_out(r1[c:] + r2[:c], z_ref, o_ref, nw, (n, slice(None)), p)

    @pl.when(t == pl.num_programs(0) - 1)
    def _():
        for n in range(n_seq):
            for p in range(GDN_PAIRS):
                s = s_ref[n, p]
                sout_ref[n, 2 * p] = s[:, :A_DK]
                sout_ref[n, 2 * p + 1] = s[:, A_DK:]


def _gdn_scan(u, wq, qkk, gl, p3, s0, nw):
    n, l, _ = p3.shape
    c = GDN_ROWS
    steps = l // c
    v5 = lambda a: a.reshape((n, steps) + a.shape[1:])
    u, wq, qkk, gl = v5(u), v5(wq), v5(qkk), v5(gl)
    unit = lambda a: pl.BlockSpec((n, 1) + a.shape[2:], lambda t: (0, t, 0, 0, 0))
    vw = A_HEADS * A_DK
    return pl.pallas_call(
        _gdn_scan_kernel,
        grid=(steps,),
        in_specs=[unit(u), unit(wq), unit(qkk), unit(gl),
                  pl.BlockSpec((n, c, vw), lambda t: (0, t, Z_BLK)),
                  pl.BlockSpec(s0.shape, lambda t: (0, 0, 0, 0)),
                  pl.BlockSpec(nw.shape, lambda t: (0, 0))],
        out_specs=[pl.BlockSpec((n, c, vw), lambda t: (0, t, 0)),
                   pl.BlockSpec(s0.shape, lambda t: (0, 0, 0, 0))],
        out_shape=[jax.ShapeDtypeStruct((n, l, vw), F32),
                   jax.ShapeDtypeStruct(s0.shape, F32)],
        scratch_shapes=[pltpu.VMEM((n, GDN_PAIRS, A_DK, PAIR_W), F32)],
        compiler_params=_cparams(("arbitrary",)),
        name="gdn_scan",
    )(u, wq, qkk, gl, p3, s0, nw)


def _gdn_step_kernel(u_ref, wq_ref, qkk_ref, gl_ref, z_ref, s0_ref, nw_ref, o_ref, sout_ref):
    ns, rows, _ = z_ref.shape
    c = GDN_ROWS
    nw = nw_ref[...]
    lane_seq = (lax.broadcasted_iota(jnp.int32, (A_DK, 2 * c), 1) % c) // rows
    for p in range(GDN_PAIRS):
        wq = wq_ref[0, p]
        qkk = qkk_ref[0, p]
        gl = gl_ref[0, p]
        ws, qs, states = [], [], []
        for s_i in range(ns):
            sl = slice(s_i * rows, (s_i + 1) * rows)
            s = jnp.concatenate([s0_ref[s_i, 2 * p], s0_ref[s_i, 2 * p + 1]], axis=1)
            states.append(s)
            lhs = jnp.concatenate([wq[sl], wq[c + s_i * rows:c + (s_i + 1) * rows]], axis=0)
            r1 = _mm(lhs, _block_diag2(s.astype(BF16)))
            ws.append(r1[:rows])
            qs.append(r1[rows:])
        vnew = u_ref[0, p] - jnp.concatenate(ws, axis=0)
        bdv = _block_diag2(vnew.astype(BF16))
        o = jnp.concatenate(qs, axis=0) + _mm(qkk[:c], bdv)
        _gdn_out(o.reshape(ns, rows, PAIR_W), z_ref, o_ref, nw, (slice(None), slice(None)), p)
        kdt = qkk[c:]
        lhs = jnp.concatenate([jnp.where(lane_seq == s_i, kdt, 0.0) for s_i in range(ns)], axis=0)
        upd = _mm(lhs, bdv)
        for s_i in range(ns):
            s_new = states[s_i] * gl[s_i:s_i + 1, :] + upd[s_i * A_DK:(s_i + 1) * A_DK]
            sout_ref[s_i, 2 * p] = s_new[:, :A_DK]
            sout_ref[s_i, 2 * p + 1] = s_new[:, A_DK:]


def _gdn_step(u, wq, qkk, gl, p3, s0, nw, ns):
    n, rows, _ = p3.shape
    unit = lambda a: pl.BlockSpec((1,) + a.shape[1:], lambda i: (i, 0, 0, 0))
    vw = A_HEADS * A_DK
    sblk = (ns,) + s0.shape[1:]
    return pl.pallas_call(
        _gdn_step_kernel,
        grid=(n // ns,),
        in_specs=[unit(u), unit(wq), unit(qkk), unit(gl),
                  pl.BlockSpec((ns, rows, vw), lambda i: (i, 0, Z_BLK)),
                  pl.BlockSpec(sblk, lambda i: (i, 0, 0, 0)),
                  pl.BlockSpec(nw.shape, lambda i: (0, 0))],
        out_specs=[pl.BlockSpec((ns, rows, vw), lambda i: (i, 0, 0)),
                   pl.BlockSpec(sblk, lambda i: (i, 0, 0, 0))],
        out_shape=[jax.ShapeDtypeStruct((n, rows, vw), F32),
                   jax.ShapeDtypeStruct(s0.shape, F32)],
        compiler_params=_cparams(("parallel",)),
        name="gdn_step",
    )(u, wq, qkk, gl, p3, s0, nw)


def _gdn(p3, s0, hist, cw, alog, dtb, nw, cpb, n_invalid):
    n, l, _ = p3.shape
    c = GDN_ROWS
    if hist is None:
        steps8 = cpb * c // GDN_HALO
        halo_spec = pl.BlockSpec((1, GDN_HALO, QKV_W), lambda i, t: (i, jnp.maximum(t * steps8 - 1, 0), 0))
        u, wq, qkk, gl = _gdn_prep(p3, p3, halo_spec, cw, alog, dtb, 1, c, cpb, c, n_invalid, BF16)
        return _gdn_scan(u, wq, qkk, gl, p3, s0, nw)
    ns = c // l
    halo_spec = pl.BlockSpec((ns, GDN_HALO, QKV_W), lambda i, t: (i, 0, 0))
    u, wq, qkk, gl = _gdn_prep(p3, hist, halo_spec, cw, alog, dtb, ns, l, 1, l, n_invalid, F32)
    return _gdn_step(u, wq, qkk, gl, p3, s0, nw, ns)


def _merge_kernel(oa_ref, ob_ref, oc_ref, gl_ref, x_ref, wb_ref, wo_ref, out_ref):
    d = x_ref.shape[1]
    merged = None
    for r, o_ref in enumerate((oa_ref, ob_ref, oc_ref)):
        pb = jnp.dot(o_ref[...].astype(BF16), wb_ref[r], preferred_element_type=F32)
        term = jax.nn.sigmoid(gl_ref[:, r * d:(r + 1) * d]) * pb
        merged = term if merged is None else merged + term
    out_ref[...] = x_ref[...] + jnp.dot(merged.astype(BF16), wo_ref[...], preferred_element_type=F32)


def _merge(oa, ob, oc, p2, x2, wb, wo, tm):
    t, d = x2.shape
    bw = oa.shape[1]
    row = lambda w, blk: pl.BlockSpec((tm, w), lambda i: (i, blk))
    return pl.pallas_call(
        _merge_kernel,
        grid=(t // tm,),
        in_specs=[row(bw, 0), row(bw, 0), row(bw, 0), row(3 * d, GL_BLK), row(d, 0),
                  pl.BlockSpec(wb.shape, lambda i: (0, 0, 0)),
                  pl.BlockSpec(wo.shape, lambda i: (0, 0))],
        out_specs=row(d, 0),
        out_shape=jax.ShapeDtypeStruct((t, d), F32),
        compiler_params=_cparams(("parallel",)),
        name="merge",
    )(oa, ob, oc, p2, x2, wb, wo)


def _route(logits):
    lane = lax.broadcasted_iota(jnp.int32, logits.shape, 1)
    is_g = (lane >= N_EXPERTS) & (lane < N_EXPERTS + N_GROUPS)
    gl = jnp.where(is_g, logits, NEG)
    gmax = jnp.max(gl, axis=-1, keepdims=True)
    gidx = jnp.min(jnp.where(gl == gmax, lane - N_EXPERTS, LANES), axis=-1, keepdims=True)
    gw = 1.0 / jnp.sum(jnp.where(is_g, jnp.exp(gl - gmax), 0.0), axis=-1, keepdims=True)
    in_grp = (lane < N_EXPERTS) & ((lane // EXP_PER_GROUP) == gidx)
    el = jnp.where(in_grp, logits, NEG)
    v1 = jnp.max(el, axis=-1, keepdims=True)
    i1 = jnp.min(jnp.where(el == v1, lane, LANES), axis=-1, keepdims=True)
    el2 = jnp.where(lane == i1, NEG, el)
    v2 = jnp.max(el2, axis=-1, keepdims=True)
    i2 = jnp.min(jnp.where(el2 == v2, lane, LANES), axis=-1, keepdims=True)
    e21 = jnp.exp(v2 - v1)
    w1 = gw / (1.0 + e21)
    w2 = gw * e21 / (1.0 + e21)
    return jnp.where(lane == i1, w1, 0.0) + jnp.where(lane == i2, w2, 0.0)


def _moe_kernel(x_ref, nw_ref, wr_ref, br_ref, wgu_ref, wd_ref, out_ref, h_ref, cmb_ref):
    g = pl.program_id(1)

    @pl.when(g == 0)
    def _():
        x = x_ref[...]
        h = _rms(x, nw_ref[...])
        h_hi = h.astype(BF16)
        h_ref[...] = h_hi
        h_lo = (h - h_hi.astype(F32)).astype(BF16)
        l2 = jnp.dot(h_hi, wr_ref[...], preferred_element_type=F32)
        logits = (l2[:, :LANES] + l2[:, LANES:]
                  + jnp.dot(h_lo, wr_ref[:, :LANES], preferred_element_type=F32) + br_ref[...])
        cmb_ref[...] = _route(logits)
        out_ref[...] = x

    gw = wd_ref.shape[1]
    de = gw // EXP_PER_GROUP
    gu = jnp.dot(h_ref[...], wgu_ref[0], preferred_element_type=F32)
    hid = _silu(gu[:, :gw]) * gu[:, gw:]
    cmb = cmb_ref[...]
    lane = lax.broadcasted_iota(jnp.int32, cmb.shape, 1)
    parts = []
    for e in range(EXP_PER_GROUP):
        ce = jnp.sum(jnp.where(lane == g * EXP_PER_GROUP + e, cmb, 0.0), axis=-1, keepdims=True)
        parts.append((hid[:, e * de:(e + 1) * de] * ce).astype(BF16))
    out_ref[...] += jnp.dot(jnp.concatenate(parts, axis=1), wd_ref[0], preferred_element_type=F32)


def _moe(x2, nw, wr, br, wgu, wd, tm):
    t, d = x2.shape
    ng, _, gw2 = wgu.shape
    return pl.pallas_call(
        _moe_kernel,
        grid=(t // tm, ng),
        in_specs=[pl.BlockSpec((tm, d), lambda i, g: (i, 0)),
                  pl.BlockSpec((1, d), lambda i, g: (0, 0)),
                  pl.BlockSpec(wr.shape, lambda i, g: (0, 0)),
                  pl.BlockSpec(br.shape, lambda i, g: (0, 0)),
                  pl.BlockSpec((1, d, gw2), lambda i, g: (g, 0, 0)),
                  pl.BlockSpec((1, gw2 // 2, d), lambda i, g: (g, 0, 0))],
        out_specs=pl.BlockSpec((tm, d), lambda i, g: (i, 0)),
        out_shape=jax.ShapeDtypeStruct((t, d), F32),
        scratch_shapes=[pltpu.VMEM((tm, d), BF16), pltpu.VMEM((tm, LANES), F32)],
        compiler_params=_cparams(("parallel", "arbitrary")),
        name="moe",
    )(x2, nw, wr, br, wgu, wd)


def _pad_lanes(v, width):
    v = v.reshape(1, -1)
    return jnp.pad(v, ((0, 0), (0, width - v.shape[1])))


def _layer_weights(l, norm1_w, norm2_w, w_in, gdn_conv_w, gdn_a_log, gdn_dt_bias, gdn_norm_w,
                   swa_q_norm_w, swa_k_norm_w, swa_sinks, conv_dw_w, conv_dw_b, conv_ln_w, conv_ln_b,
                   w_branch, w_out, router_group_w, router_group_b, router_expert_w, router_expert_b,
                   moe_w_gate, moe_w_up, moe_w_down):
    d = w_in.shape[1]
    wi = w_in[l]
    o_ab = 4 * A_KW
    o_bq = o_ab + 2 * A_HEADS
    o_cu = o_bq + B_HEADS * B_HD + BKV_W
    o_gl = o_cu + 2 * C_CH
    w_perm = jnp.concatenate(
        [wi[:, :o_ab], wi[:, o_cu:o_gl], wi[:, o_gl:], wi[:, o_bq:o_cu], wi[:, o_ab:o_bq],
         jnp.zeros((d, LANES - 2 * A_HEADS), wi.dtype)], axis=1).astype(BF16)
    assert w_perm.shape[1] == PROJ_W
    wr = jnp.concatenate([router_expert_w[l], router_group_w[l],
                          jnp.zeros((d, LANES - N_EXPERTS - N_GROUPS), F32)], axis=1)
    br = _pad_lanes(jnp.concatenate([router_expert_b[l], router_group_b[l]]), LANES)
    wr_hi = wr.astype(BF16)
    wr = jnp.concatenate([wr_hi, (wr - wr_hi.astype(F32)).astype(BF16)], axis=1)
    return dict(
        n1=norm1_w[l].reshape(1, d), n2=norm2_w[l].reshape(1, d), w_in=w_perm,
        gdn_cw=gdn_conv_w[l], alog=_pad_lanes(gdn_a_log[l], LANES), dtb=_pad_lanes(gdn_dt_bias[l], LANES),
        gdn_nw=gdn_norm_w[l].reshape(1, -1),
        qnw=swa_q_norm_w[l].reshape(1, -1), knw=swa_k_norm_w[l].reshape(1, -1), sinks=swa_sinks[l].reshape(1, -1),
        qnw_heads=jnp.tile(swa_q_norm_w[l].reshape(1, -1), (1, B_HEADS)),
        knw_heads=jnp.tile(swa_k_norm_w[l].reshape(1, -1), (1, B_KV_HEADS)),
        dww=conv_dw_w[l], dwb=conv_dw_b[l].reshape(1, -1), lnw=conv_ln_w[l].reshape(1, -1),
        lnb=conv_ln_b[l].reshape(1, -1),
        wb=w_branch[l].astype(BF16), wo=w_out[l].astype(BF16), wr=wr, br=br,
        wgu=jnp.concatenate([_group_cols(moe_w_gate[l]), _group_cols(moe_w_up[l])], axis=2),
        wd=moe_w_down[l].astype(BF16).reshape(N_GROUPS, -1, d))


def _group_cols(w):
    ne, d, f = w.shape
    w = w.astype(BF16).reshape(N_GROUPS, EXP_PER_GROUP, d, f)
    return jnp.transpose(w, (0, 2, 1, 3)).reshape(N_GROUPS, d, EXP_PER_GROUP * f)


def _tiles(n, l):
    prompt = l > BLOCK
    t = n * l
    if prompt:
        return dict(proj_tm=1536, tok_tm=512, moe_tm=768, conf_nb=1, conf_tm=384, conf_rc=64,
                    gdn_cpb=6, swa_bps=3, n_invalid=META_PAD)
    return dict(proj_tm=t, tok_tm=512, moe_tm=512, conf_nb=16, conf_tm=l, conf_rc=l,
                gdn_cpb=1, n_invalid=0)


def _layer(x3, lw, gdn_state, gdn_hist, conf_hist, swa_cache):
    n, l, d = x3.shape
    tl = _tiles(n, l)
    x2 = x3.reshape(n * l, d)
    p2 = _inproj(x2, lw['n1'], lw['w_in'], tl['proj_tm'], PROJ_W // 5)
    p3 = p2.reshape(n, l, PROJ_W)
    o_a, s_new = _gdn(p3, gdn_state, gdn_hist, lw['gdn_cw'], lw['alog'], lw['dtb'], lw['gdn_nw'],
                      tl['gdn_cpb'], tl['n_invalid'])
    gh_new = p3[:, l - (A_CONV - 1):, :QKV_W]
    if swa_cache is None:
        o_b, kvn = _swa_prompt(p3, lw['qnw_heads'], lw['knw_heads'], lw['sinks'], tl['swa_bps'])
        kv = lambda sl: kvn[:, sl].reshape(n, -1, 2, B_KV_HEADS, B_HD)
        swa_new = (kv(slice(META_PAD, BLOCK)), kv(slice(l - WINDOW, l)))
    else:
        win, meta = swa_cache
        o_b, nwin = _swa_sample(p3, win.reshape(n, win.shape[1], BKV_W), meta.reshape(n, N_META, BKV_W),
                                lw['qnw'], lw['knw'], lw['sinks'], 16)
        swa_new = nwin.reshape(win.shape)
    o_c, ch_new = _conformer(p3, conf_hist, lw['dww'], lw['dwb'], lw['lnw'], lw['lnb'],
                             tl['conf_nb'], tl['conf_tm'], tl['conf_rc'], tl['n_invalid'])
    t = n * l
    x2 = _merge(o_a.reshape(t, -1), o_b.reshape(t, -1), o_c.reshape(t, -1), p2, x2, lw['wb'], lw['wo'],
                tl['tok_tm'])
    x2 = _moe(x2, lw['n2'], lw['wr'], lw['br'], lw['wgu'], lw['wd'], tl['moe_tm'])
    ch_new = ch_new[:, CONF_HALO - (C_CONV - 1):]
    return x2.reshape(n, l, d), s_new, gh_new, swa_new, ch_new


def kernel(x_prompt, x_sample, state_gdn, cache_gdn_conv, cache_swa_kv, cache_meta_kv, cache_conv, meta_tokens, norm1_w, norm2_w, w_in, gdn_conv_w, gdn_a_log, gdn_dt_bias, gdn_norm_w, swa_q_norm_w, swa_k_norm_w, swa_sinks, conv_dw_w, conv_dw_b, conv_ln_w, conv_ln_b, w_branch, w_out, router_group_w, router_group_b, router_expert_w, router_expert_b, moe_w_gate, moe_w_up, moe_w_down):
    dtp = x_prompt.dtype
    n_p, _, d = x_prompt.shape
    n_s = x_sample.shape[0]
    depth = w_in.shape[0]
    xp = jnp.pad(x_prompt, ((0, 0), (BLOCK, 0), (0, 0)))
    xp = lax.dynamic_update_slice(
        xp, jnp.broadcast_to(meta_tokens.astype(dtp)[None], (n_p, N_META, d)), (0, META_PAD, 0))
    xs = x_sample
    outs_p, outs_s = [], []
    for l in range(depth):
        lw = _layer_weights(l, norm1_w, norm2_w, w_in, gdn_conv_w, gdn_a_log, gdn_dt_bias, gdn_norm_w,
                            swa_q_norm_w, swa_k_norm_w, swa_sinks, conv_dw_w, conv_dw_b, conv_ln_w, conv_ln_b,
                            w_branch, w_out, router_group_w, router_group_b, router_expert_w, router_expert_b,
                            moe_w_gate, moe_w_up, moe_w_down)
        xp, s_p, gh_p, (mkv_p, wkv_p), ch_p = _layer(
            xp, lw, jnp.zeros((n_p, A_HEADS, A_DK, A_DK), dtp), None,
            jnp.zeros((n_p, CONF_HALO, C_CH), dtp), None)
        outs_p.append((s_p, gh_p, wkv_p, mkv_p, ch_p))
        gh0 = jnp.pad(cache_gdn_conv[l], ((0, 0), (GDN_HALO - (A_CONV - 1), 0), (0, 0)))
        ch0 = jnp.pad(cache_conv[l], ((0, 0), (CONF_HALO - (C_CONV - 1), 0), (0, 0)))
        xs, s_s, gh_s, wkv_s, ch_s = _layer(xs, lw, state_gdn[l], gh0, ch0, (cache_swa_kv[l], cache_meta_kv[l]))
        outs_s.append((s_s, gh_s, wkv_s, ch_s))
    stack = lambda outs, i: jnp.stack([o[i] for o in outs])
    return (xp[:, BLOCK:], xs,
            stack(outs_p, 0), stack(outs_p, 1), stack(outs_p, 2), stack(outs_p, 3), stack(outs_p, 4),
            stack(outs_s, 0), stack(outs_s, 1), stack(outs_s, 2), stack(outs_s, 3))
```

```python
import functools

import jax
import jax.numpy as jnp
from jax import lax
from jax.experimental import pallas as pl
from jax.experimental.pallas import tpu as pltpu

F32 = jnp.float32
BF16 = jnp.bfloat16
HIGHEST = lax.Precision.HIGHEST
EPS = 1e-6
NEG = -1e30

VMEM_LIMIT_BYTES = 56 * 1024 * 1024
LANES = 128
SUBLANES = 8

PAST_LEN = 16384
N_META = 16
BLOCK = 128
WINDOW = 128
META_PAD = BLOCK - N_META
A_HEADS = 4
A_DK = 128
A_CONV = 4
B_HEADS = 8
B_KV_HEADS = 2
B_GROUP = B_HEADS // B_KV_HEADS
B_HD = 64
C_CH = 512
C_CONV = 31
BRANCH_W = 512
N_GROUPS = 4
EXP_PER_GROUP = 4
N_EXPERTS = N_GROUPS * EXP_PER_GROUP

A_KW = A_HEADS * A_DK
QKV_W = 3 * A_KW
Z_BLK = QKV_W // BRANCH_W
CU_BLK = 2
GL_BLK = 1
BQ_BLK = 12
BKV_W = 2 * B_KV_HEADS * B_HD
BKV_BLK = 26
AB_BLK = 54
PROJ_W = 7040
CONF_HALO = 32
GDN_HALO = SUBLANES


def _rms(x, w):
    return x * lax.rsqrt(jnp.mean(x * x, axis=-1, keepdims=True) + EPS) * w


def _silu(x):
    return x * jax.nn.sigmoid(x)


def _cparams(sem):
    return pltpu.CompilerParams(dimension_semantics=sem, vmem_limit_bytes=VMEM_LIMIT_BYTES)


def _inproj_kernel(x_ref, nw_ref, w_ref, o_ref, h_ref):
    @pl.when(pl.program_id(1) == 0)
    def _():
        h_ref[...] = _rms(x_ref[...], nw_ref[...]).astype(BF16)

    o_ref[...] = jnp.dot(h_ref[...], w_ref[...], preferred_element_type=F32)


def _inproj(x2, nw, w, tm, tn):
    t, d = x2.shape
    wd = w.shape[1]
    return pl.pallas_call(
        _inproj_kernel,
        grid=(t // tm, wd // tn),
        in_specs=[pl.BlockSpec((tm, d), lambda i, j: (i, 0)),
                  pl.BlockSpec((1, d), lambda i, j: (0, 0)),
                  pl.BlockSpec((d, tn), lambda i, j: (0, j))],
        out_specs=pl.BlockSpec((tm, tn), lambda i, j: (i, j)),
        out_shape=jax.ShapeDtypeStruct((t, wd), F32),
        scratch_shapes=[pltpu.VMEM((tm, d), BF16)],
        compiler_params=_cparams(("parallel", "arbitrary")),
        name="inproj",
    )(x2, nw, w)


def _conf_kernel(cu_ref, hist_ref, dww_ref, dwb_ref, lnw_ref, lnb_ref, y_ref, nh_ref, e_ref,
                 *, tm, rc, n_invalid):
    t = pl.program_id(1)
    nb = cu_ref.shape[0]
    halo = CONF_HALO

    @pl.when(t == 0)
    def _():
        e_ref[:, 0:SUBLANES, :] = jnp.zeros((nb, SUBLANES, C_CH), F32)
        e_ref[:, halo + tm:halo + tm + SUBLANES, :] = jnp.zeros((nb, SUBLANES, C_CH), F32)
        e_ref[:, halo - (C_CONV - 1):halo, :] = hist_ref[...]

    @pl.when(t > 0)
    def _():
        e_ref[:, 0:halo, :] = e_ref[:, tm:tm + halo, :]

    cu = cu_ref[...]
    u = cu[:, :, :C_CH] * jax.nn.sigmoid(cu[:, :, C_CH:])
    if n_invalid:
        row = lax.broadcasted_iota(jnp.int32, (1, tm, 1), 1) + t * tm
        u = jnp.where(row >= n_invalid, u, 0.0)
    e_ref[:, halo:halo + tm, :] = u
    nh_ref[...] = e_ref[:, halo + tm - (C_CONV - 1):halo + tm, :]

    off = halo - (C_CONV - 1)
    for c in range(C_CH // LANES):
        cs = slice(c * LANES, (c + 1) * LANES)

        def rbody(r, carry, cs=cs):
            r0 = pl.multiple_of(r * rc, rc)
            blk = e_ref[:, pl.ds(r0, rc + halo + SUBLANES), cs]
            acc = jnp.zeros((nb, rc, LANES), F32)
            for s in range(SUBLANES):
                z = None
                for a in range((halo + SUBLANES) // SUBLANES):
                    k = SUBLANES * a + s - off
                    if 0 <= k < C_CONV:
                        term = dww_ref[k:k + 1, cs] * blk[:, SUBLANES * a:SUBLANES * a + rc + SUBLANES, :]
                        z = term if z is None else z + term
                acc = acc + z[:, s:s + rc, :]
            y_ref[:, pl.ds(r0, rc), cs] = acc + dwb_ref[:, cs]
            return carry

        lax.fori_loop(0, tm // rc, rbody, 0)

    y = y_ref[...]
    mu = jnp.mean(y, axis=-1, keepdims=True)
    var = jnp.mean(jnp.square(y - mu), axis=-1, keepdims=True)
    yn = (y - mu) * lax.rsqrt(var + EPS) * lnw_ref[...] + lnb_ref[...]
    y_ref[...] = _silu(yn)


def _conformer(p3, hist, dww, dwb, lnw, lnb, nb, tm, rc, n_invalid):
    n, l, _ = p3.shape
    kern = functools.partial(_conf_kernel, tm=tm, rc=rc, n_invalid=n_invalid)
    full = lambda shape: pl.BlockSpec(shape, lambda i, t: (0,) * len(shape))
    return pl.pallas_call(
        kern,
        grid=(n // nb, l // tm),
        in_specs=[pl.BlockSpec((nb, tm, 2 * C_CH), lambda i, t: (i, t, CU_BLK)),
                  pl.BlockSpec((nb, C_CONV - 1, C_CH), lambda i, t: (i, 0, 0)),
                  full(dww.shape), full(dwb.shape), full(lnw.shape), full(lnb.shape)],
        out_specs=[pl.BlockSpec((nb, tm, C_CH), lambda i, t: (i, t, 0)),
                   pl.BlockSpec((nb, C_CONV - 1, C_CH), lambda i, t: (i, 0, 0))],
        out_shape=[jax.ShapeDtypeStruct((n, l, C_CH), F32),
                   jax.ShapeDtypeStruct((n, C_CONV - 1, C_CH), F32)],
        scratch_shapes=[pltpu.VMEM((nb, CONF_HALO + tm + SUBLANES, C_CH), F32)],
        compiler_params=_cparams(("parallel", "arbitrary")),
        name="conformer",
    )(p3, hist, dww, dwb, lnw, lnb)


def _softmax_sink_pv(s, mask, sink, v, dims):
    s = jnp.where(mask, s, NEG)
    m = jnp.maximum(jnp.max(s, axis=-1, keepdims=True), sink)
    p = jnp.exp(s - m)
    den = jnp.sum(p, axis=-1, keepdims=True) + jnp.exp(sink - m)
    return lax.dot_general(p, v, dims, preferred_element_type=F32) / den


def _swa_prompt_kernel(q_ref, kv_ref, kvp_ref, kvm_ref, qnw_ref, knw_ref, sink_ref, o_ref, kvn_ref, *, bps):
    step = pl.program_id(1)
    qnw = qnw_ref[...]
    knw = knw_ref[...]
    qw = B_HEADS * B_HD
    kw = B_KV_HEADS * B_HD
    nk = N_META + 2 * BLOCK
    rows = B_GROUP * BLOCK
    r = lax.broadcasted_iota(jnp.int32, (rows, nk), 0) % BLOCK
    c = lax.broadcasted_iota(jnp.int32, (rows, nk), 1)
    cp = c - N_META
    co = cp - BLOCK
    hrow = lax.broadcasted_iota(jnp.int32, (rows, 1), 0) // BLOCK
    m_meta = c < N_META
    m_own = (co >= 0) & (co <= r)
    m_prev = (cp >= 0) & (cp < BLOCK) & (cp > r)
    bias0 = jnp.where(m_meta & (META_PAD + c <= r), 0.0, NEG)
    bias1 = jnp.where(m_meta | m_own, 0.0, NEG)
    bias2 = jnp.where(m_meta | m_own | m_prev, 0.0, NEG)
    sinks = []
    for g in range(B_KV_HEADS):
        sink = jnp.zeros((rows, 1), F32)
        for i in range(B_GROUP):
            h = g * B_GROUP + i
            sink = jnp.where(hrow == i, sink_ref[0:1, h:h + 1], sink)
        sinks.append(sink)
    hb_q = (lax.broadcasted_iota(jnp.int32, (qw, qw), 0) // B_HD
            == lax.broadcasted_iota(jnp.int32, (qw, qw), 1) // B_HD).astype(BF16)
    hb_k = hb_q[:kw, :kw]
    half = lax.broadcasted_iota(jnp.int32, (1, LANES), 1) // B_HD

    def head_rms(x, hb, w):
        sq = x * x
        hi = sq.astype(BF16)
        lo = (sq - hi.astype(F32)).astype(BF16)
        ss = jnp.dot(hi, hb, preferred_element_type=F32) + jnp.dot(lo, hb, preferred_element_type=F32)
        return x * lax.rsqrt(ss * (1.0 / B_HD) + EPS) * w

    kvm = kvm_ref[META_PAD:BLOCK, :]
    km = head_rms(kvm[:, :kw], hb_k, knw)
    kv_prev = kvp_ref[...]
    k_prev = head_rms(kv_prev[:, :kw], hb_k, knw)
    for b in range(bps):
        j = step * bps + b
        rs = slice(b * BLOCK, (b + 1) * BLOCK)
        q = q_ref[rs, :]
        kv = kv_ref[rs, :]
        k_own = head_rms(kv[:, :kw], hb_k, knw)
        kvn_ref[rs, :] = jnp.concatenate([k_own, kv[:, kw:]], axis=-1)
        qn = head_rms(q, hb_q, qnw)
        bias = jnp.where(j >= 2, bias2, jnp.where(j >= 1, bias1, bias0))
        kcat = jnp.concatenate([km, k_prev, k_own], axis=0)
        vcat = jnp.concatenate([kvm[:, kw:], kv_prev[:, kw:], kv[:, kw:]], axis=0)
        kcat_b = kcat.astype(BF16)
        vcat_b = vcat.astype(BF16)
        tiles = [qn[:, m * LANES:(m + 1) * LANES] for m in range(qw // LANES)]
        swapped = [pltpu.roll(tl_, B_HD, 1) for tl_ in tiles]
        outs = [None] * B_HEADS
        for g in range(B_KV_HEADS):
            in_g = half == g
            parts = []
            for i in range(B_GROUP):
                h = g * B_GROUP + i
                src = tiles[h // 2] if h % 2 == g else swapped[h // 2]
                parts.append(jnp.where(in_g, src, 0.0))
            lhs = jnp.concatenate(parts, axis=0)
            s = lax.dot_general(lhs.astype(BF16), kcat_b, (((1,), (1,)), ((), ())), preferred_element_type=F32)
            s = s * (B_HD ** -0.5) + bias
            m = jnp.maximum(jnp.max(s, axis=-1, keepdims=True), sinks[g])
            p = jnp.exp(s - m)
            den = jnp.sum(p, axis=-1, keepdims=True) + jnp.exp(sinks[g] - m)
            res = jnp.dot(p.astype(BF16), vcat_b, preferred_element_type=F32) / den
            for i in range(B_GROUP):
                h = g * B_GROUP + i
                part = res[i * BLOCK:(i + 1) * BLOCK, :]
                outs[h] = part if h % 2 == g else pltpu.roll(part, B_HD, 1)
        for m in range(qw // LANES):
            o_ref[rs, m * LANES:(m + 1) * LANES] = jnp.where(half == 0, outs[2 * m], outs[2 * m + 1])
        kv_prev, k_prev = kv, k_own


def _swa_prompt(p3, qnw, knw, sinks, bps):
    n, l, _ = p3.shape
    full = lambda shape: pl.BlockSpec(shape, lambda i, j: (0,) * len(shape))
    tq = bps * BLOCK
    return pl.pallas_call(
        functools.partial(_swa_prompt_kernel, bps=bps),
        grid=(n, l // tq),
        in_specs=[pl.BlockSpec((None, tq, B_HEADS * B_HD), lambda i, j: (i, j, BQ_BLK)),
                  pl.BlockSpec((None, tq, BKV_W), lambda i, j: (i, j, BKV_BLK)),
                  pl.BlockSpec((None, BLOCK, BKV_W), lambda i, j: (i, jnp.maximum(j * bps - 1, 0), BKV_BLK)),
                  pl.BlockSpec((None, BLOCK, BKV_W), lambda i, j: (i, 0, BKV_BLK)),
                  full(qnw.shape), full(knw.shape), full(sinks.shape)],
        out_specs=[pl.BlockSpec((None, tq, B_HEADS * B_HD), lambda i, j: (i, j, 0)),
                   pl.BlockSpec((None, tq, BKV_W), lambda i, j: (i, j, 0))],
        out_shape=[jax.ShapeDtypeStruct((n, l, B_HEADS * B_HD), F32),
                   jax.ShapeDtypeStruct((n, l, BKV_W), F32)],
        compiler_params=_cparams(("parallel", "arbitrary")),
        name="swa_prompt",
    )(p3, p3, p3, p3, qnw, knw, sinks)


def _swa_sample_kernel(q_ref, kvn_ref, win_ref, meta_ref, qnw_ref, knw_ref, sink_ref, o_ref, nwin_ref):
    q = q_ref[...]
    kvn = kvn_ref[...]
    win = win_ref[...]
    meta = meta_ref[...]
    qnw = qnw_ref[...]
    knw = knw_ref[...]
    t_new = q.shape[1]
    w = win.shape[1]
    nk = N_META + w + t_new
    rows = B_GROUP * t_new
    tq = lax.broadcasted_iota(jnp.int32, (1, rows, nk), 1) % t_new
    c = lax.broadcasted_iota(jnp.int32, (1, rows, nk), 2)
    cw = c - N_META
    win_ok = (cw >= 0) & (cw < w) & (cw - w > tq - WINDOW) & (cw + (PAST_LEN - w) >= N_META)
    cn = cw - w
    new_ok = (cn >= 0) & (cn <= tq) & (cn > tq - WINDOW)
    mask = (c < N_META) | win_ok | new_ok
    hrow = lax.broadcasted_iota(jnp.int32, (1, rows, 1), 1) // t_new
    kn_parts = []
    for g in range(B_KV_HEADS):
        ks = slice(g * B_HD, (g + 1) * B_HD)
        vs = slice((B_KV_HEADS + g) * B_HD, (B_KV_HEADS + g + 1) * B_HD)
        k_new = _rms(kvn[:, :, ks], knw)
        kn_parts.append(k_new)
        kk = jnp.concatenate([meta[:, :, ks], win[:, :, ks], k_new], axis=1)
        vv = jnp.concatenate([meta[:, :, vs], win[:, :, vs], kvn[:, :, vs]], axis=1)
        qq = jnp.concatenate(
            [_rms(q[:, :, (g * B_GROUP + i) * B_HD:(g * B_GROUP + i + 1) * B_HD], qnw) for i in range(B_GROUP)],
            axis=1)
        sink = jnp.zeros((1, rows, 1), F32)
        for i in range(B_GROUP):
            h = g * B_GROUP + i
            sink = jnp.where(hrow == i, sink_ref[0:1, h:h + 1], sink)
        s = jnp.einsum('bqd,bkd->bqk', qq, kk, preferred_element_type=F32) * (B_HD ** -0.5)
        o = _softmax_sink_pv(s, mask, sink, vv, (((2,), (1,)), ((0,), (0,))))
        for i in range(B_GROUP):
            h = g * B_GROUP + i
            o_ref[:, :, h * B_HD:(h + 1) * B_HD] = o[:, i * t_new:(i + 1) * t_new, :]
    nwin_ref[:, 0:w - t_new, :] = win[:, t_new:w, :]
    nwin_ref[:, w - t_new:w, :] = jnp.concatenate(kn_parts + [kvn[:, :, B_KV_HEADS * B_HD:]], axis=-1)


def _swa_sample(p3, win, meta, qnw, knw, sinks, nb):
    n, t_new, _ = p3.shape
    w = win.shape[1]
    full = lambda shape: pl.BlockSpec(shape, lambda i: (0,) * len(shape))
    return pl.pallas_call(
        _swa_sample_kernel,
        grid=(n // nb,),
        in_specs=[pl.BlockSpec((nb, t_new, B_HEADS * B_HD), lambda i: (i, 0, BQ_BLK)),
                  pl.BlockSpec((nb, t_new, BKV_W), lambda i: (i, 0, BKV_BLK)),
                  pl.BlockSpec((nb, w, BKV_W), lambda i: (i, 0, 0)),
                  pl.BlockSpec((nb, N_META, BKV_W), lambda i: (i, 0, 0)),
                  full(qnw.shape), full(knw.shape), full(sinks.shape)],
        out_specs=[pl.BlockSpec((nb, t_new, B_HEADS * B_HD), lambda i: (i, 0, 0)),
                   pl.BlockSpec((nb, w, BKV_W), lambda i: (i, 0, 0))],
        out_shape=[jax.ShapeDtypeStruct((n, t_new, B_HEADS * B_HD), F32),
                   jax.ShapeDtypeStruct((n, w, BKV_W), F32)],
        compiler_params=_cparams(("parallel",)),
        name="swa_sample",
    )(p3, p3, win, meta, qnw, knw, sinks)


GDN_ROWS = 64
GDN_R = A_HEADS * GDN_ROWS
GDN_PAIRS = A_HEADS // 2
PAIR_W = 2 * A_DK


def _mm(a, b):
    return jnp.dot(a.astype(BF16), b.astype(BF16), preferred_element_type=F32)


def _block_diag2(x):
    z = jnp.zeros((x.shape[0], A_DK), x.dtype)
    return jnp.concatenate([jnp.concatenate([x[:, :A_DK], z], axis=1),
                            jnp.concatenate([z, x[:, A_DK:]], axis=1)], axis=0)


def _gdn_prep_kernel(x_ref, halo_ref, ab_ref, cw_ref, alog_ref, dtb_ref, u_ref, wq_ref, qkk_ref, gl_ref,
                     *, sb, cpb, n_invalid, seq_halo):
    t = pl.program_id(1)
    ns = x_ref.shape[0]
    rows = x_ref.shape[1] // cpb
    c = GDN_ROWS
    r = GDN_R
    halo = halo_ref[...]
    if seq_halo:
        hrow = lax.broadcasted_iota(jnp.int32, (1, GDN_HALO, 1), 1) + (t * cpb * rows - GDN_HALO)
        halo = jnp.where(hrow >= n_invalid, halo, 0.0)
    cw = cw_ref[...]
    alog = alog_ref[...]
    dtb = dtb_ref[...]
    ri = lax.broadcasted_iota(jnp.int32, (r, r), 0)
    ci = lax.broadcasted_iota(jnp.int32, (r, r), 1)
    same = (ri // sb) == (ci // sb)
    incl = same & (ri >= ci)
    strict = same & (ri > ci)
    eye = (ri == ci).astype(F32)
    rc_ = lax.broadcasted_iota(jnp.int32, (c, c), 0)
    cc_ = lax.broadcasted_iota(jnp.int32, (c, c), 1)
    same_c = (rc_ // sb) == (cc_ // sb)
    lmat = jnp.concatenate([(same_c & (rc_ >= cc_)).astype(F32), same_c.astype(F32)], axis=0)
    sel = (lax.broadcasted_iota(jnp.int32, (SUBLANES, 2 * c), 1)
           == c + (lax.broadcasted_iota(jnp.int32, (SUBLANES, 2 * c), 0) * SUBLANES // sb) * sb).astype(F32)
    off = GDN_HALO - (A_CONV - 1)
    for cc in range(cpb):
        x = x_ref[:, cc * rows:(cc + 1) * rows, :]
        ab = ab_ref[:, cc * rows:(cc + 1) * rows, :].reshape(c, LANES)
        if n_invalid:
            row0 = (t * cpb + cc) * rows
            xrow = lax.broadcasted_iota(jnp.int32, (1, rows, 1), 1) + row0
            x = jnp.where(xrow >= n_invalid, x, 0.0)
        ext = jnp.concatenate([halo, x], axis=1)
        acc = cw[0:1, :] * ext[:, off:off + rows, :]
        for k in range(1, A_CONV):
            acc = acc + cw[k:k + 1, :] * ext[:, off + k:off + k + rows, :]
        y = _silu(acc).reshape(c, QKV_W)
        halo = x[:, rows - GDN_HALO:rows, :]

        sp = jnp.maximum(ab + dtb, 0.0) + jnp.log1p(jnp.exp(-jnp.abs(ab + dtb)))
        gfull = -jnp.exp(alog) * sp
        bfull = jax.nn.sigmoid(ab)
        if n_invalid:
            valid = (lax.broadcasted_iota(jnp.int32, (c, 1), 0) + row0) >= n_invalid
            gfull = jnp.where(valid, gfull, 0.0)
            bfull = jnp.where(valid, bfull, 0.0)
        bcol = jnp.concatenate([bfull[:, A_HEADS + h:A_HEADS + h + 1] for h in range(A_HEADS)], axis=0)
        gg = jnp.dot(lmat, gfull, precision=HIGHEST, preferred_element_type=F32)
        ggt = gg.T
        gc = jnp.concatenate([gg[:c, h:h + 1] for h in range(A_HEADS)], axis=0)
        grow = jnp.concatenate([ggt[h:h + 1, :c] for h in range(A_HEADS)], axis=1)
        drow = jnp.concatenate([ggt[h:h + 1, c:] for h in range(A_HEADS)], axis=1) - grow
        g8 = jnp.exp(jnp.dot(sel, gg, precision=HIGHEST, preferred_element_type=F32))
        decay = jnp.where(incl, jnp.exp(jnp.where(incl, gc - grow, 0.0)), 0.0)

        qs, ks, vs = [], [], []
        for h in range(A_HEADS):
            q = y[:, h * A_DK:(h + 1) * A_DK]
            k = y[:, A_KW + h * A_DK:A_KW + (h + 1) * A_DK]
            qs.append(q * lax.rsqrt(jnp.sum(q * q, axis=-1, keepdims=True) + EPS) * (A_DK ** -0.5))
            ks.append(k * lax.rsqrt(jnp.sum(k * k, axis=-1, keepdims=True) + EPS))
            vs.append(y[:, 2 * A_KW + h * A_DK:2 * A_KW + (h + 1) * A_DK])
        qr = jnp.concatenate(qs, axis=0)
        kr = jnp.concatenate(ks, axis=0)
        vr = jnp.concatenate(vs, axis=0)
        krt = kr.T
        kq = _mm(jnp.concatenate([kr, qr], axis=0), krt)
        a = jnp.where(strict, kq[:r] * decay * bcol, 0.0)
        qk = kq[r:] * decay
        tinv = eye - a
        ak = a
        kpow = 2
        while kpow < sb:
            ak = _mm(ak, ak)
            tinv = tinv + _mm(tinv, ak)
            kpow *= 2
        eg = jnp.exp(gc)
        uw = _mm(tinv, jnp.concatenate([vr * bcol, kr * (bcol * eg)], axis=1))
        ur = uw[:, :A_DK]
        wr = uw[:, A_DK:]
        qd = qr * eg
        kdt = krt * jnp.exp(drow)
        for p in range(GDN_PAIRS):
            r0 = slice(2 * p * c, (2 * p + 1) * c)
            r1 = slice((2 * p + 1) * c, (2 * p + 2) * c)
            pl_ = slice(2 * p * c, (2 * p + 2) * c)
            u_ref[cc, p] = jnp.concatenate([ur[r0], ur[r1]], axis=1)
            wq = jnp.concatenate([jnp.concatenate([wr[r0], wr[r1]], axis=1),
                                  jnp.concatenate([qd[r0], qd[r1]], axis=1)], axis=0)
            wq_ref[cc, p] = wq.astype(wq_ref.dtype)
            qkk = jnp.concatenate([qk[r0, pl_] + qk[r1, pl_], kdt[:, pl_]], axis=0)
            qkk_ref[cc, p] = qkk.astype(qkk_ref.dtype)
            gl_ref[cc, p] = jnp.concatenate(
                [jnp.broadcast_to(g8[:, 2 * p:2 * p + 1], (SUBLANES, A_DK)),
                 jnp.broadcast_to(g8[:, 2 * p + 1:2 * p + 2], (SUBLANES, A_DK))], axis=1)


def _gdn_prep(x_arr, halo_arr, halo_spec, cw, alog, dtb, ns, rows, cpb, sb, n_invalid, op_dtype):
    n, l, _ = x_arr.shape
    grid = (n // ns, l // (cpb * rows))
    units = grid[0] * grid[1] * cpb
    kern = functools.partial(_gdn_prep_kernel, sb=sb, cpb=cpb, n_invalid=n_invalid,
                             seq_halo=halo_arr is x_arr)
    full = lambda shape: pl.BlockSpec(shape, lambda i, t: (0,) * len(shape))
    steps = grid[1]
    out_map = lambda i, t: (i * steps + t, 0, 0, 0)
    c = GDN_ROWS
    return pl.pallas_call(
        kern,
        grid=grid,
        in_specs=[pl.BlockSpec((ns, cpb * rows, QKV_W), lambda i, t: (i, t, 0)),
                  halo_spec,
                  pl.BlockSpec((ns, cpb * rows, LANES), lambda i, t: (i, t, AB_BLK)),
                  full(cw.shape), full(alog.shape), full(dtb.shape)],
        out_specs=[pl.BlockSpec((cpb, GDN_PAIRS, c, PAIR_W), out_map),
                   pl.BlockSpec((cpb, GDN_PAIRS, 2 * c, PAIR_W), out_map),
                   pl.BlockSpec((cpb, GDN_PAIRS, c + A_DK, 2 * c), out_map),
                   pl.BlockSpec((cpb, GDN_PAIRS, SUBLANES, PAIR_W), out_map)],
        out_shape=[jax.ShapeDtypeStruct((units, GDN_PAIRS, c, PAIR_W), F32),
                   jax.ShapeDtypeStruct((units, GDN_PAIRS, 2 * c, PAIR_W), op_dtype),
                   jax.ShapeDtypeStruct((units, GDN_PAIRS, c + A_DK, 2 * c), op_dtype),
                   jax.ShapeDtypeStruct((units, GDN_PAIRS, SUBLANES, PAIR_W), F32)],
        compiler_params=_cparams(("parallel", "arbitrary")),
        name="gdn_prep",
    )(x_arr, halo_arr, x_arr, cw, alog, dtb)


def _gdn_out(o_pair, z_ref, o_ref, nw, idx, p):
    for i in range(2):
        h = 2 * p + i
        hs = slice(h * A_DK, (h + 1) * A_DK)
        o_ref[idx + (hs,)] = _rms(o_pair[..., i * A_DK:(i + 1) * A_DK], nw) * _silu(z_ref[idx + (hs,)])


def _gdn_scan_kernel(u_ref, wq_ref, qkk_ref, gl_ref, z_ref, nw_ref, o_ref, sout_ref, s_ref):
    t = pl.program_id(0)
    n_seq = u_ref.shape[0]
    c = GDN_ROWS
    nw = nw_ref[...]

    @pl.when(t == 0)
    def _():
        s_ref[...] = jnp.zeros(s_ref.shape, F32)

    for n in range(n_seq):
        for p in range(GDN_PAIRS):
            s = s_ref[n, p]
            r1 = jnp.dot(wq_ref[n, 0, p], _block_diag2(s.astype(BF16)), preferred_element_type=F32)
            vnew = u_ref[n, 0, p] - r1[:c]
            r2 = jnp.dot(qkk_ref[n, 0, p], _block_diag2(vnew.astype(BF16)), preferred_element_type=F32)
            s_ref[n, p] = s * gl_ref[n, 0, p][0:1, :] + r2[c:]
            _gdn_out(r1[c:] + r2[:c], z_ref, o_ref, nw, (n, slice(None)), p)

    @pl.when(t == pl.num_programs(0) - 1)
    def _():
        for n in range(n_seq):
            for p in range(GDN_PAIRS):
                s = s_ref[n, p]
                sout_ref[n, 2 * p] = s[:, :A_DK]
                sout_ref[n, 2 * p + 1] = s[:, A_DK:]


def _gdn_scan(u, wq, qkk, gl, p3, nw):
    n, l, _ = p3.shape
    sshape = (n, A_HEADS, A_DK, A_DK)
    c = GDN_ROWS
    steps = l // c
    v5 = lambda a: a.reshape((n, steps) + a.shape[1:])
    u, wq, qkk, gl = v5(u), v5(wq), v5(qkk), v5(gl)
    unit = lambda a: pl.BlockSpec((n, 1) + a.shape[2:], lambda t: (0, t, 0, 0, 0))
    vw = A_HEADS * A_DK
    return pl.pallas_call(
        _gdn_scan_kernel,
        grid=(steps,),
        in_specs=[unit(u), unit(wq), unit(qkk), unit(gl),
                  pl.BlockSpec((n, c, vw), lambda t: (0, t, Z_BLK)),
                  pl.BlockSpec(nw.shape, lambda t: (0, 0))],
        out_specs=[pl.BlockSpec((n, c, vw), lambda t: (0, t, 0)),
                   pl.BlockSpec(sshape, lambda t: (0, 0, 0, 0))],
        out_shape=[jax.ShapeDtypeStruct((n, l, vw), F32),
                   jax.ShapeDtypeStruct(sshape, F32)],
        scratch_shapes=[pltpu.VMEM((n, GDN_PAIRS, A_DK, PAIR_W), F32)],
        compiler_params=_cparams(("arbitrary",)),
        name="gdn_scan",
    )(u, wq, qkk, gl, p3, nw)


def _gdn_step_kernel(u_ref, wq_ref, qkk_ref, gl_ref, z_ref, s0_ref, nw_ref, *rest):
    o_ref, sout_ref = rest[-2:]
    ns, rows, _ = z_ref.shape
    c = GDN_ROWS
    nw = nw_ref[...]
    lane_seq = (lax.broadcasted_iota(jnp.int32, (A_DK, 2 * c), 1) % c) // rows
    for p in range(GDN_PAIRS):
        wq = wq_ref[0, p]
        qkk = qkk_ref[0, p]
        gl = gl_ref[0, p]
        ws, qs, states = [], [], []
        for s_i in range(ns):
            sl = slice(s_i * rows, (s_i + 1) * rows)
            s = jnp.concatenate([s0_ref[s_i, 2 * p], s0_ref[s_i, 2 * p + 1]], axis=1)
            states.append(s)
            lhs = jnp.concatenate([wq[sl], wq[c + s_i * rows:c + (s_i + 1) * rows]], axis=0)
            r1 = _mm(lhs, _block_diag2(s.astype(BF16)))
            ws.append(r1[:rows])
            qs.append(r1[rows:])
        vnew = u_ref[0, p] - jnp.concatenate(ws, axis=0)
        bdv = _block_diag2(vnew.astype(BF16))
        o = jnp.concatenate(qs, axis=0) + _mm(qkk[:c], bdv)
        _gdn_out(o.reshape(ns, rows, PAIR_W), z_ref, o_ref, nw, (slice(None), slice(None)), p)
        kdt = qkk[c:]
        lhs = jnp.concatenate([jnp.where(lane_seq == s_i, kdt, 0.0) for s_i in range(ns)], axis=0)
        upd = _mm(lhs, bdv)
        for s_i in range(ns):
            s_new = states[s_i] * gl[s_i:s_i + 1, :] + upd[s_i * A_DK:(s_i + 1) * A_DK]
            sout_ref[s_i, 2 * p] = s_new[:, :A_DK]
            sout_ref[s_i, 2 * p + 1] = s_new[:, A_DK:]


def _gdn_step(u, wq, qkk, gl, p3, s_all, layer, s_prev, nw, ns):
    n, rows, _ = p3.shape
    unit = lambda a: pl.BlockSpec((1,) + a.shape[1:], lambda i: (i, 0, 0, 0))
    vw = A_HEADS * A_DK
    sblk = pl.BlockSpec((None, ns) + s_all.shape[2:], lambda i: (layer, i, 0, 0, 0))
    in_specs = [unit(u), unit(wq), unit(qkk), unit(gl),
                pl.BlockSpec((ns, rows, vw), lambda i: (i, 0, Z_BLK)),
                sblk,
                pl.BlockSpec(nw.shape, lambda i: (0, 0))]
    args = [u, wq, qkk, gl, p3, s_all, nw]
    aliases = {}
    if s_prev is not None:
        in_specs.append(pl.BlockSpec(memory_space=pl.ANY))
        args.append(s_prev)
        aliases = {len(args) - 1: 1}
    return pl.pallas_call(
        _gdn_step_kernel,
        grid=(n // ns,),
        in_specs=in_specs,
        out_specs=[pl.BlockSpec((ns, rows, vw), lambda i: (i, 0, 0)), sblk],
        out_shape=[jax.ShapeDtypeStruct((n, rows, vw), F32),
                   jax.ShapeDtypeStruct(s_all.shape, F32)],
        input_output_aliases=aliases,
        compiler_params=_cparams(("parallel",)),
        name="gdn_step",
    )(*args)


def _gdn(p3, state, hist, cw, alog, dtb, nw, cpb, n_invalid):
    n, l, _ = p3.shape
    c = GDN_ROWS
    if hist is None:
        steps8 = cpb * c // GDN_HALO
        halo_spec = pl.BlockSpec((1, GDN_HALO, QKV_W), lambda i, t: (i, jnp.maximum(t * steps8 - 1, 0), 0))
        u, wq, qkk, gl = _gdn_prep(p3, p3, halo_spec, cw, alog, dtb, 1, c, cpb, c, n_invalid, BF16)
        return _gdn_scan(u, wq, qkk, gl, p3, nw)
    ns = c // l
    halo_spec = pl.BlockSpec((ns, GDN_HALO, QKV_W), lambda i, t: (i, 0, 0))
    u, wq, qkk, gl = _gdn_prep(p3, hist, halo_spec, cw, alog, dtb, ns, l, 1, l, n_invalid, F32)
    s_all, layer, s_prev = state
    return _gdn_step(u, wq, qkk, gl, p3, s_all, layer, s_prev, nw, ns)


def _merge_kernel(oa_ref, ob_ref, oc_ref, gl_ref, x_ref, wb_ref, wo_ref, out_ref):
    d = x_ref.shape[1]
    merged = None
    for r, o_ref in enumerate((oa_ref, ob_ref, oc_ref)):
        pb = jnp.dot(o_ref[...].astype(BF16), wb_ref[r], preferred_element_type=F32)
        term = jax.nn.sigmoid(gl_ref[:, r * d:(r + 1) * d]) * pb
        merged = term if merged is None else merged + term
    out_ref[...] = x_ref[...] + jnp.dot(merged.astype(BF16), wo_ref[...], preferred_element_type=F32)


def _merge(oa, ob, oc, p2, x2, wb, wo, tm):
    t, d = x2.shape
    bw = oa.shape[1]
    row = lambda w, blk: pl.BlockSpec((tm, w), lambda i: (i, blk))
    return pl.pallas_call(
        _merge_kernel,
        grid=(t // tm,),
        in_specs=[row(bw, 0), row(bw, 0), row(bw, 0), row(3 * d, GL_BLK), row(d, 0),
                  pl.BlockSpec(wb.shape, lambda i: (0, 0, 0)),
                  pl.BlockSpec(wo.shape, lambda i: (0, 0))],
        out_specs=row(d, 0),
        out_shape=jax.ShapeDtypeStruct((t, d), F32),
        compiler_params=_cparams(("parallel",)),
        name="merge",
    )(oa, ob, oc, p2, x2, wb, wo)


def _route(logits):
    lane = lax.broadcasted_iota(jnp.int32, logits.shape, 1)
    is_g = (lane >= N_EXPERTS) & (lane < N_EXPERTS + N_GROUPS)
    gl = jnp.where(is_g, logits, NEG)
    gmax = jnp.max(gl, axis=-1, keepdims=True)
    gidx = jnp.min(jnp.where(gl == gmax, lane - N_EXPERTS, LANES), axis=-1, keepdims=True)
    gw = 1.0 / jnp.sum(jnp.where(is_g, jnp.exp(gl - gmax), 0.0), axis=-1, keepdims=True)
    in_grp = (lane < N_EXPERTS) & ((lane // EXP_PER_GROUP) == gidx)
    el = jnp.where(in_grp, logits, NEG)
    v1 = jnp.max(el, axis=-1, keepdims=True)
    i1 = jnp.min(jnp.where(el == v1, lane, LANES), axis=-1, keepdims=True)
    el2 = jnp.where(lane == i1, NEG, el)
    v2 = jnp.max(el2, axis=-1, keepdims=True)
    i2 = jnp.min(jnp.where(el2 == v2, lane, LANES), axis=-1, keepdims=True)
    e21 = jnp.exp(v2 - v1)
    w1 = gw / (1.0 + e21)
    w2 = gw * e21 / (1.0 + e21)
    return jnp.where(lane == i1, w1, 0.0) + jnp.where(lane == i2, w2, 0.0)


def _moe_kernel(x_ref, nw_ref, wr_ref, br_ref, wgu_ref, wd_ref, out_ref, h_ref, cmb_ref):
    g = pl.program_id(1)

    @pl.when(g == 0)
    def _():
        x = x_ref[...]
        h = _rms(x, nw_ref[...])
        h_hi = h.astype(BF16)
        h_ref[...] = h_hi
        h_lo = (h - h_hi.astype(F32)).astype(BF16)
        l2 = jnp.dot(h_hi, wr_ref[...], preferred_element_type=F32)
        logits = (l2[:, :LANES] + l2[:, LANES:]
                  + jnp.dot(h_lo, wr_ref[:, :LANES], preferred_element_type=F32) + br_ref[...])
        cmb_ref[...] = _route(logits)
        out_ref[...] = x

    gw = wd_ref.shape[1]
    de = gw // EXP_PER_GROUP
    gu = jnp.dot(h_ref[...], wgu_ref[0], preferred_element_type=F32)
    hid = _silu(gu[:, :gw]) * gu[:, gw:]
    cmb = cmb_ref[...]
    lane = lax.broadcasted_iota(jnp.int32, cmb.shape, 1)
    parts = []
    for e in range(EXP_PER_GROUP):
        ce = jnp.sum(jnp.where(lane == g * EXP_PER_GROUP + e, cmb, 0.0), axis=-1, keepdims=True)
        parts.append((hid[:, e * de:(e + 1) * de] * ce).astype(BF16))
    out_ref[...] += jnp.dot(jnp.concatenate(parts, axis=1), wd_ref[0], preferred_element_type=F32)


def _moe(x2, nw, wr, br, wgu, wd, tm):
    t, d = x2.shape
    ng, _, gw2 = wgu.shape
    return pl.pallas_call(
        _moe_kernel,
        grid=(t // tm, ng),
        in_specs=[pl.BlockSpec((tm, d), lambda i, g: (i, 0)),
                  pl.BlockSpec((1, d), lambda i, g: (0, 0)),
                  pl.BlockSpec(wr.shape, lambda i, g: (0, 0)),
                  pl.BlockSpec(br.shape, lambda i, g: (0, 0)),
                  pl.BlockSpec((1, d, gw2), lambda i, g: (g, 0, 0)),
                  pl.BlockSpec((1, gw2 // 2, d), lambda i, g: (g, 0, 0))],
        out_specs=pl.BlockSpec((tm, d), lambda i, g: (i, 0)),
        out_shape=jax.ShapeDtypeStruct((t, d), F32),
        scratch_shapes=[pltpu.VMEM((tm, d), BF16), pltpu.VMEM((tm, LANES), F32)],
        compiler_params=_cparams(("parallel", "arbitrary")),
        name="moe",
    )(x2, nw, wr, br, wgu, wd)


def _pad_lanes(v, width):
    v = v.reshape(1, -1)
    return jnp.pad(v, ((0, 0), (0, width - v.shape[1])))


def _layer_weights(l, norm1_w, norm2_w, w_in, gdn_conv_w, gdn_a_log, gdn_dt_bias, gdn_norm_w,
                   swa_q_norm_w, swa_k_norm_w, swa_sinks, conv_dw_w, conv_dw_b, conv_ln_w, conv_ln_b,
                   w_branch, w_out, router_group_w, router_group_b, router_expert_w, router_expert_b,
                   moe_w_gate, moe_w_up, moe_w_down):
    d = w_in.shape[1]
    wi = w_in[l]
    o_ab = 4 * A_KW
    o_bq = o_ab + 2 * A_HEADS
    o_cu = o_bq + B_HEADS * B_HD + BKV_W
    o_gl = o_cu + 2 * C_CH
    w_perm = jnp.concatenate(
        [wi[:, :o_ab], wi[:, o_cu:o_gl], wi[:, o_gl:], wi[:, o_bq:o_cu], wi[:, o_ab:o_bq],
         jnp.zeros((d, LANES - 2 * A_HEADS), wi.dtype)], axis=1).astype(BF16)
    assert w_perm.shape[1] == PROJ_W
    wr = jnp.concatenate([router_expert_w[l], router_group_w[l],
                          jnp.zeros((d, LANES - N_EXPERTS - N_GROUPS), F32)], axis=1)
    br = _pad_lanes(jnp.concatenate([router_expert_b[l], router_group_b[l]]), LANES)
    wr_hi = wr.astype(BF16)
    wr = jnp.concatenate([wr_hi, (wr - wr_hi.astype(F32)).astype(BF16)], axis=1)
    return dict(
        n1=norm1_w[l].reshape(1, d), n2=norm2_w[l].reshape(1, d), w_in=w_perm,
        gdn_cw=gdn_conv_w[l], alog=_pad_lanes(gdn_a_log[l], LANES), dtb=_pad_lanes(gdn_dt_bias[l], LANES),
        gdn_nw=gdn_norm_w[l].reshape(1, -1),
        qnw=swa_q_norm_w[l].reshape(1, -1), knw=swa_k_norm_w[l].reshape(1, -1), sinks=swa_sinks[l].reshape(1, -1),
        qnw_heads=jnp.tile(swa_q_norm_w[l].reshape(1, -1), (1, B_HEADS)),
        knw_heads=jnp.tile(swa_k_norm_w[l].reshape(1, -1), (1, B_KV_HEADS)),
        dww=conv_dw_w[l], dwb=conv_dw_b[l].reshape(1, -1), lnw=conv_ln_w[l].reshape(1, -1),
        lnb=conv_ln_b[l].reshape(1, -1),
        wb=w_branch[l].astype(BF16), wo=w_out[l].astype(BF16), wr=wr, br=br,
        wgu=jnp.concatenate([_group_cols(moe_w_gate[l]), _group_cols(moe_w_up[l])], axis=2),
        wd=moe_w_down[l].astype(BF16).reshape(N_GROUPS, -1, d))


def _group_cols(w):
    ne, d, f = w.shape
    w = w.astype(BF16).reshape(N_GROUPS, EXP_PER_GROUP, d, f)
    return jnp.transpose(w, (0, 2, 1, 3)).reshape(N_GROUPS, d, EXP_PER_GROUP * f)


def _tiles(n, l):
    prompt = l > BLOCK
    t = n * l
    if prompt:
        return dict(proj_tm=1536, tok_tm=512, moe_tm=1056, conf_nb=1, conf_tm=384, conf_rc=64,
                    gdn_cpb=6, swa_bps=3, n_invalid=META_PAD)
    return dict(proj_tm=t, tok_tm=512, moe_tm=512, conf_nb=16, conf_tm=l, conf_rc=l,
                gdn_cpb=1, n_invalid=0)


def _layer(x3, lw, gdn_state, gdn_hist, conf_hist, swa_cache):
    n, l, d = x3.shape
    tl = _tiles(n, l)
    x2 = x3.reshape(n * l, d)
    p2 = _inproj(x2, lw['n1'], lw['w_in'], tl['proj_tm'], PROJ_W // 5)
    p3 = p2.reshape(n, l, PROJ_W)
    o_a, s_new = _gdn(p3, gdn_state, gdn_hist, lw['gdn_cw'], lw['alog'], lw['dtb'], lw['gdn_nw'],
                      tl['gdn_cpb'], tl['n_invalid'])
    gh_new = p3[:, l - (A_CONV - 1):, :QKV_W]
    if swa_cache is None:
        o_b, kvn = _swa_prompt(p3, lw['qnw_heads'], lw['knw_heads'], lw['sinks'], tl['swa_bps'])
        kv = lambda sl: kvn[:, sl].reshape(n, -1, 2, B_KV_HEADS, B_HD)
        swa_new = (kv(slice(META_PAD, BLOCK)), kv(slice(l - WINDOW, l)))
    else:
        win, meta = swa_cache
        o_b, nwin = _swa_sample(p3, win.reshape(n, win.shape[1], BKV_W), meta.reshape(n, N_META, BKV_W),
                                lw['qnw'], lw['knw'], lw['sinks'], 16)
        swa_new = nwin.reshape(win.shape)
    o_c, ch_new = _conformer(p3, conf_hist, lw['dww'], lw['dwb'], lw['lnw'], lw['lnb'],
                             tl['conf_nb'], tl['conf_tm'], tl['conf_rc'], tl['n_invalid'])
    t = n * l
    x2 = _merge(o_a.reshape(t, -1), o_b.reshape(t, -1), o_c.reshape(t, -1), p2, x2, lw['wb'], lw['wo'],
                tl['tok_tm'])
    x2 = _moe(x2, lw['n2'], lw['wr'], lw['br'], lw['wgu'], lw['wd'], tl['moe_tm'])
    return x2.reshape(n, l, d), s_new, gh_new, swa_new, ch_new


def kernel(x_prompt, x_sample, state_gdn, cache_gdn_conv, cache_swa_kv, cache_meta_kv, cache_conv, meta_tokens, norm1_w, norm2_w, w_in, gdn_conv_w, gdn_a_log, gdn_dt_bias, gdn_norm_w, swa_q_norm_w, swa_k_norm_w, swa_sinks, conv_dw_w, conv_dw_b, conv_ln_w, conv_ln_b, w_branch, w_out, router_group_w, router_group_b, router_expert_w, router_expert_b, moe_w_gate, moe_w_up, moe_w_down):
    dtp = x_prompt.dtype
    n_p, _, d = x_prompt.shape
    n_s = x_sample.shape[0]
    depth = w_in.shape[0]
    xp = jnp.pad(x_prompt, ((0, 0), (BLOCK, 0), (0, 0)))
    xp = lax.dynamic_update_slice(
        xp, jnp.broadcast_to(meta_tokens.astype(dtp)[None], (n_p, N_META, d)), (0, META_PAD, 0))
    xs = x_sample
    state_s = None
    outs_p, outs_s = [], []
    for l in range(depth):
        lw = _layer_weights(l, norm1_w, norm2_w, w_in, gdn_conv_w, gdn_a_log, gdn_dt_bias, gdn_norm_w,
                            swa_q_norm_w, swa_k_norm_w, swa_sinks, conv_dw_w, conv_dw_b, conv_ln_w, conv_ln_b,
                            w_branch, w_out, router_group_w, router_group_b, router_expert_w, router_expert_b,
                            moe_w_gate, moe_w_up, moe_w_down)
        xp, s_p, gh_p, (mkv_p, wkv_p), ch_p = _layer(
            xp, lw, None, None, jnp.zeros((n_p, C_CONV - 1, C_CH), dtp), None)
        outs_p.append((s_p, gh_p, wkv_p, mkv_p, ch_p))
        gh0 = jnp.pad(cache_gdn_conv[l], ((0, 0), (GDN_HALO - (A_CONV - 1), 0), (0, 0)))
        xs, state_s, gh_s, wkv_s, ch_s = _layer(xs, lw, (state_gdn, l, state_s), gh0, cache_conv[l],
                                                (cache_swa_kv[l], cache_meta_kv[l]))
        outs_s.append((gh_s, wkv_s, ch_s))
    stack = lambda outs, i: jnp.stack([o[i] for o in outs])
    return (xp[:, BLOCK:], xs,
            stack(outs_p, 0), stack(outs_p, 1), stack(outs_p, 2), stack(outs_p, 3), stack(outs_p, 4),
            state_s, stack(outs_s, 0), stack(outs_s, 1), stack(outs_s, 2))
```

```python
import functools

import jax
import jax.numpy as jnp
from jax import lax
from jax.experimental import pallas as pl
from jax.experimental.pallas import tpu as pltpu

F32 = jnp.float32
BF16 = jnp.bfloat16
HIGHEST = lax.Precision.HIGHEST
EPS = 1e-6
NEG = -1e30

VMEM_LIMIT_BYTES = 56 * 1024 * 1024
LANES = 128
SUBLANES = 8

PAST_LEN = 16384
N_META = 16
BLOCK = 128
WINDOW = 128
META_PAD = BLOCK - N_META
A_HEADS = 4
A_DK = 128
A_CONV = 4
B_HEADS = 8
B_KV_HEADS = 2
B_GROUP = B_HEADS // B_KV_HEADS
B_HD = 64
C_CH = 512
C_CONV = 31
BRANCH_W = 512
N_GROUPS = 4
EXP_PER_GROUP = 4
N_EXPERTS = N_GROUPS * EXP_PER_GROUP

A_KW = A_HEADS * A_DK
QKV_W = 3 * A_KW
Z_BLK = QKV_W // BRANCH_W
CU_BLK = 2
GL_BLK = 1
BQ_BLK = 12
BKV_W = 2 * B_KV_HEADS * B_HD
BKV_BLK = 26
AB_BLK = 54
PROJ_W = 7040
CONF_HALO = 32
GDN_HALO = SUBLANES


def _rms(x, w):
    return x * lax.rsqrt(jnp.mean(x * x, axis=-1, keepdims=True) + EPS) * w


def _silu(x):
    return x * jax.nn.sigmoid(x)


def _cparams(sem):
    return pltpu.CompilerParams(dimension_semantics=sem, vmem_limit_bytes=VMEM_LIMIT_BYTES)


def _inproj_kernel(x_ref, nw_ref, w_ref, o_ref, h_ref):
    @pl.when(pl.program_id(1) == 0)
    def _():
        h_ref[...] = _rms(x_ref[...], nw_ref[...]).astype(BF16)

    o_ref[...] = jnp.dot(h_ref[...], w_ref[...], preferred_element_type=F32)


def _inproj(x2, nw, w, tm, tn):
    t, d = x2.shape
    wd = w.shape[1]
    return pl.pallas_call(
        _inproj_kernel,
        grid=(t // tm, wd // tn),
        in_specs=[pl.BlockSpec((tm, d), lambda i, j: (i, 0)),
                  pl.BlockSpec((1, d), lambda i, j: (0, 0)),
                  pl.BlockSpec((d, tn), lambda i, j: (0, j))],
        out_specs=pl.BlockSpec((tm, tn), lambda i, j: (i, j)),
        out_shape=jax.ShapeDtypeStruct((t, wd), F32),
        scratch_shapes=[pltpu.VMEM((tm, d), BF16)],
        compiler_params=_cparams(("parallel", "arbitrary")),
        name="inproj",
    )(x2, nw, w)


def _conf_kernel(cu_ref, hist_ref, dww_ref, dwb_ref, lnw_ref, lnb_ref, y_ref, nh_ref, e_ref, c_ref,
                 *, tm, rc, n_invalid):
    t = pl.program_id(1)
    nb = cu_ref.shape[0]
    halo = CONF_HALO

    @pl.when(t == 0)
    def _():
        e_ref[:, 0:SUBLANES, :] = jnp.zeros((nb, SUBLANES, C_CH), F32)
        e_ref[:, halo + tm:halo + tm + SUBLANES, :] = jnp.zeros((nb, SUBLANES, C_CH), F32)
        e_ref[:, halo - (C_CONV - 1):halo, :] = hist_ref[...]

    @pl.when(t > 0)
    def _():
        e_ref[:, 0:halo, :] = e_ref[:, tm:tm + halo, :]

    cu = cu_ref[...]
    u = cu[:, :, :C_CH] * jax.nn.sigmoid(cu[:, :, C_CH:])
    if n_invalid:
        row = lax.broadcasted_iota(jnp.int32, (1, tm, 1), 1) + t * tm
        u = jnp.where(row >= n_invalid, u, 0.0)
    e_ref[:, halo:halo + tm, :] = u
    nh_ref[...] = e_ref[:, halo + tm - (C_CONV - 1):halo + tm, :]

    off = halo - (C_CONV - 1)
    for c in range(C_CH // LANES):
        cs = slice(c * LANES, (c + 1) * LANES)

        def rbody(r, carry, cs=cs):
            r0 = pl.multiple_of(r * rc, rc)
            blk = e_ref[:, pl.ds(r0, rc + halo + SUBLANES), cs]
            acc = jnp.zeros((nb, rc, LANES), F32)
            for s in range(SUBLANES):
                z = None
                for a in range((halo + SUBLANES) // SUBLANES):
                    k = SUBLANES * a + s - off
                    if 0 <= k < C_CONV:
                        term = dww_ref[k:k + 1, cs] * blk[:, SUBLANES * a:SUBLANES * a + rc + SUBLANES, :]
                        z = term if z is None else z + term
                acc = acc + z[:, s:s + rc, :]
            c_ref[:, pl.ds(r0, rc), cs] = acc + dwb_ref[:, cs]
            return carry

        lax.fori_loop(0, tm // rc, rbody, 0)

    y = c_ref[...]
    mu = jnp.mean(y, axis=-1, keepdims=True)
    var = jnp.mean(jnp.square(y - mu), axis=-1, keepdims=True)
    yn = (y - mu) * lax.rsqrt(var + EPS) * lnw_ref[...] + lnb_ref[...]
    y_ref[...] = _silu(yn).astype(y_ref.dtype)


def _conformer(p3, hist, dww, dwb, lnw, lnb, nb, tm, rc, n_invalid, out_dtype):
    n, l, _ = p3.shape
    kern = functools.partial(_conf_kernel, tm=tm, rc=rc, n_invalid=n_invalid)
    full = lambda shape: pl.BlockSpec(shape, lambda i, t: (0,) * len(shape))
    return pl.pallas_call(
        kern,
        grid=(n // nb, l // tm),
        in_specs=[pl.BlockSpec((nb, tm, 2 * C_CH), lambda i, t: (i, t, CU_BLK)),
                  pl.BlockSpec((nb, C_CONV - 1, C_CH), lambda i, t: (i, 0, 0)),
                  full(dww.shape), full(dwb.shape), full(lnw.shape), full(lnb.shape)],
        out_specs=[pl.BlockSpec((nb, tm, C_CH), lambda i, t: (i, t, 0)),
                   pl.BlockSpec((nb, C_CONV - 1, C_CH), lambda i, t: (i, 0, 0))],
        out_shape=[jax.ShapeDtypeStruct((n, l, C_CH), out_dtype),
                   jax.ShapeDtypeStruct((n, C_CONV - 1, C_CH), F32)],
        scratch_shapes=[pltpu.VMEM((nb, CONF_HALO + tm + SUBLANES, C_CH), F32), pltpu.VMEM((nb, tm, C_CH), F32)],
        compiler_params=_cparams(("parallel", "arbitrary")),
        name="conformer",
    )(p3, hist, dww, dwb, lnw, lnb)


SWA_ROWS = 256


def _softmax_sink_pv(s, mask, sink, v, dims):
    s = jnp.where(mask, s, NEG)
    m = jnp.maximum(jnp.max(s, axis=-1, keepdims=True), sink)
    p = jnp.exp(s - m)
    den = jnp.sum(p, axis=-1, keepdims=True) + jnp.exp(sink - m)
    return lax.dot_general(p, v, dims, preferred_element_type=F32) / den


def _swa_prompt_kernel(q_ref, kv_ref, kvp_ref, kvm_ref, qnw_ref, knw_ref, sink_ref, o_ref, kvn_ref, *, bps):
    step = pl.program_id(1)
    qnw = qnw_ref[...]
    knw = knw_ref[...]
    qw = B_HEADS * B_HD
    kw = B_KV_HEADS * B_HD
    nk = N_META + 2 * BLOCK
    rows = B_GROUP * BLOCK
    r = lax.broadcasted_iota(jnp.int32, (rows, nk), 0) % BLOCK
    c = lax.broadcasted_iota(jnp.int32, (rows, nk), 1)
    cp = c - N_META
    co = cp - BLOCK
    hrow = lax.broadcasted_iota(jnp.int32, (rows, 1), 0) // BLOCK
    m_meta = c < N_META
    m_own = (co >= 0) & (co <= r)
    m_prev = (cp >= 0) & (cp < BLOCK) & (cp > r)
    bias0 = jnp.where(m_meta & (META_PAD + c <= r), 0.0, NEG)
    bias1 = jnp.where(m_meta | m_own, 0.0, NEG)
    bias2 = jnp.where(m_meta | m_own | m_prev, 0.0, NEG)
    sinks = []
    for g in range(B_KV_HEADS):
        sink = jnp.zeros((rows, 1), F32)
        for i in range(B_GROUP):
            h = g * B_GROUP + i
            sink = jnp.where(hrow == i, sink_ref[0:1, h:h + 1], sink)
        sinks.append(sink)
    hb_q = (lax.broadcasted_iota(jnp.int32, (qw, qw), 0) // B_HD
            == lax.broadcasted_iota(jnp.int32, (qw, qw), 1) // B_HD).astype(BF16)
    hb_k = hb_q[:kw, :kw]
    half = lax.broadcasted_iota(jnp.int32, (1, LANES), 1) // B_HD

    def head_rms(x, hb, w):
        sq = x * x
        hi = sq.astype(BF16)
        lo = (sq - hi.astype(F32)).astype(BF16)
        ss = jnp.dot(hi, hb, preferred_element_type=F32) + jnp.dot(lo, hb, preferred_element_type=F32)
        return x * lax.rsqrt(ss * (1.0 / B_HD) + EPS) * w

    kvm = kvm_ref[META_PAD:BLOCK, :]
    km = head_rms(kvm[:, :kw], hb_k, knw)
    kv_prev = kvp_ref[...]
    k_prev = head_rms(kv_prev[:, :kw], hb_k, knw)
    for b in range(bps):
        j = step * bps + b
        rs = slice(b * BLOCK, (b + 1) * BLOCK)
        q = q_ref[rs, :]
        kv = kv_ref[rs, :]
        k_own = head_rms(kv[:, :kw], hb_k, knw)
        kvn_ref[rs, :] = jnp.concatenate([k_own, kv[:, kw:]], axis=-1)
        qn = head_rms(q, hb_q, qnw)
        bias = jnp.where(j >= 2, bias2, jnp.where(j >= 1, bias1, bias0))
        kcat = jnp.concatenate([km, k_prev, k_own], axis=0)
        vcat = jnp.concatenate([kvm[:, kw:], kv_prev[:, kw:], kv[:, kw:]], axis=0)
        kcat_b = kcat.astype(BF16)
        vcat_b = vcat.astype(BF16)
        tiles = [qn[:, m * LANES:(m + 1) * LANES] for m in range(qw // LANES)]
        swapped = [pltpu.roll(tl_, B_HD, 1) for tl_ in tiles]
        outs = [None] * B_HEADS
        for g in range(B_KV_HEADS):
            in_g = half == g
            parts = []
            for i in range(B_GROUP):
                h = g * B_GROUP + i
                src = tiles[h // 2] if h % 2 == g else swapped[h // 2]
                parts.append(jnp.where(in_g, src, 0.0))
            lhs = jnp.concatenate(parts, axis=0).astype(BF16)
            chunks = []
            for r0 in range(0, rows, SWA_ROWS):
                rc = slice(r0, r0 + SWA_ROWS)
                s = lax.dot_general(lhs[rc], kcat_b, (((1,), (1,)), ((), ())), preferred_element_type=F32)
                s = s * (B_HD ** -0.5) + bias[rc]
                sink = sinks[g][rc]
                m = jnp.maximum(jnp.max(s, axis=-1, keepdims=True), sink)
                p = jnp.exp(s - m)
                den = jnp.sum(p, axis=-1, keepdims=True) + jnp.exp(sink - m)
                chunks.append(jnp.dot(p.astype(BF16), vcat_b, preferred_element_type=F32) / den)
            res = jnp.concatenate(chunks, axis=0)
            for i in range(B_GROUP):
                h = g * B_GROUP + i
                part = res[i * BLOCK:(i + 1) * BLOCK, :]
                outs[h] = part if h % 2 == g else pltpu.roll(part, B_HD, 1)
        for m in range(qw // LANES):
            o_ref[rs, m * LANES:(m + 1) * LANES] = jnp.where(half == 0, outs[2 * m], outs[2 * m + 1]).astype(o_ref.dtype)
        kv_prev, k_prev = kv, k_own


def _swa_prompt(p3, qnw, knw, sinks, bps):
    n, l, _ = p3.shape
    full = lambda shape: pl.BlockSpec(shape, lambda i, j: (0,) * len(shape))
    tq = bps * BLOCK
    return pl.pallas_call(
        functools.partial(_swa_prompt_kernel, bps=bps),
        grid=(n, l // tq),
        in_specs=[pl.BlockSpec((None, tq, B_HEADS * B_HD), lambda i, j: (i, j, BQ_BLK)),
                  pl.BlockSpec((None, tq, BKV_W), lambda i, j: (i, j, BKV_BLK)),
                  pl.BlockSpec((None, BLOCK, BKV_W), lambda i, j: (i, jnp.maximum(j * bps - 1, 0), BKV_BLK)),
                  pl.BlockSpec((None, BLOCK, BKV_W), lambda i, j: (i, 0, BKV_BLK)),
                  full(qnw.shape), full(knw.shape), full(sinks.shape)],
        out_specs=[pl.BlockSpec((None, tq, B_HEADS * B_HD), lambda i, j: (i, j, 0)),
                   pl.BlockSpec((None, tq, BKV_W), lambda i, j: (i, j, 0))],
        out_shape=[jax.ShapeDtypeStruct((n, l, B_HEADS * B_HD), BF16),
                   jax.ShapeDtypeStruct((n, l, BKV_W), F32)],
        compiler_params=_cparams(("parallel", "arbitrary")),
        name="swa_prompt",
    )(p3, p3, p3, p3, qnw, knw, sinks)


def _swa_sample_kernel(q_ref, kvn_ref, win_ref, meta_ref, qnw_ref, knw_ref, sink_ref, o_ref, nwin_ref):
    q = q_ref[...]
    kvn = kvn_ref[...]
    win = win_ref[...]
    meta = meta_ref[...]
    qnw = qnw_ref[...]
    knw = knw_ref[...]
    t_new = q.shape[1]
    w = win.shape[1]
    nk = N_META + w + t_new
    rows = B_GROUP * t_new
    tq = lax.broadcasted_iota(jnp.int32, (1, rows, nk), 1) % t_new
    c = lax.broadcasted_iota(jnp.int32, (1, rows, nk), 2)
    cw = c - N_META
    win_ok = (cw >= 0) & (cw < w) & (cw - w > tq - WINDOW) & (cw + (PAST_LEN - w) >= N_META)
    cn = cw - w
    new_ok = (cn >= 0) & (cn <= tq) & (cn > tq - WINDOW)
    mask = (c < N_META) | win_ok | new_ok
    hrow = lax.broadcasted_iota(jnp.int32, (1, rows, 1), 1) // t_new
    kn_parts = []
    for g in range(B_KV_HEADS):
        ks = slice(g * B_HD, (g + 1) * B_HD)
        vs = slice((B_KV_HEADS + g) * B_HD, (B_KV_HEADS + g + 1) * B_HD)
        k_new = _rms(kvn[:, :, ks], knw)
        kn_parts.append(k_new)
        kk = jnp.concatenate([meta[:, :, ks], win[:, :, ks], k_new], axis=1)
        vv = jnp.concatenate([meta[:, :, vs], win[:, :, vs], kvn[:, :, vs]], axis=1)
        qq = jnp.concatenate(
            [_rms(q[:, :, (g * B_GROUP + i) * B_HD:(g * B_GROUP + i + 1) * B_HD], qnw) for i in range(B_GROUP)],
            axis=1)
        sink = jnp.zeros((1, rows, 1), F32)
        for i in range(B_GROUP):
            h = g * B_GROUP + i
            sink = jnp.where(hrow == i, sink_ref[0:1, h:h + 1], sink)
        s = jnp.einsum('bqd,bkd->bqk', qq, kk, preferred_element_type=F32) * (B_HD ** -0.5)
        o = _softmax_sink_pv(s, mask, sink, vv, (((2,), (1,)), ((0,), (0,))))
        for i in range(B_GROUP):
            h = g * B_GROUP + i
            o_ref[:, :, h * B_HD:(h + 1) * B_HD] = o[:, i * t_new:(i + 1) * t_new, :]
    nwin_ref[:, 0:w - t_new, :] = win[:, t_new:w, :]
    nwin_ref[:, w - t_new:w, :] = jnp.concatenate(kn_parts + [kvn[:, :, B_KV_HEADS * B_HD:]], axis=-1)


def _swa_sample(p3, win, meta, qnw, knw, sinks, nb):
    n, t_new, _ = p3.shape
    w = win.shape[1]
    full = lambda shape: pl.BlockSpec(shape, lambda i: (0,) * len(shape))
    return pl.pallas_call(
        _swa_sample_kernel,
        grid=(n // nb,),
        in_specs=[pl.BlockSpec((nb, t_new, B_HEADS * B_HD), lambda i: (i, 0, BQ_BLK)),
                  pl.BlockSpec((nb, t_new, BKV_W), lambda i: (i, 0, BKV_BLK)),
                  pl.BlockSpec((nb, w, BKV_W), lambda i: (i, 0, 0)),
                  pl.BlockSpec((nb, N_META, BKV_W), lambda i: (i, 0, 0)),
                  full(qnw.shape), full(knw.shape), full(sinks.shape)],
        out_specs=[pl.BlockSpec((nb, t_new, B_HEADS * B_HD), lambda i: (i, 0, 0)),
                   pl.BlockSpec((nb, w, BKV_W), lambda i: (i, 0, 0))],
        out_shape=[jax.ShapeDtypeStruct((n, t_new, B_HEADS * B_HD), F32),
                   jax.ShapeDtypeStruct((n, w, BKV_W), F32)],
        compiler_params=_cparams(("parallel",)),
        name="swa_sample",
    )(p3, p3, win, meta, qnw, knw, sinks)


GDN_ROWS = 64
GDN_R = A_HEADS * GDN_ROWS
GDN_PAIRS = A_HEADS // 2
PAIR_W = 2 * A_DK
GDN_BATCH = 6


def _mm(a, b):
    return jnp.dot(a.astype(BF16), b.astype(BF16), preferred_element_type=F32)


def _bmm(a, b):
    return jnp.einsum('bij,bjk->bik', a.astype(BF16), b.astype(BF16), preferred_element_type=F32)


def _block_diag2(x):
    z = jnp.zeros(x.shape[:-1] + (A_DK,), x.dtype)
    return jnp.concatenate([jnp.concatenate([x[..., :A_DK], z], axis=-1),
                            jnp.concatenate([z, x[..., A_DK:]], axis=-1)], axis=-2)


def _gdn_prep_kernel(x_ref, halo_ref, ab_ref, cw_ref, alog_ref, dtb_ref, u_ref, wq_ref, qkk_ref, gl_ref,
                     *, sb, cpb, bb, n_invalid, seq_halo):
    t = pl.program_id(1)
    ns = x_ref.shape[0]
    rows = x_ref.shape[1] // cpb
    c = GDN_ROWS
    r = GDN_R
    halo = halo_ref[...]
    if seq_halo:
        hrow = lax.broadcasted_iota(jnp.int32, (1, GDN_HALO, 1), 1) + (t * cpb * rows - GDN_HALO)
        halo = jnp.where(hrow >= n_invalid, halo, 0.0)
    cw = cw_ref[...]
    alog = alog_ref[...]
    dtb = dtb_ref[...]
    ri = lax.broadcasted_iota(jnp.int32, (r, r), 0)
    ci = lax.broadcasted_iota(jnp.int32, (r, r), 1)
    same = (ri // sb) == (ci // sb)
    incl = same & (ri >= ci)
    strict = same & (ri > ci)
    eye = (ri == ci).astype(F32)
    rc_ = lax.broadcasted_iota(jnp.int32, (c, c), 0)
    cc_ = lax.broadcasted_iota(jnp.int32, (c, c), 1)
    same_c = (rc_ // sb) == (cc_ // sb)
    lmat = jnp.concatenate([(same_c & (rc_ >= cc_)).astype(F32), same_c.astype(F32)], axis=0)
    sel = (lax.broadcasted_iota(jnp.int32, (SUBLANES, 2 * c), 1)
           == c + (lax.broadcasted_iota(jnp.int32, (SUBLANES, 2 * c), 0) * SUBLANES // sb) * sb).astype(F32)
    off = GDN_HALO - (A_CONV - 1)
    assert ns == 1 or cpb == 1
    hi = lambda m, v: jnp.einsum('bij,bjk->bik', jnp.broadcast_to(m, (bb,) + m.shape), v,
                                 precision=HIGHEST, preferred_element_type=F32)
    for c0 in range(0, cpb, bb):
        nr = bb * rows
        x = x_ref[:, c0 * rows:c0 * rows + nr, :]
        ab = ab_ref[:, c0 * rows:c0 * rows + nr, :].reshape(bb, c, LANES)
        if n_invalid:
            row0 = (t * cpb + c0) * rows
            xrow = lax.broadcasted_iota(jnp.int32, (1, nr, 1), 1) + row0
            x = jnp.where(xrow >= n_invalid, x, 0.0)
        ext = jnp.concatenate([halo, x], axis=1)
        acc = cw[0:1, :] * ext[:, off:off + nr, :]
        for k in range(1, A_CONV):
            acc = acc + cw[k:k + 1, :] * ext[:, off + k:off + k + nr, :]
        y = _silu(acc).reshape(bb, c, QKV_W)
        halo = x[:, nr - GDN_HALO:nr, :]

        sp = jnp.maximum(ab + dtb, 0.0) + jnp.log1p(jnp.exp(-jnp.abs(ab + dtb)))
        gfull = -jnp.exp(alog) * sp
        bfull = jax.nn.sigmoid(ab)
        if n_invalid:
            valid = (lax.broadcasted_iota(jnp.int32, (bb, c, 1), 0) * c
                     + lax.broadcasted_iota(jnp.int32, (bb, c, 1), 1) + row0) >= n_invalid
            gfull = jnp.where(valid, gfull, 0.0)
            bfull = jnp.where(valid, bfull, 0.0)
        bcol = jnp.concatenate([bfull[:, :, A_HEADS + h:A_HEADS + h + 1] for h in range(A_HEADS)], axis=1)
        gg = hi(lmat, gfull)
        ggt = jnp.stack([gg[i].T for i in range(bb)])
        gc = jnp.concatenate([gg[:, :c, h:h + 1] for h in range(A_HEADS)], axis=1)
        grow = jnp.concatenate([ggt[:, h:h + 1, :c] for h in range(A_HEADS)], axis=2)
        drow = jnp.concatenate([ggt[:, h:h + 1, c:] for h in range(A_HEADS)], axis=2) - grow
        g8 = jnp.exp(hi(sel, gg))
        decay = jnp.where(incl, jnp.exp(jnp.where(incl, gc - grow, 0.0)), 0.0)

        qs, ks, vs = [], [], []
        for h in range(A_HEADS):
            q = y[:, :, h * A_DK:(h + 1) * A_DK]
            k = y[:, :, A_KW + h * A_DK:A_KW + (h + 1) * A_DK]
            qs.append(q * lax.rsqrt(jnp.sum(q * q, axis=-1, keepdims=True) + EPS) * (A_DK ** -0.5))
            ks.append(k * lax.rsqrt(jnp.sum(k * k, axis=-1, keepdims=True) + EPS))
            vs.append(y[:, :, 2 * A_KW + h * A_DK:2 * A_KW + (h + 1) * A_DK])
        qr = jnp.concatenate(qs, axis=1)
        kr = jnp.concatenate(ks, axis=1)
        vr = jnp.concatenate(vs, axis=1)
        krt = jnp.stack([kr[i].T for i in range(bb)])
        kq = _bmm(jnp.concatenate([kr, qr], axis=1), krt)
        a = jnp.where(strict, kq[:, :r] * decay * bcol, 0.0)
        qk = kq[:, r:] * decay
        tinv = eye - a
        ak = a
        kpow = 2
        while kpow < sb:
            ak = _bmm(ak, ak)
            tinv = tinv + _bmm(tinv, ak)
            kpow *= 2
        eg = jnp.exp(gc)
        uw = _bmm(tinv, jnp.concatenate([vr * bcol, kr * (bcol * eg)], axis=2))
        ur = uw[:, :, :A_DK]
        wr = uw[:, :, A_DK:]
        qd = qr * eg
        kdt = krt * jnp.exp(drow)
        cs = slice(c0, c0 + bb)
        for p in range(GDN_PAIRS):
            r0 = slice(2 * p * c, (2 * p + 1) * c)
            r1 = slice((2 * p + 1) * c, (2 * p + 2) * c)
            pl_ = slice(2 * p * c, (2 * p + 2) * c)
            u_ref[cs, p] = jnp.concatenate([ur[:, r0], ur[:, r1]], axis=2)
            wq = jnp.concatenate([jnp.concatenate([wr[:, r0], wr[:, r1]], axis=2),
                                  jnp.concatenate([qd[:, r0], qd[:, r1]], axis=2)], axis=1)
            wq_ref[cs, p] = wq.astype(wq_ref.dtype)
            qkk = jnp.concatenate([qk[:, r0, pl_] + qk[:, r1, pl_], kdt[:, :, pl_]], axis=1)
            qkk_ref[cs, p] = qkk.astype(qkk_ref.dtype)
            gl_ref[cs, p] = jnp.concatenate(
                [jnp.broadcast_to(g8[:, :, 2 * p:2 * p + 1], (bb, SUBLANES, A_DK)),
                 jnp.broadcast_to(g8[:, :, 2 * p + 1:2 * p + 2], (bb, SUBLANES, A_DK))], axis=2)


def _gdn_prep(x_arr, halo_arr, halo_spec, cw, alog, dtb, ns, rows, cpb, bb, sb, n_invalid, op_dtype):
    n, l, _ = x_arr.shape
    grid = (n // ns, l // (cpb * rows))
    units = grid[0] * grid[1] * cpb
    kern = functools.partial(_gdn_prep_kernel, sb=sb, cpb=cpb, bb=bb, n_invalid=n_invalid,
                             seq_halo=halo_arr is x_arr)
    full = lambda shape: pl.BlockSpec(shape, lambda i, t: (0,) * len(shape))
    steps = grid[1]
    out_map = lambda i, t: (i * steps + t, 0, 0, 0)
    c = GDN_ROWS
    return pl.pallas_call(
        kern,
        grid=grid,
        in_specs=[pl.BlockSpec((ns, cpb * rows, QKV_W), lambda i, t: (i, t, 0)),
                  halo_spec,
                  pl.BlockSpec((ns, cpb * rows, LANES), lambda i, t: (i, t, AB_BLK)),
                  full(cw.shape), full(alog.shape), full(dtb.shape)],
        out_specs=[pl.BlockSpec((cpb, GDN_PAIRS, c, PAIR_W), out_map),
                   pl.BlockSpec((cpb, GDN_PAIRS, 2 * c, PAIR_W), out_map),
                   pl.BlockSpec((cpb, GDN_PAIRS, c + A_DK, 2 * c), out_map),
                   pl.BlockSpec((cpb, GDN_PAIRS, SUBLANES, PAIR_W), out_map)],
        out_shape=[jax.ShapeDtypeStruct((units, GDN_PAIRS, c, PAIR_W), F32),
                   jax.ShapeDtypeStruct((units, GDN_PAIRS, 2 * c, PAIR_W), op_dtype),
                   jax.ShapeDtypeStruct((units, GDN_PAIRS, c + A_DK, 2 * c), op_dtype),
                   jax.ShapeDtypeStruct((units, GDN_PAIRS, SUBLANES, PAIR_W), F32)],
        compiler_params=_cparams(("parallel", "arbitrary")),
        name="gdn_prep",
    )(x_arr, halo_arr, x_arr, cw, alog, dtb)


def _gdn_out(o_pair, z_ref, o_ref, nw, idx, p):
    for i in range(2):
        h = 2 * p + i
        hs = slice(h * A_DK, (h + 1) * A_DK)
        o = _rms(o_pair[..., i * A_DK:(i + 1) * A_DK], nw) * _silu(z_ref[idx + (hs,)])
        o_ref[idx + (hs,)] = o.astype(o_ref.dtype)


def _gdn_scan_kernel(u_ref, wq_ref, qkk_ref, gl_ref, z_ref, nw_ref, o_ref, sout_ref, s_ref):
    t = pl.program_id(0)
    n_seq = u_ref.shape[0]
    c = GDN_ROWS
    nw = nw_ref[...]

    @pl.when(t == 0)
    def _():
        s_ref[...] = jnp.zeros(s_ref.shape, F32)

    for p in range(GDN_PAIRS):
        s = s_ref[:, p]
        r1 = jnp.einsum('bmk,bkn->bmn', wq_ref[:, 0, p], _block_diag2(s.astype(BF16)),
                        preferred_element_type=F32)
        vnew = u_ref[:, 0, p] - r1[:, :c]
        r2 = jnp.einsum('bmk,bkn->bmn', qkk_ref[:, 0, p], _block_diag2(vnew.astype(BF16)),
                        preferred_element_type=F32)
        s_ref[:, p] = s * gl_ref[:, 0, p][:, 0:1, :] + r2[:, c:]
        _gdn_out(r1[:, c:] + r2[:, :c], z_ref, o_ref, nw, (slice(None), slice(None)), p)

    @pl.when(t == pl.num_programs(0) - 1)
    def _():
        for n in range(n_seq):
            for p in range(GDN_PAIRS):
                s = s_ref[n, p]
                sout_ref[n, 2 * p] = s[:, :A_DK]
                sout_ref[n, 2 * p + 1] = s[:, A_DK:]


def _gdn_scan(u, wq, qkk, gl, p3, nw):
    n, l, _ = p3.shape
    sshape = (n, A_HEADS, A_DK, A_DK)
    c = GDN_ROWS
    steps = l // c
    v5 = lambda a: a.reshape((n, steps) + a.shape[1:])
    u, wq, qkk, gl = v5(u), v5(wq), v5(qkk), v5(gl)
    unit = lambda a: pl.BlockSpec((n, 1) + a.shape[2:], lambda t: (0, t, 0, 0, 0))
    vw = A_HEADS * A_DK
    return pl.pallas_call(
        _gdn_scan_kernel,
        grid=(steps,),
        in_specs=[unit(u), unit(wq), unit(qkk), unit(gl),
                  pl.BlockSpec((n, c, vw), lambda t: (0, t, Z_BLK)),
                  pl.BlockSpec(nw.shape, lambda t: (0, 0))],
        out_specs=[pl.BlockSpec((n, c, vw), lambda t: (0, t, 0)),
                   pl.BlockSpec(sshape, lambda t: (0, 0, 0, 0))],
        out_shape=[jax.ShapeDtypeStruct((n, l, vw), BF16),
                   jax.ShapeDtypeStruct(sshape, F32)],
        scratch_shapes=[pltpu.VMEM((n, GDN_PAIRS, A_DK, PAIR_W), F32)],
        compiler_params=_cparams(("arbitrary",)),
        name="gdn_scan",
    )(u, wq, qkk, gl, p3, nw)


def _gdn_step_kernel(u_ref, wq_ref, qkk_ref, gl_ref, z_ref, s0_ref, nw_ref, *rest):
    o_ref, sout_ref = rest[-2:]
    ns, rows, _ = z_ref.shape
    c = GDN_ROWS
    nw = nw_ref[...]
    lane_seq = (lax.broadcasted_iota(jnp.int32, (A_DK, 2 * c), 1) % c) // rows
    for p in range(GDN_PAIRS):
        wq = wq_ref[0, p]
        qkk = qkk_ref[0, p]
        gl = gl_ref[0, p]
        ws, qs, states = [], [], []
        for s_i in range(ns):
            sl = slice(s_i * rows, (s_i + 1) * rows)
            s = jnp.concatenate([s0_ref[s_i, 2 * p], s0_ref[s_i, 2 * p + 1]], axis=1)
            states.append(s)
            lhs = jnp.concatenate([wq[sl], wq[c + s_i * rows:c + (s_i + 1) * rows]], axis=0)
            r1 = _mm(lhs, _block_diag2(s.astype(BF16)))
            ws.append(r1[:rows])
            qs.append(r1[rows:])
        vnew = u_ref[0, p] - jnp.concatenate(ws, axis=0)
        bdv = _block_diag2(vnew.astype(BF16))
        o = jnp.concatenate(qs, axis=0) + _mm(qkk[:c], bdv)
        _gdn_out(o.reshape(ns, rows, PAIR_W), z_ref, o_ref, nw, (slice(None), slice(None)), p)
        kdt = qkk[c:]
        lhs = jnp.concatenate([jnp.where(lane_seq == s_i, kdt, 0.0) for s_i in range(ns)], axis=0)
        upd = _mm(lhs, bdv)
        for s_i in range(ns):
            s_new = states[s_i] * gl[s_i:s_i + 1, :] + upd[s_i * A_DK:(s_i + 1) * A_DK]
            sout_ref[s_i, 2 * p] = s_new[:, :A_DK]
            sout_ref[s_i, 2 * p + 1] = s_new[:, A_DK:]


def _gdn_step(u, wq, qkk, gl, p3, s_all, layer, s_prev, nw, ns):
    n, rows, _ = p3.shape
    unit = lambda a: pl.BlockSpec((1,) + a.shape[1:], lambda i: (i, 0, 0, 0))
    vw = A_HEADS * A_DK
    sblk = pl.BlockSpec((None, ns) + s_all.shape[2:], lambda i: (layer, i, 0, 0, 0))
    in_specs = [unit(u), unit(wq), unit(qkk), unit(gl),
                pl.BlockSpec((ns, rows, vw), lambda i: (i, 0, Z_BLK)),
                sblk,
                pl.BlockSpec(nw.shape, lambda i: (0, 0))]
    args = [u, wq, qkk, gl, p3, s_all, nw]
    aliases = {}
    if s_prev is not None:
        in_specs.append(pl.BlockSpec(memory_space=pl.ANY))
        args.append(s_prev)
        aliases = {len(args) - 1: 1}
    return pl.pallas_call(
        _gdn_step_kernel,
        grid=(n // ns,),
        in_specs=in_specs,
        out_specs=[pl.BlockSpec((ns, rows, vw), lambda i: (i, 0, 0)), sblk],
        out_shape=[jax.ShapeDtypeStruct((n, rows, vw), F32),
                   jax.ShapeDtypeStruct(s_all.shape, F32)],
        input_output_aliases=aliases,
        compiler_params=_cparams(("parallel",)),
        name="gdn_step",
    )(*args)


def _gdn(p3, state, hist, cw, alog, dtb, nw, cpb, n_invalid):
    n, l, _ = p3.shape
    c = GDN_ROWS
    if hist is None:
        steps8 = cpb * c // GDN_HALO
        halo_spec = pl.BlockSpec((1, GDN_HALO, QKV_W), lambda i, t: (i, jnp.maximum(t * steps8 - 1, 0), 0))
        u, wq, qkk, gl = _gdn_prep(p3, p3, halo_spec, cw, alog, dtb, 1, c, cpb, GDN_BATCH, c, n_invalid, BF16)
        return _gdn_scan(u, wq, qkk, gl, p3, nw)
    ns = c // l
    halo_spec = pl.BlockSpec((ns, GDN_HALO, QKV_W), lambda i, t: (i, 0, 0))
    u, wq, qkk, gl = _gdn_prep(p3, hist, halo_spec, cw, alog, dtb, ns, l, 1, 1, l, n_invalid, F32)
    s_all, layer, s_prev = state
    return _gdn_step(u, wq, qkk, gl, p3, s_all, layer, s_prev, nw, ns)


def _merge_kernel(oa_ref, ob_ref, oc_ref, gl_ref, x_ref, wb_ref, wo_ref, out_ref):
    d = x_ref.shape[1]
    merged = None
    for r, o_ref in enumerate((oa_ref, ob_ref, oc_ref)):
        pb = jnp.dot(o_ref[...].astype(BF16), wb_ref[r], preferred_element_type=F32)
        term = jax.nn.sigmoid(gl_ref[:, r * d:(r + 1) * d]) * pb
        merged = term if merged is None else merged + term
    out_ref[...] = x_ref[...] + jnp.dot(merged.astype(BF16), wo_ref[...], preferred_element_type=F32)


def _merge(oa, ob, oc, p2, x2, wb, wo, tm):
    t, d = x2.shape
    bw = oa.shape[1]
    row = lambda w, blk: pl.BlockSpec((tm, w), lambda i: (i, blk))
    return pl.pallas_call(
        _merge_kernel,
        grid=(t // tm,),
        in_specs=[row(bw, 0), row(bw, 0), row(bw, 0), row(3 * d, GL_BLK), row(d, 0),
                  pl.BlockSpec(wb.shape, lambda i: (0, 0, 0)),
                  pl.BlockSpec(wo.shape, lambda i: (0, 0))],
        out_specs=row(d, 0),
        out_shape=jax.ShapeDtypeStruct((t, d), F32),
        compiler_params=_cparams(("parallel",)),
        name="merge",
    )(oa, ob, oc, p2, x2, wb, wo)


def _route(logits):
    lane = lax.broadcasted_iota(jnp.int32, logits.shape, 1)
    is_g = (lane >= N_EXPERTS) & (lane < N_EXPERTS + N_GROUPS)
    gl = jnp.where(is_g, logits, NEG)
    gmax = jnp.max(gl, axis=-1, keepdims=True)
    gidx = jnp.min(jnp.where(gl == gmax, lane - N_EXPERTS, LANES), axis=-1, keepdims=True)
    gw = 1.0 / jnp.sum(jnp.where(is_g, jnp.exp(gl - gmax), 0.0), axis=-1, keepdims=True)
    in_grp = (lane < N_EXPERTS) & ((lane // EXP_PER_GROUP) == gidx)
    el = jnp.where(in_grp, logits, NEG)
    v1 = jnp.max(el, axis=-1, keepdims=True)
    i1 = jnp.min(jnp.where(el == v1, lane, LANES), axis=-1, keepdims=True)
    el2 = jnp.where(lane == i1, NEG, el)
    v2 = jnp.max(el2, axis=-1, keepdims=True)
    i2 = jnp.min(jnp.where(el2 == v2, lane, LANES), axis=-1, keepdims=True)
    e21 = jnp.exp(v2 - v1)
    w1 = gw / (1.0 + e21)
    w2 = gw * e21 / (1.0 + e21)
    return jnp.where(lane == i1, w1, 0.0) + jnp.where(lane == i2, w2, 0.0)


def _moe_kernel(x_ref, nw_ref, wr_ref, br_ref, wgu_ref, wd_ref, out_ref, h_ref, cmb_ref):
    g = pl.program_id(1)

    @pl.when(g == 0)
    def _():
        x = x_ref[...]
        h = _rms(x, nw_ref[...])
        h_hi = h.astype(BF16)
        h_ref[...] = h_hi
        h_lo = (h - h_hi.astype(F32)).astype(BF16)
        l2 = jnp.dot(h_hi, wr_ref[...], preferred_element_type=F32)
        logits = (l2[:, :LANES] + l2[:, LANES:]
                  + jnp.dot(h_lo, wr_ref[:, :LANES], preferred_element_type=F32) + br_ref[...])
        cmb_ref[...] = _route(logits)
        out_ref[...] = x

    gw = wd_ref.shape[1]
    de = gw // EXP_PER_GROUP
    gu = jnp.dot(h_ref[...], wgu_ref[0], preferred_element_type=F32)
    hid = _silu(gu[:, :gw]) * gu[:, gw:]
    cmb = cmb_ref[...]
    lane = lax.broadcasted_iota(jnp.int32, cmb.shape, 1)
    parts = []
    for e in range(EXP_PER_GROUP):
        ce = jnp.sum(jnp.where(lane == g * EXP_PER_GROUP + e, cmb, 0.0), axis=-1, keepdims=True)
        parts.append((hid[:, e * de:(e + 1) * de] * ce).astype(BF16))
    out_ref[...] += jnp.dot(jnp.concatenate(parts, axis=1), wd_ref[0], preferred_element_type=F32)


def _moe(x2, nw, wr, br, wgu, wd, tm):
    t, d = x2.shape
    ng, _, gw2 = wgu.shape
    return pl.pallas_call(
        _moe_kernel,
        grid=(t // tm, ng),
        in_specs=[pl.BlockSpec((tm, d), lambda i, g: (i, 0)),
                  pl.BlockSpec((1, d), lambda i, g: (0, 0)),
                  pl.BlockSpec(wr.shape, lambda i, g: (0, 0)),
                  pl.BlockSpec(br.shape, lambda i, g: (0, 0)),
                  pl.BlockSpec((1, d, gw2), lambda i, g: (g, 0, 0)),
                  pl.BlockSpec((1, gw2 // 2, d), lambda i, g: (g, 0, 0))],
        out_specs=pl.BlockSpec((tm, d), lambda i, g: (i, 0)),
        out_shape=jax.ShapeDtypeStruct((t, d), F32),
        scratch_shapes=[pltpu.VMEM((tm, d), BF16), pltpu.VMEM((tm, LANES), F32)],
        compiler_params=_cparams(("parallel", "arbitrary")),
        name="moe",
    )(x2, nw, wr, br, wgu, wd)


def _pad_lanes(v, width):
    v = v.reshape(1, -1)
    return jnp.pad(v, ((0, 0), (0, width - v.shape[1])))


def _layer_weights(l, norm1_w, norm2_w, w_in, gdn_conv_w, gdn_a_log, gdn_dt_bias, gdn_norm_w,
                   swa_q_norm_w, swa_k_norm_w, swa_sinks, conv_dw_w, conv_dw_b, conv_ln_w, conv_ln_b,
                   w_branch, w_out, router_group_w, router_group_b, router_expert_w, router_expert_b,
                   moe_w_gate, moe_w_up, moe_w_down):
    d = w_in.shape[1]
    wi = w_in[l]
    o_ab = 4 * A_KW
    o_bq = o_ab + 2 * A_HEADS
    o_cu = o_bq + B_HEADS * B_HD + BKV_W
    o_gl = o_cu + 2 * C_CH
    w_perm = jnp.concatenate(
        [wi[:, :o_ab], wi[:, o_cu:o_gl], wi[:, o_gl:], wi[:, o_bq:o_cu], wi[:, o_ab:o_bq],
         jnp.zeros((d, LANES - 2 * A_HEADS), wi.dtype)], axis=1).astype(BF16)
    assert w_perm.shape[1] == PROJ_W
    wr = jnp.concatenate([router_expert_w[l], router_group_w[l],
                          jnp.zeros((d, LANES - N_EXPERTS - N_GROUPS), F32)], axis=1)
    br = _pad_lanes(jnp.concatenate([router_expert_b[l], router_group_b[l]]), LANES)
    wr_hi = wr.astype(BF16)
    wr = jnp.concatenate([wr_hi, (wr - wr_hi.astype(F32)).astype(BF16)], axis=1)
    return dict(
        n1=norm1_w[l].reshape(1, d), n2=norm2_w[l].reshape(1, d), w_in=w_perm,
        gdn_cw=gdn_conv_w[l], alog=_pad_lanes(gdn_a_log[l], LANES), dtb=_pad_lanes(gdn_dt_bias[l], LANES),
        gdn_nw=gdn_norm_w[l].reshape(1, -1),
        qnw=swa_q_norm_w[l].reshape(1, -1), knw=swa_k_norm_w[l].reshape(1, -1), sinks=swa_sinks[l].reshape(1, -1),
        qnw_heads=jnp.tile(swa_q_norm_w[l].reshape(1, -1), (1, B_HEADS)),
        knw_heads=jnp.tile(swa_k_norm_w[l].reshape(1, -1), (1, B_KV_HEADS)),
        dww=conv_dw_w[l], dwb=conv_dw_b[l].reshape(1, -1), lnw=conv_ln_w[l].reshape(1, -1),
        lnb=conv_ln_b[l].reshape(1, -1),
        wb=w_branch[l].astype(BF16), wo=w_out[l].astype(BF16), wr=wr, br=br,
        wgu=jnp.concatenate([_group_cols(moe_w_gate[l]), _group_cols(moe_w_up[l])], axis=2),
        wd=moe_w_down[l].astype(BF16).reshape(N_GROUPS, -1, d))


def _group_cols(w):
    ne, d, f = w.shape
    w = w.astype(BF16).reshape(N_GROUPS, EXP_PER_GROUP, d, f)
    return jnp.transpose(w, (0, 2, 1, 3)).reshape(N_GROUPS, d, EXP_PER_GROUP * f)


def _tiles(n, l):
    prompt = l > BLOCK
    t = n * l
    if prompt:
        return dict(proj_tm=1536, tok_tm=512, moe_tm=1056, conf_nb=1, conf_tm=384, conf_rc=64,
                    gdn_cpb=6, swa_bps=3, n_invalid=META_PAD, branch_dtype=BF16)
    return dict(proj_tm=t, tok_tm=512, moe_tm=t, conf_nb=16, conf_tm=l, conf_rc=l,
                gdn_cpb=1, n_invalid=0, branch_dtype=F32)


def _layer(x3, lw, gdn_state, gdn_hist, conf_hist, swa_cache):
    n, l, d = x3.shape
    tl = _tiles(n, l)
    x2 = x3.reshape(n * l, d)
    p2 = _inproj(x2, lw['n1'], lw['w_in'], tl['proj_tm'], PROJ_W // 5)
    p3 = p2.reshape(n, l, PROJ_W)
    o_a, s_new = _gdn(p3, gdn_state, gdn_hist, lw['gdn_cw'], lw['alog'], lw['dtb'], lw['gdn_nw'],
                      tl['gdn_cpb'], tl['n_invalid'])
    gh_new = p3[:, l - (A_CONV - 1):, :QKV_W]
    if swa_cache is None:
        o_b, kvn = _swa_prompt(p3, lw['qnw_heads'], lw['knw_heads'], lw['sinks'], tl['swa_bps'])
        kv = lambda sl: kvn[:, sl].reshape(n, -1, 2, B_KV_HEADS, B_HD)
        swa_new = (kv(slice(META_PAD, BLOCK)), kv(slice(l - WINDOW, l)))
    else:
        win, meta = swa_cache
        o_b, nwin = _swa_sample(p3, win.reshape(n, win.shape[1], BKV_W), meta.reshape(n, N_META, BKV_W),
                                lw['qnw'], lw['knw'], lw['sinks'], 16)
        swa_new = nwin.reshape(win.shape)
    o_c, ch_new = _conformer(p3, conf_hist, lw['dww'], lw['dwb'], lw['lnw'], lw['lnb'],
                             tl['conf_nb'], tl['conf_tm'], tl['conf_rc'], tl['n_invalid'], tl['branch_dtype'])
    t = n * l
    x2 = _merge(o_a.reshape(t, -1), o_b.reshape(t, -1), o_c.reshape(t, -1), p2, x2, lw['wb'], lw['wo'],
                tl['tok_tm'])
    x2 = _moe(x2, lw['n2'], lw['wr'], lw['br'], lw['wgu'], lw['wd'], tl['moe_tm'])
    return x2.reshape(n, l, d), s_new, gh_new, swa_new, ch_new


def kernel(x_prompt, x_sample, state_gdn, cache_gdn_conv, cache_swa_kv, cache_meta_kv, cache_conv, meta_tokens, norm1_w, norm2_w, w_in, gdn_conv_w, gdn_a_log, gdn_dt_bias, gdn_norm_w, swa_q_norm_w, swa_k_norm_w, swa_sinks, conv_dw_w, conv_dw_b, conv_ln_w, conv_ln_b, w_branch, w_out, router_group_w, router_group_b, router_expert_w, router_expert_b, moe_w_gate, moe_w_up, moe_w_down):
    dtp = x_prompt.dtype
    n_p, _, d = x_prompt.shape
    n_s = x_sample.shape[0]
    depth = w_in.shape[0]
    xp = jnp.pad(x_prompt, ((0, 0), (BLOCK, 0), (0, 0)))
    xp = lax.dynamic_update_slice(
        xp, jnp.broadcast_to(meta_tokens.astype(dtp)[None], (n_p, N_META, d)), (0, META_PAD, 0))
    xs = x_sample
    state_s = None
    outs_p, outs_s = [], []
    for l in range(depth):
        lw = _layer_weights(l, norm1_w, norm2_w, w_in, gdn_conv_w, gdn_a_log, gdn_dt_bias, gdn_norm_w,
                            swa_q_norm_w, swa_k_norm_w, swa_sinks, conv_dw_w, conv_dw_b, conv_ln_w, conv_ln_b,
                            w_branch, w_out, router_group_w, router_group_b, router_expert_w, router_expert_b,
                            moe_w_gate, moe_w_up, moe_w_down)
        xp, s_p, gh_p, (mkv_p, wkv_p), ch_p = _layer(
            xp, lw, None, None, jnp.zeros((n_p, C_CONV - 1, C_CH), dtp), None)
        outs_p.append((s_p, gh_p, wkv_p, mkv_p, ch_p))
        gh0 = jnp.pad(cache_gdn_conv[l], ((0, 0), (GDN_HALO - (A_CONV - 1), 0), (0, 0)))
        xs, state_s, gh_s, wkv_s, ch_s = _layer(xs, lw, (state_gdn, l, state_s), gh0, cache_conv[l],
                                                (cache_swa_kv[l], cache_meta_kv[l]))
        outs_s.append((gh_s, wkv_s, ch_s))
    stack = lambda outs, i: jnp.stack([o[i] for o in outs])
    return (xp[:, BLOCK:], xs,
            stack(outs_p, 0), stack(outs_p, 1), stack(outs_p, 2), stack(outs_p, 3), stack(outs_p, 4),
            state_s, stack(outs_s, 0), stack(outs_s, 1), stack(outs_s, 2))
```

```python
import functools

import jax
import jax.numpy as jnp
from jax import lax
from jax.experimental import pallas as pl
from jax.experimental.pallas import tpu as pltpu

F32 = jnp.float32
BF16 = jnp.bfloat16
HIGHEST = lax.Precision.HIGHEST
EPS = 1e-6
NEG = -1e30

VMEM_LIMIT_BYTES = 56 * 1024 * 1024
LANES = 128
SUBLANES = 8

PAST_LEN = 16384
N_META = 16
BLOCK = 128
WINDOW = 128
META_PAD = BLOCK - N_META
A_HEADS = 4
A_DK = 128
A_CONV = 4
B_HEADS = 8
B_KV_HEADS = 2
B_GROUP = B_HEADS // B_KV_HEADS
B_HD = 64
C_CH = 512
C_CONV = 31
BRANCH_W = 512
N_GROUPS = 4
EXP_PER_GROUP = 4
N_EXPERTS = N_GROUPS * EXP_PER_GROUP

A_KW = A_HEADS * A_DK
QKV_W = 3 * A_KW
Z_BLK = QKV_W // BRANCH_W
CU_BLK = 2
GL_BLK = 1
BQ_BLK = 12
BKV_W = 2 * B_KV_HEADS * B_HD
BKV_BLK = 26
AB_BLK = 54
PROJ_W = 7040
CONF_HALO = 32
GDN_HALO = SUBLANES


def _rms(x, w):
    return x * lax.rsqrt(jnp.mean(x * x, axis=-1, keepdims=True) + EPS) * w


def _silu(x):
    return x * jax.nn.sigmoid(x)


def _cparams(sem):
    return pltpu.CompilerParams(dimension_semantics=sem, vmem_limit_bytes=VMEM_LIMIT_BYTES)


def _inproj_kernel(x_ref, nw_ref, w_ref, o_ref, h_ref):
    @pl.when(pl.program_id(1) == 0)
    def _():
        h_ref[...] = _rms(x_ref[...], nw_ref[...]).astype(BF16)

    o_ref[...] = jnp.dot(h_ref[...], w_ref[...], preferred_element_type=F32)


def _inproj(x2, nw, w, tm, tn):
    t, d = x2.shape
    wd = w.shape[1]
    return pl.pallas_call(
        _inproj_kernel,
        grid=(t // tm, wd // tn),
        in_specs=[pl.BlockSpec((tm, d), lambda i, j: (i, 0)),
                  pl.BlockSpec((1, d), lambda i, j: (0, 0)),
                  pl.BlockSpec((d, tn), lambda i, j: (0, j))],
        out_specs=pl.BlockSpec((tm, tn), lambda i, j: (i, j)),
        out_shape=jax.ShapeDtypeStruct((t, wd), F32),
        scratch_shapes=[pltpu.VMEM((tm, d), BF16)],
        compiler_params=_cparams(("parallel", "arbitrary")),
        name="inproj",
    )(x2, nw, w)


def _conf_kernel(cu_ref, hist_ref, dww_ref, dwb_ref, lnw_ref, lnb_ref, y_ref, nh_ref, e_ref, c_ref,
                 *, tm, rc, n_invalid):
    t = pl.program_id(1)
    nb = cu_ref.shape[0]
    halo = CONF_HALO

    @pl.when(t == 0)
    def _():
        e_ref[:, 0:SUBLANES, :] = jnp.zeros((nb, SUBLANES, C_CH), F32)
        e_ref[:, halo + tm:halo + tm + SUBLANES, :] = jnp.zeros((nb, SUBLANES, C_CH), F32)
        e_ref[:, halo - (C_CONV - 1):halo, :] = hist_ref[...]

    @pl.when(t > 0)
    def _():
        e_ref[:, 0:halo, :] = e_ref[:, tm:tm + halo, :]

    cu = cu_ref[...]
    u = cu[:, :, :C_CH] * jax.nn.sigmoid(cu[:, :, C_CH:])
    if n_invalid:
        row = lax.broadcasted_iota(jnp.int32, (1, tm, 1), 1) + t * tm
        u = jnp.where(row >= n_invalid, u, 0.0)
    e_ref[:, halo:halo + tm, :] = u
    nh_ref[...] = e_ref[:, halo + tm - (C_CONV - 1):halo + tm, :]

    off = halo - (C_CONV - 1)
    for c in range(C_CH // LANES):
        cs = slice(c * LANES, (c + 1) * LANES)

        def rbody(r, carry, cs=cs):
            r0 = pl.multiple_of(r * rc, rc)
            blk = e_ref[:, pl.ds(r0, rc + halo + SUBLANES), cs]
            acc = jnp.zeros((nb, rc, LANES), F32)
            for s in range(SUBLANES):
                z = None
                for a in range((halo + SUBLANES) // SUBLANES):
                    k = SUBLANES * a + s - off
                    if 0 <= k < C_CONV:
                        term = dww_ref[k:k + 1, cs] * blk[:, SUBLANES * a:SUBLANES * a + rc + SUBLANES, :]
                        z = term if z is None else z + term
                acc = acc + z[:, s:s + rc, :]
            c_ref[:, pl.ds(r0, rc), cs] = acc + dwb_ref[:, cs]
            return carry

        lax.fori_loop(0, tm // rc, rbody, 0)

    y = c_ref[...]
    mu = jnp.mean(y, axis=-1, keepdims=True)
    var = jnp.mean(jnp.square(y - mu), axis=-1, keepdims=True)
    yn = (y - mu) * lax.rsqrt(var + EPS) * lnw_ref[...] + lnb_ref[...]
    y_ref[...] = _silu(yn).astype(y_ref.dtype)


def _conformer(p3, hist, dww, dwb, lnw, lnb, nb, tm, rc, n_invalid, out_dtype):
    n, l, _ = p3.shape
    kern = functools.partial(_conf_kernel, tm=tm, rc=rc, n_invalid=n_invalid)
    full = lambda shape: pl.BlockSpec(shape, lambda i, t: (0,) * len(shape))
    return pl.pallas_call(
        kern,
        grid=(n // nb, l // tm),
        in_specs=[pl.BlockSpec((nb, tm, 2 * C_CH), lambda i, t: (i, t, CU_BLK)),
                  pl.BlockSpec((nb, C_CONV - 1, C_CH), lambda i, t: (i, 0, 0)),
                  full(dww.shape), full(dwb.shape), full(lnw.shape), full(lnb.shape)],
        out_specs=[pl.BlockSpec((nb, tm, C_CH), lambda i, t: (i, t, 0)),
                   pl.BlockSpec((nb, C_CONV - 1, C_CH), lambda i, t: (i, 0, 0))],
        out_shape=[jax.ShapeDtypeStruct((n, l, C_CH), out_dtype),
                   jax.ShapeDtypeStruct((n, C_CONV - 1, C_CH), F32)],
        scratch_shapes=[pltpu.VMEM((nb, CONF_HALO + tm + SUBLANES, C_CH), F32), pltpu.VMEM((nb, tm, C_CH), F32)],
        compiler_params=_cparams(("parallel", "arbitrary")),
        name="conformer",
    )(p3, hist, dww, dwb, lnw, lnb)


SWA_ROWS = 128


def _softmax_sink_pv(s, mask, sink, v, dims):
    s = jnp.where(mask, s, NEG)
    m = jnp.maximum(jnp.max(s, axis=-1, keepdims=True), sink)
    p = jnp.exp(s - m)
    den = jnp.sum(p, axis=-1, keepdims=True) + jnp.exp(sink - m)
    return lax.dot_general(p, v, dims, preferred_element_type=F32) / den


def _swa_prompt_kernel(q_ref, kv_ref, kvp_ref, kvm_ref, qnw_ref, knw_ref, sink_ref, o_ref, kvn_ref, *, bps):
    step = pl.program_id(1)
    qnw = qnw_ref[...]
    knw = knw_ref[...]
    qw = B_HEADS * B_HD
    kw = B_KV_HEADS * B_HD
    nk = N_META + 2 * BLOCK
    rows = B_GROUP * BLOCK
    r = lax.broadcasted_iota(jnp.int32, (rows, nk), 0) % BLOCK
    c = lax.broadcasted_iota(jnp.int32, (rows, nk), 1)
    cp = c - N_META
    co = cp - BLOCK
    hrow = lax.broadcasted_iota(jnp.int32, (rows, 1), 0) // BLOCK
    m_meta = c < N_META
    m_own = (co >= 0) & (co <= r)
    m_prev = (cp >= 0) & (cp < BLOCK) & (cp > r)
    bias0 = jnp.where(m_meta & (META_PAD + c <= r), 0.0, NEG)
    bias1 = jnp.where(m_meta | m_own, 0.0, NEG)
    bias2 = jnp.where(m_meta | m_own | m_prev, 0.0, NEG)
    sinks = []
    for g in range(B_KV_HEADS):
        sink = jnp.zeros((rows, 1), F32)
        for i in range(B_GROUP):
            h = g * B_GROUP + i
            sink = jnp.where(hrow == i, sink_ref[0:1, h:h + 1], sink)
        sinks.append(sink)
    hb_q = (lax.broadcasted_iota(jnp.int32, (qw, qw), 0) // B_HD
            == lax.broadcasted_iota(jnp.int32, (qw, qw), 1) // B_HD).astype(BF16)
    hb_k = hb_q[:kw, :kw]
    half = lax.broadcasted_iota(jnp.int32, (1, LANES), 1) // B_HD

    def head_rms(x, hb, w):
        sq = x * x
        hi = sq.astype(BF16)
        lo = (sq - hi.astype(F32)).astype(BF16)
        ss = jnp.dot(hi, hb, preferred_element_type=F32) + jnp.dot(lo, hb, preferred_element_type=F32)
        return x * lax.rsqrt(ss * (1.0 / B_HD) + EPS) * w

    kvm = kvm_ref[META_PAD:BLOCK, :]
    km = head_rms(kvm[:, :kw], hb_k, knw)
    kv_halo = kvp_ref[...]
    q = q_ref[...]
    kv = kv_ref[...]
    k_own = head_rms(kv[:, :kw], hb_k, knw)
    kvn_ref[...] = jnp.concatenate([k_own, kv[:, kw:]], axis=-1)
    qn = head_rms(q, hb_q, qnw)
    tq = bps * BLOCK
    blocks = lambda x: x.reshape(bps, BLOCK, x.shape[-1])
    k_prev = jnp.concatenate([head_rms(kv_halo[:, :kw], hb_k, knw), k_own[:tq - BLOCK]], axis=0)
    v_prev = jnp.concatenate([kv_halo[:, kw:], kv[:tq - BLOCK, kw:]], axis=0)
    rep = lambda x: jnp.broadcast_to(x[None], (bps,) + x.shape)
    kcat_b = jnp.concatenate([rep(km), blocks(k_prev), blocks(k_own)], axis=1).astype(BF16)
    vcat_b = jnp.concatenate([rep(kvm[:, kw:]), blocks(v_prev), blocks(kv[:, kw:])], axis=1).astype(BF16)
    bias = jnp.stack([jnp.where(step * bps + b >= 2, bias2, jnp.where(step * bps + b >= 1, bias1, bias0))
                      for b in range(bps)])
    tiles = [qn[:, m * LANES:(m + 1) * LANES] for m in range(qw // LANES)]
    swapped = [pltpu.roll(tl_, B_HD, 1) for tl_ in tiles]
    outs = [None] * B_HEADS
    for g in range(B_KV_HEADS):
        in_g = half == g
        parts = []
        for i in range(B_GROUP):
            h = g * B_GROUP + i
            src = tiles[h // 2] if h % 2 == g else swapped[h // 2]
            parts.append(blocks(jnp.where(in_g, src, 0.0)))
        lhs = jnp.concatenate(parts, axis=1).astype(BF16)
        chunks = []
        for r0 in range(0, rows, SWA_ROWS):
            rc = slice(r0, r0 + SWA_ROWS)
            s = jnp.einsum('bqd,bkd->bqk', lhs[:, rc], kcat_b, preferred_element_type=F32)
            s = s * (B_HD ** -0.5) + bias[:, rc]
            sink = sinks[g][rc]
            m = jnp.maximum(jnp.max(s, axis=-1, keepdims=True), sink)
            p = jnp.exp(s - m)
            den = jnp.sum(p, axis=-1, keepdims=True) + jnp.exp(sink - m)
            chunks.append(jnp.einsum('bqk,bkd->bqd', p.astype(BF16), vcat_b, preferred_element_type=F32) / den)
        res = jnp.concatenate(chunks, axis=1)
        for i in range(B_GROUP):
            h = g * B_GROUP + i
            part = res[:, i * BLOCK:(i + 1) * BLOCK, :].reshape(tq, LANES)
            outs[h] = part if h % 2 == g else pltpu.roll(part, B_HD, 1)
    for m in range(qw // LANES):
        o_ref[:, m * LANES:(m + 1) * LANES] = jnp.where(half == 0, outs[2 * m], outs[2 * m + 1]).astype(o_ref.dtype)


def _swa_prompt(p3, qnw, knw, sinks, bps):
    n, l, _ = p3.shape
    full = lambda shape: pl.BlockSpec(shape, lambda i, j: (0,) * len(shape))
    tq = bps * BLOCK
    return pl.pallas_call(
        functools.partial(_swa_prompt_kernel, bps=bps),
        grid=(n, l // tq),
        in_specs=[pl.BlockSpec((None, tq, B_HEADS * B_HD), lambda i, j: (i, j, BQ_BLK)),
                  pl.BlockSpec((None, tq, BKV_W), lambda i, j: (i, j, BKV_BLK)),
                  pl.BlockSpec((None, BLOCK, BKV_W), lambda i, j: (i, jnp.maximum(j * bps - 1, 0), BKV_BLK)),
                  pl.BlockSpec((None, BLOCK, BKV_W), lambda i, j: (i, 0, BKV_BLK)),
                  full(qnw.shape), full(knw.shape), full(sinks.shape)],
        out_specs=[pl.BlockSpec((None, tq, B_HEADS * B_HD), lambda i, j: (i, j, 0)),
                   pl.BlockSpec((None, tq, BKV_W), lambda i, j: (i, j, 0))],
        out_shape=[jax.ShapeDtypeStruct((n, l, B_HEADS * B_HD), BF16),
                   jax.ShapeDtypeStruct((n, l, BKV_W), F32)],
        compiler_params=_cparams(("parallel", "arbitrary")),
        name="swa_prompt",
    )(p3, p3, p3, p3, qnw, knw, sinks)


def _swa_sample_kernel(q_ref, kvn_ref, win_ref, meta_ref, qnw_ref, knw_ref, sink_ref, *rest):
    o_ref, nwin_ref = rest[-2:]
    q = q_ref[...]
    kvn = kvn_ref[...]
    win = win_ref[...]
    meta = meta_ref[...]
    qnw = qnw_ref[...]
    knw = knw_ref[...]
    t_new = q.shape[1]
    w = win.shape[1]
    nk = N_META + w + t_new
    rows = B_GROUP * t_new
    tq = lax.broadcasted_iota(jnp.int32, (1, rows, nk), 1) % t_new
    c = lax.broadcasted_iota(jnp.int32, (1, rows, nk), 2)
    cw = c - N_META
    win_ok = (cw >= 0) & (cw < w) & (cw - w > tq - WINDOW) & (cw + (PAST_LEN - w) >= N_META)
    cn = cw - w
    new_ok = (cn >= 0) & (cn <= tq) & (cn > tq - WINDOW)
    mask = (c < N_META) | win_ok | new_ok
    hrow = lax.broadcasted_iota(jnp.int32, (1, rows, 1), 1) // t_new
    kn_parts = []
    for g in range(B_KV_HEADS):
        ks = slice(g * B_HD, (g + 1) * B_HD)
        vs = slice((B_KV_HEADS + g) * B_HD, (B_KV_HEADS + g + 1) * B_HD)
        k_new = _rms(kvn[:, :, ks], knw)
        kn_parts.append(k_new)
        kk = jnp.concatenate([meta[:, :, ks], win[:, :, ks], k_new], axis=1)
        vv = jnp.concatenate([meta[:, :, vs], win[:, :, vs], kvn[:, :, vs]], axis=1)
        qq = jnp.concatenate(
            [_rms(q[:, :, (g * B_GROUP + i) * B_HD:(g * B_GROUP + i + 1) * B_HD], qnw) for i in range(B_GROUP)],
            axis=1)
        sink = jnp.zeros((1, rows, 1), F32)
        for i in range(B_GROUP):
            h = g * B_GROUP + i
            sink = jnp.where(hrow == i, sink_ref[0:1, h:h + 1], sink)
        s = jnp.einsum('bqd,bkd->bqk', qq, kk, preferred_element_type=F32) * (B_HD ** -0.5)
        o = _softmax_sink_pv(s, mask, sink, vv, (((2,), (1,)), ((0,), (0,))))
        for i in range(B_GROUP):
            h = g * B_GROUP + i
            o_ref[:, :, h * B_HD:(h + 1) * B_HD] = o[:, i * t_new:(i + 1) * t_new, :]
    nwin_ref[:, 0:w - t_new, :] = win[:, t_new:w, :]
    nwin_ref[:, w - t_new:w, :] = jnp.concatenate(kn_parts + [kvn[:, :, B_KV_HEADS * B_HD:]], axis=-1)


def _swa_sample(p3, win_all, meta_all, layer, win_prev, qnw, knw, sinks, nb):
    n, t_new, _ = p3.shape
    w = win_all.shape[2]
    full = lambda shape: pl.BlockSpec(shape, lambda i: (0,) * len(shape))
    lblk = lambda rows: pl.BlockSpec((None, nb, rows, BKV_W), lambda i: (layer, i, 0, 0))
    in_specs = [pl.BlockSpec((nb, t_new, B_HEADS * B_HD), lambda i: (i, 0, BQ_BLK)),
                pl.BlockSpec((nb, t_new, BKV_W), lambda i: (i, 0, BKV_BLK)),
                lblk(w), lblk(N_META),
                full(qnw.shape), full(knw.shape), full(sinks.shape)]
    args = [p3, p3, win_all, meta_all, qnw, knw, sinks]
    aliases = {}
    if win_prev is not None:
        in_specs.append(pl.BlockSpec(memory_space=pl.ANY))
        args.append(win_prev)
        aliases = {len(args) - 1: 1}
    return pl.pallas_call(
        _swa_sample_kernel,
        grid=(n // nb,),
        in_specs=in_specs,
        out_specs=[pl.BlockSpec((nb, t_new, B_HEADS * B_HD), lambda i: (i, 0, 0)), lblk(w)],
        out_shape=[jax.ShapeDtypeStruct((n, t_new, B_HEADS * B_HD), F32),
                   jax.ShapeDtypeStruct(win_all.shape, F32)],
        input_output_aliases=aliases,
        compiler_params=_cparams(("parallel",)),
        name="swa_sample",
    )(*args)


GDN_ROWS = 64
GDN_R = A_HEADS * GDN_ROWS
GDN_PAIRS = A_HEADS // 2
PAIR_W = 2 * A_DK
GDN_BATCH = 6


def _mm(a, b):
    return jnp.dot(a.astype(BF16), b.astype(BF16), preferred_element_type=F32)


def _bmm(a, b):
    return jnp.einsum('bij,bjk->bik', a.astype(BF16), b.astype(BF16), preferred_element_type=F32)


def _block_diag2(x):
    z = jnp.zeros(x.shape[:-1] + (A_DK,), x.dtype)
    return jnp.concatenate([jnp.concatenate([x[..., :A_DK], z], axis=-1),
                            jnp.concatenate([z, x[..., A_DK:]], axis=-1)], axis=-2)


def _gdn_prep_kernel(x_ref, halo_ref, ab_ref, cw_ref, alog_ref, dtb_ref, u_ref, wq_ref, qkk_ref, gl_ref,
                     *, sb, cpb, bb, n_invalid, seq_halo):
    t = pl.program_id(1)
    ns = x_ref.shape[0]
    rows = x_ref.shape[1] // cpb
    c = GDN_ROWS
    r = GDN_R
    halo = halo_ref[...]
    if seq_halo:
        hrow = lax.broadcasted_iota(jnp.int32, (1, GDN_HALO, 1), 1) + (t * cpb * rows - GDN_HALO)
        halo = jnp.where(hrow >= n_invalid, halo, 0.0)
    cw = cw_ref[...]
    alog = alog_ref[...]
    dtb = dtb_ref[...]
    ri = lax.broadcasted_iota(jnp.int32, (r, r), 0)
    ci = lax.broadcasted_iota(jnp.int32, (r, r), 1)
    same = (ri // sb) == (ci // sb)
    incl = same & (ri >= ci)
    strict = same & (ri > ci)
    eye = (ri == ci).astype(F32)
    rc_ = lax.broadcasted_iota(jnp.int32, (c, c), 0)
    cc_ = lax.broadcasted_iota(jnp.int32, (c, c), 1)
    same_c = (rc_ // sb) == (cc_ // sb)
    lmat = jnp.concatenate([(same_c & (rc_ >= cc_)).astype(F32), same_c.astype(F32)], axis=0)
    sel = (lax.broadcasted_iota(jnp.int32, (SUBLANES, 2 * c), 1)
           == c + (lax.broadcasted_iota(jnp.int32, (SUBLANES, 2 * c), 0) * SUBLANES // sb) * sb).astype(F32)
    off = GDN_HALO - (A_CONV - 1)
    assert ns == 1 or cpb == 1
    hi = lambda m, v: jnp.einsum('bij,bjk->bik', jnp.broadcast_to(m, (bb,) + m.shape), v,
                                 precision=HIGHEST, preferred_element_type=F32)
    for c0 in range(0, cpb, bb):
        nr = bb * rows
        x = x_ref[:, c0 * rows:c0 * rows + nr, :]
        ab = ab_ref[:, c0 * rows:c0 * rows + nr, :].reshape(bb, c, LANES)
        if n_invalid:
            row0 = (t * cpb + c0) * rows
            xrow = lax.broadcasted_iota(jnp.int32, (1, nr, 1), 1) + row0
            x = jnp.where(xrow >= n_invalid, x, 0.0)
        ext = jnp.concatenate([halo, x], axis=1)
        acc = cw[0:1, :] * ext[:, off:off + nr, :]
        for k in range(1, A_CONV):
            acc = acc + cw[k:k + 1, :] * ext[:, off + k:off + k + nr, :]
        y = _silu(acc).reshape(bb, c, QKV_W)
        halo = x[:, nr - GDN_HALO:nr, :]

        sp = jnp.maximum(ab + dtb, 0.0) + jnp.log1p(jnp.exp(-jnp.abs(ab + dtb)))
        gfull = -jnp.exp(alog) * sp
        bfull = jax.nn.sigmoid(ab)
        if n_invalid:
            valid = (lax.broadcasted_iota(jnp.int32, (bb, c, 1), 0) * c
                     + lax.broadcasted_iota(jnp.int32, (bb, c, 1), 1) + row0) >= n_invalid
            gfull = jnp.where(valid, gfull, 0.0)
            bfull = jnp.where(valid, bfull, 0.0)
        bcol = jnp.concatenate([bfull[:, :, A_HEADS + h:A_HEADS + h + 1] for h in range(A_HEADS)], axis=1)
        gg = hi(lmat, gfull)
        ggt = jnp.stack([gg[i].T for i in range(bb)])
        gc = jnp.concatenate([gg[:, :c, h:h + 1] for h in range(A_HEADS)], axis=1)
        grow = jnp.concatenate([ggt[:, h:h + 1, :c] for h in range(A_HEADS)], axis=2)
        drow = jnp.concatenate([ggt[:, h:h + 1, c:] for h in range(A_HEADS)], axis=2) - grow
        g8 = jnp.exp(hi(sel, gg))
        decay = jnp.where(incl, jnp.exp(jnp.where(incl, gc - grow, 0.0)), 0.0)

        qs, ks, vs = [], [], []
        for h in range(A_HEADS):
            q = y[:, :, h * A_DK:(h + 1) * A_DK]
            k = y[:, :, A_KW + h * A_DK:A_KW + (h + 1) * A_DK]
            qs.append(q * lax.rsqrt(jnp.sum(q * q, axis=-1, keepdims=True) + EPS) * (A_DK ** -0.5))
            ks.append(k * lax.rsqrt(jnp.sum(k * k, axis=-1, keepdims=True) + EPS))
            vs.append(y[:, :, 2 * A_KW + h * A_DK:2 * A_KW + (h + 1) * A_DK])
        qr = jnp.concatenate(qs, axis=1)
        kr = jnp.concatenate(ks, axis=1)
        vr = jnp.concatenate(vs, axis=1)
        krt = jnp.stack([kr[i].T for i in range(bb)])
        kq = _bmm(jnp.concatenate([kr, qr], axis=1), krt)
        a = jnp.where(strict, kq[:, :r] * decay * bcol, 0.0)
        qk = kq[:, r:] * decay
        tinv = eye - a
        ak = a
        kpow = 2
        while kpow < sb:
            ak = _bmm(ak, ak)
            tinv = tinv + _bmm(tinv, ak)
            kpow *= 2
        eg = jnp.exp(gc)
        uw = _bmm(tinv, jnp.concatenate([vr * bcol, kr * (bcol * eg)], axis=2))
        ur = uw[:, :, :A_DK]
        wr = uw[:, :, A_DK:]
        qd = qr * eg
        kdt = krt * jnp.exp(drow)
        cs = slice(c0, c0 + bb)
        for p in range(GDN_PAIRS):
            r0 = slice(2 * p * c, (2 * p + 1) * c)
            r1 = slice((2 * p + 1) * c, (2 * p + 2) * c)
            pl_ = slice(2 * p * c, (2 * p + 2) * c)
            u_ref[cs, p] = jnp.concatenate([ur[:, r0], ur[:, r1]], axis=2)
            wq = jnp.concatenate([jnp.concatenate([wr[:, r0], wr[:, r1]], axis=2),
                                  jnp.concatenate([qd[:, r0], qd[:, r1]], axis=2)], axis=1)
            wq_ref[cs, p] = wq.astype(wq_ref.dtype)
            qkk = jnp.concatenate([qk[:, r0, pl_] + qk[:, r1, pl_], kdt[:, :, pl_]], axis=1)
            qkk_ref[cs, p] = qkk.astype(qkk_ref.dtype)
            gl_ref[cs, p] = jnp.concatenate(
                [jnp.broadcast_to(g8[:, :, 2 * p:2 * p + 1], (bb, SUBLANES, A_DK)),
                 jnp.broadcast_to(g8[:, :, 2 * p + 1:2 * p + 2], (bb, SUBLANES, A_DK))], axis=2)


def _gdn_prep(x_arr, halo_arr, halo_spec, cw, alog, dtb, ns, rows, cpb, bb, sb, n_invalid, op_dtype):
    n, l, _ = x_arr.shape
    grid = (n // ns, l // (cpb * rows))
    units = grid[0] * grid[1] * cpb
    kern = functools.partial(_gdn_prep_kernel, sb=sb, cpb=cpb, bb=bb, n_invalid=n_invalid,
                             seq_halo=halo_arr is x_arr)
    full = lambda shape: pl.BlockSpec(shape, lambda i, t: (0,) * len(shape))
    steps = grid[1]
    out_map = lambda i, t: (i * steps + t, 0, 0, 0)
    c = GDN_ROWS
    return pl.pallas_call(
        kern,
        grid=grid,
        in_specs=[pl.BlockSpec((ns, cpb * rows, QKV_W), lambda i, t: (i, t, 0)),
                  halo_spec,
                  pl.BlockSpec((ns, cpb * rows, LANES), lambda i, t: (i, t, AB_BLK)),
                  full(cw.shape), full(alog.shape), full(dtb.shape)],
        out_specs=[pl.BlockSpec((cpb, GDN_PAIRS, c, PAIR_W), out_map),
                   pl.BlockSpec((cpb, GDN_PAIRS, 2 * c, PAIR_W), out_map),
                   pl.BlockSpec((cpb, GDN_PAIRS, c + A_DK, 2 * c), out_map),
                   pl.BlockSpec((cpb, GDN_PAIRS, SUBLANES, PAIR_W), out_map)],
        out_shape=[jax.ShapeDtypeStruct((units, GDN_PAIRS, c, PAIR_W), F32),
                   jax.ShapeDtypeStruct((units, GDN_PAIRS, 2 * c, PAIR_W), op_dtype),
                   jax.ShapeDtypeStruct((units, GDN_PAIRS, c + A_DK, 2 * c), op_dtype),
                   jax.ShapeDtypeStruct((units, GDN_PAIRS, SUBLANES, PAIR_W), F32)],
        compiler_params=_cparams(("parallel", "arbitrary")),
        name="gdn_prep",
    )(x_arr, halo_arr, x_arr, cw, alog, dtb)


def _gdn_out(o_pair, z_ref, o_ref, nw, idx, p):
    for i in range(2):
        h = 2 * p + i
        hs = slice(h * A_DK, (h + 1) * A_DK)
        o = _rms(o_pair[..., i * A_DK:(i + 1) * A_DK], nw) * _silu(z_ref[idx + (hs,)])
        o_ref[idx + (hs,)] = o.astype(o_ref.dtype)


def _gdn_scan_kernel(u_ref, wq_ref, qkk_ref, gl_ref, z_ref, nw_ref, o_ref, sout_ref, s_ref):
    t = pl.program_id(0)
    n_seq = u_ref.shape[0]
    c = GDN_ROWS
    nw = nw_ref[...]

    @pl.when(t == 0)
    def _():
        s_ref[...] = jnp.zeros(s_ref.shape, F32)

    for p in range(GDN_PAIRS):
        s = s_ref[:, p]
        r1 = jnp.einsum('bmk,bkn->bmn', wq_ref[:, 0, p], _block_diag2(s.astype(BF16)),
                        preferred_element_type=F32)
        vnew = u_ref[:, 0, p] - r1[:, :c]
        r2 = jnp.einsum('bmk,bkn->bmn', qkk_ref[:, 0, p], _block_diag2(vnew.astype(BF16)),
                        preferred_element_type=F32)
        s_ref[:, p] = s * gl_ref[:, 0, p][:, 0:1, :] + r2[:, c:]
        _gdn_out(r1[:, c:] + r2[:, :c], z_ref, o_ref, nw, (slice(None), slice(None)), p)

    @pl.when(t == pl.num_programs(0) - 1)
    def _():
        for n in range(n_seq):
            for p in range(GDN_PAIRS):
                s = s_ref[n, p]
                sout_ref[n, 2 * p] = s[:, :A_DK]
                sout_ref[n, 2 * p + 1] = s[:, A_DK:]


def _gdn_scan(u, wq, qkk, gl, p3, nw):
    n, l, _ = p3.shape
    sshape = (n, A_HEADS, A_DK, A_DK)
    c = GDN_ROWS
    steps = l // c
    v5 = lambda a: a.reshape((n, steps) + a.shape[1:])
    u, wq, qkk, gl = v5(u), v5(wq), v5(qkk), v5(gl)
    unit = lambda a: pl.BlockSpec((n, 1) + a.shape[2:], lambda t: (0, t, 0, 0, 0))
    vw = A_HEADS * A_DK
    return pl.pallas_call(
        _gdn_scan_kernel,
        grid=(steps,),
        in_specs=[unit(u), unit(wq), unit(qkk), unit(gl),
                  pl.BlockSpec((n, c, vw), lambda t: (0, t, Z_BLK)),
                  pl.BlockSpec(nw.shape, lambda t: (0, 0))],
        out_specs=[pl.BlockSpec((n, c, vw), lambda t: (0, t, 0)),
                   pl.BlockSpec(sshape, lambda t: (0, 0, 0, 0))],
        out_shape=[jax.ShapeDtypeStruct((n, l, vw), BF16),
                   jax.ShapeDtypeStruct(sshape, F32)],
        scratch_shapes=[pltpu.VMEM((n, GDN_PAIRS, A_DK, PAIR_W), F32)],
        compiler_params=_cparams(("arbitrary",)),
        name="gdn_scan",
    )(u, wq, qkk, gl, p3, nw)


def _gdn_step_kernel(u_ref, wq_ref, qkk_ref, gl_ref, z_ref, s0_ref, nw_ref, *rest):
    o_ref, sout_ref = rest[-2:]
    ns, rows, _ = z_ref.shape
    c = GDN_ROWS
    nw = nw_ref[...]
    lane_seq = (lax.broadcasted_iota(jnp.int32, (A_DK, 2 * c), 1) % c) // rows
    for p in range(GDN_PAIRS):
        wq = wq_ref[0, p]
        qkk = qkk_ref[0, p]
        gl = gl_ref[0, p]
        ws, qs, states = [], [], []
        for s_i in range(ns):
            sl = slice(s_i * rows, (s_i + 1) * rows)
            s = jnp.concatenate([s0_ref[s_i, 2 * p], s0_ref[s_i, 2 * p + 1]], axis=1)
            states.append(s)
            lhs = jnp.concatenate([wq[sl], wq[c + s_i * rows:c + (s_i + 1) * rows]], axis=0)
            r1 = _mm(lhs, _block_diag2(s.astype(BF16)))
            ws.append(r1[:rows])
            qs.append(r1[rows:])
        vnew = u_ref[0, p] - jnp.concatenate(ws, axis=0)
        bdv = _block_diag2(vnew.astype(BF16))
        o = jnp.concatenate(qs, axis=0) + _mm(qkk[:c], bdv)
        _gdn_out(o.reshape(ns, rows, PAIR_W), z_ref, o_ref, nw, (slice(None), slice(None)), p)
        kdt = qkk[c:]
        lhs = jnp.concatenate([jnp.where(lane_seq == s_i, kdt, 0.0) for s_i in range(ns)], axis=0)
        upd = _mm(lhs, bdv)
        for s_i in range(ns):
            s_new = states[s_i] * gl[s_i:s_i + 1, :] + upd[s_i * A_DK:(s_i + 1) * A_DK]
            sout_ref[s_i, 2 * p] = s_new[:, :A_DK]
            sout_ref[s_i, 2 * p + 1] = s_new[:, A_DK:]


def _gdn_step(u, wq, qkk, gl, p3, s_all, layer, s_prev, nw, ns):
    n, rows, _ = p3.shape
    unit = lambda a: pl.BlockSpec((1,) + a.shape[1:], lambda i: (i, 0, 0, 0))
    vw = A_HEADS * A_DK
    sblk = pl.BlockSpec((None, ns) + s_all.shape[2:], lambda i: (layer, i, 0, 0, 0))
    in_specs = [unit(u), unit(wq), unit(qkk), unit(gl),
                pl.BlockSpec((ns, rows, vw), lambda i: (i, 0, Z_BLK)),
                sblk,
                pl.BlockSpec(nw.shape, lambda i: (0, 0))]
    args = [u, wq, qkk, gl, p3, s_all, nw]
    aliases = {}
    if s_prev is not None:
        in_specs.append(pl.BlockSpec(memory_space=pl.ANY))
        args.append(s_prev)
        aliases = {len(args) - 1: 1}
    return pl.pallas_call(
        _gdn_step_kernel,
        grid=(n // ns,),
        in_specs=in_specs,
        out_specs=[pl.BlockSpec((ns, rows, vw), lambda i: (i, 0, 0)), sblk],
        out_shape=[jax.ShapeDtypeStruct((n, rows, vw), F32),
                   jax.ShapeDtypeStruct(s_all.shape, F32)],
        input_output_aliases=aliases,
        compiler_params=_cparams(("parallel",)),
        name="gdn_step",
    )(*args)


def _gdn(p3, state, hist, cw, alog, dtb, nw, cpb, n_invalid):
    n, l, _ = p3.shape
    c = GDN_ROWS
    if hist is None:
        steps8 = cpb * c // GDN_HALO
        halo_spec = pl.BlockSpec((1, GDN_HALO, QKV_W), lambda i, t: (i, jnp.maximum(t * steps8 - 1, 0), 0))
        u, wq, qkk, gl = _gdn_prep(p3, p3, halo_spec, cw, alog, dtb, 1, c, cpb, min(cpb, GDN_BATCH), c, n_invalid, BF16)
        return _gdn_scan(u, wq, qkk, gl, p3, nw)
    ns = c // l
    halo_spec = pl.BlockSpec((ns, GDN_HALO, QKV_W), lambda i, t: (i, 0, 0))
    u, wq, qkk, gl = _gdn_prep(p3, hist, halo_spec, cw, alog, dtb, ns, l, 1, 1, l, n_invalid, F32)
    s_all, layer, s_prev = state
    return _gdn_step(u, wq, qkk, gl, p3, s_all, layer, s_prev, nw, ns)


def _merge_kernel(oa_ref, ob_ref, oc_ref, gl_ref, x_ref, wb_ref, wo_ref, out_ref):
    d = x_ref.shape[1]
    merged = None
    for r, o_ref in enumerate((oa_ref, ob_ref, oc_ref)):
        pb = jnp.dot(o_ref[...].astype(BF16), wb_ref[r], preferred_element_type=F32)
        term = jax.nn.sigmoid(gl_ref[:, r * d:(r + 1) * d]) * pb
        merged = term if merged is None else merged + term
    out_ref[...] = x_ref[...] + jnp.dot(merged.astype(BF16), wo_ref[...], preferred_element_type=F32)


def _merge(oa, ob, oc, p2, x2, wb, wo, tm):
    t, d = x2.shape
    bw = oa.shape[1]
    row = lambda w, blk: pl.BlockSpec((tm, w), lambda i: (i, blk))
    return pl.pallas_call(
        _merge_kernel,
        grid=(t // tm,),
        in_specs=[row(bw, 0), row(bw, 0), row(bw, 0), row(3 * d, GL_BLK), row(d, 0),
                  pl.BlockSpec(wb.shape, lambda i: (0, 0, 0)),
                  pl.BlockSpec(wo.shape, lambda i: (0, 0))],
        out_specs=row(d, 0),
        out_shape=jax.ShapeDtypeStruct((t, d), F32),
        compiler_params=_cparams(("parallel",)),
        name="merge",
    )(oa, ob, oc, p2, x2, wb, wo)


def _route(logits):
    lane = lax.broadcasted_iota(jnp.int32, logits.shape, 1)
    is_g = (lane >= N_EXPERTS) & (lane < N_EXPERTS + N_GROUPS)
    gl = jnp.where(is_g, logits, NEG)
    gmax = jnp.max(gl, axis=-1, keepdims=True)
    gidx = jnp.min(jnp.where(gl == gmax, lane - N_EXPERTS, LANES), axis=-1, keepdims=True)
    gw = 1.0 / jnp.sum(jnp.where(is_g, jnp.exp(gl - gmax), 0.0), axis=-1, keepdims=True)
    in_grp = (lane < N_EXPERTS) & ((lane // EXP_PER_GROUP) == gidx)
    el = jnp.where(in_grp, logits, NEG)
    v1 = jnp.max(el, axis=-1, keepdims=True)
    i1 = jnp.min(jnp.where(el == v1, lane, LANES), axis=-1, keepdims=True)
    el2 = jnp.where(lane == i1, NEG, el)
    v2 = jnp.max(el2, axis=-1, keepdims=True)
    i2 = jnp.min(jnp.where(el2 == v2, lane, LANES), axis=-1, keepdims=True)
    e21 = jnp.exp(v2 - v1)
    w1 = gw / (1.0 + e21)
    w2 = gw * e21 / (1.0 + e21)
    return jnp.where(lane == i1, w1, 0.0) + jnp.where(lane == i2, w2, 0.0)


def _moe_kernel(x_ref, nw_ref, wr_ref, br_ref, wg_ref, wu_ref, wd_ref, out_ref, h_ref, cmb_ref):
    g = pl.program_id(1)

    @pl.when(g == 0)
    def _():
        x = x_ref[...]
        h = _rms(x, nw_ref[...])
        h_hi = h.astype(BF16)
        h_ref[...] = h_hi
        h_lo = (h - h_hi.astype(F32)).astype(BF16)
        l2 = jnp.dot(h_hi, wr_ref[...], preferred_element_type=F32)
        logits = (l2[:, :LANES] + l2[:, LANES:]
                  + jnp.dot(h_lo, wr_ref[:, :LANES], preferred_element_type=F32) + br_ref[...])
        cmb_ref[...] = _route(logits)
        out_ref[...] = x

    h = h_ref[...]
    cmb = cmb_ref[...]
    lane = lax.broadcasted_iota(jnp.int32, cmb.shape, 1)
    parts = []
    for e in range(EXP_PER_GROUP):
        ce = jnp.sum(jnp.where(lane == g * EXP_PER_GROUP + e, cmb, 0.0), axis=-1, keepdims=True)
        hid = (_silu(jnp.dot(h, wg_ref[e], preferred_element_type=F32))
               * jnp.dot(h, wu_ref[e], preferred_element_type=F32))
        parts.append((hid * ce).astype(BF16))
    wd = wd_ref[...].reshape(-1, wd_ref.shape[-1])
    out_ref[...] += jnp.dot(jnp.concatenate(parts, axis=1), wd, preferred_element_type=F32)


def _moe(x2, nw, wr, br, wg, wu, wd, tm):
    t, d = x2.shape
    ne, _, de = wg.shape
    return pl.pallas_call(
        _moe_kernel,
        grid=(t // tm, ne // EXP_PER_GROUP),
        in_specs=[pl.BlockSpec((tm, d), lambda i, g: (i, 0)),
                  pl.BlockSpec((1, d), lambda i, g: (0, 0)),
                  pl.BlockSpec(wr.shape, lambda i, g: (0, 0)),
                  pl.BlockSpec(br.shape, lambda i, g: (0, 0)),
                  pl.BlockSpec((EXP_PER_GROUP, d, de), lambda i, g: (g, 0, 0)),
                  pl.BlockSpec((EXP_PER_GROUP, d, de), lambda i, g: (g, 0, 0)),
                  pl.BlockSpec((EXP_PER_GROUP, de, d), lambda i, g: (g, 0, 0))],
        out_specs=pl.BlockSpec((tm, d), lambda i, g: (i, 0)),
        out_shape=jax.ShapeDtypeStruct((t, d), F32),
        scratch_shapes=[pltpu.VMEM((tm, d), BF16), pltpu.VMEM((tm, LANES), F32)],
        compiler_params=_cparams(("parallel", "arbitrary")),
        name="moe",
    )(x2, nw, wr, br, wg, wu, wd)


def _pad_lanes(v, width):
    v = v.reshape(1, -1)
    return jnp.pad(v, ((0, 0), (0, width - v.shape[1])))


def _layer_weights(l, norm1_w, norm2_w, w_in, gdn_conv_w, gdn_a_log, gdn_dt_bias, gdn_norm_w,
                   swa_q_norm_w, swa_k_norm_w, swa_sinks, conv_dw_w, conv_dw_b, conv_ln_w, conv_ln_b,
                   w_branch, w_out, router_group_w, router_group_b, router_expert_w, router_expert_b,
                   moe_w_gate, moe_w_up, moe_w_down):
    d = w_in.shape[1]
    wi = w_in[l]
    o_ab = 4 * A_KW
    o_bq = o_ab + 2 * A_HEADS
    o_cu = o_bq + B_HEADS * B_HD + BKV_W
    o_gl = o_cu + 2 * C_CH
    w_perm = jnp.concatenate(
        [wi[:, :o_ab], wi[:, o_cu:o_gl], wi[:, o_gl:], wi[:, o_bq:o_cu], wi[:, o_ab:o_bq],
         jnp.zeros((d, LANES - 2 * A_HEADS), wi.dtype)], axis=1).astype(BF16)
    assert w_perm.shape[1] == PROJ_W
    wr = jnp.concatenate([router_expert_w[l], router_group_w[l],
                          jnp.zeros((d, LANES - N_EXPERTS - N_GROUPS), F32)], axis=1)
    br = _pad_lanes(jnp.concatenate([router_expert_b[l], router_group_b[l]]), LANES)
    wr_hi = wr.astype(BF16)
    wr = jnp.concatenate([wr_hi, (wr - wr_hi.astype(F32)).astype(BF16)], axis=1)
    return dict(
        n1=norm1_w[l].reshape(1, d), n2=norm2_w[l].reshape(1, d), w_in=w_perm,
        gdn_cw=gdn_conv_w[l], alog=_pad_lanes(gdn_a_log[l], LANES), dtb=_pad_lanes(gdn_dt_bias[l], LANES),
        gdn_nw=gdn_norm_w[l].reshape(1, -1),
        qnw=swa_q_norm_w[l].reshape(1, -1), knw=swa_k_norm_w[l].reshape(1, -1), sinks=swa_sinks[l].reshape(1, -1),
        qnw_heads=jnp.tile(swa_q_norm_w[l].reshape(1, -1), (1, B_HEADS)),
        knw_heads=jnp.tile(swa_k_norm_w[l].reshape(1, -1), (1, B_KV_HEADS)),
        dww=conv_dw_w[l], dwb=conv_dw_b[l].reshape(1, -1), lnw=conv_ln_w[l].reshape(1, -1),
        lnb=conv_ln_b[l].reshape(1, -1),
        wb=w_branch[l].astype(BF16), wo=w_out[l].astype(BF16), wr=wr, br=br,
        wg=moe_w_gate[l].astype(BF16), wu=moe_w_up[l].astype(BF16), wd=moe_w_down[l].astype(BF16))


def _tiles(n, l):
    prompt = l > BLOCK
    t = n * l
    if prompt:
        return dict(proj_tm=1536, tok_tm=512, moe_tm=1056, conf_nb=1, conf_tm=384, conf_rc=64,
                    gdn_cpb=6, swa_bps=3, n_invalid=META_PAD, branch_dtype=BF16)
    return dict(proj_tm=t, tok_tm=512, moe_tm=t, conf_nb=16, conf_tm=l, conf_rc=l,
                gdn_cpb=1, n_invalid=0, branch_dtype=F32)


def _layer(x3, lw, gdn_state, gdn_hist, conf_hist, swa_cache):
    n, l, d = x3.shape
    tl = _tiles(n, l)
    x2 = x3.reshape(n * l, d)
    p2 = _inproj(x2, lw['n1'], lw['w_in'], tl['proj_tm'], PROJ_W // 5)
    p3 = p2.reshape(n, l, PROJ_W)
    o_a, s_new = _gdn(p3, gdn_state, gdn_hist, lw['gdn_cw'], lw['alog'], lw['dtb'], lw['gdn_nw'],
                      tl['gdn_cpb'], tl['n_invalid'])
    gh_new = p3[:, l - (A_CONV - 1):, :QKV_W]
    if swa_cache is None:
        o_b, kvn = _swa_prompt(p3, lw['qnw_heads'], lw['knw_heads'], lw['sinks'], tl['swa_bps'])
        kv = lambda sl: kvn[:, sl].reshape(n, -1, 2, B_KV_HEADS, B_HD)
        swa_new = (kv(slice(META_PAD, BLOCK)), kv(slice(l - WINDOW, l)))
    else:
        win_all, meta_all, layer, win_prev = swa_cache
        o_b, swa_new = _swa_sample(p3, win_all, meta_all, layer, win_prev, lw['qnw'], lw['knw'], lw['sinks'], 16)
    o_c, ch_new = _conformer(p3, conf_hist, lw['dww'], lw['dwb'], lw['lnw'], lw['lnb'],
                             tl['conf_nb'], tl['conf_tm'], tl['conf_rc'], tl['n_invalid'], tl['branch_dtype'])
    t = n * l
    x2 = _merge(o_a.reshape(t, -1), o_b.reshape(t, -1), o_c.reshape(t, -1), p2, x2, lw['wb'], lw['wo'],
                tl['tok_tm'])
    x2 = _moe(x2, lw['n2'], lw['wr'], lw['br'], lw['wg'], lw['wu'], lw['wd'], tl['moe_tm'])
    return x2.reshape(n, l, d), s_new, gh_new, swa_new, ch_new


def kernel(x_prompt, x_sample, state_gdn, cache_gdn_conv, cache_swa_kv, cache_meta_kv, cache_conv, meta_tokens, norm1_w, norm2_w, w_in, gdn_conv_w, gdn_a_log, gdn_dt_bias, gdn_norm_w, swa_q_norm_w, swa_k_norm_w, swa_sinks, conv_dw_w, conv_dw_b, conv_ln_w, conv_ln_b, w_branch, w_out, router_group_w, router_group_b, router_expert_w, router_expert_b, moe_w_gate, moe_w_up, moe_w_down):
    dtp = x_prompt.dtype
    n_p, _, d = x_prompt.shape
    n_s = x_sample.shape[0]
    depth = w_in.shape[0]
    xp = jnp.pad(x_prompt, ((0, 0), (BLOCK, 0), (0, 0)))
    xp = lax.dynamic_update_slice(
        xp, jnp.broadcast_to(meta_tokens.astype(dtp)[None], (n_p, N_META, d)), (0, META_PAD, 0))
    xs = x_sample
    state_s = None
    win_s = None
    win_all = cache_swa_kv.reshape(depth, n_s, -1, BKV_W)
    meta_all = cache_meta_kv.reshape(depth, n_s, N_META, BKV_W)
    outs_p, outs_s = [], []
    for l in range(depth):
        lw = _layer_weights(l, norm1_w, norm2_w, w_in, gdn_conv_w, gdn_a_log, gdn_dt_bias, gdn_norm_w,
                            swa_q_norm_w, swa_k_norm_w, swa_sinks, conv_dw_w, conv_dw_b, conv_ln_w, conv_ln_b,
                            w_branch, w_out, router_group_w, router_group_b, router_expert_w, router_expert_b,
                            moe_w_gate, moe_w_up, moe_w_down)
        xp, s_p, gh_p, (mkv_p, wkv_p), ch_p = _layer(
            xp, lw, None, None, jnp.zeros((n_p, C_CONV - 1, C_CH), dtp), None)
        outs_p.append((s_p, gh_p, wkv_p, mkv_p, ch_p))
        gh0 = jnp.pad(cache_gdn_conv[l], ((0, 0), (GDN_HALO - (A_CONV - 1), 0), (0, 0)))
        xs, state_s, gh_s, win_s, ch_s = _layer(xs, lw, (state_gdn, l, state_s), gh0, cache_conv[l],
                                                (win_all, meta_all, l, win_s))
        outs_s.append((gh_s, ch_s))
    stack = lambda outs, i: jnp.stack([o[i] for o in outs])
    return (xp[:, BLOCK:], xs,
            stack(outs_p, 0), stack(outs_p, 1), stack(outs_p, 2), stack(outs_p, 3), stack(outs_p, 4),
            state_s, stack(outs_s, 0), win_s.reshape(cache_swa_kv.shape), stack(outs_s, 1))
```

```python
import functools

import jax
import jax.numpy as jnp
from jax import lax
from jax.experimental import pallas as pl
from jax.experimental.pallas import tpu as pltpu

F32 = jnp.float32
BF16 = jnp.bfloat16
HIGHEST = lax.Precision.HIGHEST
EPS = 1e-6
NEG = -1e30

VMEM_LIMIT_BYTES = 56 * 1024 * 1024
LANES = 128
SUBLANES = 8

PAST_LEN = 16384
N_META = 16
BLOCK = 128
WINDOW = 128
META_PAD = BLOCK - N_META
A_HEADS = 4
A_DK = 128
A_CONV = 4
B_HEADS = 8
B_KV_HEADS = 2
B_GROUP = B_HEADS // B_KV_HEADS
B_HD = 64
C_CH = 512
C_CONV = 31
BRANCH_W = 512
N_GROUPS = 4
EXP_PER_GROUP = 4
N_EXPERTS = N_GROUPS * EXP_PER_GROUP

A_KW = A_HEADS * A_DK
QKV_W = 3 * A_KW
Z_BLK = QKV_W // BRANCH_W
CU_BLK = 2
GL_BLK = 1
BQ_BLK = 12
BKV_W = 2 * B_KV_HEADS * B_HD
BKV_BLK = 26
AB_BLK = 54
PROJ_W = 7040
CONF_HALO = 32
GDN_HALO = SUBLANES


def _rms(x, w):
    return x * lax.rsqrt(jnp.mean(x * x, axis=-1, keepdims=True) + EPS) * w


def _silu(x):
    return x * jax.nn.sigmoid(x)


def _cparams(sem):
    return pltpu.CompilerParams(dimension_semantics=sem, vmem_limit_bytes=VMEM_LIMIT_BYTES)


def _inproj_kernel(x_ref, nw_ref, w_ref, o_ref, h_ref):
    @pl.when(pl.program_id(1) == 0)
    def _():
        h_ref[...] = _rms(x_ref[...], nw_ref[...]).astype(BF16)

    o_ref[...] = jnp.dot(h_ref[...], w_ref[...], preferred_element_type=F32)


def _inproj(x2, nw, w, tm, tn):
    t, d = x2.shape
    wd = w.shape[1]
    return pl.pallas_call(
        _inproj_kernel,
        grid=(t // tm, wd // tn),
        in_specs=[pl.BlockSpec((tm, d), lambda i, j: (i, 0)),
                  pl.BlockSpec((1, d), lambda i, j: (0, 0)),
                  pl.BlockSpec((d, tn), lambda i, j: (0, j))],
        out_specs=pl.BlockSpec((tm, tn), lambda i, j: (i, j)),
        out_shape=jax.ShapeDtypeStruct((t, wd), F32),
        scratch_shapes=[pltpu.VMEM((tm, d), BF16)],
        compiler_params=_cparams(("parallel", "arbitrary")),
        name="inproj",
    )(x2, nw, w)


def _conf_kernel(cu_ref, hist_ref, dww_ref, dwb_ref, lnw_ref, lnb_ref, y_ref, nh_ref, e_ref, c_ref,
                 *, tm, rc, n_invalid):
    t = pl.program_id(1)
    nb = cu_ref.shape[0]
    halo = CONF_HALO

    @pl.when(t == 0)
    def _():
        e_ref[:, 0:SUBLANES, :] = jnp.zeros((nb, SUBLANES, C_CH), F32)
        e_ref[:, halo + tm:halo + tm + SUBLANES, :] = jnp.zeros((nb, SUBLANES, C_CH), F32)
        e_ref[:, halo - (C_CONV - 1):halo, :] = hist_ref[...]

    @pl.when(t > 0)
    def _():
        e_ref[:, 0:halo, :] = e_ref[:, tm:tm + halo, :]

    cu = cu_ref[...]
    u = cu[:, :, :C_CH] * jax.nn.sigmoid(cu[:, :, C_CH:])
    if n_invalid:
        row = lax.broadcasted_iota(jnp.int32, (1, tm, 1), 1) + t * tm
        u = jnp.where(row >= n_invalid, u, 0.0)
    e_ref[:, halo:halo + tm, :] = u
    nh_ref[...] = e_ref[:, halo + tm - (C_CONV - 1):halo + tm, :]

    off = halo - (C_CONV - 1)
    for c in range(C_CH // LANES):
        cs = slice(c * LANES, (c + 1) * LANES)

        def rbody(r, carry, cs=cs):
            r0 = pl.multiple_of(r * rc, rc)
            blk = e_ref[:, pl.ds(r0, rc + halo + SUBLANES), cs]
            acc = jnp.zeros((nb, rc, LANES), F32)
            for s in range(SUBLANES):
                z = None
                for a in range((halo + SUBLANES) // SUBLANES):
                    k = SUBLANES * a + s - off
                    if 0 <= k < C_CONV:
                        term = dww_ref[k:k + 1, cs] * blk[:, SUBLANES * a:SUBLANES * a + rc + SUBLANES, :]
                        z = term if z is None else z + term
                acc = acc + z[:, s:s + rc, :]
            c_ref[:, pl.ds(r0, rc), cs] = acc + dwb_ref[:, cs]
            return carry

        lax.fori_loop(0, tm // rc, rbody, 0)

    y = c_ref[...]
    mu = jnp.mean(y, axis=-1, keepdims=True)
    var = jnp.mean(jnp.square(y - mu), axis=-1, keepdims=True)
    yn = (y - mu) * lax.rsqrt(var + EPS) * lnw_ref[...] + lnb_ref[...]
    y_ref[...] = _silu(yn).astype(y_ref.dtype)


def _conformer(p3, hist, dww, dwb, lnw, lnb, nb, tm, rc, n_invalid, out_dtype):
    n, l, _ = p3.shape
    kern = functools.partial(_conf_kernel, tm=tm, rc=rc, n_invalid=n_invalid)
    full = lambda shape: pl.BlockSpec(shape, lambda i, t: (0,) * len(shape))
    return pl.pallas_call(
        kern,
        grid=(n // nb, l // tm),
        in_specs=[pl.BlockSpec((nb, tm, 2 * C_CH), lambda i, t: (i, t, CU_BLK)),
                  pl.BlockSpec((nb, C_CONV - 1, C_CH), lambda i, t: (i, 0, 0)),
                  full(dww.shape), full(dwb.shape), full(lnw.shape), full(lnb.shape)],
        out_specs=[pl.BlockSpec((nb, tm, C_CH), lambda i, t: (i, t, 0)),
                   pl.BlockSpec((nb, C_CONV - 1, C_CH), lambda i, t: (i, 0, 0))],
        out_shape=[jax.ShapeDtypeStruct((n, l, C_CH), out_dtype),
                   jax.ShapeDtypeStruct((n, C_CONV - 1, C_CH), F32)],
        scratch_shapes=[pltpu.VMEM((nb, CONF_HALO + tm + SUBLANES, C_CH), F32), pltpu.VMEM((nb, tm, C_CH), F32)],
        compiler_params=_cparams(("parallel", "arbitrary")),
        name="conformer",
    )(p3, hist, dww, dwb, lnw, lnb)


SWA_ROWS = 128


def _softmax_sink_pv(s, mask, sink, v, dims):
    s = jnp.where(mask, s, NEG)
    m = jnp.maximum(jnp.max(s, axis=-1, keepdims=True), sink)
    p = jnp.exp(s - m)
    den = jnp.sum(p, axis=-1, keepdims=True) + jnp.exp(sink - m)
    return lax.dot_general(p, v, dims, preferred_element_type=F32) / den


def _swa_prompt_kernel(q_ref, kv_ref, kvp_ref, kvm_ref, qnw_ref, knw_ref, sink_ref, o_ref, kvn_ref, *, bps):
    step = pl.program_id(1)
    qnw = qnw_ref[...]
    knw = knw_ref[...]
    qw = B_HEADS * B_HD
    kw = B_KV_HEADS * B_HD
    nk = N_META + 2 * BLOCK
    rows = B_GROUP * BLOCK
    r = lax.broadcasted_iota(jnp.int32, (rows, nk), 0) % BLOCK
    c = lax.broadcasted_iota(jnp.int32, (rows, nk), 1)
    cp = c - N_META
    co = cp - BLOCK
    hrow = lax.broadcasted_iota(jnp.int32, (rows, 1), 0) // BLOCK
    m_meta = c < N_META
    m_own = (co >= 0) & (co <= r)
    m_prev = (cp >= 0) & (cp < BLOCK) & (cp > r)
    bias0 = jnp.where(m_meta & (META_PAD + c <= r), 0.0, NEG)
    bias1 = jnp.where(m_meta | m_own, 0.0, NEG)
    bias2 = jnp.where(m_meta | m_own | m_prev, 0.0, NEG)
    sinks = []
    for g in range(B_KV_HEADS):
        sink = jnp.zeros((rows, 1), F32)
        for i in range(B_GROUP):
            h = g * B_GROUP + i
            sink = jnp.where(hrow == i, sink_ref[0:1, h:h + 1], sink)
        sinks.append(sink)
    hb_q = (lax.broadcasted_iota(jnp.int32, (qw, qw), 0) // B_HD
            == lax.broadcasted_iota(jnp.int32, (qw, qw), 1) // B_HD).astype(BF16)
    hb_k = hb_q[:kw, :kw]
    half = lax.broadcasted_iota(jnp.int32, (1, LANES), 1) // B_HD

    def head_rms(x, hb, w):
        sq = x * x
        hi = sq.astype(BF16)
        lo = (sq - hi.astype(F32)).astype(BF16)
        ss = jnp.dot(hi, hb, preferred_element_type=F32) + jnp.dot(lo, hb, preferred_element_type=F32)
        return x * lax.rsqrt(ss * (1.0 / B_HD) + EPS) * w

    kvm = kvm_ref[META_PAD:BLOCK, :]
    km = head_rms(kvm[:, :kw], hb_k, knw)
    kv_halo = kvp_ref[...]
    q = q_ref[...]
    kv = kv_ref[...]
    k_own = head_rms(kv[:, :kw], hb_k, knw)
    kvn_ref[...] = jnp.concatenate([k_own, kv[:, kw:]], axis=-1)
    qn = head_rms(q, hb_q, qnw)
    tq = bps * BLOCK
    blocks = lambda x: x.reshape(bps, BLOCK, x.shape[-1])
    k_prev = jnp.concatenate([head_rms(kv_halo[:, :kw], hb_k, knw), k_own[:tq - BLOCK]], axis=0)
    v_prev = jnp.concatenate([kv_halo[:, kw:], kv[:tq - BLOCK, kw:]], axis=0)
    rep = lambda x: jnp.broadcast_to(x[None], (bps,) + x.shape)
    kcat_b = jnp.concatenate([rep(km), blocks(k_prev), blocks(k_own)], axis=1).astype(BF16)
    vcat_b = jnp.concatenate([rep(kvm[:, kw:]), blocks(v_prev), blocks(kv[:, kw:])], axis=1).astype(BF16)
    bias = jnp.stack([jnp.where(step * bps + b >= 2, bias2, jnp.where(step * bps + b >= 1, bias1, bias0))
                      for b in range(bps)])
    tiles = [qn[:, m * LANES:(m + 1) * LANES] for m in range(qw // LANES)]
    swapped = [pltpu.roll(tl_, B_HD, 1) for tl_ in tiles]
    outs = [None] * B_HEADS
    for g in range(B_KV_HEADS):
        in_g = half == g
        parts = []
        for i in range(B_GROUP):
            h = g * B_GROUP + i
            src = tiles[h // 2] if h % 2 == g else swapped[h // 2]
            parts.append(blocks(jnp.where(in_g, src, 0.0)))
        lhs = jnp.concatenate(parts, axis=1).astype(BF16)
        chunks = []
        for r0 in range(0, rows, SWA_ROWS):
            rc = slice(r0, r0 + SWA_ROWS)
            s = jnp.einsum('bqd,bkd->bqk', lhs[:, rc], kcat_b, preferred_element_type=F32)
            s = s * (B_HD ** -0.5) + bias[:, rc]
            sink = sinks[g][rc]
            m = jnp.maximum(jnp.max(s, axis=-1, keepdims=True), sink)
            p = jnp.exp(s - m)
            den = jnp.sum(p, axis=-1, keepdims=True) + jnp.exp(sink - m)
            chunks.append(jnp.einsum('bqk,bkd->bqd', p.astype(BF16), vcat_b, preferred_element_type=F32) / den)
        res = jnp.concatenate(chunks, axis=1)
        for i in range(B_GROUP):
            h = g * B_GROUP + i
            part = res[:, i * BLOCK:(i + 1) * BLOCK, :].reshape(tq, LANES)
            outs[h] = part if h % 2 == g else pltpu.roll(part, B_HD, 1)
    for m in range(qw // LANES):
        o_ref[:, m * LANES:(m + 1) * LANES] = jnp.where(half == 0, outs[2 * m], outs[2 * m + 1]).astype(o_ref.dtype)


def _swa_prompt(p3, qnw, knw, sinks, bps):
    n, l, _ = p3.shape
    full = lambda shape: pl.BlockSpec(shape, lambda i, j: (0,) * len(shape))
    tq = bps * BLOCK
    return pl.pallas_call(
        functools.partial(_swa_prompt_kernel, bps=bps),
        grid=(n, l // tq),
        in_specs=[pl.BlockSpec((None, tq, B_HEADS * B_HD), lambda i, j: (i, j, BQ_BLK)),
                  pl.BlockSpec((None, tq, BKV_W), lambda i, j: (i, j, BKV_BLK)),
                  pl.BlockSpec((None, BLOCK, BKV_W), lambda i, j: (i, jnp.maximum(j * bps - 1, 0), BKV_BLK)),
                  pl.BlockSpec((None, BLOCK, BKV_W), lambda i, j: (i, 0, BKV_BLK)),
                  full(qnw.shape), full(knw.shape), full(sinks.shape)],
        out_specs=[pl.BlockSpec((None, tq, B_HEADS * B_HD), lambda i, j: (i, j, 0)),
                   pl.BlockSpec((None, tq, BKV_W), lambda i, j: (i, j, 0))],
        out_shape=[jax.ShapeDtypeStruct((n, l, B_HEADS * B_HD), BF16),
                   jax.ShapeDtypeStruct((n, l, BKV_W), F32)],
        compiler_params=_cparams(("parallel", "arbitrary")),
        name="swa_prompt",
    )(p3, p3, p3, p3, qnw, knw, sinks)


def _swa_sample_kernel(q_ref, kvn_ref, win_ref, meta_ref, qnw_ref, knw_ref, sink_ref, *rest):
    o_ref, nwin_ref = rest[-2:]
    q = q_ref[...]
    kvn = kvn_ref[...]
    win = win_ref[...]
    meta = meta_ref[...]
    qnw = qnw_ref[...]
    knw = knw_ref[...]
    t_new = q.shape[1]
    w = win.shape[1]
    nk = N_META + w + t_new
    rows = B_GROUP * t_new
    tq = lax.broadcasted_iota(jnp.int32, (1, rows, nk), 1) % t_new
    c = lax.broadcasted_iota(jnp.int32, (1, rows, nk), 2)
    cw = c - N_META
    win_ok = (cw >= 0) & (cw < w) & (cw - w > tq - WINDOW) & (cw + (PAST_LEN - w) >= N_META)
    cn = cw - w
    new_ok = (cn >= 0) & (cn <= tq) & (cn > tq - WINDOW)
    mask = (c < N_META) | win_ok | new_ok
    hrow = lax.broadcasted_iota(jnp.int32, (1, rows, 1), 1) // t_new
    kn_parts = []
    for g in range(B_KV_HEADS):
        ks = slice(g * B_HD, (g + 1) * B_HD)
        vs = slice((B_KV_HEADS + g) * B_HD, (B_KV_HEADS + g + 1) * B_HD)
        k_new = _rms(kvn[:, :, ks], knw)
        kn_parts.append(k_new)
        kk = jnp.concatenate([meta[:, :, ks], win[:, :, ks], k_new], axis=1)
        vv = jnp.concatenate([meta[:, :, vs], win[:, :, vs], kvn[:, :, vs]], axis=1)
        qq = jnp.concatenate(
            [_rms(q[:, :, (g * B_GROUP + i) * B_HD:(g * B_GROUP + i + 1) * B_HD], qnw) for i in range(B_GROUP)],
            axis=1)
        sink = jnp.zeros((1, rows, 1), F32)
        for i in range(B_GROUP):
            h = g * B_GROUP + i
            sink = jnp.where(hrow == i, sink_ref[0:1, h:h + 1], sink)
        s = jnp.einsum('bqd,bkd->bqk', qq, kk, preferred_element_type=F32) * (B_HD ** -0.5)
        o = _softmax_sink_pv(s, mask, sink, vv, (((2,), (1,)), ((0,), (0,))))
        for i in range(B_GROUP):
            h = g * B_GROUP + i
            o_ref[:, :, h * B_HD:(h + 1) * B_HD] = o[:, i * t_new:(i + 1) * t_new, :]
    nwin_ref[:, 0:w - t_new, :] = win[:, t_new:w, :]
    nwin_ref[:, w - t_new:w, :] = jnp.concatenate(kn_parts + [kvn[:, :, B_KV_HEADS * B_HD:]], axis=-1)


def _swa_sample(p3, win_all, meta_all, layer, win_prev, qnw, knw, sinks, nb):
    n, t_new, _ = p3.shape
    w = win_all.shape[2]
    full = lambda shape: pl.BlockSpec(shape, lambda i: (0,) * len(shape))
    lblk = lambda rows: pl.BlockSpec((None, nb, rows, BKV_W), lambda i: (layer, i, 0, 0))
    in_specs = [pl.BlockSpec((nb, t_new, B_HEADS * B_HD), lambda i: (i, 0, BQ_BLK)),
                pl.BlockSpec((nb, t_new, BKV_W), lambda i: (i, 0, BKV_BLK)),
                lblk(w), lblk(N_META),
                full(qnw.shape), full(knw.shape), full(sinks.shape)]
    args = [p3, p3, win_all, meta_all, qnw, knw, sinks]
    aliases = {}
    if win_prev is not None:
        in_specs.append(pl.BlockSpec(memory_space=pl.ANY))
        args.append(win_prev)
        aliases = {len(args) - 1: 1}
    return pl.pallas_call(
        _swa_sample_kernel,
        grid=(n // nb,),
        in_specs=in_specs,
        out_specs=[pl.BlockSpec((nb, t_new, B_HEADS * B_HD), lambda i: (i, 0, 0)), lblk(w)],
        out_shape=[jax.ShapeDtypeStruct((n, t_new, B_HEADS * B_HD), F32),
                   jax.ShapeDtypeStruct(win_all.shape, F32)],
        input_output_aliases=aliases,
        compiler_params=_cparams(("parallel",)),
        name="swa_sample",
    )(*args)


GDN_ROWS = 64
GDN_R = A_HEADS * GDN_ROWS
GDN_PAIRS = A_HEADS // 2
PAIR_W = 2 * A_DK
GDN_BATCH = 6


def _mm(a, b):
    return jnp.dot(a.astype(BF16), b.astype(BF16), preferred_element_type=F32)


def _bmm(a, b):
    return jnp.einsum('bij,bjk->bik', a.astype(BF16), b.astype(BF16), preferred_element_type=F32)


def _block_diag2(x):
    z = jnp.zeros(x.shape[:-1] + (A_DK,), x.dtype)
    return jnp.concatenate([jnp.concatenate([x[..., :A_DK], z], axis=-1),
                            jnp.concatenate([z, x[..., A_DK:]], axis=-1)], axis=-2)


def _gdn_prep_kernel(x_ref, halo_ref, ab_ref, cw_ref, alog_ref, dtb_ref, u_ref, wq_ref, qkk_ref, gl_ref,
                     *, sb, cpb, bb, n_invalid, seq_halo):
    t = pl.program_id(1)
    ns = x_ref.shape[0]
    rows = x_ref.shape[1] // cpb
    c = GDN_ROWS
    r = GDN_R
    halo = halo_ref[...]
    if seq_halo:
        hrow = lax.broadcasted_iota(jnp.int32, (1, GDN_HALO, 1), 1) + (t * cpb * rows - GDN_HALO)
        halo = jnp.where(hrow >= n_invalid, halo, 0.0)
    cw = cw_ref[...]
    alog = alog_ref[...]
    dtb = dtb_ref[...]
    ri = lax.broadcasted_iota(jnp.int32, (r, r), 0)
    ci = lax.broadcasted_iota(jnp.int32, (r, r), 1)
    same = (ri // sb) == (ci // sb)
    incl = same & (ri >= ci)
    strict = same & (ri > ci)
    eye = (ri == ci).astype(F32)
    rc_ = lax.broadcasted_iota(jnp.int32, (c, c), 0)
    cc_ = lax.broadcasted_iota(jnp.int32, (c, c), 1)
    same_c = (rc_ // sb) == (cc_ // sb)
    lmat = jnp.concatenate([(same_c & (rc_ >= cc_)).astype(F32), same_c.astype(F32)], axis=0)
    sel = (lax.broadcasted_iota(jnp.int32, (SUBLANES, 2 * c), 1)
           == c + (lax.broadcasted_iota(jnp.int32, (SUBLANES, 2 * c), 0) * SUBLANES // sb) * sb).astype(F32)
    off = GDN_HALO - (A_CONV - 1)
    assert ns == 1 or cpb == 1
    hi = lambda m, v: jnp.einsum('bij,bjk->bik', jnp.broadcast_to(m, (bb,) + m.shape), v,
                                 precision=HIGHEST, preferred_element_type=F32)
    for c0 in range(0, cpb, bb):
        nr = bb * rows
        x = x_ref[:, c0 * rows:c0 * rows + nr, :]
        ab = ab_ref[:, c0 * rows:c0 * rows + nr, :].reshape(bb, c, LANES)
        if n_invalid:
            row0 = (t * cpb + c0) * rows
            xrow = lax.broadcasted_iota(jnp.int32, (1, nr, 1), 1) + row0
            x = jnp.where(xrow >= n_invalid, x, 0.0)
        ext = jnp.concatenate([halo, x], axis=1)
        acc = cw[0:1, :] * ext[:, off:off + nr, :]
        for k in range(1, A_CONV):
            acc = acc + cw[k:k + 1, :] * ext[:, off + k:off + k + nr, :]
        y = _silu(acc).reshape(bb, c, QKV_W)
        halo = x[:, nr - GDN_HALO:nr, :]

        sp = jnp.maximum(ab + dtb, 0.0) + jnp.log1p(jnp.exp(-jnp.abs(ab + dtb)))
        gfull = -jnp.exp(alog) * sp
        bfull = jax.nn.sigmoid(ab)
        if n_invalid:
            valid = (lax.broadcasted_iota(jnp.int32, (bb, c, 1), 0) * c
                     + lax.broadcasted_iota(jnp.int32, (bb, c, 1), 1) + row0) >= n_invalid
            gfull = jnp.where(valid, gfull, 0.0)
            bfull = jnp.where(valid, bfull, 0.0)
        bcol = jnp.concatenate([bfull[:, :, A_HEADS + h:A_HEADS + h + 1] for h in range(A_HEADS)], axis=1)
        gg = hi(lmat, gfull)
        ggt = jnp.stack([gg[i].T for i in range(bb)])
        gc = jnp.concatenate([gg[:, :c, h:h + 1] for h in range(A_HEADS)], axis=1)
        grow = jnp.concatenate([ggt[:, h:h + 1, :c] for h in range(A_HEADS)], axis=2)
        drow = jnp.concatenate([ggt[:, h:h + 1, c:] for h in range(A_HEADS)], axis=2) - grow
        g8 = jnp.exp(hi(sel, gg))
        decay = jnp.where(incl, jnp.exp(jnp.where(incl, gc - grow, 0.0)), 0.0)

        qs, ks, vs = [], [], []
        for h in range(A_HEADS):
            q = y[:, :, h * A_DK:(h + 1) * A_DK]
            k = y[:, :, A_KW + h * A_DK:A_KW + (h + 1) * A_DK]
            qs.append(q * lax.rsqrt(jnp.sum(q * q, axis=-1, keepdims=True) + EPS) * (A_DK ** -0.5))
            ks.append(k * lax.rsqrt(jnp.sum(k * k, axis=-1, keepdims=True) + EPS))
            vs.append(y[:, :, 2 * A_KW + h * A_DK:2 * A_KW + (h + 1) * A_DK])
        qr = jnp.concatenate(qs, axis=1)
        kr = jnp.concatenate(ks, axis=1)
        vr = jnp.concatenate(vs, axis=1)
        krt = jnp.stack([kr[i].T for i in range(bb)])
        kq = _bmm(jnp.concatenate([kr, qr], axis=1), krt)
        a = jnp.where(strict, kq[:, :r] * decay * bcol, 0.0)
        qk = kq[:, r:] * decay
        tinv = eye - a
        ak = a
        kpow = 2
        while kpow < sb:
            ak = _bmm(ak, ak)
            tinv = tinv + _bmm(tinv, ak)
            kpow *= 2
        eg = jnp.exp(gc)
        uw = _bmm(tinv, jnp.concatenate([vr * bcol, kr * (bcol * eg)], axis=2))
        ur = uw[:, :, :A_DK]
        wr = uw[:, :, A_DK:]
        qd = qr * eg
        kdt = krt * jnp.exp(drow)
        cs = slice(c0, c0 + bb)
        for p in range(GDN_PAIRS):
            r0 = slice(2 * p * c, (2 * p + 1) * c)
            r1 = slice((2 * p + 1) * c, (2 * p + 2) * c)
            pl_ = slice(2 * p * c, (2 * p + 2) * c)
            u_ref[cs, p] = jnp.concatenate([ur[:, r0], ur[:, r1]], axis=2)
            wq = jnp.concatenate([jnp.concatenate([wr[:, r0], wr[:, r1]], axis=2),
                                  jnp.concatenate([qd[:, r0], qd[:, r1]], axis=2)], axis=1)
            wq_ref[cs, p] = wq.astype(wq_ref.dtype)
            qkk = jnp.concatenate([qk[:, r0, pl_] + qk[:, r1, pl_], kdt[:, :, pl_]], axis=1)
            qkk_ref[cs, p] = qkk.astype(qkk_ref.dtype)
            gl_ref[cs, p] = jnp.concatenate(
                [jnp.broadcast_to(g8[:, :, 2 * p:2 * p + 1], (bb, SUBLANES, A_DK)),
                 jnp.broadcast_to(g8[:, :, 2 * p + 1:2 * p + 2], (bb, SUBLANES, A_DK))], axis=2)


def _gdn_prep(x_arr, halo_arr, halo_spec, cw, alog, dtb, ns, rows, cpb, bb, sb, n_invalid, op_dtype):
    n, l, _ = x_arr.shape
    grid = (n // ns, l // (cpb * rows))
    units = grid[0] * grid[1] * cpb
    kern = functools.partial(_gdn_prep_kernel, sb=sb, cpb=cpb, bb=bb, n_invalid=n_invalid,
                             seq_halo=halo_arr is x_arr)
    full = lambda shape: pl.BlockSpec(shape, lambda i, t: (0,) * len(shape))
    steps = grid[1]
    out_map = lambda i, t: (i * steps + t, 0, 0, 0)
    c = GDN_ROWS
    return pl.pallas_call(
        kern,
        grid=grid,
        in_specs=[pl.BlockSpec((ns, cpb * rows, QKV_W), lambda i, t: (i, t, 0)),
                  halo_spec,
                  pl.BlockSpec((ns, cpb * rows, LANES), lambda i, t: (i, t, AB_BLK)),
                  full(cw.shape), full(alog.shape), full(dtb.shape)],
        out_specs=[pl.BlockSpec((cpb, GDN_PAIRS, c, PAIR_W), out_map),
                   pl.BlockSpec((cpb, GDN_PAIRS, 2 * c, PAIR_W), out_map),
                   pl.BlockSpec((cpb, GDN_PAIRS, c + A_DK, 2 * c), out_map),
                   pl.BlockSpec((cpb, GDN_PAIRS, SUBLANES, PAIR_W), out_map)],
        out_shape=[jax.ShapeDtypeStruct((units, GDN_PAIRS, c, PAIR_W), F32),
                   jax.ShapeDtypeStruct((units, GDN_PAIRS, 2 * c, PAIR_W), op_dtype),
                   jax.ShapeDtypeStruct((units, GDN_PAIRS, c + A_DK, 2 * c), op_dtype),
                   jax.ShapeDtypeStruct((units, GDN_PAIRS, SUBLANES, PAIR_W), F32)],
        compiler_params=_cparams(("parallel", "arbitrary")),
        name="gdn_prep",
    )(x_arr, halo_arr, x_arr, cw, alog, dtb)


def _gdn_out(o_pair, z_ref, o_ref, nw, idx, p):
    for i in range(2):
        h = 2 * p + i
        hs = slice(h * A_DK, (h + 1) * A_DK)
        o = _rms(o_pair[..., i * A_DK:(i + 1) * A_DK], nw) * _silu(z_ref[idx + (hs,)])
        o_ref[idx + (hs,)] = o.astype(o_ref.dtype)


def _gdn_scan_kernel(u_ref, wq_ref, qkk_ref, gl_ref, z_ref, nw_ref, o_ref, sout_ref, s_ref):
    t = pl.program_id(0)
    n_seq = u_ref.shape[0]
    c = GDN_ROWS
    nw = nw_ref[...]

    @pl.when(t == 0)
    def _():
        s_ref[...] = jnp.zeros(s_ref.shape, F32)

    nb = n_seq * GDN_PAIRS
    flat = lambda ref: ref[:, 0].reshape((nb,) + ref.shape[3:])
    s = s_ref[...].reshape(nb, A_DK, PAIR_W)
    r1 = jnp.einsum('bmk,bkn->bmn', flat(wq_ref), _block_diag2(s.astype(BF16)), preferred_element_type=F32)
    vnew = flat(u_ref) - r1[:, :c]
    r2 = jnp.einsum('bmk,bkn->bmn', flat(qkk_ref), _block_diag2(vnew.astype(BF16)), preferred_element_type=F32)
    s_ref[...] = (s * flat(gl_ref)[:, 0:1, :] + r2[:, c:]).reshape(s_ref.shape)
    o = (r1[:, c:] + r2[:, :c]).reshape(n_seq, GDN_PAIRS, c, PAIR_W)
    for p in range(GDN_PAIRS):
        _gdn_out(o[:, p], z_ref, o_ref, nw, (slice(None), slice(None)), p)

    @pl.when(t == pl.num_programs(0) - 1)
    def _():
        for n in range(n_seq):
            for p in range(GDN_PAIRS):
                s = s_ref[n, p]
                sout_ref[n, 2 * p] = s[:, :A_DK]
                sout_ref[n, 2 * p + 1] = s[:, A_DK:]


def _gdn_scan(u, wq, qkk, gl, p3, nw):
    n, l, _ = p3.shape
    sshape = (n, A_HEADS, A_DK, A_DK)
    c = GDN_ROWS
    steps = l // c
    v5 = lambda a: a.reshape((n, steps) + a.shape[1:])
    u, wq, qkk, gl = v5(u), v5(wq), v5(qkk), v5(gl)
    unit = lambda a: pl.BlockSpec((n, 1) + a.shape[2:], lambda t: (0, t, 0, 0, 0))
    vw = A_HEADS * A_DK
    return pl.pallas_call(
        _gdn_scan_kernel,
        grid=(steps,),
        in_specs=[unit(u), unit(wq), unit(qkk), unit(gl),
                  pl.BlockSpec((n, c, vw), lambda t: (0, t, Z_BLK)),
                  pl.BlockSpec(nw.shape, lambda t: (0, 0))],
        out_specs=[pl.BlockSpec((n, c, vw), lambda t: (0, t, 0)),
                   pl.BlockSpec(sshape, lambda t: (0, 0, 0, 0))],
        out_shape=[jax.ShapeDtypeStruct((n, l, vw), BF16),
                   jax.ShapeDtypeStruct(sshape, F32)],
        scratch_shapes=[pltpu.VMEM((n, GDN_PAIRS, A_DK, PAIR_W), F32)],
        compiler_params=_cparams(("arbitrary",)),
        name="gdn_scan",
    )(u, wq, qkk, gl, p3, nw)


def _gdn_step_kernel(u_ref, wq_ref, qkk_ref, gl_ref, z_ref, s0_ref, nw_ref, *rest):
    o_ref, sout_ref = rest[-2:]
    ns, rows, _ = z_ref.shape
    c = GDN_ROWS
    nw = nw_ref[...]
    lane_seq = (lax.broadcasted_iota(jnp.int32, (A_DK, 2 * c), 1) % c) // rows
    for p in range(GDN_PAIRS):
        wq = wq_ref[0, p]
        qkk = qkk_ref[0, p]
        gl = gl_ref[0, p]
        ws, qs, states = [], [], []
        for s_i in range(ns):
            sl = slice(s_i * rows, (s_i + 1) * rows)
            s = jnp.concatenate([s0_ref[s_i, 2 * p], s0_ref[s_i, 2 * p + 1]], axis=1)
            states.append(s)
            lhs = jnp.concatenate([wq[sl], wq[c + s_i * rows:c + (s_i + 1) * rows]], axis=0)
            r1 = _mm(lhs, _block_diag2(s.astype(BF16)))
            ws.append(r1[:rows])
            qs.append(r1[rows:])
        vnew = u_ref[0, p] - jnp.concatenate(ws, axis=0)
        bdv = _block_diag2(vnew.astype(BF16))
        o = jnp.concatenate(qs, axis=0) + _mm(qkk[:c], bdv)
        _gdn_out(o.reshape(ns, rows, PAIR_W), z_ref, o_ref, nw, (slice(None), slice(None)), p)
        kdt = qkk[c:]
        lhs = jnp.concatenate([jnp.where(lane_seq == s_i, kdt, 0.0) for s_i in range(ns)], axis=0)
        upd = _mm(lhs, bdv)
        for s_i in range(ns):
            s_new = states[s_i] * gl[s_i:s_i + 1, :] + upd[s_i * A_DK:(s_i + 1) * A_DK]
            sout_ref[s_i, 2 * p] = s_new[:, :A_DK]
            sout_ref[s_i, 2 * p + 1] = s_new[:, A_DK:]


def _gdn_step(u, wq, qkk, gl, p3, s_all, layer, s_prev, nw, ns):
    n, rows, _ = p3.shape
    unit = lambda a: pl.BlockSpec((1,) + a.shape[1:], lambda i: (i, 0, 0, 0))
    vw = A_HEADS * A_DK
    sblk = pl.BlockSpec((None, ns) + s_all.shape[2:], lambda i: (layer, i, 0, 0, 0))
    in_specs = [unit(u), unit(wq), unit(qkk), unit(gl),
                pl.BlockSpec((ns, rows, vw), lambda i: (i, 0, Z_BLK)),
                sblk,
                pl.BlockSpec(nw.shape, lambda i: (0, 0))]
    args = [u, wq, qkk, gl, p3, s_all, nw]
    aliases = {}
    if s_prev is not None:
        in_specs.append(pl.BlockSpec(memory_space=pl.ANY))
        args.append(s_prev)
        aliases = {len(args) - 1: 1}
    return pl.pallas_call(
        _gdn_step_kernel,
        grid=(n // ns,),
        in_specs=in_specs,
        out_specs=[pl.BlockSpec((ns, rows, vw), lambda i: (i, 0, 0)), sblk],
        out_shape=[jax.ShapeDtypeStruct((n, rows, vw), F32),
                   jax.ShapeDtypeStruct(s_all.shape, F32)],
        input_output_aliases=aliases,
        compiler_params=_cparams(("parallel",)),
        name="gdn_step",
    )(*args)


def _gdn(p3, state, hist, cw, alog, dtb, nw, cpb, n_invalid):
    n, l, _ = p3.shape
    c = GDN_ROWS
    if hist is None:
        steps8 = cpb * c // GDN_HALO
        halo_spec = pl.BlockSpec((1, GDN_HALO, QKV_W), lambda i, t: (i, jnp.maximum(t * steps8 - 1, 0), 0))
        u, wq, qkk, gl = _gdn_prep(p3, p3, halo_spec, cw, alog, dtb, 1, c, cpb, min(cpb, GDN_BATCH), c, n_invalid, BF16)
        return _gdn_scan(u, wq, qkk, gl, p3, nw)
    ns = c // l
    halo_spec = pl.BlockSpec((ns, GDN_HALO, QKV_W), lambda i, t: (i, 0, 0))
    u, wq, qkk, gl = _gdn_prep(p3, hist, halo_spec, cw, alog, dtb, ns, l, 1, 1, l, n_invalid, F32)
    s_all, layer, s_prev = state
    return _gdn_step(u, wq, qkk, gl, p3, s_all, layer, s_prev, nw, ns)


def _merge_kernel(oa_ref, ob_ref, oc_ref, gl_ref, x_ref, wb_ref, wo_ref, out_ref):
    d = x_ref.shape[1]
    merged = None
    for r, o_ref in enumerate((oa_ref, ob_ref, oc_ref)):
        pb = jnp.dot(o_ref[...].astype(BF16), wb_ref[r], preferred_element_type=F32)
        term = jax.nn.sigmoid(gl_ref[:, r * d:(r + 1) * d]) * pb
        merged = term if merged is None else merged + term
    out_ref[...] = x_ref[...] + jnp.dot(merged.astype(BF16), wo_ref[...], preferred_element_type=F32)


def _merge(oa, ob, oc, p2, x2, wb, wo, tm):
    t, d = x2.shape
    bw = oa.shape[1]
    row = lambda w, blk: pl.BlockSpec((tm, w), lambda i: (i, blk))
    return pl.pallas_call(
        _merge_kernel,
        grid=(t // tm,),
        in_specs=[row(bw, 0), row(bw, 0), row(bw, 0), row(3 * d, GL_BLK), row(d, 0),
                  pl.BlockSpec(wb.shape, lambda i: (0, 0, 0)),
                  pl.BlockSpec(wo.shape, lambda i: (0, 0))],
        out_specs=row(d, 0),
        out_shape=jax.ShapeDtypeStruct((t, d), F32),
        compiler_params=_cparams(("parallel",)),
        name="merge",
    )(oa, ob, oc, p2, x2, wb, wo)


def _route(logits):
    lane = lax.broadcasted_iota(jnp.int32, logits.shape, 1)
    is_g = (lane >= N_EXPERTS) & (lane < N_EXPERTS + N_GROUPS)
    gl = jnp.where(is_g, logits, NEG)
    gmax = jnp.max(gl, axis=-1, keepdims=True)
    gidx = jnp.min(jnp.where(gl == gmax, lane - N_EXPERTS, LANES), axis=-1, keepdims=True)
    gw = 1.0 / jnp.sum(jnp.where(is_g, jnp.exp(gl - gmax), 0.0), axis=-1, keepdims=True)
    in_grp = (lane < N_EXPERTS) & ((lane // EXP_PER_GROUP) == gidx)
    el = jnp.where(in_grp, logits, NEG)
    v1 = jnp.max(el, axis=-1, keepdims=True)
    i1 = jnp.min(jnp.where(el == v1, lane, LANES), axis=-1, keepdims=True)
    el2 = jnp.where(lane == i1, NEG, el)
    v2 = jnp.max(el2, axis=-1, keepdims=True)
    i2 = jnp.min(jnp.where(el2 == v2, lane, LANES), axis=-1, keepdims=True)
    e21 = jnp.exp(v2 - v1)
    w1 = gw / (1.0 + e21)
    w2 = gw * e21 / (1.0 + e21)
    return jnp.where(lane == i1, w1, 0.0) + jnp.where(lane == i2, w2, 0.0)


def _moe_kernel(x_ref, nw_ref, wr_ref, br_ref, wg_ref, wu_ref, wd_ref, out_ref, h_ref, cmb_ref):
    g = pl.program_id(1)

    @pl.when(g == 0)
    def _():
        x = x_ref[...]
        h = _rms(x, nw_ref[...])
        h_hi = h.astype(BF16)
        h_ref[...] = h_hi
        h_lo = (h - h_hi.astype(F32)).astype(BF16)
        l2 = jnp.dot(h_hi, wr_ref[...], preferred_element_type=F32)
        logits = (l2[:, :LANES] + l2[:, LANES:]
                  + jnp.dot(h_lo, wr_ref[:, :LANES], preferred_element_type=F32) + br_ref[...])
        cmb_ref[...] = _route(logits)
        out_ref[...] = x

    h = h_ref[...]
    cmb = cmb_ref[...]
    lane = lax.broadcasted_iota(jnp.int32, cmb.shape, 1)
    parts = []
    for e in range(EXP_PER_GROUP):
        ce = jnp.sum(jnp.where(lane == g * EXP_PER_GROUP + e, cmb, 0.0), axis=-1, keepdims=True)
        hid = (_silu(jnp.dot(h, wg_ref[e], preferred_element_type=F32))
               * jnp.dot(h, wu_ref[e], preferred_element_type=F32))
        parts.append((hid * ce).astype(BF16))
    wd = wd_ref[...].reshape(-1, wd_ref.shape[-1])
    out_ref[...] += jnp.dot(jnp.concatenate(parts, axis=1), wd, preferred_element_type=F32)


def _moe(x2, nw, wr, br, wg, wu, wd, tm):
    t, d = x2.shape
    ne, _, de = wg.shape
    return pl.pallas_call(
        _moe_kernel,
        grid=(t // tm, ne // EXP_PER_GROUP),
        in_specs=[pl.BlockSpec((tm, d), lambda i, g: (i, 0)),
                  pl.BlockSpec((1, d), lambda i, g: (0, 0)),
                  pl.BlockSpec(wr.shape, lambda i, g: (0, 0)),
                  pl.BlockSpec(br.shape, lambda i, g: (0, 0)),
                  pl.BlockSpec((EXP_PER_GROUP, d, de), lambda i, g: (g, 0, 0)),
                  pl.BlockSpec((EXP_PER_GROUP, d, de), lambda i, g: (g, 0, 0)),
                  pl.BlockSpec((EXP_PER_GROUP, de, d), lambda i, g: (g, 0, 0))],
        out_specs=pl.BlockSpec((tm, d), lambda i, g: (i, 0)),
        out_shape=jax.ShapeDtypeStruct((t, d), F32),
        scratch_shapes=[pltpu.VMEM((tm, d), BF16), pltpu.VMEM((tm, LANES), F32)],
        compiler_params=_cparams(("parallel", "arbitrary")),
        name="moe",
    )(x2, nw, wr, br, wg, wu, wd)


def _pad_lanes(v, width):
    v = v.reshape(1, -1)
    return jnp.pad(v, ((0, 0), (0, width - v.shape[1])))


def _layer_weights(l, norm1_w, norm2_w, w_in, gdn_conv_w, gdn_a_log, gdn_dt_bias, gdn_norm_w,
                   swa_q_norm_w, swa_k_norm_w, swa_sinks, conv_dw_w, conv_dw_b, conv_ln_w, conv_ln_b,
                   w_branch, w_out, router_group_w, router_group_b, router_expert_w, router_expert_b,
                   moe_w_gate, moe_w_up, moe_w_down):
    d = w_in.shape[1]
    wi = w_in[l]
    o_ab = 4 * A_KW
    o_bq = o_ab + 2 * A_HEADS
    o_cu = o_bq + B_HEADS * B_HD + BKV_W
    o_gl = o_cu + 2 * C_CH
    w_perm = jnp.concatenate(
        [wi[:, :o_ab], wi[:, o_cu:o_gl], wi[:, o_gl:], wi[:, o_bq:o_cu], wi[:, o_ab:o_bq],
         jnp.zeros((d, LANES - 2 * A_HEADS), wi.dtype)], axis=1).astype(BF16)
    assert w_perm.shape[1] == PROJ_W
    wr = jnp.concatenate([router_expert_w[l], router_group_w[l],
                          jnp.zeros((d, LANES - N_EXPERTS - N_GROUPS), F32)], axis=1)
    br = _pad_lanes(jnp.concatenate([router_expert_b[l], router_group_b[l]]), LANES)
    wr_hi = wr.astype(BF16)
    wr = jnp.concatenate([wr_hi, (wr - wr_hi.astype(F32)).astype(BF16)], axis=1)
    return dict(
        n1=norm1_w[l].reshape(1, d), n2=norm2_w[l].reshape(1, d), w_in=w_perm,
        gdn_cw=gdn_conv_w[l], alog=_pad_lanes(gdn_a_log[l], LANES), dtb=_pad_lanes(gdn_dt_bias[l], LANES),
        gdn_nw=gdn_norm_w[l].reshape(1, -1),
        qnw=swa_q_norm_w[l].reshape(1, -1), knw=swa_k_norm_w[l].reshape(1, -1), sinks=swa_sinks[l].reshape(1, -1),
        qnw_heads=jnp.tile(swa_q_norm_w[l].reshape(1, -1), (1, B_HEADS)),
        knw_heads=jnp.tile(swa_k_norm_w[l].reshape(1, -1), (1, B_KV_HEADS)),
        dww=conv_dw_w[l], dwb=conv_dw_b[l].reshape(1, -1), lnw=conv_ln_w[l].reshape(1, -1),
        lnb=conv_ln_b[l].reshape(1, -1),
        wb=w_branch[l].astype(BF16), wo=w_out[l].astype(BF16), wr=wr, br=br,
        wg=moe_w_gate[l].astype(BF16), wu=moe_w_up[l].astype(BF16), wd=moe_w_down[l].astype(BF16))


def _tiles(n, l):
    prompt = l > BLOCK
    t = n * l
    if prompt:
        return dict(proj_tm=1536, tok_tm=512, moe_tm=1408, conf_nb=1, conf_tm=384, conf_rc=64,
                    gdn_cpb=6, swa_bps=3, n_invalid=META_PAD, branch_dtype=BF16)
    return dict(proj_tm=t, tok_tm=512, moe_tm=t, conf_nb=16, conf_tm=l, conf_rc=l,
                gdn_cpb=1, n_invalid=0, branch_dtype=F32)


def _layer(x3, lw, gdn_state, gdn_hist, conf_hist, swa_cache):
    n, l, d = x3.shape
    tl = _tiles(n, l)
    x2 = x3.reshape(n * l, d)
    p2 = _inproj(x2, lw['n1'], lw['w_in'], tl['proj_tm'], PROJ_W // 5)
    p3 = p2.reshape(n, l, PROJ_W)
    o_a, s_new = _gdn(p3, gdn_state, gdn_hist, lw['gdn_cw'], lw['alog'], lw['dtb'], lw['gdn_nw'],
                      tl['gdn_cpb'], tl['n_invalid'])
    gh_new = p3[:, l - (A_CONV - 1):, :QKV_W]
    if swa_cache is None:
        o_b, kvn = _swa_prompt(p3, lw['qnw_heads'], lw['knw_heads'], lw['sinks'], tl['swa_bps'])
        kv = lambda sl: kvn[:, sl].reshape(n, -1, 2, B_KV_HEADS, B_HD)
        swa_new = (kv(slice(META_PAD, BLOCK)), kv(slice(l - WINDOW, l)))
    else:
        win_all, meta_all, layer, win_prev = swa_cache
        o_b, swa_new = _swa_sample(p3, win_all, meta_all, layer, win_prev, lw['qnw'], lw['knw'], lw['sinks'], 16)
    o_c, ch_new = _conformer(p3, conf_hist, lw['dww'], lw['dwb'], lw['lnw'], lw['lnb'],
                             tl['conf_nb'], tl['conf_tm'], tl['conf_rc'], tl['n_invalid'], tl['branch_dtype'])
    t = n * l
    x2 = _merge(o_a.reshape(t, -1), o_b.reshape(t, -1), o_c.reshape(t, -1), p2, x2, lw['wb'], lw['wo'],
                tl['tok_tm'])
    x2 = _moe(x2, lw['n2'], lw['wr'], lw['br'], lw['wg'], lw['wu'], lw['wd'], tl['moe_tm'])
    return x2.reshape(n, l, d), s_new, gh_new, swa_new, ch_new


def kernel(x_prompt, x_sample, state_gdn, cache_gdn_conv, cache_swa_kv, cache_meta_kv, cache_conv, meta_tokens, norm1_w, norm2_w, w_in, gdn_conv_w, gdn_a_log, gdn_dt_bias, gdn_norm_w, swa_q_norm_w, swa_k_norm_w, swa_sinks, conv_dw_w, conv_dw_b, conv_ln_w, conv_ln_b, w_branch, w_out, router_group_w, router_group_b, router_expert_w, router_expert_b, moe_w_gate, moe_w_up, moe_w_down):
    dtp = x_prompt.dtype
    n_p, _, d = x_prompt.shape
    n_s = x_sample.shape[0]
    depth = w_in.shape[0]
    xp = jnp.pad(x_prompt, ((0, 0), (BLOCK, 0), (0, 0)))
    xp = lax.dynamic_update_slice(
        xp, jnp.broadcast_to(meta_tokens.astype(dtp)[None], (n_p, N_META, d)), (0, META_PAD, 0))
    xs = x_sample
    state_s = None
    win_s = None
    win_all = cache_swa_kv.reshape(depth, n_s, -1, BKV_W)
    meta_all = cache_meta_kv.reshape(depth, n_s, N_META, BKV_W)
    outs_p, outs_s = [], []
    for l in range(depth):
        lw = _layer_weights(l, norm1_w, norm2_w, w_in, gdn_conv_w, gdn_a_log, gdn_dt_bias, gdn_norm_w,
                            swa_q_norm_w, swa_k_norm_w, swa_sinks, conv_dw_w, conv_dw_b, conv_ln_w, conv_ln_b,
                            w_branch, w_out, router_group_w, router_group_b, router_expert_w, router_expert_b,
                            moe_w_gate, moe_w_up, moe_w_down)
        xp, s_p, gh_p, (mkv_p, wkv_p), ch_p = _layer(
            xp, lw, None, None, jnp.zeros((n_p, C_CONV - 1, C_CH), dtp), None)
        outs_p.append((s_p, gh_p, wkv_p, mkv_p, ch_p))
        gh0 = jnp.pad(cache_gdn_conv[l], ((0, 0), (GDN_HALO - (A_CONV - 1), 0), (0, 0)))
        xs, state_s, gh_s, win_s, ch_s = _layer(xs, lw, (state_gdn, l, state_s), gh0, cache_conv[l],
                                                (win_all, meta_all, l, win_s))
        outs_s.append((gh_s, ch_s))
    stack = lambda outs, i: jnp.stack([o[i] for o in outs])
    return (xp[:, BLOCK:], xs,
            stack(outs_p, 0), stack(outs_p, 1), stack(outs_p, 2), stack(outs_p, 3), stack(outs_p, 4),
            state_s, stack(outs_s, 0), win_s.reshape(cache_swa_kv.shape), stack(outs_s, 1))
```

```python
import functools

import jax
import jax.numpy as jnp
from jax import lax
from jax.experimental import pallas as pl
from jax.experimental.pallas import tpu as pltpu

F32 = jnp.float32
BF16 = jnp.bfloat16
HIGHEST = lax.Precision.HIGHEST
EPS = 1e-6
NEG = -1e30

VMEM_LIMIT_BYTES = 56 * 1024 * 1024
LANES = 128
SUBLANES = 8

PAST_LEN = 16384
N_META = 16
BLOCK = 128
WINDOW = 128
META_PAD = BLOCK - N_META
A_HEADS = 4
A_DK = 128
A_CONV = 4
B_HEADS = 8
B_KV_HEADS = 2
B_GROUP = B_HEADS // B_KV_HEADS
B_HD = 64
C_CH = 512
C_CONV = 31
BRANCH_W = 512
N_GROUPS = 4
EXP_PER_GROUP = 4
N_EXPERTS = N_GROUPS * EXP_PER_GROUP

A_KW = A_HEADS * A_DK
QKV_W = 3 * A_KW
Z_BLK = QKV_W // BRANCH_W
CU_BLK = 2
GL_BLK = 1
BQ_BLK = 12
BKV_W = 2 * B_KV_HEADS * B_HD
BKV_BLK = 26
AB_BLK = 54
PROJ_W = 7040
CONF_HALO = 32
GDN_HALO = SUBLANES


def _rms(x, w):
    return x * lax.rsqrt(jnp.mean(x * x, axis=-1, keepdims=True) + EPS) * w


def _silu(x):
    return x * jax.nn.sigmoid(x)


def _cparams(sem):
    return pltpu.CompilerParams(dimension_semantics=sem, vmem_limit_bytes=VMEM_LIMIT_BYTES)


def _inproj_kernel(x_ref, nw_ref, w_ref, o_ref, h_ref):
    @pl.when(pl.program_id(1) == 0)
    def _():
        h_ref[...] = _rms(x_ref[...], nw_ref[...]).astype(BF16)

    o_ref[...] = jnp.dot(h_ref[...], w_ref[...], preferred_element_type=F32)


def _inproj(x2, nw, w, tm, tn):
    t, d = x2.shape
    wd = w.shape[1]
    return pl.pallas_call(
        _inproj_kernel,
        grid=(t // tm, wd // tn),
        in_specs=[pl.BlockSpec((tm, d), lambda i, j: (i, 0)),
                  pl.BlockSpec((1, d), lambda i, j: (0, 0)),
                  pl.BlockSpec((d, tn), lambda i, j: (0, j))],
        out_specs=pl.BlockSpec((tm, tn), lambda i, j: (i, j)),
        out_shape=jax.ShapeDtypeStruct((t, wd), F32),
        scratch_shapes=[pltpu.VMEM((tm, d), BF16)],
        compiler_params=_cparams(("parallel", "arbitrary")),
        name="inproj",
    )(x2, nw, w)


def _conf_kernel(cu_ref, hist_ref, dww_ref, dwb_ref, lnw_ref, lnb_ref, y_ref, nh_ref, e_ref, c_ref,
                 *, tm, rc, n_invalid):
    t = pl.program_id(1)
    nb = cu_ref.shape[0]
    halo = CONF_HALO

    @pl.when(t == 0)
    def _():
        e_ref[:, 0:SUBLANES, :] = jnp.zeros((nb, SUBLANES, C_CH), F32)
        e_ref[:, halo + tm:halo + tm + SUBLANES, :] = jnp.zeros((nb, SUBLANES, C_CH), F32)
        e_ref[:, halo - (C_CONV - 1):halo, :] = hist_ref[...]

    @pl.when(t > 0)
    def _():
        e_ref[:, 0:halo, :] = e_ref[:, tm:tm + halo, :]

    cu = cu_ref[...]
    u = cu[:, :, :C_CH] * jax.nn.sigmoid(cu[:, :, C_CH:])
    if n_invalid:
        row = lax.broadcasted_iota(jnp.int32, (1, tm, 1), 1) + t * tm
        u = jnp.where(row >= n_invalid, u, 0.0)
    e_ref[:, halo:halo + tm, :] = u
    nh_ref[...] = e_ref[:, halo + tm - (C_CONV - 1):halo + tm, :]

    off = halo - (C_CONV - 1)
    for c in range(C_CH // LANES):
        cs = slice(c * LANES, (c + 1) * LANES)

        def rbody(r, carry, cs=cs):
            r0 = pl.multiple_of(r * rc, rc)
            blk = e_ref[:, pl.ds(r0, rc + halo + SUBLANES), cs]
            acc = jnp.zeros((nb, rc, LANES), F32)
            for s in range(SUBLANES):
                z = None
                for a in range((halo + SUBLANES) // SUBLANES):
                    k = SUBLANES * a + s - off
                    if 0 <= k < C_CONV:
                        term = dww_ref[k:k + 1, cs] * blk[:, SUBLANES * a:SUBLANES * a + rc + SUBLANES, :]
                        z = term if z is None else z + term
                acc = acc + z[:, s:s + rc, :]
            c_ref[:, pl.ds(r0, rc), cs] = acc + dwb_ref[:, cs]
            return carry

        lax.fori_loop(0, tm // rc, rbody, 0)

    y = c_ref[...]
    mu = jnp.mean(y, axis=-1, keepdims=True)
    var = jnp.mean(jnp.square(y - mu), axis=-1, keepdims=True)
    yn = (y - mu) * lax.rsqrt(var + EPS) * lnw_ref[...] + lnb_ref[...]
    y_ref[...] = _silu(yn).astype(y_ref.dtype)


def _conformer(p3, hist, dww, dwb, lnw, lnb, nb, tm, rc, n_invalid, out_dtype):
    n, l, _ = p3.shape
    kern = functools.partial(_conf_kernel, tm=tm, rc=rc, n_invalid=n_invalid)
    full = lambda shape: pl.BlockSpec(shape, lambda i, t: (0,) * len(shape))
    return pl.pallas_call(
        kern,
        grid=(n // nb, l // tm),
        in_specs=[pl.BlockSpec((nb, tm, 2 * C_CH), lambda i, t: (i, t, CU_BLK)),
                  pl.BlockSpec((nb, C_CONV - 1, C_CH), lambda i, t: (i, 0, 0)),
                  full(dww.shape), full(dwb.shape), full(lnw.shape), full(lnb.shape)],
        out_specs=[pl.BlockSpec((nb, tm, C_CH), lambda i, t: (i, t, 0)),
                   pl.BlockSpec((nb, C_CONV - 1, C_CH), lambda i, t: (i, 0, 0))],
        out_shape=[jax.ShapeDtypeStruct((n, l, C_CH), out_dtype),
                   jax.ShapeDtypeStruct((n, C_CONV - 1, C_CH), F32)],
        scratch_shapes=[pltpu.VMEM((nb, CONF_HALO + tm + SUBLANES, C_CH), F32), pltpu.VMEM((nb, tm, C_CH), F32)],
        compiler_params=_cparams(("parallel", "arbitrary")),
        name="conformer",
    )(p3, hist, dww, dwb, lnw, lnb)


SWA_ROWS = 512


def _softmax_sink_pv(s, mask, sink, v, dims):
    s = jnp.where(mask, s, NEG)
    m = jnp.maximum(jnp.max(s, axis=-1, keepdims=True), sink)
    p = jnp.exp(s - m)
    den = jnp.sum(p, axis=-1, keepdims=True) + jnp.exp(sink - m)
    return lax.dot_general(p, v, dims, preferred_element_type=F32) / den


def _swa_prompt_kernel(q_ref, kv_ref, kvp_ref, kvm_ref, qnw_ref, knw_ref, sink_ref, o_ref, kvn_ref, *, bps):
    step = pl.program_id(1)
    qnw = qnw_ref[...]
    knw = knw_ref[...]
    qw = B_HEADS * B_HD
    kw = B_KV_HEADS * B_HD
    nk = N_META + 2 * BLOCK
    nke = nk + SUBLANES
    rows = B_GROUP * BLOCK
    r = lax.broadcasted_iota(jnp.int32, (rows, nke), 0) % BLOCK
    c = lax.broadcasted_iota(jnp.int32, (rows, nke), 1)
    cp = c - N_META
    co = cp - BLOCK
    hrow = lax.broadcasted_iota(jnp.int32, (rows, 1), 0) // BLOCK
    m_meta = c < N_META
    m_own = (co >= 0) & (co <= r) & (c < nk)
    m_prev = (cp >= 0) & (cp < BLOCK) & (cp > r)
    bias0 = jnp.where(m_meta & (META_PAD + c <= r), 0.0, NEG)
    bias1 = jnp.where(m_meta | m_own, 0.0, NEG)
    bias2 = jnp.where(m_meta | m_own | m_prev, 0.0, NEG)
    sinks = []
    for g in range(B_KV_HEADS):
        sink = jnp.zeros((rows, 1), F32)
        for i in range(B_GROUP):
            h = g * B_GROUP + i
            sink = jnp.where(hrow == i, sink_ref[0:1, h:h + 1], sink)
        sinks.append(sink)
    hb_q = (lax.broadcasted_iota(jnp.int32, (qw, qw), 0) // B_HD
            == lax.broadcasted_iota(jnp.int32, (qw, qw), 1) // B_HD).astype(BF16)
    hb_k = hb_q[:kw, :kw]
    half = lax.broadcasted_iota(jnp.int32, (1, LANES), 1) // B_HD

    def head_rms(x, hb, w):
        sq = x * x
        hi = sq.astype(BF16)
        lo = (sq - hi.astype(F32)).astype(BF16)
        ss = jnp.dot(hi, hb, preferred_element_type=F32) + jnp.dot(lo, hb, preferred_element_type=F32)
        return x * lax.rsqrt(ss * (1.0 / B_HD) + EPS) * w

    kvm = kvm_ref[META_PAD:BLOCK, :]
    km = head_rms(kvm[:, :kw], hb_k, knw)
    kv_halo = kvp_ref[...]
    q = q_ref[...]
    kv = kv_ref[...]
    k_own = head_rms(kv[:, :kw], hb_k, knw)
    kvn_ref[...] = jnp.concatenate([k_own, kv[:, kw:]], axis=-1)
    qn = head_rms(q, hb_q, qnw)
    tq = bps * BLOCK
    blocks = lambda x: x.reshape(bps, BLOCK, x.shape[-1])
    k_prev = jnp.concatenate([head_rms(kv_halo[:, :kw], hb_k, knw), k_own[:tq - BLOCK]], axis=0)
    v_prev = jnp.concatenate([kv_halo[:, kw:], kv[:tq - BLOCK, kw:]], axis=0)
    rep = lambda x: jnp.broadcast_to(x[None], (bps,) + x.shape)
    zpad = jnp.zeros((bps, SUBLANES, kw), F32)
    kcat_b = jnp.concatenate([rep(km), blocks(k_prev), blocks(k_own), zpad], axis=1).astype(BF16)
    vcat_b = jnp.concatenate([rep(kvm[:, kw:]), blocks(v_prev), blocks(kv[:, kw:]), zpad], axis=1).astype(BF16)
    ones_b = jnp.ones((bps, nke, kw), BF16)
    bias = jnp.stack([jnp.where(step * bps + b >= 2, bias2, jnp.where(step * bps + b >= 1, bias1, bias0))
                      for b in range(bps)])
    tiles = [qn[:, m * LANES:(m + 1) * LANES] for m in range(qw // LANES)]
    swapped = [pltpu.roll(tl_, B_HD, 1) for tl_ in tiles]
    outs = [None] * B_HEADS
    for g in range(B_KV_HEADS):
        in_g = half == g
        parts = []
        for i in range(B_GROUP):
            h = g * B_GROUP + i
            src = tiles[h // 2] if h % 2 == g else swapped[h // 2]
            parts.append(blocks(jnp.where(in_g, src, 0.0)))
        lhs = jnp.concatenate(parts, axis=1).astype(BF16)
        bias_g = jnp.where(c == nk, sinks[g], bias)
        chunks = []
        for r0 in range(0, rows, SWA_ROWS):
            rc = slice(r0, r0 + SWA_ROWS)
            s = jnp.einsum('bqd,bkd->bqk', lhs[:, rc], kcat_b, preferred_element_type=F32)
            s = s * (B_HD ** -0.5) + bias_g[:, rc]
            p = jnp.exp(s - jnp.max(s, axis=-1, keepdims=True)).astype(BF16)
            den = jnp.einsum('bqk,bkd->bqd', p, ones_b, preferred_element_type=F32)
            chunks.append(jnp.einsum('bqk,bkd->bqd', p, vcat_b, preferred_element_type=F32) / den)
        res = jnp.concatenate(chunks, axis=1)
        for i in range(B_GROUP):
            h = g * B_GROUP + i
            part = res[:, i * BLOCK:(i + 1) * BLOCK, :].reshape(tq, LANES)
            outs[h] = part if h % 2 == g else pltpu.roll(part, B_HD, 1)
    for m in range(qw // LANES):
        o_ref[:, m * LANES:(m + 1) * LANES] = jnp.where(half == 0, outs[2 * m], outs[2 * m + 1]).astype(o_ref.dtype)


def _swa_prompt(p3, qnw, knw, sinks, bps):
    n, l, _ = p3.shape
    full = lambda shape: pl.BlockSpec(shape, lambda i, j: (0,) * len(shape))
    tq = bps * BLOCK
    return pl.pallas_call(
        functools.partial(_swa_prompt_kernel, bps=bps),
        grid=(n, l // tq),
        in_specs=[pl.BlockSpec((None, tq, B_HEADS * B_HD), lambda i, j: (i, j, BQ_BLK)),
                  pl.BlockSpec((None, tq, BKV_W), lambda i, j: (i, j, BKV_BLK)),
                  pl.BlockSpec((None, BLOCK, BKV_W), lambda i, j: (i, jnp.maximum(j * bps - 1, 0), BKV_BLK)),
                  pl.BlockSpec((None, BLOCK, BKV_W), lambda i, j: (i, 0, BKV_BLK)),
                  full(qnw.shape), full(knw.shape), full(sinks.shape)],
        out_specs=[pl.BlockSpec((None, tq, B_HEADS * B_HD), lambda i, j: (i, j, 0)),
                   pl.BlockSpec((None, tq, BKV_W), lambda i, j: (i, j, 0))],
        out_shape=[jax.ShapeDtypeStruct((n, l, B_HEADS * B_HD), BF16),
                   jax.ShapeDtypeStruct((n, l, BKV_W), F32)],
        compiler_params=_cparams(("parallel", "arbitrary")),
        name="swa_prompt",
    )(p3, p3, p3, p3, qnw, knw, sinks)


def _swa_sample_kernel(q_ref, kvn_ref, win_ref, meta_ref, qnw_ref, knw_ref, sink_ref, *rest):
    o_ref, nwin_ref = rest[-2:]
    q = q_ref[...]
    kvn = kvn_ref[...]
    win = win_ref[...]
    meta = meta_ref[...]
    qnw = qnw_ref[...]
    knw = knw_ref[...]
    t_new = q.shape[1]
    w = win.shape[1]
    nk = N_META + w + t_new
    rows = B_GROUP * t_new
    tq = lax.broadcasted_iota(jnp.int32, (1, rows, nk), 1) % t_new
    c = lax.broadcasted_iota(jnp.int32, (1, rows, nk), 2)
    cw = c - N_META
    win_ok = (cw >= 0) & (cw < w) & (cw - w > tq - WINDOW) & (cw + (PAST_LEN - w) >= N_META)
    cn = cw - w
    new_ok = (cn >= 0) & (cn <= tq) & (cn > tq - WINDOW)
    mask = (c < N_META) | win_ok | new_ok
    hrow = lax.broadcasted_iota(jnp.int32, (1, rows, 1), 1) // t_new
    kn_parts = []
    for g in range(B_KV_HEADS):
        ks = slice(g * B_HD, (g + 1) * B_HD)
        vs = slice((B_KV_HEADS + g) * B_HD, (B_KV_HEADS + g + 1) * B_HD)
        k_new = _rms(kvn[:, :, ks], knw)
        kn_parts.append(k_new)
        kk = jnp.concatenate([meta[:, :, ks], win[:, :, ks], k_new], axis=1)
        vv = jnp.concatenate([meta[:, :, vs], win[:, :, vs], kvn[:, :, vs]], axis=1)
        qq = jnp.concatenate(
            [_rms(q[:, :, (g * B_GROUP + i) * B_HD:(g * B_GROUP + i + 1) * B_HD], qnw) for i in range(B_GROUP)],
            axis=1)
        sink = jnp.zeros((1, rows, 1), F32)
        for i in range(B_GROUP):
            h = g * B_GROUP + i
            sink = jnp.where(hrow == i, sink_ref[0:1, h:h + 1], sink)
        s = jnp.einsum('bqd,bkd->bqk', qq, kk, preferred_element_type=F32) * (B_HD ** -0.5)
        o = _softmax_sink_pv(s, mask, sink, vv, (((2,), (1,)), ((0,), (0,))))
        for i in range(B_GROUP):
            h = g * B_GROUP + i
            o_ref[:, :, h * B_HD:(h + 1) * B_HD] = o[:, i * t_new:(i + 1) * t_new, :]
    nwin_ref[:, 0:w - t_new, :] = win[:, t_new:w, :]
    nwin_ref[:, w - t_new:w, :] = jnp.concatenate(kn_parts + [kvn[:, :, B_KV_HEADS * B_HD:]], axis=-1)


def _swa_sample(p3, win_all, meta_all, layer, win_prev, qnw, knw, sinks, nb):
    n, t_new, _ = p3.shape
    w = win_all.shape[2]
    full = lambda shape: pl.BlockSpec(shape, lambda i: (0,) * len(shape))
    lblk = lambda rows: pl.BlockSpec((None, nb, rows, BKV_W), lambda i: (layer, i, 0, 0))
    in_specs = [pl.BlockSpec((nb, t_new, B_HEADS * B_HD), lambda i: (i, 0, BQ_BLK)),
                pl.BlockSpec((nb, t_new, BKV_W), lambda i: (i, 0, BKV_BLK)),
                lblk(w), lblk(N_META),
                full(qnw.shape), full(knw.shape), full(sinks.shape)]
    args = [p3, p3, win_all, meta_all, qnw, knw, sinks]
    aliases = {}
    if win_prev is not None:
        in_specs.append(pl.BlockSpec(memory_space=pl.ANY))
        args.append(win_prev)
        aliases = {len(args) - 1: 1}
    return pl.pallas_call(
        _swa_sample_kernel,
        grid=(n // nb,),
        in_specs=in_specs,
        out_specs=[pl.BlockSpec((nb, t_new, B_HEADS * B_HD), lambda i: (i, 0, 0)), lblk(w)],
        out_shape=[jax.ShapeDtypeStruct((n, t_new, B_HEADS * B_HD), F32),
                   jax.ShapeDtypeStruct(win_all.shape, F32)],
        input_output_aliases=aliases,
        compiler_params=_cparams(("parallel",)),
        name="swa_sample",
    )(*args)


GDN_ROWS = 64
GDN_R = A_HEADS * GDN_ROWS
GDN_PAIRS = A_HEADS // 2
PAIR_W = 2 * A_DK
GDN_BATCH = 6


def _mm(a, b):
    return jnp.dot(a.astype(BF16), b.astype(BF16), preferred_element_type=F32)


def _bmm(a, b):
    return jnp.einsum('bij,bjk->bik', a.astype(BF16), b.astype(BF16), preferred_element_type=F32)


def _block_diag2(x):
    z = jnp.zeros(x.shape[:-1] + (A_DK,), x.dtype)
    return jnp.concatenate([jnp.concatenate([x[..., :A_DK], z], axis=-1),
                            jnp.concatenate([z, x[..., A_DK:]], axis=-1)], axis=-2)


def _gdn_prep_kernel(x_ref, halo_ref, ab_ref, cw_ref, alog_ref, dtb_ref, u_ref, wq_ref, qkk_ref, gl_ref,
                     *, sb, cpb, bb, n_invalid, seq_halo):
    t = pl.program_id(1)
    ns = x_ref.shape[0]
    rows = x_ref.shape[1] // cpb
    c = GDN_ROWS
    r = GDN_R
    halo = halo_ref[...]
    if seq_halo:
        hrow = lax.broadcasted_iota(jnp.int32, (1, GDN_HALO, 1), 1) + (t * cpb * rows - GDN_HALO)
        halo = jnp.where(hrow >= n_invalid, halo, 0.0)
    cw = cw_ref[...]
    alog = alog_ref[...]
    dtb = dtb_ref[...]
    ri = lax.broadcasted_iota(jnp.int32, (r, r), 0)
    ci = lax.broadcasted_iota(jnp.int32, (r, r), 1)
    same = (ri // sb) == (ci // sb)
    incl = same & (ri >= ci)
    strict = same & (ri > ci)
    eye = (ri == ci).astype(F32)
    rc_ = lax.broadcasted_iota(jnp.int32, (c, c), 0)
    cc_ = lax.broadcasted_iota(jnp.int32, (c, c), 1)
    same_c = (rc_ // sb) == (cc_ // sb)
    lmat = jnp.concatenate([(same_c & (rc_ >= cc_)).astype(F32), same_c.astype(F32)], axis=0)
    sel = (lax.broadcasted_iota(jnp.int32, (SUBLANES, 2 * c), 1)
           == c + (lax.broadcasted_iota(jnp.int32, (SUBLANES, 2 * c), 0) * SUBLANES // sb) * sb).astype(F32)
    off = GDN_HALO - (A_CONV - 1)
    assert ns == 1 or cpb == 1
    hi = lambda m, v: jnp.einsum('bij,bjk->bik', jnp.broadcast_to(m, (bb,) + m.shape), v,
                                 precision=HIGHEST, preferred_element_type=F32)
    for c0 in range(0, cpb, bb):
        nr = bb * rows
        x = x_ref[:, c0 * rows:c0 * rows + nr, :]
        ab = ab_ref[:, c0 * rows:c0 * rows + nr, :].reshape(bb, c, LANES)
        if n_invalid:
            row0 = (t * cpb + c0) * rows
            xrow = lax.broadcasted_iota(jnp.int32, (1, nr, 1), 1) + row0
            x = jnp.where(xrow >= n_invalid, x, 0.0)
        ext = jnp.concatenate([halo, x], axis=1)
        acc = cw[0:1, :] * ext[:, off:off + nr, :]
        for k in range(1, A_CONV):
            acc = acc + cw[k:k + 1, :] * ext[:, off + k:off + k + nr, :]
        y = _silu(acc).reshape(bb, c, QKV_W)
        halo = x[:, nr - GDN_HALO:nr, :]

        sp = jnp.maximum(ab + dtb, 0.0) + jnp.log1p(jnp.exp(-jnp.abs(ab + dtb)))
        gfull = -jnp.exp(alog) * sp
        bfull = jax.nn.sigmoid(ab)
        if n_invalid:
            valid = (lax.broadcasted_iota(jnp.int32, (bb, c, 1), 0) * c
                     + lax.broadcasted_iota(jnp.int32, (bb, c, 1), 1) + row0) >= n_invalid
            gfull = jnp.where(valid, gfull, 0.0)
            bfull = jnp.where(valid, bfull, 0.0)
        bcol = jnp.concatenate([bfull[:, :, A_HEADS + h:A_HEADS + h + 1] for h in range(A_HEADS)], axis=1)
        gg = hi(lmat, gfull)
        ggt = jnp.stack([gg[i].T for i in range(bb)])
        gc = jnp.concatenate([gg[:, :c, h:h + 1] for h in range(A_HEADS)], axis=1)
        grow = jnp.concatenate([ggt[:, h:h + 1, :c] for h in range(A_HEADS)], axis=2)
        drow = jnp.concatenate([ggt[:, h:h + 1, c:] for h in range(A_HEADS)], axis=2) - grow
        g8 = jnp.exp(hi(sel, gg))
        decay = jnp.where(incl, jnp.exp(jnp.where(incl, gc - grow, 0.0)), 0.0)

        qs, ks, vs = [], [], []
        for h in range(A_HEADS):
            q = y[:, :, h * A_DK:(h + 1) * A_DK]
            k = y[:, :, A_KW + h * A_DK:A_KW + (h + 1) * A_DK]
            qs.append(q * lax.rsqrt(jnp.sum(q * q, axis=-1, keepdims=True) + EPS) * (A_DK ** -0.5))
            ks.append(k * lax.rsqrt(jnp.sum(k * k, axis=-1, keepdims=True) + EPS))
            vs.append(y[:, :, 2 * A_KW + h * A_DK:2 * A_KW + (h + 1) * A_DK])
        qr = jnp.concatenate(qs, axis=1)
        kr = jnp.concatenate(ks, axis=1)
        vr = jnp.concatenate(vs, axis=1)
        krt = jnp.stack([kr[i].T for i in range(bb)])
        kq = _bmm(jnp.concatenate([kr, qr], axis=1), krt)
        a = jnp.where(strict, kq[:, :r] * decay * bcol, 0.0)
        qk = kq[:, r:] * decay
        tinv = eye - a
        ak = a
        kpow = 2
        while kpow < sb:
            ak = _bmm(ak, ak)
            tinv = tinv + _bmm(tinv, ak)
            kpow *= 2
        eg = jnp.exp(gc)
        uw = _bmm(tinv, jnp.concatenate([vr * bcol, kr * (bcol * eg)], axis=2))
        ur = uw[:, :, :A_DK]
        wr = uw[:, :, A_DK:]
        qd = qr * eg
        kdt = krt * jnp.exp(drow)
        cs = slice(c0, c0 + bb)
        for p in range(GDN_PAIRS):
            r0 = slice(2 * p * c, (2 * p + 1) * c)
            r1 = slice((2 * p + 1) * c, (2 * p + 2) * c)
            pl_ = slice(2 * p * c, (2 * p + 2) * c)
            u_ref[cs, p] = jnp.concatenate([ur[:, r0], ur[:, r1]], axis=2)
            wq = jnp.concatenate([jnp.concatenate([wr[:, r0], wr[:, r1]], axis=2),
                                  jnp.concatenate([qd[:, r0], qd[:, r1]], axis=2)], axis=1)
            wq_ref[cs, p] = wq.astype(wq_ref.dtype)
            qkk = jnp.concatenate([qk[:, r0, pl_] + qk[:, r1, pl_], kdt[:, :, pl_]], axis=1)
            qkk_ref[cs, p] = qkk.astype(qkk_ref.dtype)
            gl_ref[cs, p] = jnp.concatenate(
                [jnp.broadcast_to(g8[:, :, 2 * p:2 * p + 1], (bb, SUBLANES, A_DK)),
                 jnp.broadcast_to(g8[:, :, 2 * p + 1:2 * p + 2], (bb, SUBLANES, A_DK))], axis=2)


def _gdn_prep(x_arr, halo_arr, halo_spec, cw, alog, dtb, ns, rows, cpb, bb, sb, n_invalid, op_dtype):
    n, l, _ = x_arr.shape
    grid = (n // ns, l // (cpb * rows))
    units = grid[0] * grid[1] * cpb
    kern = functools.partial(_gdn_prep_kernel, sb=sb, cpb=cpb, bb=bb, n_invalid=n_invalid,
                             seq_halo=halo_arr is x_arr)
    full = lambda shape: pl.BlockSpec(shape, lambda i, t: (0,) * len(shape))
    steps = grid[1]
    out_map = lambda i, t: (i * steps + t, 0, 0, 0)
    c = GDN_ROWS
    return pl.pallas_call(
        kern,
        grid=grid,
        in_specs=[pl.BlockSpec((ns, cpb * rows, QKV_W), lambda i, t: (i, t, 0)),
                  halo_spec,
                  pl.BlockSpec((ns, cpb * rows, LANES), lambda i, t: (i, t, AB_BLK)),
                  full(cw.shape), full(alog.shape), full(dtb.shape)],
        out_specs=[pl.BlockSpec((cpb, GDN_PAIRS, c, PAIR_W), out_map),
                   pl.BlockSpec((cpb, GDN_PAIRS, 2 * c, PAIR_W), out_map),
                   pl.BlockSpec((cpb, GDN_PAIRS, c + A_DK, 2 * c), out_map),
                   pl.BlockSpec((cpb, GDN_PAIRS, SUBLANES, PAIR_W), out_map)],
        out_shape=[jax.ShapeDtypeStruct((units, GDN_PAIRS, c, PAIR_W), F32),
                   jax.ShapeDtypeStruct((units, GDN_PAIRS, 2 * c, PAIR_W), op_dtype),
                   jax.ShapeDtypeStruct((units, GDN_PAIRS, c + A_DK, 2 * c), op_dtype),
                   jax.ShapeDtypeStruct((units, GDN_PAIRS, SUBLANES, PAIR_W), F32)],
        compiler_params=_cparams(("parallel", "arbitrary")),
        name="gdn_prep",
    )(x_arr, halo_arr, x_arr, cw, alog, dtb)


def _gdn_out(o_pair, z_ref, o_ref, nw, idx, p):
    for i in range(2):
        h = 2 * p + i
        hs = slice(h * A_DK, (h + 1) * A_DK)
        o = _rms(o_pair[..., i * A_DK:(i + 1) * A_DK], nw) * _silu(z_ref[idx + (hs,)])
        o_ref[idx + (hs,)] = o.astype(o_ref.dtype)


def _gdn_scan_kernel(u_ref, wq_ref, qkk_ref, gl_ref, z_ref, nw_ref, o_ref, sout_ref, s_ref):
    t = pl.program_id(0)
    n_seq = u_ref.shape[0]
    c = GDN_ROWS
    nw = nw_ref[...]

    @pl.when(t == 0)
    def _():
        s_ref[...] = jnp.zeros(s_ref.shape, F32)

    nb = n_seq * GDN_PAIRS
    flat = lambda ref: ref[:, 0].reshape((nb,) + ref.shape[3:])
    s = s_ref[...].reshape(nb, A_DK, PAIR_W)
    r1 = jnp.einsum('bmk,bkn->bmn', flat(wq_ref), _block_diag2(s.astype(BF16)), preferred_element_type=F32)
    vnew = flat(u_ref) - r1[:, :c]
    r2 = jnp.einsum('bmk,bkn->bmn', flat(qkk_ref), _block_diag2(vnew.astype(BF16)), preferred_element_type=F32)
    s_ref[...] = (s * flat(gl_ref)[:, 0:1, :] + r2[:, c:]).reshape(s_ref.shape)
    o = (r1[:, c:] + r2[:, :c]).reshape(n_seq, GDN_PAIRS, c, PAIR_W)
    for p in range(GDN_PAIRS):
        _gdn_out(o[:, p], z_ref, o_ref, nw, (slice(None), slice(None)), p)

    @pl.when(t == pl.num_programs(0) - 1)
    def _():
        for n in range(n_seq):
            for p in range(GDN_PAIRS):
                s = s_ref[n, p]
                sout_ref[n, 2 * p] = s[:, :A_DK]
                sout_ref[n, 2 * p + 1] = s[:, A_DK:]


def _gdn_scan(u, wq, qkk, gl, p3, nw):
    n, l, _ = p3.shape
    sshape = (n, A_HEADS, A_DK, A_DK)
    c = GDN_ROWS
    steps = l // c
    v5 = lambda a: a.reshape((n, steps) + a.shape[1:])
    u, wq, qkk, gl = v5(u), v5(wq), v5(qkk), v5(gl)
    unit = lambda a: pl.BlockSpec((n, 1) + a.shape[2:], lambda t: (0, t, 0, 0, 0))
    vw = A_HEADS * A_DK
    return pl.pallas_call(
        _gdn_scan_kernel,
        grid=(steps,),
        in_specs=[unit(u), unit(wq), unit(qkk), unit(gl),
                  pl.BlockSpec((n, c, vw), lambda t: (0, t, Z_BLK)),
                  pl.BlockSpec(nw.shape, lambda t: (0, 0))],
        out_specs=[pl.BlockSpec((n, c, vw), lambda t: (0, t, 0)),
                   pl.BlockSpec(sshape, lambda t: (0, 0, 0, 0))],
        out_shape=[jax.ShapeDtypeStruct((n, l, vw), BF16),
                   jax.ShapeDtypeStruct(sshape, F32)],
        scratch_shapes=[pltpu.VMEM((n, GDN_PAIRS, A_DK, PAIR_W), F32)],
        compiler_params=_cparams(("arbitrary",)),
        name="gdn_scan",
    )(u, wq, qkk, gl, p3, nw)


def _gdn_step_kernel(u_ref, wq_ref, qkk_ref, gl_ref, z_ref, s0_ref, nw_ref, *rest):
    o_ref, sout_ref = rest[-2:]
    ns, rows, _ = z_ref.shape
    c = GDN_ROWS
    nw = nw_ref[...]
    lane_seq = (lax.broadcasted_iota(jnp.int32, (A_DK, 2 * c), 1) % c) // rows
    for p in range(GDN_PAIRS):
        wq = wq_ref[0, p]
        qkk = qkk_ref[0, p]
        gl = gl_ref[0, p]
        ws, qs, states = [], [], []
        for s_i in range(ns):
            sl = slice(s_i * rows, (s_i + 1) * rows)
            s = jnp.concatenate([s0_ref[s_i, 2 * p], s0_ref[s_i, 2 * p + 1]], axis=1)
            states.append(s)
            lhs = jnp.concatenate([wq[sl], wq[c + s_i * rows:c + (s_i + 1) * rows]], axis=0)
            r1 = _mm(lhs, _block_diag2(s.astype(BF16)))
            ws.append(r1[:rows])
            qs.append(r1[rows:])
        vnew = u_ref[0, p] - jnp.concatenate(ws, axis=0)
        bdv = _block_diag2(vnew.astype(BF16))
        o = jnp.concatenate(qs, axis=0) + _mm(qkk[:c], bdv)
        _gdn_out(o.reshape(ns, rows, PAIR_W), z_ref, o_ref, nw, (slice(None), slice(None)), p)
        kdt = qkk[c:]
        lhs = jnp.concatenate([jnp.where(lane_seq == s_i, kdt, 0.0) for s_i in range(ns)], axis=0)
        upd = _mm(lhs, bdv)
        for s_i in range(ns):
            s_new = states[s_i] * gl[s_i:s_i + 1, :] + upd[s_i * A_DK:(s_i + 1) * A_DK]
            sout_ref[s_i, 2 * p] = s_new[:, :A_DK]
            sout_ref[s_i, 2 * p + 1] = s_new[:, A_DK:]


def _gdn_step(u, wq, qkk, gl, p3, s_all, layer, s_prev, nw, ns):
    n, rows, _ = p3.shape
    unit = lambda a: pl.BlockSpec((1,) + a.shape[1:], lambda i: (i, 0, 0, 0))
    vw = A_HEADS * A_DK
    sblk = pl.BlockSpec((None, ns) + s_all.shape[2:], lambda i: (layer, i, 0, 0, 0))
    in_specs = [unit(u), unit(wq), unit(qkk), unit(gl),
                pl.BlockSpec((ns, rows, vw), lambda i: (i, 0, Z_BLK)),
                sblk,
                pl.BlockSpec(nw.shape, lambda i: (0, 0))]
    args = [u, wq, qkk, gl, p3, s_all, nw]
    aliases = {}
    if s_prev is not None:
        in_specs.append(pl.BlockSpec(memory_space=pl.ANY))
        args.append(s_prev)
        aliases = {len(args) - 1: 1}
    return pl.pallas_call(
        _gdn_step_kernel,
        grid=(n // ns,),
        in_specs=in_specs,
        out_specs=[pl.BlockSpec((ns, rows, vw), lambda i: (i, 0, 0)), sblk],
        out_shape=[jax.ShapeDtypeStruct((n, rows, vw), F32),
                   jax.ShapeDtypeStruct(s_all.shape, F32)],
        input_output_aliases=aliases,
        compiler_params=_cparams(("parallel",)),
        name="gdn_step",
    )(*args)


def _gdn(p3, state, hist, cw, alog, dtb, nw, cpb, n_invalid):
    n, l, _ = p3.shape
    c = GDN_ROWS
    if hist is None:
        steps8 = cpb * c // GDN_HALO
        halo_spec = pl.BlockSpec((1, GDN_HALO, QKV_W), lambda i, t: (i, jnp.maximum(t * steps8 - 1, 0), 0))
        u, wq, qkk, gl = _gdn_prep(p3, p3, halo_spec, cw, alog, dtb, 1, c, cpb, min(cpb, GDN_BATCH), c, n_invalid, BF16)
        return _gdn_scan(u, wq, qkk, gl, p3, nw)
    ns = c // l
    halo_spec = pl.BlockSpec((ns, GDN_HALO, QKV_W), lambda i, t: (i, 0, 0))
    u, wq, qkk, gl = _gdn_prep(p3, hist, halo_spec, cw, alog, dtb, ns, l, 1, 1, l, n_invalid, F32)
    s_all, layer, s_prev = state
    return _gdn_step(u, wq, qkk, gl, p3, s_all, layer, s_prev, nw, ns)


def _merge_kernel(oa_ref, ob_ref, oc_ref, gl_ref, x_ref, wb_ref, wo_ref, out_ref):
    d = x_ref.shape[1]
    merged = None
    for r, o_ref in enumerate((oa_ref, ob_ref, oc_ref)):
        pb = jnp.dot(o_ref[...].astype(BF16), wb_ref[r], preferred_element_type=F32)
        term = jax.nn.sigmoid(gl_ref[:, r * d:(r + 1) * d]) * pb
        merged = term if merged is None else merged + term
    out_ref[...] = x_ref[...] + jnp.dot(merged.astype(BF16), wo_ref[...], preferred_element_type=F32)


def _merge(oa, ob, oc, p2, x2, wb, wo, tm):
    t, d = x2.shape
    bw = oa.shape[1]
    row = lambda w, blk: pl.BlockSpec((tm, w), lambda i: (i, blk))
    return pl.pallas_call(
        _merge_kernel,
        grid=(t // tm,),
        in_specs=[row(bw, 0), row(bw, 0), row(bw, 0), row(3 * d, GL_BLK), row(d, 0),
                  pl.BlockSpec(wb.shape, lambda i: (0, 0, 0)),
                  pl.BlockSpec(wo.shape, lambda i: (0, 0))],
        out_specs=row(d, 0),
        out_shape=jax.ShapeDtypeStruct((t, d), F32),
        compiler_params=_cparams(("parallel",)),
        name="merge",
    )(oa, ob, oc, p2, x2, wb, wo)


def _route(logits):
    lane = lax.broadcasted_iota(jnp.int32, logits.shape, 1)
    is_g = (lane >= N_EXPERTS) & (lane < N_EXPERTS + N_GROUPS)
    gl = jnp.where(is_g, logits, NEG)
    gmax = jnp.max(gl, axis=-1, keepdims=True)
    gidx = jnp.min(jnp.where(gl == gmax, lane - N_EXPERTS, LANES), axis=-1, keepdims=True)
    gw = 1.0 / jnp.sum(jnp.where(is_g, jnp.exp(gl - gmax), 0.0), axis=-1, keepdims=True)
    in_grp = (lane < N_EXPERTS) & ((lane // EXP_PER_GROUP) == gidx)
    el = jnp.where(in_grp, logits, NEG)
    v1 = jnp.max(el, axis=-1, keepdims=True)
    i1 = jnp.min(jnp.where(el == v1, lane, LANES), axis=-1, keepdims=True)
    el2 = jnp.where(lane == i1, NEG, el)
    v2 = jnp.max(el2, axis=-1, keepdims=True)
    i2 = jnp.min(jnp.where(el2 == v2, lane, LANES), axis=-1, keepdims=True)
    e21 = jnp.exp(v2 - v1)
    w1 = gw / (1.0 + e21)
    w2 = gw * e21 / (1.0 + e21)
    return jnp.where(lane == i1, w1, 0.0) + jnp.where(lane == i2, w2, 0.0)


def _moe_kernel(x_ref, nw_ref, wr_ref, br_ref, wg_ref, wu_ref, wd_ref, out_ref, h_ref, cmb_ref):
    g = pl.program_id(1)

    @pl.when(g == 0)
    def _():
        x = x_ref[...]
        h = _rms(x, nw_ref[...])
        h_hi = h.astype(BF16)
        h_ref[...] = h_hi
        h_lo = (h - h_hi.astype(F32)).astype(BF16)
        l2 = jnp.dot(h_hi, wr_ref[...], preferred_element_type=F32)
        logits = (l2[:, :LANES] + l2[:, LANES:]
                  + jnp.dot(h_lo, wr_ref[:, :LANES], preferred_element_type=F32) + br_ref[...])
        cmb_ref[...] = _route(logits)
        out_ref[...] = x

    h = h_ref[...]
    cmb = cmb_ref[...]
    lane = lax.broadcasted_iota(jnp.int32, cmb.shape, 1)
    parts = []
    for e in range(EXP_PER_GROUP):
        ce = jnp.sum(jnp.where(lane == g * EXP_PER_GROUP + e, cmb, 0.0), axis=-1, keepdims=True)
        hid = (_silu(jnp.dot(h, wg_ref[e], preferred_element_type=F32))
               * jnp.dot(h, wu_ref[e], preferred_element_type=F32))
        parts.append((hid * ce).astype(BF16))
    wd = wd_ref[...].reshape(-1, wd_ref.shape[-1])
    out_ref[...] += jnp.dot(jnp.concatenate(parts, axis=1), wd, preferred_element_type=F32)


def _moe(x2, nw, wr, br, wg, wu, wd, tm):
    t, d = x2.shape
    ne, _, de = wg.shape
    return pl.pallas_call(
        _moe_kernel,
        grid=(t // tm, ne // EXP_PER_GROUP),
        in_specs=[pl.BlockSpec((tm, d), lambda i, g: (i, 0)),
                  pl.BlockSpec((1, d), lambda i, g: (0, 0)),
                  pl.BlockSpec(wr.shape, lambda i, g: (0, 0)),
                  pl.BlockSpec(br.shape, lambda i, g: (0, 0)),
                  pl.BlockSpec((EXP_PER_GROUP, d, de), lambda i, g: (g, 0, 0)),
                  pl.BlockSpec((EXP_PER_GROUP, d, de), lambda i, g: (g, 0, 0)),
                  pl.BlockSpec((EXP_PER_GROUP, de, d), lambda i, g: (g, 0, 0))],
        out_specs=pl.BlockSpec((tm, d), lambda i, g: (i, 0)),
        out_shape=jax.ShapeDtypeStruct((t, d), F32),
        scratch_shapes=[pltpu.VMEM((tm, d), BF16), pltpu.VMEM((tm, LANES), F32)],
        compiler_params=_cparams(("parallel", "arbitrary")),
        name="moe",
    )(x2, nw, wr, br, wg, wu, wd)


def _pad_lanes(v, width):
    v = v.reshape(1, -1)
    return jnp.pad(v, ((0, 0), (0, width - v.shape[1])))


def _layer_weights(l, norm1_w, norm2_w, w_in, gdn_conv_w, gdn_a_log, gdn_dt_bias, gdn_norm_w,
                   swa_q_norm_w, swa_k_norm_w, swa_sinks, conv_dw_w, conv_dw_b, conv_ln_w, conv_ln_b,
                   w_branch, w_out, router_group_w, router_group_b, router_expert_w, router_expert_b,
                   moe_w_gate, moe_w_up, moe_w_down):
    d = w_in.shape[1]
    wi = w_in[l]
    o_ab = 4 * A_KW
    o_bq = o_ab + 2 * A_HEADS
    o_cu = o_bq + B_HEADS * B_HD + BKV_W
    o_gl = o_cu + 2 * C_CH
    w_perm = jnp.concatenate(
        [wi[:, :o_ab], wi[:, o_cu:o_gl], wi[:, o_gl:], wi[:, o_bq:o_cu], wi[:, o_ab:o_bq],
         jnp.zeros((d, LANES - 2 * A_HEADS), wi.dtype)], axis=1).astype(BF16)
    assert w_perm.shape[1] == PROJ_W
    wr = jnp.concatenate([router_expert_w[l], router_group_w[l],
                          jnp.zeros((d, LANES - N_EXPERTS - N_GROUPS), F32)], axis=1)
    br = _pad_lanes(jnp.concatenate([router_expert_b[l], router_group_b[l]]), LANES)
    wr_hi = wr.astype(BF16)
    wr = jnp.concatenate([wr_hi, (wr - wr_hi.astype(F32)).astype(BF16)], axis=1)
    return dict(
        n1=norm1_w[l].reshape(1, d), n2=norm2_w[l].reshape(1, d), w_in=w_perm,
        gdn_cw=gdn_conv_w[l], alog=_pad_lanes(gdn_a_log[l], LANES), dtb=_pad_lanes(gdn_dt_bias[l], LANES),
        gdn_nw=gdn_norm_w[l].reshape(1, -1),
        qnw=swa_q_norm_w[l].reshape(1, -1), knw=swa_k_norm_w[l].reshape(1, -1), sinks=swa_sinks[l].reshape(1, -1),
        qnw_heads=jnp.tile(swa_q_norm_w[l].reshape(1, -1), (1, B_HEADS)),
        knw_heads=jnp.tile(swa_k_norm_w[l].reshape(1, -1), (1, B_KV_HEADS)),
        dww=conv_dw_w[l], dwb=conv_dw_b[l].reshape(1, -1), lnw=conv_ln_w[l].reshape(1, -1),
        lnb=conv_ln_b[l].reshape(1, -1),
        wb=w_branch[l].astype(BF16), wo=w_out[l].astype(BF16), wr=wr, br=br,
        wg=moe_w_gate[l].astype(BF16), wu=moe_w_up[l].astype(BF16), wd=moe_w_down[l].astype(BF16))


def _tiles(n, l):
    prompt = l > BLOCK
    t = n * l
    if prompt:
        return dict(proj_tm=1536, tok_tm=512, moe_tm=1408, conf_nb=1, conf_tm=384, conf_rc=64,
                    gdn_cpb=6, swa_bps=3, n_invalid=META_PAD, branch_dtype=BF16)
    return dict(proj_tm=t, tok_tm=512, moe_tm=t, conf_nb=16, conf_tm=l, conf_rc=l,
                gdn_cpb=1, n_invalid=0, branch_dtype=F32)


def _layer(x3, lw, gdn_state, gdn_hist, conf_hist, swa_cache):
    n, l, d = x3.shape
    tl = _tiles(n, l)
    x2 = x3.reshape(n * l, d)
    p2 = _inproj(x2, lw['n1'], lw['w_in'], tl['proj_tm'], PROJ_W // 5)
    p3 = p2.reshape(n, l, PROJ_W)
    o_a, s_new = _gdn(p3, gdn_state, gdn_hist, lw['gdn_cw'], lw['alog'], lw['dtb'], lw['gdn_nw'],
                      tl['gdn_cpb'], tl['n_invalid'])
    gh_new = p3[:, l - (A_CONV - 1):, :QKV_W]
    if swa_cache is None:
        o_b, kvn = _swa_prompt(p3, lw['qnw_heads'], lw['knw_heads'], lw['sinks'], tl['swa_bps'])
        kv = lambda sl: kvn[:, sl].reshape(n, -1, 2, B_KV_HEADS, B_HD)
        swa_new = (kv(slice(META_PAD, BLOCK)), kv(slice(l - WINDOW, l)))
    else:
        win_all, meta_all, layer, win_prev = swa_cache
        o_b, swa_new = _swa_sample(p3, win_all, meta_all, layer, win_prev, lw['qnw'], lw['knw'], lw['sinks'], 16)
    o_c, ch_new = _conformer(p3, conf_hist, lw['dww'], lw['dwb'], lw['lnw'], lw['lnb'],
                             tl['conf_nb'], tl['conf_tm'], tl['conf_rc'], tl['n_invalid'], tl['branch_dtype'])
    t = n * l
    x2 = _merge(o_a.reshape(t, -1), o_b.reshape(t, -1), o_c.reshape(t, -1), p2, x2, lw['wb'], lw['wo'],
                tl['tok_tm'])
    x2 = _moe(x2, lw['n2'], lw['wr'], lw['br'], lw['wg'], lw['wu'], lw['wd'], tl['moe_tm'])
    return x2.reshape(n, l, d), s_new, gh_new, swa_new, ch_new


def kernel(x_prompt, x_sample, state_gdn, cache_gdn_conv, cache_swa_kv, cache_meta_kv, cache_conv, meta_tokens, norm1_w, norm2_w, w_in, gdn_conv_w, gdn_a_log, gdn_dt_bias, gdn_norm_w, swa_q_norm_w, swa_k_norm_w, swa_sinks, conv_dw_w, conv_dw_b, conv_ln_w, conv_ln_b, w_branch, w_out, router_group_w, router_group_b, router_expert_w, router_expert_b, moe_w_gate, moe_w_up, moe_w_down):
    dtp = x_prompt.dtype
    n_p, _, d = x_prompt.shape
    n_s = x_sample.shape[0]
    depth = w_in.shape[0]
    xp = jnp.pad(x_prompt, ((0, 0), (BLOCK, 0), (0, 0)))
    xp = lax.dynamic_update_slice(
        xp, jnp.broadcast_to(meta_tokens.astype(dtp)[None], (n_p, N_META, d)), (0, META_PAD, 0))
    xs = x_sample
    state_s = None
    win_s = None
    win_all = cache_swa_kv.reshape(depth, n_s, -1, BKV_W)
    meta_all = cache_meta_kv.reshape(depth, n_s, N_META, BKV_W)
    outs_p, outs_s = [], []
    for l in range(depth):
        lw = _layer_weights(l, norm1_w, norm2_w, w_in, gdn_conv_w, gdn_a_log, gdn_dt_bias, gdn_norm_w,
                            swa_q_norm_w, swa_k_norm_w, swa_sinks, conv_dw_w, conv_dw_b, conv_ln_w, conv_ln_b,
                            w_branch, w_out, router_group_w, router_group_b, router_expert_w, router_expert_b,
                            moe_w_gate, moe_w_up, moe_w_down)
        xp, s_p, gh_p, (mkv_p, wkv_p), ch_p = _layer(
            xp, lw, None, None, jnp.zeros((n_p, C_CONV - 1, C_CH), dtp), None)
        outs_p.append((s_p, gh_p, wkv_p, mkv_p, ch_p))
        gh0 = jnp.pad(cache_gdn_conv[l], ((0, 0), (GDN_HALO - (A_CONV - 1), 0), (0, 0)))
        xs, state_s, gh_s, win_s, ch_s = _layer(xs, lw, (state_gdn, l, state_s), gh0, cache_conv[l],
                                                (win_all, meta_all, l, win_s))
        outs_s.append((gh_s, ch_s))
    stack = lambda outs, i: jnp.stack([o[i] for o in outs])
    return (xp[:, BLOCK:], xs,
            stack(outs_p, 0), stack(outs_p, 1), stack(outs_p, 2), stack(outs_p, 3), stack(outs_p, 4),
            state_s, stack(outs_s, 0), win_s.reshape(cache_swa_kv.shape), stack(outs_s, 1))
```

```python
import functools

import jax
import jax.numpy as jnp
from jax import lax
from jax.experimental import pallas as pl
from jax.experimental.pallas import tpu as pltpu

F32 = jnp.float32
BF16 = jnp.bfloat16
HIGHEST = lax.Precision.HIGHEST
EPS = 1e-6
NEG = -1e30

VMEM_LIMIT_BYTES = 56 * 1024 * 1024
LANES = 128
SUBLANES = 8

PAST_LEN = 16384
N_META = 16
BLOCK = 128
WINDOW = 128
META_PAD = BLOCK - N_META
A_HEADS = 4
A_DK = 128
A_CONV = 4
B_HEADS = 8
B_KV_HEADS = 2
B_GROUP = B_HEADS // B_KV_HEADS
B_HD = 64
C_CH = 512
C_CONV = 31
BRANCH_W = 512
N_GROUPS = 4
EXP_PER_GROUP = 4
N_EXPERTS = N_GROUPS * EXP_PER_GROUP

A_KW = A_HEADS * A_DK
QKV_W = 3 * A_KW
Z_BLK = QKV_W // BRANCH_W
CU_BLK = 2
GL_BLK = 1
BQ_BLK = 12
BKV_W = 2 * B_KV_HEADS * B_HD
BKV_BLK = 26
AB_BLK = 54
PROJ_W = 7040
CONF_HALO = 32
GDN_HALO = SUBLANES


def _rms(x, w):
    return x * lax.rsqrt(jnp.mean(x * x, axis=-1, keepdims=True) + EPS) * w


def _silu(x):
    return x * jax.nn.sigmoid(x)


def _cparams(sem):
    return pltpu.CompilerParams(dimension_semantics=sem, vmem_limit_bytes=VMEM_LIMIT_BYTES)


def _inproj_kernel(x_ref, nw_ref, w_ref, o_ref, h_ref):
    @pl.when(pl.program_id(1) == 0)
    def _():
        h_ref[...] = _rms(x_ref[...], nw_ref[...]).astype(BF16)

    o_ref[...] = jnp.dot(h_ref[...], w_ref[...], preferred_element_type=F32)


def _inproj(x2, nw, w, tm, tn):
    t, d = x2.shape
    wd = w.shape[1]
    return pl.pallas_call(
        _inproj_kernel,
        grid=(t // tm, wd // tn),
        in_specs=[pl.BlockSpec((tm, d), lambda i, j: (i, 0)),
                  pl.BlockSpec((1, d), lambda i, j: (0, 0)),
                  pl.BlockSpec((d, tn), lambda i, j: (0, j))],
        out_specs=pl.BlockSpec((tm, tn), lambda i, j: (i, j)),
        out_shape=jax.ShapeDtypeStruct((t, wd), F32),
        scratch_shapes=[pltpu.VMEM((tm, d), BF16)],
        compiler_params=_cparams(("parallel", "arbitrary")),
        name="inproj",
    )(x2, nw, w)


def _conf_kernel(cu_ref, hist_ref, dww_ref, dwb_ref, lnw_ref, lnb_ref, y_ref, nh_ref, e_ref, c_ref,
                 *, tm, rc, n_invalid):
    t = pl.program_id(1)
    nb = cu_ref.shape[0]
    halo = CONF_HALO

    @pl.when(t == 0)
    def _():
        e_ref[:, 0:SUBLANES, :] = jnp.zeros((nb, SUBLANES, C_CH), F32)
        e_ref[:, halo + tm:halo + tm + SUBLANES, :] = jnp.zeros((nb, SUBLANES, C_CH), F32)
        e_ref[:, halo - (C_CONV - 1):halo, :] = hist_ref[...]

    @pl.when(t > 0)
    def _():
        e_ref[:, 0:halo, :] = e_ref[:, tm:tm + halo, :]

    cu = cu_ref[...]
    u = cu[:, :, :C_CH] * jax.nn.sigmoid(cu[:, :, C_CH:])
    if n_invalid:
        row = lax.broadcasted_iota(jnp.int32, (1, tm, 1), 1) + t * tm
        u = jnp.where(row >= n_invalid, u, 0.0)
    e_ref[:, halo:halo + tm, :] = u
    nh_ref[...] = e_ref[:, halo + tm - (C_CONV - 1):halo + tm, :]

    off = halo - (C_CONV - 1)
    for c in range(C_CH // LANES):
        cs = slice(c * LANES, (c + 1) * LANES)

        def rbody(r, carry, cs=cs):
            r0 = pl.multiple_of(r * rc, rc)
            blk = e_ref[:, pl.ds(r0, rc + halo + SUBLANES), cs]
            acc = jnp.zeros((nb, rc, LANES), F32)
            for s in range(SUBLANES):
                z = None
                for a in range((halo + SUBLANES) // SUBLANES):
                    k = SUBLANES * a + s - off
                    if 0 <= k < C_CONV:
                        term = dww_ref[k:k + 1, cs] * blk[:, SUBLANES * a:SUBLANES * a + rc + SUBLANES, :]
                        z = term if z is None else z + term
                acc = acc + z[:, s:s + rc, :]
            c_ref[:, pl.ds(r0, rc), cs] = acc + dwb_ref[:, cs]
            return carry

        lax.fori_loop(0, tm // rc, rbody, 0)

    y = c_ref[...]
    mu = jnp.mean(y, axis=-1, keepdims=True)
    var = jnp.mean(jnp.square(y - mu), axis=-1, keepdims=True)
    yn = (y - mu) * lax.rsqrt(var + EPS) * lnw_ref[...] + lnb_ref[...]
    y_ref[...] = _silu(yn).astype(y_ref.dtype)


def _conformer(p3, hist, dww, dwb, lnw, lnb, nb, tm, rc, n_invalid, out_dtype):
    n, l, _ = p3.shape
    kern = functools.partial(_conf_kernel, tm=tm, rc=rc, n_invalid=n_invalid)
    full = lambda shape: pl.BlockSpec(shape, lambda i, t: (0,) * len(shape))
    return pl.pallas_call(
        kern,
        grid=(n // nb, l // tm),
        in_specs=[pl.BlockSpec((nb, tm, 2 * C_CH), lambda i, t: (i, t, CU_BLK)),
                  pl.BlockSpec((nb, C_CONV - 1, C_CH), lambda i, t: (i, 0, 0)),
                  full(dww.shape), full(dwb.shape), full(lnw.shape), full(lnb.shape)],
        out_specs=[pl.BlockSpec((nb, tm, C_CH), lambda i, t: (i, t, 0)),
                   pl.BlockSpec((nb, C_CONV - 1, C_CH), lambda i, t: (i, 0, 0))],
        out_shape=[jax.ShapeDtypeStruct((n, l, C_CH), out_dtype),
                   jax.ShapeDtypeStruct((n, C_CONV - 1, C_CH), F32)],
        scratch_shapes=[pltpu.VMEM((nb, CONF_HALO + tm + SUBLANES, C_CH), F32), pltpu.VMEM((nb, tm, C_CH), F32)],
        compiler_params=_cparams(("parallel", "arbitrary")),
        name="conformer",
    )(p3, hist, dww, dwb, lnw, lnb)


SWA_ROWS = 512


def _softmax_sink_pv(s, mask, sink, v, dims):
    s = jnp.where(mask, s, NEG)
    m = jnp.maximum(jnp.max(s, axis=-1, keepdims=True), sink)
    p = jnp.exp(s - m)
    den = jnp.sum(p, axis=-1, keepdims=True) + jnp.exp(sink - m)
    return lax.dot_general(p, v, dims, preferred_element_type=F32) / den


def _swa_prompt_kernel(q_ref, kv_ref, kvp_ref, kvm_ref, qnw_ref, knw_ref, sink_ref, o_ref, kvn_ref, *, bps):
    step = pl.program_id(1)
    qnw = qnw_ref[...]
    knw = knw_ref[...]
    qw = B_HEADS * B_HD
    kw = B_KV_HEADS * B_HD
    nk = N_META + 2 * BLOCK
    nke = nk + SUBLANES
    rows = B_GROUP * BLOCK
    r = lax.broadcasted_iota(jnp.int32, (rows, nke), 0) % BLOCK
    c = lax.broadcasted_iota(jnp.int32, (rows, nke), 1)
    cp = c - N_META
    co = cp - BLOCK
    hrow = lax.broadcasted_iota(jnp.int32, (rows, 1), 0) // BLOCK
    m_meta = c < N_META
    m_own = (co >= 0) & (co <= r) & (c < nk)
    m_prev = (cp >= 0) & (cp < BLOCK) & (cp > r)
    bias0 = jnp.where(m_meta & (META_PAD + c <= r), 0.0, NEG)
    bias1 = jnp.where(m_meta | m_own, 0.0, NEG)
    bias2 = jnp.where(m_meta | m_own | m_prev, 0.0, NEG)
    sinks = []
    for g in range(B_KV_HEADS):
        sink = jnp.zeros((rows, 1), F32)
        for i in range(B_GROUP):
            h = g * B_GROUP + i
            sink = jnp.where(hrow == i, sink_ref[0:1, h:h + 1], sink)
        sinks.append(sink)
    hb_q = (lax.broadcasted_iota(jnp.int32, (qw, qw), 0) // B_HD
            == lax.broadcasted_iota(jnp.int32, (qw, qw), 1) // B_HD).astype(BF16)
    hb_k = hb_q[:kw, :kw]
    half = lax.broadcasted_iota(jnp.int32, (1, LANES), 1) // B_HD

    def head_rms(x, hb, w):
        sq = x * x
        hi = sq.astype(BF16)
        lo = (sq - hi.astype(F32)).astype(BF16)
        ss = jnp.dot(hi, hb, preferred_element_type=F32) + jnp.dot(lo, hb, preferred_element_type=F32)
        return x * lax.rsqrt(ss * (1.0 / B_HD) + EPS) * w

    kvm = kvm_ref[META_PAD:BLOCK, :]
    km = head_rms(kvm[:, :kw], hb_k, knw)
    kv_halo = kvp_ref[...]
    q = q_ref[...]
    kv = kv_ref[...]
    k_own = head_rms(kv[:, :kw], hb_k, knw)
    kvn_ref[...] = jnp.concatenate([k_own, kv[:, kw:]], axis=-1)
    qn = head_rms(q, hb_q, qnw)
    tq = bps * BLOCK
    blocks = lambda x: x.reshape(bps, BLOCK, x.shape[-1])
    k_prev = jnp.concatenate([head_rms(kv_halo[:, :kw], hb_k, knw), k_own[:tq - BLOCK]], axis=0)
    v_prev = jnp.concatenate([kv_halo[:, kw:], kv[:tq - BLOCK, kw:]], axis=0)
    rep = lambda x: jnp.broadcast_to(x[None], (bps,) + x.shape)
    zpad = jnp.zeros((bps, SUBLANES, kw), F32)
    kcat_b = jnp.concatenate([rep(km), blocks(k_prev), blocks(k_own), zpad], axis=1).astype(BF16)
    vcat_b = jnp.concatenate([rep(kvm[:, kw:]), blocks(v_prev), blocks(kv[:, kw:]), zpad], axis=1).astype(BF16)
    ones_b = jnp.ones((bps, nke, kw), BF16)
    bias = jnp.stack([jnp.where(step * bps + b >= 2, bias2, jnp.where(step * bps + b >= 1, bias1, bias0))
                      for b in range(bps)])
    tiles = [qn[:, m * LANES:(m + 1) * LANES] for m in range(qw // LANES)]
    swapped = [pltpu.roll(tl_, B_HD, 1) for tl_ in tiles]
    outs = [None] * B_HEADS
    for g in range(B_KV_HEADS):
        in_g = half == g
        parts = []
        for i in range(B_GROUP):
            h = g * B_GROUP + i
            src = tiles[h // 2] if h % 2 == g else swapped[h // 2]
            parts.append(blocks(jnp.where(in_g, src, 0.0)))
        lhs = jnp.concatenate(parts, axis=1).astype(BF16)
        bias_g = jnp.where(c == nk, sinks[g], bias)
        chunks = []
        for r0 in range(0, rows, SWA_ROWS):
            rc = slice(r0, r0 + SWA_ROWS)
            s = jnp.einsum('bqd,bkd->bqk', lhs[:, rc], kcat_b, preferred_element_type=F32)
            s = s * (B_HD ** -0.5) + bias_g[:, rc]
            p = jnp.exp(s - jnp.max(s, axis=-1, keepdims=True)).astype(BF16)
            den = jnp.einsum('bqk,bkd->bqd', p, ones_b, preferred_element_type=F32)
            chunks.append(jnp.einsum('bqk,bkd->bqd', p, vcat_b, preferred_element_type=F32) / den)
        res = jnp.concatenate(chunks, axis=1)
        for i in range(B_GROUP):
            h = g * B_GROUP + i
            part = res[:, i * BLOCK:(i + 1) * BLOCK, :].reshape(tq, LANES)
            outs[h] = part if h % 2 == g else pltpu.roll(part, B_HD, 1)
    for m in range(qw // LANES):
        o_ref[:, m * LANES:(m + 1) * LANES] = jnp.where(half == 0, outs[2 * m], outs[2 * m + 1]).astype(o_ref.dtype)


def _swa_prompt(p3, qnw, knw, sinks, bps):
    n, l, _ = p3.shape
    full = lambda shape: pl.BlockSpec(shape, lambda i, j: (0,) * len(shape))
    tq = bps * BLOCK
    return pl.pallas_call(
        functools.partial(_swa_prompt_kernel, bps=bps),
        grid=(n, l // tq),
        in_specs=[pl.BlockSpec((None, tq, B_HEADS * B_HD), lambda i, j: (i, j, BQ_BLK)),
                  pl.BlockSpec((None, tq, BKV_W), lambda i, j: (i, j, BKV_BLK)),
                  pl.BlockSpec((None, BLOCK, BKV_W), lambda i, j: (i, jnp.maximum(j * bps - 1, 0), BKV_BLK)),
                  pl.BlockSpec((None, BLOCK, BKV_W), lambda i, j: (i, 0, BKV_BLK)),
                  full(qnw.shape), full(knw.shape), full(sinks.shape)],
        out_specs=[pl.BlockSpec((None, tq, B_HEADS * B_HD), lambda i, j: (i, j, 0)),
                   pl.BlockSpec((None, tq, BKV_W), lambda i, j: (i, j, 0))],
        out_shape=[jax.ShapeDtypeStruct((n, l, B_HEADS * B_HD), BF16),
                   jax.ShapeDtypeStruct((n, l, BKV_W), F32)],
        compiler_params=_cparams(("parallel", "arbitrary")),
        name="swa_prompt",
    )(p3, p3, p3, p3, qnw, knw, sinks)


def _swa_sample_kernel(q_ref, kvn_ref, win_ref, meta_ref, qnw_ref, knw_ref, sink_ref, *rest):
    o_ref, nwin_ref = rest[-2:]
    q = q_ref[...]
    kvn = kvn_ref[...]
    win = win_ref[...]
    meta = meta_ref[...]
    qnw = qnw_ref[...]
    knw = knw_ref[...]
    t_new = q.shape[1]
    w = win.shape[1]
    nk = N_META + w + t_new
    rows = B_GROUP * t_new
    tq = lax.broadcasted_iota(jnp.int32, (1, rows, nk), 1) % t_new
    c = lax.broadcasted_iota(jnp.int32, (1, rows, nk), 2)
    cw = c - N_META
    win_ok = (cw >= 0) & (cw < w) & (cw - w > tq - WINDOW) & (cw + (PAST_LEN - w) >= N_META)
    cn = cw - w
    new_ok = (cn >= 0) & (cn <= tq) & (cn > tq - WINDOW)
    mask = (c < N_META) | win_ok | new_ok
    hrow = lax.broadcasted_iota(jnp.int32, (1, rows, 1), 1) // t_new
    kn_parts = []
    for g in range(B_KV_HEADS):
        ks = slice(g * B_HD, (g + 1) * B_HD)
        vs = slice((B_KV_HEADS + g) * B_HD, (B_KV_HEADS + g + 1) * B_HD)
        k_new = _rms(kvn[:, :, ks], knw)
        kn_parts.append(k_new)
        kk = jnp.concatenate([meta[:, :, ks], win[:, :, ks], k_new], axis=1)
        vv = jnp.concatenate([meta[:, :, vs], win[:, :, vs], kvn[:, :, vs]], axis=1)
        qq = jnp.concatenate(
            [_rms(q[:, :, (g * B_GROUP + i) * B_HD:(g * B_GROUP + i + 1) * B_HD], qnw) for i in range(B_GROUP)],
            axis=1)
        sink = jnp.zeros((1, rows, 1), F32)
        for i in range(B_GROUP):
            h = g * B_GROUP + i
            sink = jnp.where(hrow == i, sink_ref[0:1, h:h + 1], sink)
        s = jnp.einsum('bqd,bkd->bqk', qq, kk, preferred_element_type=F32) * (B_HD ** -0.5)
        o = _softmax_sink_pv(s, mask, sink, vv, (((2,), (1,)), ((0,), (0,))))
        for i in range(B_GROUP):
            h = g * B_GROUP + i
            o_ref[:, :, h * B_HD:(h + 1) * B_HD] = o[:, i * t_new:(i + 1) * t_new, :]
    nwin_ref[:, 0:w - t_new, :] = win[:, t_new:w, :]
    nwin_ref[:, w - t_new:w, :] = jnp.concatenate(kn_parts + [kvn[:, :, B_KV_HEADS * B_HD:]], axis=-1)


def _swa_sample(p3, win_all, meta_all, layer, win_prev, qnw, knw, sinks, nb):
    n, t_new, _ = p3.shape
    w = win_all.shape[2]
    full = lambda shape: pl.BlockSpec(shape, lambda i: (0,) * len(shape))
    lblk = lambda rows: pl.BlockSpec((None, nb, rows, BKV_W), lambda i: (layer, i, 0, 0))
    in_specs = [pl.BlockSpec((nb, t_new, B_HEADS * B_HD), lambda i: (i, 0, BQ_BLK)),
                pl.BlockSpec((nb, t_new, BKV_W), lambda i: (i, 0, BKV_BLK)),
                lblk(w), lblk(N_META),
                full(qnw.shape), full(knw.shape), full(sinks.shape)]
    args = [p3, p3, win_all, meta_all, qnw, knw, sinks]
    aliases = {}
    if win_prev is not None:
        in_specs.append(pl.BlockSpec(memory_space=pl.ANY))
        args.append(win_prev)
        aliases = {len(args) - 1: 1}
    return pl.pallas_call(
        _swa_sample_kernel,
        grid=(n // nb,),
        in_specs=in_specs,
        out_specs=[pl.BlockSpec((nb, t_new, B_HEADS * B_HD), lambda i: (i, 0, 0)), lblk(w)],
        out_shape=[jax.ShapeDtypeStruct((n, t_new, B_HEADS * B_HD), F32),
                   jax.ShapeDtypeStruct(win_all.shape, F32)],
        input_output_aliases=aliases,
        compiler_params=_cparams(("parallel",)),
        name="swa_sample",
    )(*args)


GDN_ROWS = 64
GDN_R = A_HEADS * GDN_ROWS
GDN_PAIRS = A_HEADS // 2
PAIR_W = 2 * A_DK
GDN_BATCH = 6


def _mm(a, b):
    return jnp.dot(a.astype(BF16), b.astype(BF16), preferred_element_type=F32)


def _bmm(a, b):
    return jnp.einsum('bij,bjk->bik', a.astype(BF16), b.astype(BF16), preferred_element_type=F32)


def _block_diag2(x):
    z = jnp.zeros(x.shape[:-1] + (A_DK,), x.dtype)
    return jnp.concatenate([jnp.concatenate([x[..., :A_DK], z], axis=-1),
                            jnp.concatenate([z, x[..., A_DK:]], axis=-1)], axis=-2)


def _gdn_prep_kernel(x_ref, halo_ref, ab_ref, cw_ref, alog_ref, dtb_ref, u_ref, wq_ref, qkk_ref, gl_ref,
                     *, sb, cpb, bb, n_invalid, seq_halo):
    t = pl.program_id(1)
    ns = x_ref.shape[0]
    rows = x_ref.shape[1] // cpb
    c = GDN_ROWS
    r = GDN_R
    halo = halo_ref[...]
    if seq_halo:
        hrow = lax.broadcasted_iota(jnp.int32, (1, GDN_HALO, 1), 1) + (t * cpb * rows - GDN_HALO)
        halo = jnp.where(hrow >= n_invalid, halo, 0.0)
    cw = cw_ref[...]
    alog = alog_ref[...]
    dtb = dtb_ref[...]
    ri = lax.broadcasted_iota(jnp.int32, (r, r), 0)
    ci = lax.broadcasted_iota(jnp.int32, (r, r), 1)
    same = (ri // sb) == (ci // sb)
    incl = same & (ri >= ci)
    strict = same & (ri > ci)
    eye = (ri == ci).astype(F32)
    rc_ = lax.broadcasted_iota(jnp.int32, (c, c), 0)
    cc_ = lax.broadcasted_iota(jnp.int32, (c, c), 1)
    same_c = (rc_ // sb) == (cc_ // sb)
    lmat = jnp.concatenate([(same_c & (rc_ >= cc_)).astype(F32), same_c.astype(F32)], axis=0)
    sel = (lax.broadcasted_iota(jnp.int32, (SUBLANES, 2 * c), 1)
           == c + (lax.broadcasted_iota(jnp.int32, (SUBLANES, 2 * c), 0) * SUBLANES // sb) * sb).astype(F32)
    off = GDN_HALO - (A_CONV - 1)
    assert ns == 1 or cpb == 1
    hi = lambda m, v: jnp.einsum('bij,bjk->bik', jnp.broadcast_to(m, (bb,) + m.shape), v,
                                 precision=HIGHEST, preferred_element_type=F32)
    for c0 in range(0, cpb, bb):
        nr = bb * rows
        x = x_ref[:, c0 * rows:c0 * rows + nr, :]
        ab = ab_ref[:, c0 * rows:c0 * rows + nr, :].reshape(bb, c, LANES)
        if n_invalid:
            row0 = (t * cpb + c0) * rows
            xrow = lax.broadcasted_iota(jnp.int32, (1, nr, 1), 1) + row0
            x = jnp.where(xrow >= n_invalid, x, 0.0)
        ext = jnp.concatenate([halo, x], axis=1)
        acc = cw[0:1, :] * ext[:, off:off + nr, :]
        for k in range(1, A_CONV):
            acc = acc + cw[k:k + 1, :] * ext[:, off + k:off + k + nr, :]
        y = _silu(acc).reshape(bb, c, QKV_W)
        halo = x[:, nr - GDN_HALO:nr, :]

        sp = jnp.maximum(ab + dtb, 0.0) + jnp.log1p(jnp.exp(-jnp.abs(ab + dtb)))
        gfull = -jnp.exp(alog) * sp
        bfull = jax.nn.sigmoid(ab)
        if n_invalid:
            valid = (lax.broadcasted_iota(jnp.int32, (bb, c, 1), 0) * c
                     + lax.broadcasted_iota(jnp.int32, (bb, c, 1), 1) + row0) >= n_invalid
            gfull = jnp.where(valid, gfull, 0.0)
            bfull = jnp.where(valid, bfull, 0.0)
        bcol = jnp.concatenate([bfull[:, :, A_HEADS + h:A_HEADS + h + 1] for h in range(A_HEADS)], axis=1)
        gg = hi(lmat, gfull)
        ggt = jnp.stack([gg[i].T for i in range(bb)])
        gc = jnp.concatenate([gg[:, :c, h:h + 1] for h in range(A_HEADS)], axis=1)
        grow = jnp.concatenate([ggt[:, h:h + 1, :c] for h in range(A_HEADS)], axis=2)
        drow = jnp.concatenate([ggt[:, h:h + 1, c:] for h in range(A_HEADS)], axis=2) - grow
        g8 = jnp.exp(hi(sel, gg))
        decay = jnp.where(incl, jnp.exp(jnp.where(incl, gc - grow, 0.0)), 0.0)

        qs, ks, vs = [], [], []
        for h in range(A_HEADS):
            q = y[:, :, h * A_DK:(h + 1) * A_DK]
            k = y[:, :, A_KW + h * A_DK:A_KW + (h + 1) * A_DK]
            qs.append(q * lax.rsqrt(jnp.sum(q * q, axis=-1, keepdims=True) + EPS) * (A_DK ** -0.5))
            ks.append(k * lax.rsqrt(jnp.sum(k * k, axis=-1, keepdims=True) + EPS))
            vs.append(y[:, :, 2 * A_KW + h * A_DK:2 * A_KW + (h + 1) * A_DK])
        qr = jnp.concatenate(qs, axis=1)
        kr = jnp.concatenate(ks, axis=1)
        vr = jnp.concatenate(vs, axis=1)
        krt = jnp.stack([kr[i].T for i in range(bb)])
        kq = _bmm(jnp.concatenate([kr, qr], axis=1), krt)
        a = jnp.where(strict, kq[:, :r] * decay * bcol, 0.0)
        qk = kq[:, r:] * decay
        tinv = eye - a
        ak = a
        kpow = 2
        while kpow < sb:
            ak = _bmm(ak, ak)
            tinv = tinv + _bmm(tinv, ak)
            kpow *= 2
        eg = jnp.exp(gc)
        uw = _bmm(tinv, jnp.concatenate([vr * bcol, kr * (bcol * eg)], axis=2))
        ur = uw[:, :, :A_DK]
        wr = uw[:, :, A_DK:]
        qd = qr * eg
        kdt = krt * jnp.exp(drow)
        cs = slice(c0, c0 + bb)
        for p in range(GDN_PAIRS):
            r0 = slice(2 * p * c, (2 * p + 1) * c)
            r1 = slice((2 * p + 1) * c, (2 * p + 2) * c)
            pl_ = slice(2 * p * c, (2 * p + 2) * c)
            u_ref[cs, p] = jnp.concatenate([ur[:, r0], ur[:, r1]], axis=2)
            wq = jnp.concatenate([jnp.concatenate([wr[:, r0], wr[:, r1]], axis=2),
                                  jnp.concatenate([qd[:, r0], qd[:, r1]], axis=2)], axis=1)
            wq_ref[cs, p] = wq.astype(wq_ref.dtype)
            qkk = jnp.concatenate([qk[:, r0, pl_] + qk[:, r1, pl_], kdt[:, :, pl_]], axis=1)
            qkk_ref[cs, p] = qkk.astype(qkk_ref.dtype)
            gl_ref[cs, p] = jnp.concatenate(
                [jnp.broadcast_to(g8[:, :, 2 * p:2 * p + 1], (bb, SUBLANES, A_DK)),
                 jnp.broadcast_to(g8[:, :, 2 * p + 1:2 * p + 2], (bb, SUBLANES, A_DK))], axis=2)


def _gdn_prep(x_arr, halo_arr, halo_spec, cw, alog, dtb, ns, rows, cpb, bb, sb, n_invalid, op_dtype):
    n, l, _ = x_arr.shape
    grid = (n // ns, l // (cpb * rows))
    units = grid[0] * grid[1] * cpb
    kern = functools.partial(_gdn_prep_kernel, sb=sb, cpb=cpb, bb=bb, n_invalid=n_invalid,
                             seq_halo=halo_arr is x_arr)
    full = lambda shape: pl.BlockSpec(shape, lambda i, t: (0,) * len(shape))
    steps = grid[1]
    out_map = lambda i, t: (i * steps + t, 0, 0, 0)
    c = GDN_ROWS
    return pl.pallas_call(
        kern,
        grid=grid,
        in_specs=[pl.BlockSpec((ns, cpb * rows, QKV_W), lambda i, t: (i, t, 0)),
                  halo_spec,
                  pl.BlockSpec((ns, cpb * rows, LANES), lambda i, t: (i, t, AB_BLK)),
                  full(cw.shape), full(alog.shape), full(dtb.shape)],
        out_specs=[pl.BlockSpec((cpb, GDN_PAIRS, c, PAIR_W), out_map),
                   pl.BlockSpec((cpb, GDN_PAIRS, 2 * c, PAIR_W), out_map),
                   pl.BlockSpec((cpb, GDN_PAIRS, c + A_DK, 2 * c), out_map),
                   pl.BlockSpec((cpb, GDN_PAIRS, SUBLANES, PAIR_W), out_map)],
        out_shape=[jax.ShapeDtypeStruct((units, GDN_PAIRS, c, PAIR_W), F32),
                   jax.ShapeDtypeStruct((units, GDN_PAIRS, 2 * c, PAIR_W), op_dtype),
                   jax.ShapeDtypeStruct((units, GDN_PAIRS, c + A_DK, 2 * c), op_dtype),
                   jax.ShapeDtypeStruct((units, GDN_PAIRS, SUBLANES, PAIR_W), F32)],
        compiler_params=_cparams(("parallel", "arbitrary")),
        name="gdn_prep",
    )(x_arr, halo_arr, x_arr, cw, alog, dtb)


def _gdn_out(o_pair, z_ref, o_ref, nw, idx, p):
    for i in range(2):
        h = 2 * p + i
        hs = slice(h * A_DK, (h + 1) * A_DK)
        o = _rms(o_pair[..., i * A_DK:(i + 1) * A_DK], nw) * _silu(z_ref[idx + (hs,)])
        o_ref[idx + (hs,)] = o.astype(o_ref.dtype)


def _gdn_scan_kernel(u_ref, wq_ref, qkk_ref, gl_ref, z_ref, nw_ref, o_ref, sout_ref, s_ref):
    t = pl.program_id(0)
    n_seq = u_ref.shape[0]
    c = GDN_ROWS
    nw = nw_ref[...]

    @pl.when(t == 0)
    def _():
        s_ref[...] = jnp.zeros(s_ref.shape, F32)

    nb = n_seq * GDN_PAIRS
    flat = lambda ref: ref[:, 0].reshape((nb,) + ref.shape[3:])
    s = s_ref[...].reshape(nb, A_DK, PAIR_W)
    r1 = jnp.einsum('bmk,bkn->bmn', flat(wq_ref), _block_diag2(s.astype(BF16)), preferred_element_type=F32)
    vnew = flat(u_ref) - r1[:, :c]
    r2 = jnp.einsum('bmk,bkn->bmn', flat(qkk_ref), _block_diag2(vnew.astype(BF16)), preferred_element_type=F32)
    s_ref[...] = (s * flat(gl_ref)[:, 0:1, :] + r2[:, c:]).reshape(s_ref.shape)
    o = (r1[:, c:] + r2[:, :c]).reshape(n_seq, GDN_PAIRS, c, PAIR_W)
    for p in range(GDN_PAIRS):
        _gdn_out(o[:, p], z_ref, o_ref, nw, (slice(None), slice(None)), p)

    @pl.when(t == pl.num_programs(0) - 1)
    def _():
        for n in range(n_seq):
            for p in range(GDN_PAIRS):
                s = s_ref[n, p]
                sout_ref[n, 2 * p] = s[:, :A_DK]
                sout_ref[n, 2 * p + 1] = s[:, A_DK:]


def _gdn_scan(u, wq, qkk, gl, p3, nw):
    n, l, _ = p3.shape
    sshape = (n, A_HEADS, A_DK, A_DK)
    c = GDN_ROWS
    steps = l // c
    v5 = lambda a: a.reshape((n, steps) + a.shape[1:])
    u, wq, qkk, gl = v5(u), v5(wq), v5(qkk), v5(gl)
    unit = lambda a: pl.BlockSpec((n, 1) + a.shape[2:], lambda t: (0, t, 0, 0, 0))
    vw = A_HEADS * A_DK
    return pl.pallas_call(
        _gdn_scan_kernel,
        grid=(steps,),
        in_specs=[unit(u), unit(wq), unit(qkk), unit(gl),
                  pl.BlockSpec((n, c, vw), lambda t: (0, t, Z_BLK)),
                  pl.BlockSpec(nw.shape, lambda t: (0, 0))],
        out_specs=[pl.BlockSpec((n, c, vw), lambda t: (0, t, 0)),
                   pl.BlockSpec(sshape, lambda t: (0, 0, 0, 0))],
        out_shape=[jax.ShapeDtypeStruct((n, l, vw), BF16),
                   jax.ShapeDtypeStruct(sshape, F32)],
        scratch_shapes=[pltpu.VMEM((n, GDN_PAIRS, A_DK, PAIR_W), F32)],
        compiler_params=_cparams(("arbitrary",)),
        name="gdn_scan",
    )(u, wq, qkk, gl, p3, nw)


def _gdn_step_kernel(u_ref, wq_ref, qkk_ref, gl_ref, z_ref, s0_ref, nw_ref, *rest):
    o_ref, sout_ref = rest[-2:]
    ns, rows, _ = z_ref.shape
    c = GDN_ROWS
    nw = nw_ref[...]
    lane_seq = (lax.broadcasted_iota(jnp.int32, (A_DK, 2 * c), 1) % c) // rows
    for p in range(GDN_PAIRS):
        wq = wq_ref[0, p]
        qkk = qkk_ref[0, p]
        gl = gl_ref[0, p]
        s = jnp.concatenate([s0_ref[:, 2 * p], s0_ref[:, 2 * p + 1]], axis=2)
        lhs = jnp.concatenate([wq[:c].reshape(ns, rows, PAIR_W), wq[c:].reshape(ns, rows, PAIR_W)], axis=1)
        r1 = _bmm(lhs, _block_diag2(s.astype(BF16)))
        vnew = u_ref[0, p] - r1[:, :rows].reshape(c, PAIR_W)
        bdv = _block_diag2(vnew.astype(BF16))
        o = r1[:, rows:].reshape(c, PAIR_W) + _mm(qkk[:c], bdv)
        _gdn_out(o.reshape(ns, rows, PAIR_W), z_ref, o_ref, nw, (slice(None), slice(None)), p)
        kdt = qkk[c:]
        lhs = jnp.concatenate([jnp.where(lane_seq == s_i, kdt, 0.0) for s_i in range(ns)], axis=0)
        upd = _mm(lhs, bdv).reshape(ns, A_DK, PAIR_W)
        s_new = s * gl.reshape(ns, 1, PAIR_W) + upd
        sout_ref[:, 2 * p] = s_new[:, :, :A_DK]
        sout_ref[:, 2 * p + 1] = s_new[:, :, A_DK:]


def _gdn_step(u, wq, qkk, gl, p3, s_all, layer, s_prev, nw, ns):
    n, rows, _ = p3.shape
    unit = lambda a: pl.BlockSpec((1,) + a.shape[1:], lambda i: (i, 0, 0, 0))
    vw = A_HEADS * A_DK
    sblk = pl.BlockSpec((None, ns) + s_all.shape[2:], lambda i: (layer, i, 0, 0, 0))
    in_specs = [unit(u), unit(wq), unit(qkk), unit(gl),
                pl.BlockSpec((ns, rows, vw), lambda i: (i, 0, Z_BLK)),
                sblk,
                pl.BlockSpec(nw.shape, lambda i: (0, 0))]
    args = [u, wq, qkk, gl, p3, s_all, nw]
    aliases = {}
    if s_prev is not None:
        in_specs.append(pl.BlockSpec(memory_space=pl.ANY))
        args.append(s_prev)
        aliases = {len(args) - 1: 1}
    return pl.pallas_call(
        _gdn_step_kernel,
        grid=(n // ns,),
        in_specs=in_specs,
        out_specs=[pl.BlockSpec((ns, rows, vw), lambda i: (i, 0, 0)), sblk],
        out_shape=[jax.ShapeDtypeStruct((n, rows, vw), F32),
                   jax.ShapeDtypeStruct(s_all.shape, F32)],
        input_output_aliases=aliases,
        compiler_params=_cparams(("parallel",)),
        name="gdn_step",
    )(*args)


def _gdn(p3, state, hist, cw, alog, dtb, nw, cpb, n_invalid):
    n, l, _ = p3.shape
    c = GDN_ROWS
    if hist is None:
        steps8 = cpb * c // GDN_HALO
        halo_spec = pl.BlockSpec((1, GDN_HALO, QKV_W), lambda i, t: (i, jnp.maximum(t * steps8 - 1, 0), 0))
        u, wq, qkk, gl = _gdn_prep(p3, p3, halo_spec, cw, alog, dtb, 1, c, cpb, min(cpb, GDN_BATCH), c, n_invalid, BF16)
        return _gdn_scan(u, wq, qkk, gl, p3, nw)
    ns = c // l
    halo_spec = pl.BlockSpec((ns, GDN_HALO, QKV_W), lambda i, t: (i, 0, 0))
    u, wq, qkk, gl = _gdn_prep(p3, hist, halo_spec, cw, alog, dtb, ns, l, 1, 1, l, n_invalid, F32)
    s_all, layer, s_prev = state
    return _gdn_step(u, wq, qkk, gl, p3, s_all, layer, s_prev, nw, ns)


def _merge_kernel(oa_ref, ob_ref, oc_ref, gl_ref, x_ref, wb_ref, wo_ref, out_ref):
    d = x_ref.shape[1]
    merged = None
    for r, o_ref in enumerate((oa_ref, ob_ref, oc_ref)):
        pb = jnp.dot(o_ref[...].astype(BF16), wb_ref[r], preferred_element_type=F32)
        term = jax.nn.sigmoid(gl_ref[:, r * d:(r + 1) * d]) * pb
        merged = term if merged is None else merged + term
    out_ref[...] = x_ref[...] + jnp.dot(merged.astype(BF16), wo_ref[...], preferred_element_type=F32)


def _merge(oa, ob, oc, p2, x2, wb, wo, tm):
    t, d = x2.shape
    bw = oa.shape[1]
    row = lambda w, blk: pl.BlockSpec((tm, w), lambda i: (i, blk))
    return pl.pallas_call(
        _merge_kernel,
        grid=(t // tm,),
        in_specs=[row(bw, 0), row(bw, 0), row(bw, 0), row(3 * d, GL_BLK), row(d, 0),
                  pl.BlockSpec(wb.shape, lambda i: (0, 0, 0)),
                  pl.BlockSpec(wo.shape, lambda i: (0, 0))],
        out_specs=row(d, 0),
        out_shape=jax.ShapeDtypeStruct((t, d), F32),
        compiler_params=_cparams(("parallel",)),
        name="merge",
    )(oa, ob, oc, p2, x2, wb, wo)


def _route(logits):
    lane = lax.broadcasted_iota(jnp.int32, logits.shape, 1)
    is_g = (lane >= N_EXPERTS) & (lane < N_EXPERTS + N_GROUPS)
    gl = jnp.where(is_g, logits, NEG)
    gmax = jnp.max(gl, axis=-1, keepdims=True)
    gidx = jnp.min(jnp.where(gl == gmax, lane - N_EXPERTS, LANES), axis=-1, keepdims=True)
    gw = 1.0 / jnp.sum(jnp.where(is_g, jnp.exp(gl - gmax), 0.0), axis=-1, keepdims=True)
    in_grp = (lane < N_EXPERTS) & ((lane // EXP_PER_GROUP) == gidx)
    el = jnp.where(in_grp, logits, NEG)
    v1 = jnp.max(el, axis=-1, keepdims=True)
    i1 = jnp.min(jnp.where(el == v1, lane, LANES), axis=-1, keepdims=True)
    el2 = jnp.where(lane == i1, NEG, el)
    v2 = jnp.max(el2, axis=-1, keepdims=True)
    i2 = jnp.min(jnp.where(el2 == v2, lane, LANES), axis=-1, keepdims=True)
    e21 = jnp.exp(v2 - v1)
    w1 = gw / (1.0 + e21)
    w2 = gw * e21 / (1.0 + e21)
    return jnp.where(lane == i1, w1, 0.0) + jnp.where(lane == i2, w2, 0.0)


def _moe_kernel(x_ref, nw_ref, wr_ref, br_ref, wg_ref, wu_ref, wd_ref, out_ref, h_ref, cmb_ref):
    g = pl.program_id(1)

    @pl.when(g == 0)
    def _():
        x = x_ref[...]
        h = _rms(x, nw_ref[...])
        h_hi = h.astype(BF16)
        h_ref[...] = h_hi
        h_lo = (h - h_hi.astype(F32)).astype(BF16)
        l2 = jnp.dot(h_hi, wr_ref[...], preferred_element_type=F32)
        logits = (l2[:, :LANES] + l2[:, LANES:]
                  + jnp.dot(h_lo, wr_ref[:, :LANES], preferred_element_type=F32) + br_ref[...])
        cmb_ref[...] = _route(logits)
        out_ref[...] = x

    h = h_ref[...]
    cmb = cmb_ref[...]
    lane = lax.broadcasted_iota(jnp.int32, cmb.shape, 1)
    parts = []
    for e in range(EXP_PER_GROUP):
        ce = jnp.sum(jnp.where(lane == g * EXP_PER_GROUP + e, cmb, 0.0), axis=-1, keepdims=True)
        hid = (_silu(jnp.dot(h, wg_ref[e], preferred_element_type=F32))
               * jnp.dot(h, wu_ref[e], preferred_element_type=F32))
        parts.append((hid * ce).astype(BF16))
    wd = wd_ref[...].reshape(-1, wd_ref.shape[-1])
    out_ref[...] += jnp.dot(jnp.concatenate(parts, axis=1), wd, preferred_element_type=F32)


def _moe(x2, nw, wr, br, wg, wu, wd, tm):
    t, d = x2.shape
    ne, _, de = wg.shape
    return pl.pallas_call(
        _moe_kernel,
        grid=(t // tm, ne // EXP_PER_GROUP),
        in_specs=[pl.BlockSpec((tm, d), lambda i, g: (i, 0)),
                  pl.BlockSpec((1, d), lambda i, g: (0, 0)),
                  pl.BlockSpec(wr.shape, lambda i, g: (0, 0)),
                  pl.BlockSpec(br.shape, lambda i, g: (0, 0)),
                  pl.BlockSpec((EXP_PER_GROUP, d, de), lambda i, g: (g, 0, 0)),
                  pl.BlockSpec((EXP_PER_GROUP, d, de), lambda i, g: (g, 0, 0)),
                  pl.BlockSpec((EXP_PER_GROUP, de, d), lambda i, g: (g, 0, 0))],
        out_specs=pl.BlockSpec((tm, d), lambda i, g: (i, 0)),
        out_shape=jax.ShapeDtypeStruct((t, d), F32),
        scratch_shapes=[pltpu.VMEM((tm, d), BF16), pltpu.VMEM((tm, LANES), F32)],
        compiler_params=_cparams(("parallel", "arbitrary")),
        name="moe",
    )(x2, nw, wr, br, wg, wu, wd)


def _pad_lanes(v, width):
    v = v.reshape(1, -1)
    return jnp.pad(v, ((0, 0), (0, width - v.shape[1])))


def _layer_weights(l, norm1_w, norm2_w, w_in, gdn_conv_w, gdn_a_log, gdn_dt_bias, gdn_norm_w,
                   swa_q_norm_w, swa_k_norm_w, swa_sinks, conv_dw_w, conv_dw_b, conv_ln_w, conv_ln_b,
                   w_branch, w_out, router_group_w, router_group_b, router_expert_w, router_expert_b,
                   moe_w_gate, moe_w_up, moe_w_down):
    d = w_in.shape[1]
    wi = w_in[l]
    o_ab = 4 * A_KW
    o_bq = o_ab + 2 * A_HEADS
    o_cu = o_bq + B_HEADS * B_HD + BKV_W
    o_gl = o_cu + 2 * C_CH
    w_perm = jnp.concatenate(
        [wi[:, :o_ab], wi[:, o_cu:o_gl], wi[:, o_gl:], wi[:, o_bq:o_cu], wi[:, o_ab:o_bq],
         jnp.zeros((d, LANES - 2 * A_HEADS), wi.dtype)], axis=1).astype(BF16)
    assert w_perm.shape[1] == PROJ_W
    wr = jnp.concatenate([router_expert_w[l], router_group_w[l],
                          jnp.zeros((d, LANES - N_EXPERTS - N_GROUPS), F32)], axis=1)
    br = _pad_lanes(jnp.concatenate([router_expert_b[l], router_group_b[l]]), LANES)
    wr_hi = wr.astype(BF16)
    wr = jnp.concatenate([wr_hi, (wr - wr_hi.astype(F32)).astype(BF16)], axis=1)
    return dict(
        n1=norm1_w[l].reshape(1, d), n2=norm2_w[l].reshape(1, d), w_in=w_perm,
        gdn_cw=gdn_conv_w[l], alog=_pad_lanes(gdn_a_log[l], LANES), dtb=_pad_lanes(gdn_dt_bias[l], LANES),
        gdn_nw=gdn_norm_w[l].reshape(1, -1),
        qnw=swa_q_norm_w[l].reshape(1, -1), knw=swa_k_norm_w[l].reshape(1, -1), sinks=swa_sinks[l].reshape(1, -1),
        qnw_heads=jnp.tile(swa_q_norm_w[l].reshape(1, -1), (1, B_HEADS)),
        knw_heads=jnp.tile(swa_k_norm_w[l].reshape(1, -1), (1, B_KV_HEADS)),
        dww=conv_dw_w[l], dwb=conv_dw_b[l].reshape(1, -1), lnw=conv_ln_w[l].reshape(1, -1),
        lnb=conv_ln_b[l].reshape(1, -1),
        wb=w_branch[l].astype(BF16), wo=w_out[l].astype(BF16), wr=wr, br=br,
        wg=moe_w_gate[l].astype(BF16), wu=moe_w_up[l].astype(BF16), wd=moe_w_down[l].astype(BF16))


def _tiles(n, l):
    prompt = l > BLOCK
    t = n * l
    if prompt:
        return dict(proj_tm=1536, tok_tm=512, moe_tm=1408, conf_nb=1, conf_tm=384, conf_rc=64,
                    gdn_cpb=6, swa_bps=11, n_invalid=META_PAD, branch_dtype=BF16)
    return dict(proj_tm=t, tok_tm=512, moe_tm=t, conf_nb=16, conf_tm=l, conf_rc=l,
                gdn_cpb=1, n_invalid=0, branch_dtype=F32)


def _layer(x3, lw, gdn_state, gdn_hist, conf_hist, swa_cache):
    n, l, d = x3.shape
    tl = _tiles(n, l)
    x2 = x3.reshape(n * l, d)
    p2 = _inproj(x2, lw['n1'], lw['w_in'], tl['proj_tm'], PROJ_W // 5)
    p3 = p2.reshape(n, l, PROJ_W)
    o_a, s_new = _gdn(p3, gdn_state, gdn_hist, lw['gdn_cw'], lw['alog'], lw['dtb'], lw['gdn_nw'],
                      tl['gdn_cpb'], tl['n_invalid'])
    gh_new = p3[:, l - (A_CONV - 1):, :QKV_W]
    if swa_cache is None:
        o_b, kvn = _swa_prompt(p3, lw['qnw_heads'], lw['knw_heads'], lw['sinks'], tl['swa_bps'])
        kv = lambda sl: kvn[:, sl].reshape(n, -1, 2, B_KV_HEADS, B_HD)
        swa_new = (kv(slice(META_PAD, BLOCK)), kv(slice(l - WINDOW, l)))
    else:
        win_all, meta_all, layer, win_prev = swa_cache
        o_b, swa_new = _swa_sample(p3, win_all, meta_all, layer, win_prev, lw['qnw'], lw['knw'], lw['sinks'], 16)
    o_c, ch_new = _conformer(p3, conf_hist, lw['dww'], lw['dwb'], lw['lnw'], lw['lnb'],
                             tl['conf_nb'], tl['conf_tm'], tl['conf_rc'], tl['n_invalid'], tl['branch_dtype'])
    t = n * l
    x2 = _merge(o_a.reshape(t, -1), o_b.reshape(t, -1), o_c.reshape(t, -1), p2, x2, lw['wb'], lw['wo'],
                tl['tok_tm'])
    x2 = _moe(x2, lw['n2'], lw['wr'], lw['br'], lw['wg'], lw['wu'], lw['wd'], tl['moe_tm'])
    return x2.reshape(n, l, d), s_new, gh_new, swa_new, ch_new


def kernel(x_prompt, x_sample, state_gdn, cache_gdn_conv, cache_swa_kv, cache_meta_kv, cache_conv, meta_tokens, norm1_w, norm2_w, w_in, gdn_conv_w, gdn_a_log, gdn_dt_bias, gdn_norm_w, swa_q_norm_w, swa_k_norm_w, swa_sinks, conv_dw_w, conv_dw_b, conv_ln_w, conv_ln_b, w_branch, w_out, router_group_w, router_group_b, router_expert_w, router_expert_b, moe_w_gate, moe_w_up, moe_w_down):
    dtp = x_prompt.dtype
    n_p, _, d = x_prompt.shape
    n_s = x_sample.shape[0]
    depth = w_in.shape[0]
    xp = jnp.pad(x_prompt, ((0, 0), (BLOCK, 0), (0, 0)))
    xp = lax.dynamic_update_slice(
        xp, jnp.broadcast_to(meta_tokens.astype(dtp)[None], (n_p, N_META, d)), (0, META_PAD, 0))
    xs = x_sample
    state_s = None
    win_s = None
    win_all = cache_swa_kv.reshape(depth, n_s, -1, BKV_W)
    meta_all = cache_meta_kv.reshape(depth, n_s, N_META, BKV_W)
    outs_p, outs_s = [], []
    for l in range(depth):
        lw = _layer_weights(l, norm1_w, norm2_w, w_in, gdn_conv_w, gdn_a_log, gdn_dt_bias, gdn_norm_w,
                            swa_q_norm_w, swa_k_norm_w, swa_sinks, conv_dw_w, conv_dw_b, conv_ln_w, conv_ln_b,
                            w_branch, w_out, router_group_w, router_group_b, router_expert_w, router_expert_b,
                            moe_w_gate, moe_w_up, moe_w_down)
        xp, s_p, gh_p, (mkv_p, wkv_p), ch_p = _layer(
            xp, lw, None, None, jnp.zeros((n_p, C_CONV - 1, C_CH), dtp), None)
        outs_p.append((s_p, gh_p, wkv_p, mkv_p, ch_p))
        gh0 = jnp.pad(cache_gdn_conv[l], ((0, 0), (GDN_HALO - (A_CONV - 1), 0), (0, 0)))
        xs, state_s, gh_s, win_s, ch_s = _layer(xs, lw, (state_gdn, l, state_s), gh0, cache_conv[l],
                                                (win_all, meta_all, l, win_s))
        outs_s.append((gh_s, ch_s))
    stack = lambda outs, i: jnp.stack([o[i] for o in outs])
    return (xp[:, BLOCK:], xs,
            stack(outs_p, 0), stack(outs_p, 1), stack(outs_p, 2), stack(outs_p, 3), stack(outs_p, 4),
            state_s, stack(outs_s, 0), win_s.reshape(cache_swa_kv.shape), stack(outs_s, 1))
```

```python
import functools

import jax
import jax.numpy as jnp
from jax import lax
from jax.experimental import pallas as pl
from jax.experimental.pallas import tpu as pltpu

F32 = jnp.float32
BF16 = jnp.bfloat16
HIGHEST = lax.Precision.HIGHEST
EPS = 1e-6
NEG = -1e30

VMEM_LIMIT_BYTES = 56 * 1024 * 1024
LANES = 128
SUBLANES = 8

PAST_LEN = 16384
N_META = 16
BLOCK = 128
WINDOW = 128
META_PAD = BLOCK - N_META
A_HEADS = 4
A_DK = 128
A_CONV = 4
B_HEADS = 8
B_KV_HEADS = 2
B_GROUP = B_HEADS // B_KV_HEADS
B_HD = 64
C_CH = 512
C_CONV = 31
BRANCH_W = 512
N_GROUPS = 4
EXP_PER_GROUP = 4
N_EXPERTS = N_GROUPS * EXP_PER_GROUP

A_KW = A_HEADS * A_DK
QKV_W = 3 * A_KW
Z_BLK = QKV_W // BRANCH_W
CU_BLK = 2
GL_BLK = 1
BQ_BLK = 12
BKV_W = 2 * B_KV_HEADS * B_HD
BKV_BLK = 26
AB_BLK = 54
PROJ_W = 7040
CONF_HALO = 32
GDN_HALO = SUBLANES


def _rms(x, w):
    return x * lax.rsqrt(jnp.mean(x * x, axis=-1, keepdims=True) + EPS) * w


def _silu(x):
    return x * jax.nn.sigmoid(x)


def _cparams(sem):
    return pltpu.CompilerParams(dimension_semantics=sem, vmem_limit_bytes=VMEM_LIMIT_BYTES)


def _inproj_kernel(x_ref, nw_ref, w_ref, o_ref, h_ref):
    @pl.when(pl.program_id(1) == 0)
    def _():
        h_ref[...] = _rms(x_ref[...], nw_ref[...]).astype(BF16)

    o_ref[...] = jnp.dot(h_ref[...], w_ref[...], preferred_element_type=F32)


def _inproj(x2, nw, w, tm, tn):
    t, d = x2.shape
    wd = w.shape[1]
    return pl.pallas_call(
        _inproj_kernel,
        grid=(t // tm, wd // tn),
        in_specs=[pl.BlockSpec((tm, d), lambda i, j: (i, 0)),
                  pl.BlockSpec((1, d), lambda i, j: (0, 0)),
                  pl.BlockSpec((d, tn), lambda i, j: (0, j))],
        out_specs=pl.BlockSpec((tm, tn), lambda i, j: (i, j)),
        out_shape=jax.ShapeDtypeStruct((t, wd), F32),
        scratch_shapes=[pltpu.VMEM((tm, d), BF16)],
        compiler_params=_cparams(("parallel", "arbitrary")),
        name="inproj",
    )(x2, nw, w)


def _conf_kernel(cu_ref, hist_ref, dww_ref, dwb_ref, lnw_ref, lnb_ref, y_ref, nh_ref, e_ref, c_ref,
                 *, tm, rc, n_invalid):
    t = pl.program_id(1)
    nb = cu_ref.shape[0]
    halo = CONF_HALO

    @pl.when(t == 0)
    def _():
        e_ref[:, 0:SUBLANES, :] = jnp.zeros((nb, SUBLANES, C_CH), F32)
        e_ref[:, halo + tm:halo + tm + SUBLANES, :] = jnp.zeros((nb, SUBLANES, C_CH), F32)
        e_ref[:, halo - (C_CONV - 1):halo, :] = hist_ref[...]

    @pl.when(t > 0)
    def _():
        e_ref[:, 0:halo, :] = e_ref[:, tm:tm + halo, :]

    cu = cu_ref[...]
    u = cu[:, :, :C_CH] * jax.nn.sigmoid(cu[:, :, C_CH:])
    if n_invalid:
        row = lax.broadcasted_iota(jnp.int32, (1, tm, 1), 1) + t * tm
        u = jnp.where(row >= n_invalid, u, 0.0)
    e_ref[:, halo:halo + tm, :] = u
    nh_ref[...] = e_ref[:, halo + tm - (C_CONV - 1):halo + tm, :]

    off = halo - (C_CONV - 1)
    for c in range(C_CH // LANES):
        cs = slice(c * LANES, (c + 1) * LANES)

        def rbody(r, carry, cs=cs):
            r0 = pl.multiple_of(r * rc, rc)
            blk = e_ref[:, pl.ds(r0, rc + halo + SUBLANES), cs]
            acc = jnp.zeros((nb, rc, LANES), F32)
            for s in range(SUBLANES):
                z = None
                for a in range((halo + SUBLANES) // SUBLANES):
                    k = SUBLANES * a + s - off
                    if 0 <= k < C_CONV:
                        term = dww_ref[k:k + 1, cs] * blk[:, SUBLANES * a:SUBLANES * a + rc + SUBLANES, :]
                        z = term if z is None else z + term
                acc = acc + z[:, s:s + rc, :]
            c_ref[:, pl.ds(r0, rc), cs] = acc + dwb_ref[:, cs]
            return carry

        lax.fori_loop(0, tm // rc, rbody, 0)

    y = c_ref[...]
    mu = jnp.mean(y, axis=-1, keepdims=True)
    var = jnp.mean(jnp.square(y - mu), axis=-1, keepdims=True)
    yn = (y - mu) * lax.rsqrt(var + EPS) * lnw_ref[...] + lnb_ref[...]
    y_ref[...] = _silu(yn).astype(y_ref.dtype)


def _conformer(p3, hist, dww, dwb, lnw, lnb, nb, tm, rc, n_invalid, out_dtype):
    n, l, _ = p3.shape
    kern = functools.partial(_conf_kernel, tm=tm, rc=rc, n_invalid=n_invalid)
    full = lambda shape: pl.BlockSpec(shape, lambda i, t: (0,) * len(shape))
    return pl.pallas_call(
        kern,
        grid=(n // nb, l // tm),
        in_specs=[pl.BlockSpec((nb, tm, 2 * C_CH), lambda i, t: (i, t, CU_BLK)),
                  pl.BlockSpec((nb, C_CONV - 1, C_CH), lambda i, t: (i, 0, 0)),
                  full(dww.shape), full(dwb.shape), full(lnw.shape), full(lnb.shape)],
        out_specs=[pl.BlockSpec((nb, tm, C_CH), lambda i, t: (i, t, 0)),
                   pl.BlockSpec((nb, C_CONV - 1, C_CH), lambda i, t: (i, 0, 0))],
        out_shape=[jax.ShapeDtypeStruct((n, l, C_CH), out_dtype),
                   jax.ShapeDtypeStruct((n, C_CONV - 1, C_CH), F32)],
        scratch_shapes=[pltpu.VMEM((nb, CONF_HALO + tm + SUBLANES, C_CH), F32), pltpu.VMEM((nb, tm, C_CH), F32)],
        compiler_params=_cparams(("parallel", "arbitrary")),
        name="conformer",
    )(p3, hist, dww, dwb, lnw, lnb)


SWA_ROWS = 512


def _softmax_sink_pv(s, mask, sink, v, dims):
    s = jnp.where(mask, s, NEG)
    m = jnp.maximum(jnp.max(s, axis=-1, keepdims=True), sink)
    p = jnp.exp(s - m)
    den = jnp.sum(p, axis=-1, keepdims=True) + jnp.exp(sink - m)
    return lax.dot_general(p, v, dims, preferred_element_type=F32) / den


def _swa_prompt_kernel(q_ref, kv_ref, kvp_ref, kvm_ref, qnw_ref, knw_ref, sink_ref, o_ref, kvn_ref, *, bps):
    step = pl.program_id(1)
    qnw = qnw_ref[...]
    knw = knw_ref[...]
    qw = B_HEADS * B_HD
    kw = B_KV_HEADS * B_HD
    nk = N_META + 2 * BLOCK
    nke = nk + SUBLANES
    rows = B_GROUP * BLOCK
    r = lax.broadcasted_iota(jnp.int32, (rows, nke), 0) % BLOCK
    c = lax.broadcasted_iota(jnp.int32, (rows, nke), 1)
    cp = c - N_META
    co = cp - BLOCK
    hrow = lax.broadcasted_iota(jnp.int32, (rows, 1), 0) // BLOCK
    m_meta = c < N_META
    m_own = (co >= 0) & (co <= r) & (c < nk)
    m_prev = (cp >= 0) & (cp < BLOCK) & (cp > r)
    bias0 = jnp.where(m_meta & (META_PAD + c <= r), 0.0, NEG)
    bias1 = jnp.where(m_meta | m_own, 0.0, NEG)
    bias2 = jnp.where(m_meta | m_own | m_prev, 0.0, NEG)
    sinks = []
    for g in range(B_KV_HEADS):
        sink = jnp.zeros((rows, 1), F32)
        for i in range(B_GROUP):
            h = g * B_GROUP + i
            sink = jnp.where(hrow == i, sink_ref[0:1, h:h + 1], sink)
        sinks.append(sink)
    hb_q = (lax.broadcasted_iota(jnp.int32, (qw, qw), 0) // B_HD
            == lax.broadcasted_iota(jnp.int32, (qw, qw), 1) // B_HD).astype(BF16)
    hb_k = hb_q[:kw, :kw]
    half = lax.broadcasted_iota(jnp.int32, (1, LANES), 1) // B_HD

    def head_rms(x, hb, w):
        sq = x * x
        hi = sq.astype(BF16)
        lo = (sq - hi.astype(F32)).astype(BF16)
        ss = jnp.dot(hi, hb, preferred_element_type=F32) + jnp.dot(lo, hb, preferred_element_type=F32)
        return x * lax.rsqrt(ss * (1.0 / B_HD) + EPS) * w

    kvm = kvm_ref[META_PAD:BLOCK, :]
    km = head_rms(kvm[:, :kw], hb_k, knw)
    kv_halo = kvp_ref[...]
    q = q_ref[...]
    kv = kv_ref[...]
    k_own = head_rms(kv[:, :kw], hb_k, knw)
    kvn_ref[...] = jnp.concatenate([k_own, kv[:, kw:]], axis=-1)
    qn = head_rms(q, hb_q, qnw)
    tq = bps * BLOCK
    blocks = lambda x: x.reshape(bps, BLOCK, x.shape[-1])
    k_prev = jnp.concatenate([head_rms(kv_halo[:, :kw], hb_k, knw), k_own[:tq - BLOCK]], axis=0)
    v_prev = jnp.concatenate([kv_halo[:, kw:], kv[:tq - BLOCK, kw:]], axis=0)
    rep = lambda x: jnp.broadcast_to(x[None], (bps,) + x.shape)
    zpad = jnp.zeros((bps, SUBLANES, kw), F32)
    kcat_b = jnp.concatenate([rep(km), blocks(k_prev), blocks(k_own), zpad], axis=1).astype(BF16)
    vcat_b = jnp.concatenate([rep(kvm[:, kw:]), blocks(v_prev), blocks(kv[:, kw:]), zpad], axis=1).astype(BF16)
    ones_b = jnp.ones((bps, nke, kw), BF16)
    bias = jnp.stack([jnp.where(step * bps + b >= 2, bias2, jnp.where(step * bps + b >= 1, bias1, bias0))
                      for b in range(bps)])
    tiles = [qn[:, m * LANES:(m + 1) * LANES] for m in range(qw // LANES)]
    swapped = [pltpu.roll(tl_, B_HD, 1) for tl_ in tiles]
    outs = [None] * B_HEADS
    for g in range(B_KV_HEADS):
        in_g = half == g
        parts = []
        for i in range(B_GROUP):
            h = g * B_GROUP + i
            src = tiles[h // 2] if h % 2 == g else swapped[h // 2]
            parts.append(blocks(jnp.where(in_g, src, 0.0)))
        lhs = jnp.concatenate(parts, axis=1).astype(BF16)
        bias_g = jnp.where(c == nk, sinks[g], bias)
        chunks = []
        for r0 in range(0, rows, SWA_ROWS):
            rc = slice(r0, r0 + SWA_ROWS)
            s = jnp.einsum('bqd,bkd->bqk', lhs[:, rc], kcat_b, preferred_element_type=F32)
            s = s * (B_HD ** -0.5) + bias_g[:, rc]
            p = jnp.exp(s - jnp.max(s, axis=-1, keepdims=True)).astype(BF16)
            den = jnp.einsum('bqk,bkd->bqd', p, ones_b, preferred_element_type=F32)
            chunks.append(jnp.einsum('bqk,bkd->bqd', p, vcat_b, preferred_element_type=F32) / den)
        res = jnp.concatenate(chunks, axis=1)
        for i in range(B_GROUP):
            h = g * B_GROUP + i
            part = res[:, i * BLOCK:(i + 1) * BLOCK, :].reshape(tq, LANES)
            outs[h] = part if h % 2 == g else pltpu.roll(part, B_HD, 1)
    for m in range(qw // LANES):
        o_ref[:, m * LANES:(m + 1) * LANES] = jnp.where(half == 0, outs[2 * m], outs[2 * m + 1]).astype(o_ref.dtype)


def _swa_prompt(p3, qnw, knw, sinks, bps):
    n, l, _ = p3.shape
    full = lambda shape: pl.BlockSpec(shape, lambda i, j: (0,) * len(shape))
    tq = bps * BLOCK
    return pl.pallas_call(
        functools.partial(_swa_prompt_kernel, bps=bps),
        grid=(n, l // tq),
        in_specs=[pl.BlockSpec((None, tq, B_HEADS * B_HD), lambda i, j: (i, j, BQ_BLK)),
                  pl.BlockSpec((None, tq, BKV_W), lambda i, j: (i, j, BKV_BLK)),
                  pl.BlockSpec((None, BLOCK, BKV_W), lambda i, j: (i, jnp.maximum(j * bps - 1, 0), BKV_BLK)),
                  pl.BlockSpec((None, BLOCK, BKV_W), lambda i, j: (i, 0, BKV_BLK)),
                  full(qnw.shape), full(knw.shape), full(sinks.shape)],
        out_specs=[pl.BlockSpec((None, tq, B_HEADS * B_HD), lambda i, j: (i, j, 0)),
                   pl.BlockSpec((None, tq, BKV_W), lambda i, j: (i, j, 0))],
        out_shape=[jax.ShapeDtypeStruct((n, l, B_HEADS * B_HD), BF16),
                   jax.ShapeDtypeStruct((n, l, BKV_W), F32)],
        compiler_params=_cparams(("parallel", "arbitrary")),
        name="swa_prompt",
    )(p3, p3, p3, p3, qnw, knw, sinks)


def _swa_sample_kernel(q_ref, kvn_ref, win_ref, meta_ref, qnw_ref, knw_ref, sink_ref, *rest):
    o_ref, nwin_ref = rest[-2:]
    q = q_ref[...]
    kvn = kvn_ref[...]
    win = win_ref[...]
    meta = meta_ref[...]
    qnw = qnw_ref[...]
    knw = knw_ref[...]
    t_new = q.shape[1]
    w = win.shape[1]
    nk = N_META + w + t_new
    rows = B_GROUP * t_new
    tq = lax.broadcasted_iota(jnp.int32, (1, rows, nk), 1) % t_new
    c = lax.broadcasted_iota(jnp.int32, (1, rows, nk), 2)
    cw = c - N_META
    win_ok = (cw >= 0) & (cw < w) & (cw - w > tq - WINDOW) & (cw + (PAST_LEN - w) >= N_META)
    cn = cw - w
    new_ok = (cn >= 0) & (cn <= tq) & (cn > tq - WINDOW)
    mask = (c < N_META) | win_ok | new_ok
    hrow = lax.broadcasted_iota(jnp.int32, (1, rows, 1), 1) // t_new
    kn_parts = []
    for g in range(B_KV_HEADS):
        ks = slice(g * B_HD, (g + 1) * B_HD)
        vs = slice((B_KV_HEADS + g) * B_HD, (B_KV_HEADS + g + 1) * B_HD)
        k_new = _rms(kvn[:, :, ks], knw)
        kn_parts.append(k_new)
        kk = jnp.concatenate([meta[:, :, ks], win[:, :, ks], k_new], axis=1)
        vv = jnp.concatenate([meta[:, :, vs], win[:, :, vs], kvn[:, :, vs]], axis=1)
        qq = jnp.concatenate(
            [_rms(q[:, :, (g * B_GROUP + i) * B_HD:(g * B_GROUP + i + 1) * B_HD], qnw) for i in range(B_GROUP)],
            axis=1)
        sink = jnp.zeros((1, rows, 1), F32)
        for i in range(B_GROUP):
            h = g * B_GROUP + i
            sink = jnp.where(hrow == i, sink_ref[0:1, h:h + 1], sink)
        s = jnp.einsum('bqd,bkd->bqk', qq, kk, preferred_element_type=F32) * (B_HD ** -0.5)
        o = _softmax_sink_pv(s, mask, sink, vv, (((2,), (1,)), ((0,), (0,))))
        for i in range(B_GROUP):
            h = g * B_GROUP + i
            o_ref[:, :, h * B_HD:(h + 1) * B_HD] = o[:, i * t_new:(i + 1) * t_new, :]
    nwin_ref[:, 0:w - t_new, :] = win[:, t_new:w, :]
    nwin_ref[:, w - t_new:w, :] = jnp.concatenate(kn_parts + [kvn[:, :, B_KV_HEADS * B_HD:]], axis=-1)


def _swa_sample(p3, win_all, meta_all, layer, win_prev, qnw, knw, sinks, nb):
    n, t_new, _ = p3.shape
    w = win_all.shape[2]
    full = lambda shape: pl.BlockSpec(shape, lambda i: (0,) * len(shape))
    lblk = lambda rows: pl.BlockSpec((None, nb, rows, BKV_W), lambda i: (layer, i, 0, 0))
    in_specs = [pl.BlockSpec((nb, t_new, B_HEADS * B_HD), lambda i: (i, 0, BQ_BLK)),
                pl.BlockSpec((nb, t_new, BKV_W), lambda i: (i, 0, BKV_BLK)),
                lblk(w), lblk(N_META),
                full(qnw.shape), full(knw.shape), full(sinks.shape)]
    args = [p3, p3, win_all, meta_all, qnw, knw, sinks]
    aliases = {}
    if win_prev is not None:
        in_specs.append(pl.BlockSpec(memory_space=pl.ANY))
        args.append(win_prev)
        aliases = {len(args) - 1: 1}
    return pl.pallas_call(
        _swa_sample_kernel,
        grid=(n // nb,),
        in_specs=in_specs,
        out_specs=[pl.BlockSpec((nb, t_new, B_HEADS * B_HD), lambda i: (i, 0, 0)), lblk(w)],
        out_shape=[jax.ShapeDtypeStruct((n, t_new, B_HEADS * B_HD), F32),
                   jax.ShapeDtypeStruct(win_all.shape, F32)],
        input_output_aliases=aliases,
        compiler_params=_cparams(("parallel",)),
        name="swa_sample",
    )(*args)


GDN_ROWS = 64
GDN_R = A_HEADS * GDN_ROWS
GDN_PAIRS = A_HEADS // 2
PAIR_W = 2 * A_DK
GDN_BATCH = 6
GDN_SCAN_CHUNKS = 2


def _mm(a, b):
    return jnp.dot(a.astype(BF16), b.astype(BF16), preferred_element_type=F32)


def _bmm(a, b):
    return jnp.einsum('bij,bjk->bik', a.astype(BF16), b.astype(BF16), preferred_element_type=F32)


def _block_diag2(x):
    z = jnp.zeros(x.shape[:-1] + (A_DK,), x.dtype)
    return jnp.concatenate([jnp.concatenate([x[..., :A_DK], z], axis=-1),
                            jnp.concatenate([z, x[..., A_DK:]], axis=-1)], axis=-2)


def _gdn_prep_kernel(x_ref, halo_ref, ab_ref, cw_ref, alog_ref, dtb_ref, u_ref, wq_ref, qkk_ref, gl_ref,
                     *, sb, cpb, bb, n_invalid, seq_halo):
    t = pl.program_id(1)
    ns = x_ref.shape[0]
    rows = x_ref.shape[1] // cpb
    c = GDN_ROWS
    r = GDN_R
    halo = halo_ref[...]
    if seq_halo:
        hrow = lax.broadcasted_iota(jnp.int32, (1, GDN_HALO, 1), 1) + (t * cpb * rows - GDN_HALO)
        halo = jnp.where(hrow >= n_invalid, halo, 0.0)
    cw = cw_ref[...]
    alog = alog_ref[...]
    dtb = dtb_ref[...]
    ri = lax.broadcasted_iota(jnp.int32, (r, r), 0)
    ci = lax.broadcasted_iota(jnp.int32, (r, r), 1)
    same = (ri // sb) == (ci // sb)
    incl = same & (ri >= ci)
    strict = same & (ri > ci)
    eye = (ri == ci).astype(F32)
    rc_ = lax.broadcasted_iota(jnp.int32, (c, c), 0)
    cc_ = lax.broadcasted_iota(jnp.int32, (c, c), 1)
    same_c = (rc_ // sb) == (cc_ // sb)
    lmat = jnp.concatenate([(same_c & (rc_ >= cc_)).astype(F32), same_c.astype(F32)], axis=0)
    sel = (lax.broadcasted_iota(jnp.int32, (SUBLANES, 2 * c), 1)
           == c + (lax.broadcasted_iota(jnp.int32, (SUBLANES, 2 * c), 0) * SUBLANES // sb) * sb).astype(F32)
    off = GDN_HALO - (A_CONV - 1)
    assert ns == 1 or cpb == 1
    hi = lambda m, v: jnp.einsum('bij,bjk->bik', jnp.broadcast_to(m, (bb,) + m.shape), v,
                                 precision=HIGHEST, preferred_element_type=F32)
    for c0 in range(0, cpb, bb):
        nr = bb * rows
        x = x_ref[:, c0 * rows:c0 * rows + nr, :]
        ab = ab_ref[:, c0 * rows:c0 * rows + nr, :].reshape(bb, c, LANES)
        if n_invalid:
            row0 = (t * cpb + c0) * rows
            xrow = lax.broadcasted_iota(jnp.int32, (1, nr, 1), 1) + row0
            x = jnp.where(xrow >= n_invalid, x, 0.0)
        ext = jnp.concatenate([halo, x], axis=1)
        acc = cw[0:1, :] * ext[:, off:off + nr, :]
        for k in range(1, A_CONV):
            acc = acc + cw[k:k + 1, :] * ext[:, off + k:off + k + nr, :]
        y = _silu(acc).reshape(bb, c, QKV_W)
        halo = x[:, nr - GDN_HALO:nr, :]

        sp = jnp.maximum(ab + dtb, 0.0) + jnp.log1p(jnp.exp(-jnp.abs(ab + dtb)))
        gfull = -jnp.exp(alog) * sp
        bfull = jax.nn.sigmoid(ab)
        if n_invalid:
            valid = (lax.broadcasted_iota(jnp.int32, (bb, c, 1), 0) * c
                     + lax.broadcasted_iota(jnp.int32, (bb, c, 1), 1) + row0) >= n_invalid
            gfull = jnp.where(valid, gfull, 0.0)
            bfull = jnp.where(valid, bfull, 0.0)
        bcol = jnp.concatenate([bfull[:, :, A_HEADS + h:A_HEADS + h + 1] for h in range(A_HEADS)], axis=1)
        gg = hi(lmat, gfull)
        ggt = jnp.stack([gg[i].T for i in range(bb)])
        gc = jnp.concatenate([gg[:, :c, h:h + 1] for h in range(A_HEADS)], axis=1)
        grow = jnp.concatenate([ggt[:, h:h + 1, :c] for h in range(A_HEADS)], axis=2)
        drow = jnp.concatenate([ggt[:, h:h + 1, c:] for h in range(A_HEADS)], axis=2) - grow
        g8 = jnp.exp(hi(sel, gg))
        decay = jnp.where(incl, jnp.exp(jnp.where(incl, gc - grow, 0.0)), 0.0)

        qs, ks, vs = [], [], []
        for h in range(A_HEADS):
            q = y[:, :, h * A_DK:(h + 1) * A_DK]
            k = y[:, :, A_KW + h * A_DK:A_KW + (h + 1) * A_DK]
            qs.append(q * lax.rsqrt(jnp.sum(q * q, axis=-1, keepdims=True) + EPS) * (A_DK ** -0.5))
            ks.append(k * lax.rsqrt(jnp.sum(k * k, axis=-1, keepdims=True) + EPS))
            vs.append(y[:, :, 2 * A_KW + h * A_DK:2 * A_KW + (h + 1) * A_DK])
        qr = jnp.concatenate(qs, axis=1)
        kr = jnp.concatenate(ks, axis=1)
        vr = jnp.concatenate(vs, axis=1)
        krt = jnp.stack([kr[i].T for i in range(bb)])
        kq = _bmm(jnp.concatenate([kr, qr], axis=1), krt)
        a = jnp.where(strict, kq[:, :r] * decay * bcol, 0.0)
        qk = kq[:, r:] * decay
        tinv = eye - a
        ak = a
        kpow = 2
        while kpow < sb:
            ak = _bmm(ak, ak)
            tinv = tinv + _bmm(tinv, ak)
            kpow *= 2
        eg = jnp.exp(gc)
        uw = _bmm(tinv, jnp.concatenate([vr * bcol, kr * (bcol * eg)], axis=2))
        ur = uw[:, :, :A_DK]
        wr = uw[:, :, A_DK:]
        qd = qr * eg
        kdt = krt * jnp.exp(drow)
        cs = slice(c0, c0 + bb)
        for p in range(GDN_PAIRS):
            r0 = slice(2 * p * c, (2 * p + 1) * c)
            r1 = slice((2 * p + 1) * c, (2 * p + 2) * c)
            pl_ = slice(2 * p * c, (2 * p + 2) * c)
            u_ref[cs, p] = jnp.concatenate([ur[:, r0], ur[:, r1]], axis=2)
            wq = jnp.concatenate([jnp.concatenate([wr[:, r0], wr[:, r1]], axis=2),
                                  jnp.concatenate([qd[:, r0], qd[:, r1]], axis=2)], axis=1)
            wq_ref[cs, p] = wq.astype(wq_ref.dtype)
            qkk = jnp.concatenate([qk[:, r0, pl_] + qk[:, r1, pl_], kdt[:, :, pl_]], axis=1)
            qkk_ref[cs, p] = qkk.astype(qkk_ref.dtype)
            gl_ref[cs, p] = jnp.concatenate(
                [jnp.broadcast_to(g8[:, :, 2 * p:2 * p + 1], (bb, SUBLANES, A_DK)),
                 jnp.broadcast_to(g8[:, :, 2 * p + 1:2 * p + 2], (bb, SUBLANES, A_DK))], axis=2)


def _gdn_prep(x_arr, halo_arr, halo_spec, cw, alog, dtb, ns, rows, cpb, bb, sb, n_invalid, op_dtype):
    n, l, _ = x_arr.shape
    grid = (n // ns, l // (cpb * rows))
    units = grid[0] * grid[1] * cpb
    kern = functools.partial(_gdn_prep_kernel, sb=sb, cpb=cpb, bb=bb, n_invalid=n_invalid,
                             seq_halo=halo_arr is x_arr)
    full = lambda shape: pl.BlockSpec(shape, lambda i, t: (0,) * len(shape))
    steps = grid[1]
    out_map = lambda i, t: (i * steps + t, 0, 0, 0)
    c = GDN_ROWS
    return pl.pallas_call(
        kern,
        grid=grid,
        in_specs=[pl.BlockSpec((ns, cpb * rows, QKV_W), lambda i, t: (i, t, 0)),
                  halo_spec,
                  pl.BlockSpec((ns, cpb * rows, LANES), lambda i, t: (i, t, AB_BLK)),
                  full(cw.shape), full(alog.shape), full(dtb.shape)],
        out_specs=[pl.BlockSpec((cpb, GDN_PAIRS, c, PAIR_W), out_map),
                   pl.BlockSpec((cpb, GDN_PAIRS, 2 * c, PAIR_W), out_map),
                   pl.BlockSpec((cpb, GDN_PAIRS, c + A_DK, 2 * c), out_map),
                   pl.BlockSpec((cpb, GDN_PAIRS, SUBLANES, PAIR_W), out_map)],
        out_shape=[jax.ShapeDtypeStruct((units, GDN_PAIRS, c, PAIR_W), F32),
                   jax.ShapeDtypeStruct((units, GDN_PAIRS, 2 * c, PAIR_W), op_dtype),
                   jax.ShapeDtypeStruct((units, GDN_PAIRS, c + A_DK, 2 * c), op_dtype),
                   jax.ShapeDtypeStruct((units, GDN_PAIRS, SUBLANES, PAIR_W), F32)],
        compiler_params=_cparams(("parallel", "arbitrary")),
        name="gdn_prep",
    )(x_arr, halo_arr, x_arr, cw, alog, dtb)


def _gdn_out(o_pair, z_ref, o_ref, nw, idx, p):
    for i in range(2):
        h = 2 * p + i
        hs = slice(h * A_DK, (h + 1) * A_DK)
        o = _rms(o_pair[..., i * A_DK:(i + 1) * A_DK], nw) * _silu(z_ref[idx + (hs,)])
        o_ref[idx + (hs,)] = o.astype(o_ref.dtype)


def _gdn_scan_kernel(u_ref, wq_ref, qkk_ref, gl_ref, z_ref, nw_ref, o_ref, sout_ref, s_ref):
    t = pl.program_id(0)
    n_seq = u_ref.shape[0]
    c = GDN_ROWS
    nw = nw_ref[...]

    @pl.when(t == 0)
    def _():
        s_ref[...] = jnp.zeros(s_ref.shape, F32)

    nb = n_seq * GDN_PAIRS
    s = s_ref[...].reshape(nb, A_DK, PAIR_W)
    for ci in range(u_ref.shape[1]):
        flat = lambda ref: ref[:, ci].reshape((nb,) + ref.shape[3:])
        r1 = jnp.einsum('bmk,bkn->bmn', flat(wq_ref), _block_diag2(s.astype(BF16)), preferred_element_type=F32)
        vnew = flat(u_ref) - r1[:, :c]
        r2 = jnp.einsum('bmk,bkn->bmn', flat(qkk_ref), _block_diag2(vnew.astype(BF16)),
                        preferred_element_type=F32)
        s = s * flat(gl_ref)[:, 0:1, :] + r2[:, c:]
        o = (r1[:, c:] + r2[:, :c]).reshape(n_seq, GDN_PAIRS, c, PAIR_W)
        rows = slice(ci * c, (ci + 1) * c)
        for p in range(GDN_PAIRS):
            _gdn_out(o[:, p], z_ref, o_ref, nw, (slice(None), rows), p)
    s_ref[...] = s.reshape(s_ref.shape)

    @pl.when(t == pl.num_programs(0) - 1)
    def _():
        for n in range(n_seq):
            for p in range(GDN_PAIRS):
                s = s_ref[n, p]
                sout_ref[n, 2 * p] = s[:, :A_DK]
                sout_ref[n, 2 * p + 1] = s[:, A_DK:]


def _gdn_scan(u, wq, qkk, gl, p3, nw):
    n, l, _ = p3.shape
    sshape = (n, A_HEADS, A_DK, A_DK)
    c = GDN_ROWS
    cps = GDN_SCAN_CHUNKS
    steps = l // (c * cps)
    v5 = lambda a: a.reshape((n, steps * cps) + a.shape[1:])
    u, wq, qkk, gl = v5(u), v5(wq), v5(qkk), v5(gl)
    unit = lambda a: pl.BlockSpec((n, cps) + a.shape[2:], lambda t: (0, t, 0, 0, 0))
    vw = A_HEADS * A_DK
    return pl.pallas_call(
        _gdn_scan_kernel,
        grid=(steps,),
        in_specs=[unit(u), unit(wq), unit(qkk), unit(gl),
                  pl.BlockSpec((n, cps * c, vw), lambda t: (0, t, Z_BLK)),
                  pl.BlockSpec(nw.shape, lambda t: (0, 0))],
        out_specs=[pl.BlockSpec((n, cps * c, vw), lambda t: (0, t, 0)),
                   pl.BlockSpec(sshape, lambda t: (0, 0, 0, 0))],
        out_shape=[jax.ShapeDtypeStruct((n, l, vw), BF16),
                   jax.ShapeDtypeStruct(sshape, F32)],
        scratch_shapes=[pltpu.VMEM((n, GDN_PAIRS, A_DK, PAIR_W), F32)],
        compiler_params=_cparams(("arbitrary",)),
        name="gdn_scan",
    )(u, wq, qkk, gl, p3, nw)


def _gdn_step_kernel(u_ref, wq_ref, qkk_ref, gl_ref, z_ref, s0_ref, nw_ref, *rest):
    o_ref, sout_ref = rest[-2:]
    ns, rows, _ = z_ref.shape
    c = GDN_ROWS
    nw = nw_ref[...]
    lane_seq = (lax.broadcasted_iota(jnp.int32, (A_DK, 2 * c), 1) % c) // rows
    for p in range(GDN_PAIRS):
        wq = wq_ref[0, p]
        qkk = qkk_ref[0, p]
        gl = gl_ref[0, p]
        s = jnp.concatenate([s0_ref[:, 2 * p], s0_ref[:, 2 * p + 1]], axis=2)
        lhs = jnp.concatenate([wq[:c].reshape(ns, rows, PAIR_W), wq[c:].reshape(ns, rows, PAIR_W)], axis=1)
        r1 = _bmm(lhs, _block_diag2(s.astype(BF16)))
        vnew = u_ref[0, p] - r1[:, :rows].reshape(c, PAIR_W)
        bdv = _block_diag2(vnew.astype(BF16))
        o = r1[:, rows:].reshape(c, PAIR_W) + _mm(qkk[:c], bdv)
        _gdn_out(o.reshape(ns, rows, PAIR_W), z_ref, o_ref, nw, (slice(None), slice(None)), p)
        kdt = qkk[c:]
        lhs = jnp.concatenate([jnp.where(lane_seq == s_i, kdt, 0.0) for s_i in range(ns)], axis=0)
        upd = _mm(lhs, bdv).reshape(ns, A_DK, PAIR_W)
        s_new = s * gl.reshape(ns, 1, PAIR_W) + upd
        sout_ref[:, 2 * p] = s_new[:, :, :A_DK]
        sout_ref[:, 2 * p + 1] = s_new[:, :, A_DK:]


def _gdn_step(u, wq, qkk, gl, p3, s_all, layer, s_prev, nw, ns):
    n, rows, _ = p3.shape
    unit = lambda a: pl.BlockSpec((1,) + a.shape[1:], lambda i: (i, 0, 0, 0))
    vw = A_HEADS * A_DK
    sblk = pl.BlockSpec((None, ns) + s_all.shape[2:], lambda i: (layer, i, 0, 0, 0))
    in_specs = [unit(u), unit(wq), unit(qkk), unit(gl),
                pl.BlockSpec((ns, rows, vw), lambda i: (i, 0, Z_BLK)),
                sblk,
                pl.BlockSpec(nw.shape, lambda i: (0, 0))]
    args = [u, wq, qkk, gl, p3, s_all, nw]
    aliases = {}
    if s_prev is not None:
        in_specs.append(pl.BlockSpec(memory_space=pl.ANY))
        args.append(s_prev)
        aliases = {len(args) - 1: 1}
    return pl.pallas_call(
        _gdn_step_kernel,
        grid=(n // ns,),
        in_specs=in_specs,
        out_specs=[pl.BlockSpec((ns, rows, vw), lambda i: (i, 0, 0)), sblk],
        out_shape=[jax.ShapeDtypeStruct((n, rows, vw), F32),
                   jax.ShapeDtypeStruct(s_all.shape, F32)],
        input_output_aliases=aliases,
        compiler_params=_cparams(("parallel",)),
        name="gdn_step",
    )(*args)


def _gdn(p3, state, hist, cw, alog, dtb, nw, cpb, n_invalid):
    n, l, _ = p3.shape
    c = GDN_ROWS
    if hist is None:
        steps8 = cpb * c // GDN_HALO
        halo_spec = pl.BlockSpec((1, GDN_HALO, QKV_W), lambda i, t: (i, jnp.maximum(t * steps8 - 1, 0), 0))
        u, wq, qkk, gl = _gdn_prep(p3, p3, halo_spec, cw, alog, dtb, 1, c, cpb, min(cpb, GDN_BATCH), c, n_invalid, BF16)
        return _gdn_scan(u, wq, qkk, gl, p3, nw)
    ns = c // l
    halo_spec = pl.BlockSpec((ns, GDN_HALO, QKV_W), lambda i, t: (i, 0, 0))
    u, wq, qkk, gl = _gdn_prep(p3, hist, halo_spec, cw, alog, dtb, ns, l, 1, 1, l, n_invalid, F32)
    s_all, layer, s_prev = state
    return _gdn_step(u, wq, qkk, gl, p3, s_all, layer, s_prev, nw, ns)


def _merge_kernel(oa_ref, ob_ref, oc_ref, gl_ref, x_ref, wb_ref, wo_ref, out_ref):
    d = x_ref.shape[1]
    merged = None
    for r, o_ref in enumerate((oa_ref, ob_ref, oc_ref)):
        pb = jnp.dot(o_ref[...].astype(BF16), wb_ref[r], preferred_element_type=F32)
        term = jax.nn.sigmoid(gl_ref[:, r * d:(r + 1) * d]) * pb
        merged = term if merged is None else merged + term
    out_ref[...] = x_ref[...] + jnp.dot(merged.astype(BF16), wo_ref[...], preferred_element_type=F32)


def _merge(oa, ob, oc, p2, x2, wb, wo, tm):
    t, d = x2.shape
    bw = oa.shape[1]
    row = lambda w, blk: pl.BlockSpec((tm, w), lambda i: (i, blk))
    return pl.pallas_call(
        _merge_kernel,
        grid=(t // tm,),
        in_specs=[row(bw, 0), row(bw, 0), row(bw, 0), row(3 * d, GL_BLK), row(d, 0),
                  pl.BlockSpec(wb.shape, lambda i: (0, 0, 0)),
                  pl.BlockSpec(wo.shape, lambda i: (0, 0))],
        out_specs=row(d, 0),
        out_shape=jax.ShapeDtypeStruct((t, d), F32),
        compiler_params=_cparams(("parallel",)),
        name="merge",
    )(oa, ob, oc, p2, x2, wb, wo)


def _route(logits):
    lane = lax.broadcasted_iota(jnp.int32, logits.shape, 1)
    is_g = (lane >= N_EXPERTS) & (lane < N_EXPERTS + N_GROUPS)
    gl = jnp.where(is_g, logits, NEG)
    gmax = jnp.max(gl, axis=-1, keepdims=True)
    gidx = jnp.min(jnp.where(gl == gmax, lane - N_EXPERTS, LANES), axis=-1, keepdims=True)
    gw = 1.0 / jnp.sum(jnp.where(is_g, jnp.exp(gl - gmax), 0.0), axis=-1, keepdims=True)
    in_grp = (lane < N_EXPERTS) & ((lane // EXP_PER_GROUP) == gidx)
    el = jnp.where(in_grp, logits, NEG)
    v1 = jnp.max(el, axis=-1, keepdims=True)
    i1 = jnp.min(jnp.where(el == v1, lane, LANES), axis=-1, keepdims=True)
    el2 = jnp.where(lane == i1, NEG, el)
    v2 = jnp.max(el2, axis=-1, keepdims=True)
    i2 = jnp.min(jnp.where(el2 == v2, lane, LANES), axis=-1, keepdims=True)
    e21 = jnp.exp(v2 - v1)
    w1 = gw / (1.0 + e21)
    w2 = gw * e21 / (1.0 + e21)
    return jnp.where(lane == i1, w1, 0.0) + jnp.where(lane == i2, w2, 0.0)


def _moe_kernel(x_ref, nw_ref, wr_ref, br_ref, wg_ref, wu_ref, wd_ref, out_ref, h_ref, cmb_ref):
    g = pl.program_id(1)

    @pl.when(g == 0)
    def _():
        x = x_ref[...]
        h = _rms(x, nw_ref[...])
        h_hi = h.astype(BF16)
        h_ref[...] = h_hi
        h_lo = (h - h_hi.astype(F32)).astype(BF16)
        l2 = jnp.dot(h_hi, wr_ref[...], preferred_element_type=F32)
        logits = (l2[:, :LANES] + l2[:, LANES:]
                  + jnp.dot(h_lo, wr_ref[:, :LANES], preferred_element_type=F32) + br_ref[...])
        cmb_ref[...] = _route(logits)
        out_ref[...] = x

    h = h_ref[...]
    cmb = cmb_ref[...]
    lane = lax.broadcasted_iota(jnp.int32, cmb.shape, 1)
    parts = []
    for e in range(EXP_PER_GROUP):
        ce = jnp.sum(jnp.where(lane == g * EXP_PER_GROUP + e, cmb, 0.0), axis=-1, keepdims=True)
        hid = (_silu(jnp.dot(h, wg_ref[e], preferred_element_type=F32))
               * jnp.dot(h, wu_ref[e], preferred_element_type=F32))
        parts.append((hid * ce).astype(BF16))
    wd = wd_ref[...].reshape(-1, wd_ref.shape[-1])
    out_ref[...] += jnp.dot(jnp.concatenate(parts, axis=1), wd, preferred_element_type=F32)


def _moe(x2, nw, wr, br, wg, wu, wd, tm):
    t, d = x2.shape
    ne, _, de = wg.shape
    return pl.pallas_call(
        _moe_kernel,
        grid=(t // tm, ne // EXP_PER_GROUP),
        in_specs=[pl.BlockSpec((tm, d), lambda i, g: (i, 0)),
                  pl.BlockSpec((1, d), lambda i, g: (0, 0)),
                  pl.BlockSpec(wr.shape, lambda i, g: (0, 0)),
                  pl.BlockSpec(br.shape, lambda i, g: (0, 0)),
                  pl.BlockSpec((EXP_PER_GROUP, d, de), lambda i, g: (g, 0, 0)),
                  pl.BlockSpec((EXP_PER_GROUP, d, de), lambda i, g: (g, 0, 0)),
                  pl.BlockSpec((EXP_PER_GROUP, de, d), lambda i, g: (g, 0, 0))],
        out_specs=pl.BlockSpec((tm, d), lambda i, g: (i, 0)),
        out_shape=jax.ShapeDtypeStruct((t, d), F32),
        scratch_shapes=[pltpu.VMEM((tm, d), BF16), pltpu.VMEM((tm, LANES), F32)],
        compiler_params=_cparams(("parallel", "arbitrary")),
        name="moe",
    )(x2, nw, wr, br, wg, wu, wd)


def _pad_lanes(v, width):
    v = v.reshape(1, -1)
    return jnp.pad(v, ((0, 0), (0, width - v.shape[1])))


def _layer_weights(l, norm1_w, norm2_w, w_in, gdn_conv_w, gdn_a_log, gdn_dt_bias, gdn_norm_w,
                   swa_q_norm_w, swa_k_norm_w, swa_sinks, conv_dw_w, conv_dw_b, conv_ln_w, conv_ln_b,
                   w_branch, w_out, router_group_w, router_group_b, router_expert_w, router_expert_b,
                   moe_w_gate, moe_w_up, moe_w_down):
    d = w_in.shape[1]
    wi = w_in[l]
    o_ab = 4 * A_KW
    o_bq = o_ab + 2 * A_HEADS
    o_cu = o_bq + B_HEADS * B_HD + BKV_W
    o_gl = o_cu + 2 * C_CH
    w_perm = jnp.concatenate(
        [wi[:, :o_ab], wi[:, o_cu:o_gl], wi[:, o_gl:], wi[:, o_bq:o_cu], wi[:, o_ab:o_bq],
         jnp.zeros((d, LANES - 2 * A_HEADS), wi.dtype)], axis=1).astype(BF16)
    assert w_perm.shape[1] == PROJ_W
    wr = jnp.concatenate([router_expert_w[l], router_group_w[l],
                          jnp.zeros((d, LANES - N_EXPERTS - N_GROUPS), F32)], axis=1)
    br = _pad_lanes(jnp.concatenate([router_expert_b[l], router_group_b[l]]), LANES)
    wr_hi = wr.astype(BF16)
    wr = jnp.concatenate([wr_hi, (wr - wr_hi.astype(F32)).astype(BF16)], axis=1)
    return dict(
        n1=norm1_w[l].reshape(1, d), n2=norm2_w[l].reshape(1, d), w_in=w_perm,
        gdn_cw=gdn_conv_w[l], alog=_pad_lanes(gdn_a_log[l], LANES), dtb=_pad_lanes(gdn_dt_bias[l], LANES),
        gdn_nw=gdn_norm_w[l].reshape(1, -1),
        qnw=swa_q_norm_w[l].reshape(1, -1), knw=swa_k_norm_w[l].reshape(1, -1), sinks=swa_sinks[l].reshape(1, -1),
        qnw_heads=jnp.tile(swa_q_norm_w[l].reshape(1, -1), (1, B_HEADS)),
        knw_heads=jnp.tile(swa_k_norm_w[l].reshape(1, -1), (1, B_KV_HEADS)),
        dww=conv_dw_w[l], dwb=conv_dw_b[l].reshape(1, -1), lnw=conv_ln_w[l].reshape(1, -1),
        lnb=conv_ln_b[l].reshape(1, -1),
        wb=w_branch[l].astype(BF16), wo=w_out[l].astype(BF16), wr=wr, br=br,
        wg=moe_w_gate[l].astype(BF16), wu=moe_w_up[l].astype(BF16), wd=moe_w_down[l].astype(BF16))


def _tiles(n, l):
    prompt = l > BLOCK
    t = n * l
    if prompt:
        return dict(proj_tm=1536, tok_tm=512, moe_tm=1056, conf_nb=1, conf_tm=384, conf_rc=64,
                    gdn_cpb=6, swa_bps=11, n_invalid=META_PAD, branch_dtype=BF16)
    return dict(proj_tm=t, tok_tm=512, moe_tm=t, conf_nb=16, conf_tm=l, conf_rc=l,
                gdn_cpb=1, n_invalid=0, branch_dtype=F32)


def _layer(x3, lw, gdn_state, gdn_hist, conf_hist, swa_cache):
    n, l, d = x3.shape
    tl = _tiles(n, l)
    x2 = x3.reshape(n * l, d)
    p2 = _inproj(x2, lw['n1'], lw['w_in'], tl['proj_tm'], PROJ_W // 5)
    p3 = p2.reshape(n, l, PROJ_W)
    o_a, s_new = _gdn(p3, gdn_state, gdn_hist, lw['gdn_cw'], lw['alog'], lw['dtb'], lw['gdn_nw'],
                      tl['gdn_cpb'], tl['n_invalid'])
    gh_new = p3[:, l - (A_CONV - 1):, :QKV_W]
    if swa_cache is None:
        o_b, kvn = _swa_prompt(p3, lw['qnw_heads'], lw['knw_heads'], lw['sinks'], tl['swa_bps'])
        kv = lambda sl: kvn[:, sl].reshape(n, -1, 2, B_KV_HEADS, B_HD)
        swa_new = (kv(slice(META_PAD, BLOCK)), kv(slice(l - WINDOW, l)))
    else:
        win_all, meta_all, layer, win_prev = swa_cache
        o_b, swa_new = _swa_sample(p3, win_all, meta_all, layer, win_prev, lw['qnw'], lw['knw'], lw['sinks'], 16)
    o_c, ch_new = _conformer(p3, conf_hist, lw['dww'], lw['dwb'], lw['lnw'], lw['lnb'],
                             tl['conf_nb'], tl['conf_tm'], tl['conf_rc'], tl['n_invalid'], tl['branch_dtype'])
    t = n * l
    x2 = _merge(o_a.reshape(t, -1), o_b.reshape(t, -1), o_c.reshape(t, -1), p2, x2, lw['wb'], lw['wo'],
                tl['tok_tm'])
    x2 = _moe(x2, lw['n2'], lw['wr'], lw['br'], lw['wg'], lw['wu'], lw['wd'], tl['moe_tm'])
    return x2.reshape(n, l, d), s_new, gh_new, swa_new, ch_new


def kernel(x_prompt, x_sample, state_gdn, cache_gdn_conv, cache_swa_kv, cache_meta_kv, cache_conv, meta_tokens, norm1_w, norm2_w, w_in, gdn_conv_w, gdn_a_log, gdn_dt_bias, gdn_norm_w, swa_q_norm_w, swa_k_norm_w, swa_sinks, conv_dw_w, conv_dw_b, conv_ln_w, conv_ln_b, w_branch, w_out, router_group_w, router_group_b, router_expert_w, router_expert_b, moe_w_gate, moe_w_up, moe_w_down):
    dtp = x_prompt.dtype
    n_p, _, d = x_prompt.shape
    n_s = x_sample.shape[0]
    depth = w_in.shape[0]
    xp = jnp.pad(x_prompt, ((0, 0), (BLOCK, 0), (0, 0)))
    xp = lax.dynamic_update_slice(
        xp, jnp.broadcast_to(meta_tokens.astype(dtp)[None], (n_p, N_META, d)), (0, META_PAD, 0))
    xs = x_sample
    state_s = None
    win_s = None
    win_all = cache_swa_kv.reshape(depth, n_s, -1, BKV_W)
    meta_all = cache_meta_kv.reshape(depth, n_s, N_META, BKV_W)
    outs_p, outs_s = [], []
    for l in range(depth):
        lw = _layer_weights(l, norm1_w, norm2_w, w_in, gdn_conv_w, gdn_a_log, gdn_dt_bias, gdn_norm_w,
                            swa_q_norm_w, swa_k_norm_w, swa_sinks, conv_dw_w, conv_dw_b, conv_ln_w, conv_ln_b,
                            w_branch, w_out, router_group_w, router_group_b, router_expert_w, router_expert_b,
                            moe_w_gate, moe_w_up, moe_w_down)
        xp, s_p, gh_p, (mkv_p, wkv_p), ch_p = _layer(
            xp, lw, None, None, jnp.zeros((n_p, C_CONV - 1, C_CH), dtp), None)
        outs_p.append((s_p, gh_p, wkv_p, mkv_p, ch_p))
        gh0 = jnp.pad(cache_gdn_conv[l], ((0, 0), (GDN_HALO - (A_CONV - 1), 0), (0, 0)))
        xs, state_s, gh_s, win_s, ch_s = _layer(xs, lw, (state_gdn, l, state_s), gh0, cache_conv[l],
                                                (win_all, meta_all, l, win_s))
        outs_s.append((gh_s, ch_s))
    stack = lambda outs, i: jnp.stack([o[i] for o in outs])
    return (xp[:, BLOCK:], xs,
            stack(outs_p, 0), stack(outs_p, 1), stack(outs_p, 2), stack(outs_p, 3), stack(outs_p, 4),
            state_s, stack(outs_s, 0), win_s.reshape(cache_swa_kv.shape), stack(outs_s, 1))
```

```python
import functools

import jax
import jax.numpy as jnp
from jax import lax
from jax.experimental import pallas as pl
from jax.experimental.pallas import tpu as pltpu

F32 = jnp.float32
BF16 = jnp.bfloat16
HIGHEST = lax.Precision.HIGHEST
EPS = 1e-6
NEG = -1e30

VMEM_LIMIT_BYTES = 56 * 1024 * 1024
LANES = 128
SUBLANES = 8

PAST_LEN = 16384
N_META = 16
BLOCK = 128
WINDOW = 128
META_PAD = BLOCK - N_META
A_HEADS = 4
A_DK = 128
A_CONV = 4
B_HEADS = 8
B_KV_HEADS = 2
B_GROUP = B_HEADS // B_KV_HEADS
B_HD = 64
C_CH = 512
C_CONV = 31
BRANCH_W = 512
N_GROUPS = 4
EXP_PER_GROUP = 4
N_EXPERTS = N_GROUPS * EXP_PER_GROUP

A_KW = A_HEADS * A_DK
QKV_W = 3 * A_KW
Z_BLK = QKV_W // BRANCH_W
CU_BLK = 2
GL_BLK = 1
BQ_BLK = 12
BKV_W = 2 * B_KV_HEADS * B_HD
BKV_BLK = 26
AB_BLK = 54
PROJ_W = 7040
CONF_HALO = 32
GDN_HALO = SUBLANES


def _rms(x, w):
    return x * lax.rsqrt(jnp.mean(x * x, axis=-1, keepdims=True) + EPS) * w


def _silu(x):
    return x * jax.nn.sigmoid(x)


def _cparams(sem):
    return pltpu.CompilerParams(dimension_semantics=sem, vmem_limit_bytes=VMEM_LIMIT_BYTES)


def _inproj_kernel(x_ref, nw_ref, w_ref, o_ref, h_ref):
    @pl.when(pl.program_id(1) == 0)
    def _():
        h_ref[...] = _rms(x_ref[...], nw_ref[...]).astype(BF16)

    o_ref[...] = jnp.dot(h_ref[...], w_ref[...], preferred_element_type=F32)


def _inproj(x2, nw, w, tm, tn):
    t, d = x2.shape
    wd = w.shape[1]
    return pl.pallas_call(
        _inproj_kernel,
        grid=(t // tm, wd // tn),
        in_specs=[pl.BlockSpec((tm, d), lambda i, j: (i, 0)),
                  pl.BlockSpec((1, d), lambda i, j: (0, 0)),
                  pl.BlockSpec((d, tn), lambda i, j: (0, j))],
        out_specs=pl.BlockSpec((tm, tn), lambda i, j: (i, j)),
        out_shape=jax.ShapeDtypeStruct((t, wd), F32),
        scratch_shapes=[pltpu.VMEM((tm, d), BF16)],
        compiler_params=_cparams(("parallel", "arbitrary")),
        name="inproj",
    )(x2, nw, w)


def _conf_kernel(cu_ref, hist_ref, dww_ref, dwb_ref, lnw_ref, lnb_ref, y_ref, nh_ref, e_ref, c_ref,
                 *, tm, rc, n_invalid):
    t = pl.program_id(1)
    nb = cu_ref.shape[0]
    halo = CONF_HALO

    @pl.when(t == 0)
    def _():
        e_ref[:, 0:SUBLANES, :] = jnp.zeros((nb, SUBLANES, C_CH), F32)
        e_ref[:, halo + tm:halo + tm + SUBLANES, :] = jnp.zeros((nb, SUBLANES, C_CH), F32)
        e_ref[:, halo - (C_CONV - 1):halo, :] = hist_ref[...]

    @pl.when(t > 0)
    def _():
        e_ref[:, 0:halo, :] = e_ref[:, tm:tm + halo, :]

    cu = cu_ref[...]
    u = cu[:, :, :C_CH] * jax.nn.sigmoid(cu[:, :, C_CH:])
    if n_invalid:
        row = lax.broadcasted_iota(jnp.int32, (1, tm, 1), 1) + t * tm
        u = jnp.where(row >= n_invalid, u, 0.0)
    e_ref[:, halo:halo + tm, :] = u
    nh_ref[...] = e_ref[:, halo + tm - (C_CONV - 1):halo + tm, :]

    off = halo - (C_CONV - 1)
    for c in range(C_CH // LANES):
        cs = slice(c * LANES, (c + 1) * LANES)

        def rbody(r, carry, cs=cs):
            r0 = pl.multiple_of(r * rc, rc)
            blk = e_ref[:, pl.ds(r0, rc + halo + SUBLANES), cs]
            acc = jnp.zeros((nb, rc, LANES), F32)
            for s in range(SUBLANES):
                z = None
                for a in range((halo + SUBLANES) // SUBLANES):
                    k = SUBLANES * a + s - off
                    if 0 <= k < C_CONV:
                        term = dww_ref[k:k + 1, cs] * blk[:, SUBLANES * a:SUBLANES * a + rc + SUBLANES, :]
                        z = term if z is None else z + term
                acc = acc + z[:, s:s + rc, :]
            c_ref[:, pl.ds(r0, rc), cs] = acc + dwb_ref[:, cs]
            return carry

        lax.fori_loop(0, tm // rc, rbody, 0)

    y = c_ref[...]
    mu = jnp.mean(y, axis=-1, keepdims=True)
    var = jnp.mean(jnp.square(y - mu), axis=-1, keepdims=True)
    yn = (y - mu) * lax.rsqrt(var + EPS) * lnw_ref[...] + lnb_ref[...]
    y_ref[...] = _silu(yn).astype(y_ref.dtype)


def _conformer(p3, hist, dww, dwb, lnw, lnb, nb, tm, rc, n_invalid, out_dtype):
    n, l, _ = p3.shape
    kern = functools.partial(_conf_kernel, tm=tm, rc=rc, n_invalid=n_invalid)
    full = lambda shape: pl.BlockSpec(shape, lambda i, t: (0,) * len(shape))
    return pl.pallas_call(
        kern,
        grid=(n // nb, l // tm),
        in_specs=[pl.BlockSpec((nb, tm, 2 * C_CH), lambda i, t: (i, t, CU_BLK)),
                  pl.BlockSpec((nb, C_CONV - 1, C_CH), lambda i, t: (i, 0, 0)),
                  full(dww.shape), full(dwb.shape), full(lnw.shape), full(lnb.shape)],
        out_specs=[pl.BlockSpec((nb, tm, C_CH), lambda i, t: (i, t, 0)),
                   pl.BlockSpec((nb, C_CONV - 1, C_CH), lambda i, t: (i, 0, 0))],
        out_shape=[jax.ShapeDtypeStruct((n, l, C_CH), out_dtype),
                   jax.ShapeDtypeStruct((n, C_CONV - 1, C_CH), F32)],
        scratch_shapes=[pltpu.VMEM((nb, CONF_HALO + tm + SUBLANES, C_CH), F32), pltpu.VMEM((nb, tm, C_CH), F32)],
        compiler_params=_cparams(("parallel", "arbitrary")),
        name="conformer",
    )(p3, hist, dww, dwb, lnw, lnb)


SWA_ROWS = 512


def _softmax_sink_pv(s, mask, sink, v, dims):
    s = jnp.where(mask, s, NEG)
    m = jnp.maximum(jnp.max(s, axis=-1, keepdims=True), sink)
    p = jnp.exp(s - m)
    den = jnp.sum(p, axis=-1, keepdims=True) + jnp.exp(sink - m)
    return lax.dot_general(p, v, dims, preferred_element_type=F32) / den


def _swa_prompt_kernel(q_ref, kv_ref, kvp_ref, kvm_ref, qnw_ref, knw_ref, sink_ref, o_ref, kvn_ref, *, bps):
    step = pl.program_id(1)
    qnw = qnw_ref[...]
    knw = knw_ref[...]
    qw = B_HEADS * B_HD
    kw = B_KV_HEADS * B_HD
    nk = N_META + 2 * BLOCK
    nke = nk + SUBLANES
    rows = B_GROUP * BLOCK
    r = lax.broadcasted_iota(jnp.int32, (rows, nke), 0) % BLOCK
    c = lax.broadcasted_iota(jnp.int32, (rows, nke), 1)
    cp = c - N_META
    co = cp - BLOCK
    hrow = lax.broadcasted_iota(jnp.int32, (rows, 1), 0) // BLOCK
    m_meta = c < N_META
    m_own = (co >= 0) & (co <= r) & (c < nk)
    m_prev = (cp >= 0) & (cp < BLOCK) & (cp > r)
    bias0 = jnp.where(m_meta & (META_PAD + c <= r), 0.0, NEG)
    bias1 = jnp.where(m_meta | m_own, 0.0, NEG)
    bias2 = jnp.where(m_meta | m_own | m_prev, 0.0, NEG)
    sinks = []
    for g in range(B_KV_HEADS):
        sink = jnp.zeros((rows, 1), F32)
        for i in range(B_GROUP):
            h = g * B_GROUP + i
            sink = jnp.where(hrow == i, sink_ref[0:1, h:h + 1], sink)
        sinks.append(sink)
    hb_q = (lax.broadcasted_iota(jnp.int32, (qw, qw), 0) // B_HD
            == lax.broadcasted_iota(jnp.int32, (qw, qw), 1) // B_HD).astype(BF16)
    hb_k = hb_q[:kw, :kw]
    half = lax.broadcasted_iota(jnp.int32, (1, LANES), 1) // B_HD

    def head_rms(x, hb, w):
        sq = x * x
        hi = sq.astype(BF16)
        lo = (sq - hi.astype(F32)).astype(BF16)
        ss = jnp.dot(hi, hb, preferred_element_type=F32) + jnp.dot(lo, hb, preferred_element_type=F32)
        return x * lax.rsqrt(ss * (1.0 / B_HD) + EPS) * w

    kvm = kvm_ref[META_PAD:BLOCK, :]
    km = head_rms(kvm[:, :kw], hb_k, knw)
    kv_halo = kvp_ref[...]
    q = q_ref[...]
    kv = kv_ref[...]
    k_own = head_rms(kv[:, :kw], hb_k, knw)
    kvn_ref[...] = jnp.concatenate([k_own, kv[:, kw:]], axis=-1)
    qn = head_rms(q, hb_q, qnw)
    tq = bps * BLOCK
    blocks = lambda x: x.reshape(bps, BLOCK, x.shape[-1])
    k_prev = jnp.concatenate([head_rms(kv_halo[:, :kw], hb_k, knw), k_own[:tq - BLOCK]], axis=0)
    v_prev = jnp.concatenate([kv_halo[:, kw:], kv[:tq - BLOCK, kw:]], axis=0)
    rep = lambda x: jnp.broadcast_to(x[None], (bps,) + x.shape)
    zpad = jnp.zeros((bps, SUBLANES, kw), F32)
    kcat_b = jnp.concatenate([rep(km), blocks(k_prev), blocks(k_own), zpad], axis=1).astype(BF16)
    vcat_b = jnp.concatenate([rep(kvm[:, kw:]), blocks(v_prev), blocks(kv[:, kw:]), zpad], axis=1).astype(BF16)
    ones_b = jnp.ones((bps, nke, kw), BF16)
    bias = jnp.stack([jnp.where(step * bps + b >= 2, bias2, jnp.where(step * bps + b >= 1, bias1, bias0))
                      for b in range(bps)])
    tiles = [qn[:, m * LANES:(m + 1) * LANES] for m in range(qw // LANES)]
    swapped = [pltpu.roll(tl_, B_HD, 1) for tl_ in tiles]
    outs = [None] * B_HEADS
    for g in range(B_KV_HEADS):
        in_g = half == g
        parts = []
        for i in range(B_GROUP):
            h = g * B_GROUP + i
            src = tiles[h // 2] if h % 2 == g else swapped[h // 2]
            parts.append(blocks(jnp.where(in_g, src, 0.0)))
        lhs = jnp.concatenate(parts, axis=1).astype(BF16)
        bias_g = jnp.where(c == nk, sinks[g], bias)
        chunks = []
        for r0 in range(0, rows, SWA_ROWS):
            rc = slice(r0, r0 + SWA_ROWS)
            s = jnp.einsum('bqd,bkd->bqk', lhs[:, rc], kcat_b, preferred_element_type=F32)
            s = s * (B_HD ** -0.5) + bias_g[:, rc]
            p = jnp.exp(s - jnp.max(s, axis=-1, keepdims=True)).astype(BF16)
            den = jnp.einsum('bqk,bkd->bqd', p, ones_b, preferred_element_type=F32)
            chunks.append(jnp.einsum('bqk,bkd->bqd', p, vcat_b, preferred_element_type=F32) / den)
        res = jnp.concatenate(chunks, axis=1)
        for i in range(B_GROUP):
            h = g * B_GROUP + i
            part = res[:, i * BLOCK:(i + 1) * BLOCK, :].reshape(tq, LANES)
            outs[h] = part if h % 2 == g else pltpu.roll(part, B_HD, 1)
    for m in range(qw // LANES):
        o_ref[:, m * LANES:(m + 1) * LANES] = jnp.where(half == 0, outs[2 * m], outs[2 * m + 1]).astype(o_ref.dtype)


def _swa_prompt(p3, qnw, knw, sinks, bps):
    n, l, _ = p3.shape
    full = lambda shape: pl.BlockSpec(shape, lambda i, j: (0,) * len(shape))
    tq = bps * BLOCK
    return pl.pallas_call(
        functools.partial(_swa_prompt_kernel, bps=bps),
        grid=(n, l // tq),
        in_specs=[pl.BlockSpec((None, tq, B_HEADS * B_HD), lambda i, j: (i, j, BQ_BLK)),
                  pl.BlockSpec((None, tq, BKV_W), lambda i, j: (i, j, BKV_BLK)),
                  pl.BlockSpec((None, BLOCK, BKV_W), lambda i, j: (i, jnp.maximum(j * bps - 1, 0), BKV_BLK)),
                  pl.BlockSpec((None, BLOCK, BKV_W), lambda i, j: (i, 0, BKV_BLK)),
                  full(qnw.shape), full(knw.shape), full(sinks.shape)],
        out_specs=[pl.BlockSpec((None, tq, B_HEADS * B_HD), lambda i, j: (i, j, 0)),
                   pl.BlockSpec((None, tq, BKV_W), lambda i, j: (i, j, 0))],
        out_shape=[jax.ShapeDtypeStruct((n, l, B_HEADS * B_HD), BF16),
                   jax.ShapeDtypeStruct((n, l, BKV_W), F32)],
        compiler_params=_cparams(("parallel", "arbitrary")),
        name="swa_prompt",
    )(p3, p3, p3, p3, qnw, knw, sinks)


def _swa_sample_kernel(q_ref, kvn_ref, win_ref, meta_ref, qnw_ref, knw_ref, sink_ref, *rest):
    o_ref, nwin_ref = rest[-2:]
    q = q_ref[...]
    kvn = kvn_ref[...]
    win = win_ref[...]
    meta = meta_ref[...]
    qnw = qnw_ref[...]
    knw = knw_ref[...]
    t_new = q.shape[1]
    w = win.shape[1]
    nk = N_META + w + t_new
    rows = B_GROUP * t_new
    tq = lax.broadcasted_iota(jnp.int32, (1, rows, nk), 1) % t_new
    c = lax.broadcasted_iota(jnp.int32, (1, rows, nk), 2)
    cw = c - N_META
    win_ok = (cw >= 0) & (cw < w) & (cw - w > tq - WINDOW) & (cw + (PAST_LEN - w) >= N_META)
    cn = cw - w
    new_ok = (cn >= 0) & (cn <= tq) & (cn > tq - WINDOW)
    mask = (c < N_META) | win_ok | new_ok
    hrow = lax.broadcasted_iota(jnp.int32, (1, rows, 1), 1) // t_new
    kn_parts = []
    for g in range(B_KV_HEADS):
        ks = slice(g * B_HD, (g + 1) * B_HD)
        vs = slice((B_KV_HEADS + g) * B_HD, (B_KV_HEADS + g + 1) * B_HD)
        k_new = _rms(kvn[:, :, ks], knw)
        kn_parts.append(k_new)
        kk = jnp.concatenate([meta[:, :, ks], win[:, :, ks], k_new], axis=1)
        vv = jnp.concatenate([meta[:, :, vs], win[:, :, vs], kvn[:, :, vs]], axis=1)
        qq = jnp.concatenate(
            [_rms(q[:, :, (g * B_GROUP + i) * B_HD:(g * B_GROUP + i + 1) * B_HD], qnw) for i in range(B_GROUP)],
            axis=1)
        sink = jnp.zeros((1, rows, 1), F32)
        for i in range(B_GROUP):
            h = g * B_GROUP + i
            sink = jnp.where(hrow == i, sink_ref[0:1, h:h + 1], sink)
        s = jnp.einsum('bqd,bkd->bqk', qq, kk, preferred_element_type=F32) * (B_HD ** -0.5)
        o = _softmax_sink_pv(s, mask, sink, vv, (((2,), (1,)), ((0,), (0,))))
        for i in range(B_GROUP):
            h = g * B_GROUP + i
            o_ref[:, :, h * B_HD:(h + 1) * B_HD] = o[:, i * t_new:(i + 1) * t_new, :]
    nwin_ref[:, 0:w - t_new, :] = win[:, t_new:w, :]
    nwin_ref[:, w - t_new:w, :] = jnp.concatenate(kn_parts + [kvn[:, :, B_KV_HEADS * B_HD:]], axis=-1)


def _swa_sample(p3, win_all, meta_all, layer, win_prev, qnw, knw, sinks, nb):
    n, t_new, _ = p3.shape
    w = win_all.shape[2]
    full = lambda shape: pl.BlockSpec(shape, lambda i: (0,) * len(shape))
    lblk = lambda rows: pl.BlockSpec((None, nb, rows, BKV_W), lambda i: (layer, i, 0, 0))
    in_specs = [pl.BlockSpec((nb, t_new, B_HEADS * B_HD), lambda i: (i, 0, BQ_BLK)),
                pl.BlockSpec((nb, t_new, BKV_W), lambda i: (i, 0, BKV_BLK)),
                lblk(w), lblk(N_META),
                full(qnw.shape), full(knw.shape), full(sinks.shape)]
    args = [p3, p3, win_all, meta_all, qnw, knw, sinks]
    aliases = {}
    if win_prev is not None:
        in_specs.append(pl.BlockSpec(memory_space=pl.ANY))
        args.append(win_prev)
        aliases = {len(args) - 1: 1}
    return pl.pallas_call(
        _swa_sample_kernel,
        grid=(n // nb,),
        in_specs=in_specs,
        out_specs=[pl.BlockSpec((nb, t_new, B_HEADS * B_HD), lambda i: (i, 0, 0)), lblk(w)],
        out_shape=[jax.ShapeDtypeStruct((n, t_new, B_HEADS * B_HD), F32),
                   jax.ShapeDtypeStruct(win_all.shape, F32)],
        input_output_aliases=aliases,
        compiler_params=_cparams(("parallel",)),
        name="swa_sample",
    )(*args)


GDN_ROWS = 64
GDN_R = A_HEADS * GDN_ROWS
GDN_PAIRS = A_HEADS // 2
PAIR_W = 2 * A_DK
GDN_BATCH = 6
GDN_SCAN_CHUNKS = 6


def _mm(a, b):
    return jnp.dot(a.astype(BF16), b.astype(BF16), preferred_element_type=F32)


def _bmm(a, b):
    return jnp.einsum('bij,bjk->bik', a.astype(BF16), b.astype(BF16), preferred_element_type=F32)


def _block_diag2(x):
    z = jnp.zeros(x.shape[:-1] + (A_DK,), x.dtype)
    return jnp.concatenate([jnp.concatenate([x[..., :A_DK], z], axis=-1),
                            jnp.concatenate([z, x[..., A_DK:]], axis=-1)], axis=-2)


def _gdn_prep_kernel(x_ref, halo_ref, ab_ref, cw_ref, alog_ref, dtb_ref, u_ref, wq_ref, qkk_ref, gl_ref,
                     *, sb, cpb, bb, n_invalid, seq_halo):
    t = pl.program_id(1)
    ns = x_ref.shape[0]
    rows = x_ref.shape[1] // cpb
    c = GDN_ROWS
    r = GDN_R
    halo = halo_ref[...]
    if seq_halo:
        hrow = lax.broadcasted_iota(jnp.int32, (1, GDN_HALO, 1), 1) + (t * cpb * rows - GDN_HALO)
        halo = jnp.where(hrow >= n_invalid, halo, 0.0)
    cw = cw_ref[...]
    alog = alog_ref[...]
    dtb = dtb_ref[...]
    ri = lax.broadcasted_iota(jnp.int32, (r, r), 0)
    ci = lax.broadcasted_iota(jnp.int32, (r, r), 1)
    same = (ri // sb) == (ci // sb)
    incl = same & (ri >= ci)
    strict = same & (ri > ci)
    eye = (ri == ci).astype(F32)
    rc_ = lax.broadcasted_iota(jnp.int32, (c, c), 0)
    cc_ = lax.broadcasted_iota(jnp.int32, (c, c), 1)
    same_c = (rc_ // sb) == (cc_ // sb)
    lmat = jnp.concatenate([(same_c & (rc_ >= cc_)).astype(F32), same_c.astype(F32)], axis=0)
    sel = (lax.broadcasted_iota(jnp.int32, (SUBLANES, 2 * c), 1)
           == c + (lax.broadcasted_iota(jnp.int32, (SUBLANES, 2 * c), 0) * SUBLANES // sb) * sb).astype(F32)
    off = GDN_HALO - (A_CONV - 1)
    assert ns == 1 or cpb == 1
    hi = lambda m, v: jnp.einsum('bij,bjk->bik', jnp.broadcast_to(m, (bb,) + m.shape), v,
                                 precision=HIGHEST, preferred_element_type=F32)
    for c0 in range(0, cpb, bb):
        nr = bb * rows
        x = x_ref[:, c0 * rows:c0 * rows + nr, :]
        ab = ab_ref[:, c0 * rows:c0 * rows + nr, :].reshape(bb, c, LANES)
        if n_invalid:
            row0 = (t * cpb + c0) * rows
            xrow = lax.broadcasted_iota(jnp.int32, (1, nr, 1), 1) + row0
            x = jnp.where(xrow >= n_invalid, x, 0.0)
        ext = jnp.concatenate([halo, x], axis=1)
        acc = cw[0:1, :] * ext[:, off:off + nr, :]
        for k in range(1, A_CONV):
            acc = acc + cw[k:k + 1, :] * ext[:, off + k:off + k + nr, :]
        y = _silu(acc).reshape(bb, c, QKV_W)
        halo = x[:, nr - GDN_HALO:nr, :]

        sp = jnp.maximum(ab + dtb, 0.0) + jnp.log1p(jnp.exp(-jnp.abs(ab + dtb)))
        gfull = -jnp.exp(alog) * sp
        bfull = jax.nn.sigmoid(ab)
        if n_invalid:
            valid = (lax.broadcasted_iota(jnp.int32, (bb, c, 1), 0) * c
                     + lax.broadcasted_iota(jnp.int32, (bb, c, 1), 1) + row0) >= n_invalid
            gfull = jnp.where(valid, gfull, 0.0)
            bfull = jnp.where(valid, bfull, 0.0)
        bcol = jnp.concatenate([bfull[:, :, A_HEADS + h:A_HEADS + h + 1] for h in range(A_HEADS)], axis=1)
        gg = hi(lmat, gfull)
        ggt = jnp.stack([gg[i].T for i in range(bb)])
        gc = jnp.concatenate([gg[:, :c, h:h + 1] for h in range(A_HEADS)], axis=1)
        grow = jnp.concatenate([ggt[:, h:h + 1, :c] for h in range(A_HEADS)], axis=2)
        drow = jnp.concatenate([ggt[:, h:h + 1, c:] for h in range(A_HEADS)], axis=2) - grow
        g8 = jnp.exp(hi(sel, gg))
        decay = jnp.where(incl, jnp.exp(jnp.where(incl, gc - grow, 0.0)), 0.0)

        qs, ks, vs = [], [], []
        for h in range(A_HEADS):
            q = y[:, :, h * A_DK:(h + 1) * A_DK]
            k = y[:, :, A_KW + h * A_DK:A_KW + (h + 1) * A_DK]
            qs.append(q * lax.rsqrt(jnp.sum(q * q, axis=-1, keepdims=True) + EPS) * (A_DK ** -0.5))
            ks.append(k * lax.rsqrt(jnp.sum(k * k, axis=-1, keepdims=True) + EPS))
            vs.append(y[:, :, 2 * A_KW + h * A_DK:2 * A_KW + (h + 1) * A_DK])
        qr = jnp.concatenate(qs, axis=1)
        kr = jnp.concatenate(ks, axis=1)
        vr = jnp.concatenate(vs, axis=1)
        krt = jnp.stack([kr[i].T for i in range(bb)])
        kq = _bmm(jnp.concatenate([kr, qr], axis=1), krt)
        a = jnp.where(strict, kq[:, :r] * decay * bcol, 0.0)
        qk = kq[:, r:] * decay
        tinv = eye - a
        ak = a
        kpow = 2
        while kpow < sb:
            ak = _bmm(ak, ak)
            tinv = tinv + _bmm(tinv, ak)
            kpow *= 2
        eg = jnp.exp(gc)
        uw = _bmm(tinv, jnp.concatenate([vr * bcol, kr * (bcol * eg)], axis=2))
        ur = uw[:, :, :A_DK]
        wr = uw[:, :, A_DK:]
        qd = qr * eg
        kdt = krt * jnp.exp(drow)
        cs = slice(c0, c0 + bb)
        for p in range(GDN_PAIRS):
            r0 = slice(2 * p * c, (2 * p + 1) * c)
            r1 = slice((2 * p + 1) * c, (2 * p + 2) * c)
            pl_ = slice(2 * p * c, (2 * p + 2) * c)
            u_ref[cs, p] = jnp.concatenate([ur[:, r0], ur[:, r1]], axis=2)
            wq = jnp.concatenate([jnp.concatenate([wr[:, r0], wr[:, r1]], axis=2),
                                  jnp.concatenate([qd[:, r0], qd[:, r1]], axis=2)], axis=1)
            wq_ref[cs, p] = wq.astype(wq_ref.dtype)
            qkk = jnp.concatenate([qk[:, r0, pl_] + qk[:, r1, pl_], kdt[:, :, pl_]], axis=1)
            qkk_ref[cs, p] = qkk.astype(qkk_ref.dtype)
            gl_ref[cs, p] = jnp.concatenate(
                [jnp.broadcast_to(g8[:, :, 2 * p:2 * p + 1], (bb, SUBLANES, A_DK)),
                 jnp.broadcast_to(g8[:, :, 2 * p + 1:2 * p + 2], (bb, SUBLANES, A_DK))], axis=2)


def _gdn_prep(x_arr, halo_arr, halo_spec, cw, alog, dtb, ns, rows, cpb, bb, sb, n_invalid, op_dtype):
    n, l, _ = x_arr.shape
    grid = (n // ns, l // (cpb * rows))
    units = grid[0] * grid[1] * cpb
    kern = functools.partial(_gdn_prep_kernel, sb=sb, cpb=cpb, bb=bb, n_invalid=n_invalid,
                             seq_halo=halo_arr is x_arr)
    full = lambda shape: pl.BlockSpec(shape, lambda i, t: (0,) * len(shape))
    steps = grid[1]
    out_map = lambda i, t: (i * steps + t, 0, 0, 0)
    c = GDN_ROWS
    return pl.pallas_call(
        kern,
        grid=grid,
        in_specs=[pl.BlockSpec((ns, cpb * rows, QKV_W), lambda i, t: (i, t, 0)),
                  halo_spec,
                  pl.BlockSpec((ns, cpb * rows, LANES), lambda i, t: (i, t, AB_BLK)),
                  full(cw.shape), full(alog.shape), full(dtb.shape)],
        out_specs=[pl.BlockSpec((cpb, GDN_PAIRS, c, PAIR_W), out_map),
                   pl.BlockSpec((cpb, GDN_PAIRS, 2 * c, PAIR_W), out_map),
                   pl.BlockSpec((cpb, GDN_PAIRS, c + A_DK, 2 * c), out_map),
                   pl.BlockSpec((cpb, GDN_PAIRS, SUBLANES, PAIR_W), out_map)],
        out_shape=[jax.ShapeDtypeStruct((units, GDN_PAIRS, c, PAIR_W), F32),
                   jax.ShapeDtypeStruct((units, GDN_PAIRS, 2 * c, PAIR_W), op_dtype),
                   jax.ShapeDtypeStruct((units, GDN_PAIRS, c + A_DK, 2 * c), op_dtype),
                   jax.ShapeDtypeStruct((units, GDN_PAIRS, SUBLANES, PAIR_W), F32)],
        compiler_params=_cparams(("parallel", "arbitrary")),
        name="gdn_prep",
    )(x_arr, halo_arr, x_arr, cw, alog, dtb)


def _gdn_out(o_pair, z_ref, o_ref, nw, idx, p):
    for i in range(2):
        h = 2 * p + i
        hs = slice(h * A_DK, (h + 1) * A_DK)
        o = _rms(o_pair[..., i * A_DK:(i + 1) * A_DK], nw) * _silu(z_ref[idx + (hs,)])
        o_ref[idx + (hs,)] = o.astype(o_ref.dtype)


def _gdn_scan_kernel(u_ref, wq_ref, qkk_ref, gl_ref, z_ref, nw_ref, o_ref, sout_ref, s_ref):
    t = pl.program_id(0)
    n_seq = u_ref.shape[0]
    c = GDN_ROWS
    nw = nw_ref[...]

    @pl.when(t == 0)
    def _():
        s_ref[...] = jnp.zeros(s_ref.shape, F32)

    nb = n_seq * GDN_PAIRS
    s = s_ref[...].reshape(nb, A_DK, PAIR_W)
    for ci in range(u_ref.shape[1]):
        flat = lambda ref: ref[:, ci].reshape((nb,) + ref.shape[3:])
        r1 = jnp.einsum('bmk,bkn->bmn', flat(wq_ref), _block_diag2(s.astype(BF16)), preferred_element_type=F32)
        vnew = flat(u_ref) - r1[:, :c]
        r2 = jnp.einsum('bmk,bkn->bmn', flat(qkk_ref), _block_diag2(vnew.astype(BF16)),
                        preferred_element_type=F32)
        s = s * flat(gl_ref)[:, 0:1, :] + r2[:, c:]
        o = (r1[:, c:] + r2[:, :c]).reshape(n_seq, GDN_PAIRS, c, PAIR_W)
        rows = slice(ci * c, (ci + 1) * c)
        for p in range(GDN_PAIRS):
            _gdn_out(o[:, p], z_ref, o_ref, nw, (slice(None), rows), p)
    s_ref[...] = s.reshape(s_ref.shape)

    @pl.when(t == pl.num_programs(0) - 1)
    def _():
        for n in range(n_seq):
            for p in range(GDN_PAIRS):
                s = s_ref[n, p]
                sout_ref[n, 2 * p] = s[:, :A_DK]
                sout_ref[n, 2 * p + 1] = s[:, A_DK:]


def _gdn_scan(u, wq, qkk, gl, p3, nw):
    n, l, _ = p3.shape
    sshape = (n, A_HEADS, A_DK, A_DK)
    c = GDN_ROWS
    cps = GDN_SCAN_CHUNKS
    steps = l // (c * cps)
    v5 = lambda a: a.reshape((n, steps * cps) + a.shape[1:])
    u, wq, qkk, gl = v5(u), v5(wq), v5(qkk), v5(gl)
    unit = lambda a: pl.BlockSpec((n, cps) + a.shape[2:], lambda t: (0, t, 0, 0, 0))
    vw = A_HEADS * A_DK
    return pl.pallas_call(
        _gdn_scan_kernel,
        grid=(steps,),
        in_specs=[unit(u), unit(wq), unit(qkk), unit(gl),
                  pl.BlockSpec((n, cps * c, vw), lambda t: (0, t, Z_BLK)),
                  pl.BlockSpec(nw.shape, lambda t: (0, 0))],
        out_specs=[pl.BlockSpec((n, cps * c, vw), lambda t: (0, t, 0)),
                   pl.BlockSpec(sshape, lambda t: (0, 0, 0, 0))],
        out_shape=[jax.ShapeDtypeStruct((n, l, vw), BF16),
                   jax.ShapeDtypeStruct(sshape, F32)],
        scratch_shapes=[pltpu.VMEM((n, GDN_PAIRS, A_DK, PAIR_W), F32)],
        compiler_params=_cparams(("arbitrary",)),
        name="gdn_scan",
    )(u, wq, qkk, gl, p3, nw)


def _gdn_step_kernel(u_ref, wq_ref, qkk_ref, gl_ref, z_ref, s0_ref, nw_ref, *rest):
    o_ref, sout_ref = rest[-2:]
    ns, rows, _ = z_ref.shape
    c = GDN_ROWS
    nw = nw_ref[...]
    lane_seq = (lax.broadcasted_iota(jnp.int32, (A_DK, 2 * c), 1) % c) // rows
    for p in range(GDN_PAIRS):
        wq = wq_ref[0, p]
        qkk = qkk_ref[0, p]
        gl = gl_ref[0, p]
        s = jnp.concatenate([s0_ref[:, 2 * p], s0_ref[:, 2 * p + 1]], axis=2)
        lhs = jnp.concatenate([wq[:c].reshape(ns, rows, PAIR_W), wq[c:].reshape(ns, rows, PAIR_W)], axis=1)
        r1 = _bmm(lhs, _block_diag2(s.astype(BF16)))
        vnew = u_ref[0, p] - r1[:, :rows].reshape(c, PAIR_W)
        bdv = _block_diag2(vnew.astype(BF16))
        o = r1[:, rows:].reshape(c, PAIR_W) + _mm(qkk[:c], bdv)
        _gdn_out(o.reshape(ns, rows, PAIR_W), z_ref, o_ref, nw, (slice(None), slice(None)), p)
        kdt = qkk[c:]
        lhs = jnp.concatenate([jnp.where(lane_seq == s_i, kdt, 0.0) for s_i in range(ns)], axis=0)
        upd = _mm(lhs, bdv).reshape(ns, A_DK, PAIR_W)
        s_new = s * gl.reshape(ns, 1, PAIR_W) + upd
        sout_ref[:, 2 * p] = s_new[:, :, :A_DK]
        sout_ref[:, 2 * p + 1] = s_new[:, :, A_DK:]


def _gdn_step(u, wq, qkk, gl, p3, s_all, layer, s_prev, nw, ns):
    n, rows, _ = p3.shape
    unit = lambda a: pl.BlockSpec((1,) + a.shape[1:], lambda i: (i, 0, 0, 0))
    vw = A_HEADS * A_DK
    sblk = pl.BlockSpec((None, ns) + s_all.shape[2:], lambda i: (layer, i, 0, 0, 0))
    in_specs = [unit(u), unit(wq), unit(qkk), unit(gl),
                pl.BlockSpec((ns, rows, vw), lambda i: (i, 0, Z_BLK)),
                sblk,
                pl.BlockSpec(nw.shape, lambda i: (0, 0))]
    args = [u, wq, qkk, gl, p3, s_all, nw]
    aliases = {}
    if s_prev is not None:
        in_specs.append(pl.BlockSpec(memory_space=pl.ANY))
        args.append(s_prev)
        aliases = {len(args) - 1: 1}
    return pl.pallas_call(
        _gdn_step_kernel,
        grid=(n // ns,),
        in_specs=in_specs,
        out_specs=[pl.BlockSpec((ns, rows, vw), lambda i: (i, 0, 0)), sblk],
        out_shape=[jax.ShapeDtypeStruct((n, rows, vw), F32),
                   jax.ShapeDtypeStruct(s_all.shape, F32)],
        input_output_aliases=aliases,
        compiler_params=_cparams(("parallel",)),
        name="gdn_step",
    )(*args)


def _gdn(p3, state, hist, cw, alog, dtb, nw, cpb, n_invalid):
    n, l, _ = p3.shape
    c = GDN_ROWS
    if hist is None:
        steps8 = cpb * c // GDN_HALO
        halo_spec = pl.BlockSpec((1, GDN_HALO, QKV_W), lambda i, t: (i, jnp.maximum(t * steps8 - 1, 0), 0))
        u, wq, qkk, gl = _gdn_prep(p3, p3, halo_spec, cw, alog, dtb, 1, c, cpb, min(cpb, GDN_BATCH), c, n_invalid, BF16)
        return _gdn_scan(u, wq, qkk, gl, p3, nw)
    ns = c // l
    halo_spec = pl.BlockSpec((ns, GDN_HALO, QKV_W), lambda i, t: (i, 0, 0))
    u, wq, qkk, gl = _gdn_prep(p3, hist, halo_spec, cw, alog, dtb, ns, l, 1, 1, l, n_invalid, F32)
    s_all, layer, s_prev = state
    return _gdn_step(u, wq, qkk, gl, p3, s_all, layer, s_prev, nw, ns)


def _merge_kernel(oa_ref, ob_ref, oc_ref, gl_ref, x_ref, wb_ref, wo_ref, out_ref):
    d = x_ref.shape[1]
    merged = None
    for r, o_ref in enumerate((oa_ref, ob_ref, oc_ref)):
        pb = jnp.dot(o_ref[...].astype(BF16), wb_ref[r], preferred_element_type=F32)
        term = jax.nn.sigmoid(gl_ref[:, r * d:(r + 1) * d]) * pb
        merged = term if merged is None else merged + term
    out_ref[...] = x_ref[...] + jnp.dot(merged.astype(BF16), wo_ref[...], preferred_element_type=F32)


def _merge(oa, ob, oc, p2, x2, wb, wo, tm):
    t, d = x2.shape
    bw = oa.shape[1]
    row = lambda w, blk: pl.BlockSpec((tm, w), lambda i: (i, blk))
    return pl.pallas_call(
        _merge_kernel,
        grid=(t // tm,),
        in_specs=[row(bw, 0), row(bw, 0), row(bw, 0), row(3 * d, GL_BLK), row(d, 0),
                  pl.BlockSpec(wb.shape, lambda i: (0, 0, 0)),
                  pl.BlockSpec(wo.shape, lambda i: (0, 0))],
        out_specs=row(d, 0),
        out_shape=jax.ShapeDtypeStruct((t, d), F32),
        compiler_params=_cparams(("parallel",)),
        name="merge",
    )(oa, ob, oc, p2, x2, wb, wo)


def _route(logits):
    lane = lax.broadcasted_iota(jnp.int32, logits.shape, 1)
    is_g = (lane >= N_EXPERTS) & (lane < N_EXPERTS + N_GROUPS)
    gl = jnp.where(is_g, logits, NEG)
    gmax = jnp.max(gl, axis=-1, keepdims=True)
    gidx = jnp.min(jnp.where(gl == gmax, lane - N_EXPERTS, LANES), axis=-1, keepdims=True)
    gw = 1.0 / jnp.sum(jnp.where(is_g, jnp.exp(gl - gmax), 0.0), axis=-1, keepdims=True)
    in_grp = (lane < N_EXPERTS) & ((lane // EXP_PER_GROUP) == gidx)
    el = jnp.where(in_grp, logits, NEG)
    v1 = jnp.max(el, axis=-1, keepdims=True)
    i1 = jnp.min(jnp.where(el == v1, lane, LANES), axis=-1, keepdims=True)
    el2 = jnp.where(lane == i1, NEG, el)
    v2 = jnp.max(el2, axis=-1, keepdims=True)
    i2 = jnp.min(jnp.where(el2 == v2, lane, LANES), axis=-1, keepdims=True)
    e21 = jnp.exp(v2 - v1)
    w1 = gw / (1.0 + e21)
    w2 = gw * e21 / (1.0 + e21)
    return jnp.where(lane == i1, w1, 0.0) + jnp.where(lane == i2, w2, 0.0)


def _moe_kernel(x_ref, nw_ref, wr_ref, br_ref, wg_ref, wu_ref, wd_ref, out_ref, h_ref, cmb_ref):
    g = pl.program_id(1)

    @pl.when(g == 0)
    def _():
        x = x_ref[...]
        h = _rms(x, nw_ref[...])
        h_hi = h.astype(BF16)
        h_ref[...] = h_hi
        h_lo = (h - h_hi.astype(F32)).astype(BF16)
        l2 = jnp.dot(h_hi, wr_ref[...], preferred_element_type=F32)
        logits = (l2[:, :LANES] + l2[:, LANES:]
                  + jnp.dot(h_lo, wr_ref[:, :LANES], preferred_element_type=F32) + br_ref[...])
        cmb_ref[...] = _route(logits)
        out_ref[...] = x

    h = h_ref[...]
    cmb = cmb_ref[...]
    lane = lax.broadcasted_iota(jnp.int32, cmb.shape, 1)
    parts = []
    for e in range(EXP_PER_GROUP):
        ce = jnp.sum(jnp.where(lane == g * EXP_PER_GROUP + e, cmb, 0.0), axis=-1, keepdims=True)
        hid = (_silu(jnp.dot(h, wg_ref[e], preferred_element_type=F32))
               * jnp.dot(h, wu_ref[e], preferred_element_type=F32))
        parts.append((hid * ce).astype(BF16))
    wd = wd_ref[...].reshape(-1, wd_ref.shape[-1])
    out_ref[...] += jnp.dot(jnp.concatenate(parts, axis=1), wd, preferred_element_type=F32)


def _moe(x2, nw, wr, br, wg, wu, wd, tm):
    t, d = x2.shape
    ne, _, de = wg.shape
    return pl.pallas_call(
        _moe_kernel,
        grid=(t // tm, ne // EXP_PER_GROUP),
        in_specs=[pl.BlockSpec((tm, d), lambda i, g: (i, 0)),
                  pl.BlockSpec((1, d), lambda i, g: (0, 0)),
                  pl.BlockSpec(wr.shape, lambda i, g: (0, 0)),
                  pl.BlockSpec(br.shape, lambda i, g: (0, 0)),
                  pl.BlockSpec((EXP_PER_GROUP, d, de), lambda i, g: (g, 0, 0)),
                  pl.BlockSpec((EXP_PER_GROUP, d, de), lambda i, g: (g, 0, 0)),
                  pl.BlockSpec((EXP_PER_GROUP, de, d), lambda i, g: (g, 0, 0))],
        out_specs=pl.BlockSpec((tm, d), lambda i, g: (i, 0)),
        out_shape=jax.ShapeDtypeStruct((t, d), F32),
        scratch_shapes=[pltpu.VMEM((tm, d), BF16), pltpu.VMEM((tm, LANES), F32)],
        compiler_params=_cparams(("parallel", "arbitrary")),
        name="moe",
    )(x2, nw, wr, br, wg, wu, wd)


def _pad_lanes(v, width):
    v = v.reshape(1, -1)
    return jnp.pad(v, ((0, 0), (0, width - v.shape[1])))


def _layer_weights(l, norm1_w, norm2_w, w_in, gdn_conv_w, gdn_a_log, gdn_dt_bias, gdn_norm_w,
                   swa_q_norm_w, swa_k_norm_w, swa_sinks, conv_dw_w, conv_dw_b, conv_ln_w, conv_ln_b,
                   w_branch, w_out, router_group_w, router_group_b, router_expert_w, router_expert_b,
                   moe_w_gate, moe_w_up, moe_w_down):
    d = w_in.shape[1]
    wi = w_in[l]
    o_ab = 4 * A_KW
    o_bq = o_ab + 2 * A_HEADS
    o_cu = o_bq + B_HEADS * B_HD + BKV_W
    o_gl = o_cu + 2 * C_CH
    w_perm = jnp.concatenate(
        [wi[:, :o_ab], wi[:, o_cu:o_gl], wi[:, o_gl:], wi[:, o_bq:o_cu], wi[:, o_ab:o_bq],
         jnp.zeros((d, LANES - 2 * A_HEADS), wi.dtype)], axis=1).astype(BF16)
    assert w_perm.shape[1] == PROJ_W
    wr = jnp.concatenate([router_expert_w[l], router_group_w[l],
                          jnp.zeros((d, LANES - N_EXPERTS - N_GROUPS), F32)], axis=1)
    br = _pad_lanes(jnp.concatenate([router_expert_b[l], router_group_b[l]]), LANES)
    wr_hi = wr.astype(BF16)
    wr = jnp.concatenate([wr_hi, (wr - wr_hi.astype(F32)).astype(BF16)], axis=1)
    return dict(
        n1=norm1_w[l].reshape(1, d), n2=norm2_w[l].reshape(1, d), w_in=w_perm,
        gdn_cw=gdn_conv_w[l], alog=_pad_lanes(gdn_a_log[l], LANES), dtb=_pad_lanes(gdn_dt_bias[l], LANES),
        gdn_nw=gdn_norm_w[l].reshape(1, -1),
        qnw=swa_q_norm_w[l].reshape(1, -1), knw=swa_k_norm_w[l].reshape(1, -1), sinks=swa_sinks[l].reshape(1, -1),
        qnw_heads=jnp.tile(swa_q_norm_w[l].reshape(1, -1), (1, B_HEADS)),
        knw_heads=jnp.tile(swa_k_norm_w[l].reshape(1, -1), (1, B_KV_HEADS)),
        dww=conv_dw_w[l], dwb=conv_dw_b[l].reshape(1, -1), lnw=conv_ln_w[l].reshape(1, -1),
        lnb=conv_ln_b[l].reshape(1, -1),
        wb=w_branch[l].astype(BF16), wo=w_out[l].astype(BF16), wr=wr, br=br,
        wg=moe_w_gate[l].astype(BF16), wu=moe_w_up[l].astype(BF16), wd=moe_w_down[l].astype(BF16))


def _tiles(n, l):
    prompt = l > BLOCK
    t = n * l
    if prompt:
        return dict(proj_tm=1536, tok_tm=512, moe_tm=1056, conf_nb=1, conf_tm=384, conf_rc=64,
                    gdn_cpb=6, swa_bps=11, n_invalid=META_PAD, branch_dtype=BF16)
    return dict(proj_tm=t, tok_tm=512, moe_tm=t, conf_nb=16, conf_tm=l, conf_rc=l,
                gdn_cpb=1, n_invalid=0, branch_dtype=F32)


def _layer(x3, lw, gdn_state, gdn_hist, conf_hist, swa_cache):
    n, l, d = x3.shape
    tl = _tiles(n, l)
    x2 = x3.reshape(n * l, d)
    p2 = _inproj(x2, lw['n1'], lw['w_in'], tl['proj_tm'], PROJ_W // 5)
    p3 = p2.reshape(n, l, PROJ_W)
    o_a, s_new = _gdn(p3, gdn_state, gdn_hist, lw['gdn_cw'], lw['alog'], lw['dtb'], lw['gdn_nw'],
                      tl['gdn_cpb'], tl['n_invalid'])
    gh_new = p3[:, l - (A_CONV - 1):, :QKV_W]
    if swa_cache is None:
        o_b, kvn = _swa_prompt(p3, lw['qnw_heads'], lw['knw_heads'], lw['sinks'], tl['swa_bps'])
        kv = lambda sl: kvn[:, sl].reshape(n, -1, 2, B_KV_HEADS, B_HD)
        swa_new = (kv(slice(META_PAD, BLOCK)), kv(slice(l - WINDOW, l)))
    else:
        win_all, meta_all, layer, win_prev = swa_cache
        o_b, swa_new = _swa_sample(p3, win_all, meta_all, layer, win_prev, lw['qnw'], lw['knw'], lw['sinks'], 16)
    o_c, ch_new = _conformer(p3, conf_hist, lw['dww'], lw['dwb'], lw['lnw'], lw['lnb'],
                             tl['conf_nb'], tl['conf_tm'], tl['conf_rc'], tl['n_invalid'], tl['branch_dtype'])
    t = n * l
    x2 = _merge(o_a.reshape(t, -1), o_b.reshape(t, -1), o_c.reshape(t, -1), p2, x2, lw['wb'], lw['wo'],
                tl['tok_tm'])
    x2 = _moe(x2, lw['n2'], lw['wr'], lw['br'], lw['wg'], lw['wu'], lw['wd'], tl['moe_tm'])
    return x2.reshape(n, l, d), s_new, gh_new, swa_new, ch_new


def kernel(x_prompt, x_sample, state_gdn, cache_gdn_conv, cache_swa_kv, cache_meta_kv, cache_conv, meta_tokens, norm1_w, norm2_w, w_in, gdn_conv_w, gdn_a_log, gdn_dt_bias, gdn_norm_w, swa_q_norm_w, swa_k_norm_w, swa_sinks, conv_dw_w, conv_dw_b, conv_ln_w, conv_ln_b, w_branch, w_out, router_group_w, router_group_b, router_expert_w, router_expert_b, moe_w_gate, moe_w_up, moe_w_down):
    dtp = x_prompt.dtype
    n_p, _, d = x_prompt.shape
    n_s = x_sample.shape[0]
    depth = w_in.shape[0]
    xp = jnp.pad(x_prompt, ((0, 0), (BLOCK, 0), (0, 0)))
    xp = lax.dynamic_update_slice(
        xp, jnp.broadcast_to(meta_tokens.astype(dtp)[None], (n_p, N_META, d)), (0, META_PAD, 0))
    xs = x_sample
    state_s = None
    win_s = None
    win_all = cache_swa_kv.reshape(depth, n_s, -1, BKV_W)
    meta_all = cache_meta_kv.reshape(depth, n_s, N_META, BKV_W)
    outs_p, outs_s = [], []
    for l in range(depth):
        lw = _layer_weights(l, norm1_w, norm2_w, w_in, gdn_conv_w, gdn_a_log, gdn_dt_bias, gdn_norm_w,
                            swa_q_norm_w, swa_k_norm_w, swa_sinks, conv_dw_w, conv_dw_b, conv_ln_w, conv_ln_b,
                            w_branch, w_out, router_group_w, router_group_b, router_expert_w, router_expert_b,
                            moe_w_gate, moe_w_up, moe_w_down)
        xp, s_p, gh_p, (mkv_p, wkv_p), ch_p = _layer(
            xp, lw, None, None, jnp.zeros((n_p, C_CONV - 1, C_CH), dtp), None)
        outs_p.append((s_p, gh_p, wkv_p, mkv_p, ch_p))
        gh0 = jnp.pad(cache_gdn_conv[l], ((0, 0), (GDN_HALO - (A_CONV - 1), 0), (0, 0)))
        xs, state_s, gh_s, win_s, ch_s = _layer(xs, lw, (state_gdn, l, state_s), gh0, cache_conv[l],
                                                (win_all, meta_all, l, win_s))
        outs_s.append((gh_s, ch_s))
    stack = lambda outs, i: jnp.stack([o[i] for o in outs])
    return (xp[:, BLOCK:], xs,
            stack(outs_p, 0), stack(outs_p, 1), stack(outs_p, 2), stack(outs_p, 3), stack(outs_p, 4),
            state_s, stack(outs_s, 0), win_s.reshape(cache_swa_kv.shape), stack(outs_s, 1))
```
